```python
import jax, jax.numpy as jnp
from jax import lax
import numpy as np

D_MODEL = 1024
BATCH = 16
SEQ = 256
DEPTH = 2
DEC_BATCH = 2
DEC_SEQ = 1024
PAST_LEN = 512

GRID_W = 64
HEAD_DIM = 64
N_HEADS_A = 8
N_KV_A = 2
N_HEADS_B = 8
WIN_ROWS_MAX = 8
WIN_COLS = 16
Q_BLOCK = 128
N_EXPERTS = 16
EXPERT_FF = 2048
CAPACITY_FACTOR = 2
ROPE_THETA = 10000.0
EPS = 1e-6
NEG = -1e30
DIM_QA = N_HEADS_A * HEAD_DIM
DIM_KVA = N_KV_A * HEAD_DIM
DIM_B = N_HEADS_B * HEAD_DIM
IN_DIM = DIM_QA + 2 * DIM_KVA + 3 * DIM_B + 2 * D_MODEL
SPLITS = [DIM_QA, DIM_QA + DIM_KVA, DIM_QA + 2 * DIM_KVA,
          DIM_QA + 2 * DIM_KVA + DIM_B, DIM_QA + 2 * DIM_KVA + 2 * DIM_B,
          DIM_QA + 2 * DIM_KVA + 3 * DIM_B, DIM_QA + 2 * DIM_KVA + 3 * DIM_B + D_MODEL]

kernel_name = "hybrid_diffusion_gqa_natten_ec_step"


def rmsnorm(x, g):
    xf = x.astype(jnp.float32)
    y = xf * lax.rsqrt(jnp.mean(xf * xf, axis=-1, keepdims=True) + EPS)
    return (y * g.astype(jnp.float32)).astype(x.dtype)


def rope_1d(x, pos):
    half = x.shape[-1] // 2
    freqs = ROPE_THETA ** (-jnp.arange(half, dtype=jnp.float32) / half)
    ang = pos.astype(jnp.float32)[:, None] * freqs[None, :]
    cos = jnp.cos(ang)[:, None, :]
    sin = jnp.sin(ang)[:, None, :]
    xf = x.astype(jnp.float32)
    x1, x2 = xf[..., :half], xf[..., half:]
    return jnp.concatenate([x1 * cos - x2 * sin, x1 * sin + x2 * cos], axis=-1).astype(x.dtype)


def rope_2d(x):
    n = x.shape[1]
    pos = jnp.arange(n)
    half = x.shape[-1] // 2
    return jnp.concatenate([rope_1d(x[..., :half], pos // GRID_W),
                            rope_1d(x[..., half:], pos % GRID_W)], axis=-1)


def modulation(cond, w_ada, b_ada):
    mod = jnp.einsum("...d,de->...e", jax.nn.silu(cond), w_ada) + b_ada
    return jnp.split(mod, 6, axis=-1)


def modulated_norm(x, g, shift, scale):
    return rmsnorm(x, g) * (1 + scale) + shift


def project_heads(h, w_in, q_norm, k_norm, use_rope):
    b, n, _ = h.shape
    y = jnp.einsum("bnd,de->bne", h, w_in)
    qa, ka, va, qb, kb, vb, ga, gb = jnp.split(y, SPLITS, axis=-1)
    qa = rmsnorm(qa.reshape(b, n, N_HEADS_A, HEAD_DIM), q_norm)
    ka = rmsnorm(ka.reshape(b, n, N_KV_A, HEAD_DIM), k_norm)
    if use_rope:
        qa, ka = rope_2d(qa), rope_2d(ka)
    va = va.reshape(b, n, N_KV_A, HEAD_DIM)
    qb = qb.reshape(b, n, N_HEADS_B, HEAD_DIM)
    kb = kb.reshape(b, n, N_HEADS_B, HEAD_DIM)
    vb = vb.reshape(b, n, N_HEADS_B, HEAD_DIM)
    return qa, ka, va, qb, kb, vb, ga, gb


def blocked_attention(q, k, v):
    b, n, h, d = q.shape
    kv = k.shape[2]
    g = h // kv
    nb = n // Q_BLOCK
    qb = q.reshape(b, nb, Q_BLOCK, kv, g, d).transpose(1, 0, 2, 3, 4, 5)
    scale = d ** -0.5

    def one_block(qblk):
        s = jnp.einsum("bqkgd,bmkd->bkgqm", qblk, k).astype(jnp.float32) * scale
        p = jax.nn.softmax(s, axis=-1).astype(v.dtype)
        return jnp.einsum("bkgqm,bmkd->bqkgd", p, v)

    o = lax.map(one_block, qb)
    return o.transpose(1, 0, 2, 3, 4, 5).reshape(b, n, h * d)


def neighborhood_attention(q, k, v, k_ctx, v_ctx, rpb):
    b, n, h, d = q.shape
    rows = n // GRID_W
    wr = min(WIN_ROWS_MAX, rows)
    r = jnp.arange(rows)
    row_start = jnp.clip(r - wr // 2, 0, rows - wr)
    row_idx = row_start[:, None] + jnp.arange(wr)[None, :]
    cq = jnp.arange(GRID_W)
    col_start = jnp.clip(cq - WIN_COLS // 2, 0, GRID_W - WIN_COLS)
    col_valid = (cq[None, :] >= col_start[:, None]) & (cq[None, :] < col_start[:, None] + WIN_COLS)
    dr = row_idx - r[:, None] + (WIN_ROWS_MAX - 1)
    dc = jnp.clip(cq[None, :] - cq[:, None], -(WIN_COLS - 1), WIN_COLS - 1) + (WIN_COLS - 1)
    bias = rpb[:, dr[:, None, :, None], dc[None, :, None, :]]
    qg = q.reshape(b, rows, GRID_W, h, d)
    kg = k.reshape(b, rows, GRID_W, h, d)[:, row_idx]
    vg = v.reshape(b, rows, GRID_W, h, d)[:, row_idx]
    scale = d ** -0.5
    s_loc = jnp.einsum("brqhd,brikhd->bhrqik", qg, kg).astype(jnp.float32) * scale
    s_loc = s_loc + bias[None].astype(jnp.float32)
    s_loc = jnp.where(col_valid[:, None, :], s_loc, NEG).reshape(b, h, rows, GRID_W, wr * GRID_W)
    s_ctx = jnp.einsum("brqhd,bmhd->bhrqm", qg, k_ctx).astype(jnp.float32) * scale
    p = jax.nn.softmax(jnp.concatenate([s_loc, s_ctx], axis=-1), axis=-1).astype(v.dtype)
    p_loc = p[..., :wr * GRID_W].reshape(b, h, rows, GRID_W, wr, GRID_W)
    p_ctx = p[..., wr * GRID_W:]
    o = jnp.einsum("bhrqik,brikhd->brqhd", p_loc, vg) + jnp.einsum("bhrqm,bmhd->brqhd", p_ctx, v_ctx)
    return o.reshape(b, n, h * d)


def gated_merge(ya, yb, ga, gb, w_ba, w_bb, w_out):
    za = jnp.einsum("bnk,kd->bnd", ya, w_ba)
    zb = jnp.einsum("bnk,kd->bnd", yb, w_bb)
    m = jax.nn.sigmoid(ga) * za + jax.nn.sigmoid(gb) * zb
    return jnp.einsum("bnd,de->bne", m, w_out)


def expert_choice_ffn(h, w_router, w_gate, w_up, w_down):
    b, n, d = h.shape
    cap = CAPACITY_FACTOR * n // N_EXPERTS
    aff = jax.nn.softmax(jnp.einsum("bnd,de->bne", h, w_router).astype(jnp.float32), axis=-1)
    g, idx = lax.top_k(jnp.swapaxes(aff, 1, 2), cap)
    xg = jax.vmap(lambda hb, ib: hb[ib])(h, idx)
    hid = jax.nn.silu(jnp.einsum("becd,edf->becf", xg, w_gate)) * jnp.einsum("becd,edf->becf", xg, w_up)
    out = jnp.einsum("becf,efd->becd", hid, w_down) * g[..., None].astype(h.dtype)
    seg = (idx + (jnp.arange(b) * n)[:, None, None]).reshape(-1)
    y = jax.ops.segment_sum(out.reshape(-1, d), seg, num_segments=b * n)
    return y.reshape(b, n, d)


def setup_inputs(seed: int = 0) -> dict:
    key = jax.random.key(seed)
    ks = jax.random.split(key, 24)
    f32 = jnp.float32
    nrm = lambda k, shape, s: jax.random.normal(k, shape, f32) * s
    return {
        "x_prompt": nrm(ks[0], (BATCH, SEQ, D_MODEL), 1.0),
        "x_sample": nrm(ks[1], (DEC_BATCH, DEC_SEQ, D_MODEL), 1.0),
        "cache_attn_k": nrm(ks[2], (DEC_BATCH, DEPTH, PAST_LEN, N_KV_A, HEAD_DIM), 1.0),
        "cache_attn_v": nrm(ks[3], (DEC_BATCH, DEPTH, PAST_LEN, N_KV_A, HEAD_DIM), 1.0),
        "cache_na_k": nrm(ks[4], (DEC_BATCH, DEPTH, PAST_LEN, N_HEADS_B, HEAD_DIM), 1.0),
        "cache_na_v": nrm(ks[5], (DEC_BATCH, DEPTH, PAST_LEN, N_HEADS_B, HEAD_DIM), 1.0),
        "c": nrm(ks[6], (DEC_BATCH, D_MODEL), 1.0),
        "c_ctx": nrm(ks[7], (D_MODEL,), 1.0),
        "w_ada": nrm(ks[8], (DEPTH, D_MODEL, 6 * D_MODEL), 0.5 * D_MODEL ** -0.5),
        "b_ada": nrm(ks[9], (DEPTH, 6 * D_MODEL), 0.01),
        "norm_mix": 1.0 + nrm(ks[10], (DEPTH, D_MODEL), 0.01),
        "norm_ffn": 1.0 + nrm(ks[11], (DEPTH, D_MODEL), 0.01),
        "w_in": nrm(ks[12], (DEPTH, D_MODEL, IN_DIM), D_MODEL ** -0.5),
        "q_norm": 1.0 + nrm(ks[13], (DEPTH, HEAD_DIM), 0.01),
        "k_norm": 1.0 + nrm(ks[14], (DEPTH, HEAD_DIM), 0.01),
        "rpb": nrm(ks[15], (DEPTH, N_HEADS_B, 2 * WIN_ROWS_MAX - 1, 2 * WIN_COLS - 1), 0.1),
        "w_branch_a": nrm(ks[16], (DEPTH, DIM_QA, D_MODEL), DIM_QA ** -0.5),
        "w_branch_b": nrm(ks[17], (DEPTH, DIM_B, D_MODEL), DIM_B ** -0.5),
        "w_out": nrm(ks[18], (DEPTH, D_MODEL, D_MODEL), D_MODEL ** -0.5),
        "w_router": nrm(ks[19], (DEPTH, D_MODEL, N_EXPERTS), D_MODEL ** -0.5),
        "w_gate": nrm(ks[20], (DEPTH, N_EXPERTS, D_MODEL, EXPERT_FF), D_MODEL ** -0.5),
        "w_up": nrm(ks[21], (DEPTH, N_EXPERTS, D_MODEL, EXPERT_FF), D_MODEL ** -0.5),
        "w_down": nrm(ks[22], (DEPTH, N_EXPERTS, EXPERT_FF, D_MODEL), EXPERT_FF ** -0.5),
        "final_norm": 1.0 + nrm(ks[23], (D_MODEL,), 0.01),
    }


def reference(x_prompt, x_sample, cache_attn_k, cache_attn_v, cache_na_k, cache_na_v, c, c_ctx,
              w_ada, b_ada, norm_mix, norm_ffn, w_in, q_norm, k_norm, rpb, w_branch_a, w_branch_b,
              w_out, w_router, w_gate, w_up, w_down, final_norm):
    xp, xs = x_prompt, x_sample
    new_ak, new_av, new_nk, new_nv = [], [], [], []
    for l in range(DEPTH):
        sh1, sc1, gt1, sh2, sc2, gt2 = modulation(c_ctx[None, None, :], w_ada[l], b_ada[l])
        h = modulated_norm(xp, norm_mix[l], sh1, sc1)
        qa, ka, va, qb, kb, vb, ga, gb = project_heads(h, w_in[l], q_norm[l], k_norm[l], False)
        ya = blocked_attention(qa, ka, va)
        yb = blocked_attention(qb, kb, vb)
        xp = xp + gt1 * gated_merge(ya, yb, ga, gb, w_branch_a[l], w_branch_b[l], w_out[l])
        h = modulated_norm(xp, norm_ffn[l], sh2, sc2)
        xp = xp + gt2 * expert_choice_ffn(h, w_router[l], w_gate[l], w_up[l], w_down[l])
        new_ak.append(ka)
        new_av.append(va)
        new_nk.append(kb)
        new_nv.append(vb)
        sh1, sc1, gt1, sh2, sc2, gt2 = modulation(c[:, None, :], w_ada[l], b_ada[l])
        h = modulated_norm(xs, norm_mix[l], sh1, sc1)
        qa, ka, va, qb, kb, vb, ga, gb = project_heads(h, w_in[l], q_norm[l], k_norm[l], True)
        ya = blocked_attention(qa, jnp.concatenate([cache_attn_k[:, l], ka], axis=1),
                               jnp.concatenate([cache_attn_v[:, l], va], axis=1))
        yb = neighborhood_attention(qb, kb, vb, cache_na_k[:, l], cache_na_v[:, l], rpb[l])
        xs = xs + gt1 * gated_merge(ya, yb, ga, gb, w_branch_a[l], w_branch_b[l], w_out[l])
        h = modulated_norm(xs, norm_ffn[l], sh2, sc2)
        xs = xs + gt2 * expert_choice_ffn(h, w_router[l], w_gate[l], w_up[l], w_down[l])
    y_prompt = rmsnorm(xp, final_norm)
    y_sample = rmsnorm(xs, final_norm)
    new_attn_k = jnp.stack(new_ak, axis=1)
    new_attn_v = jnp.stack(new_av, axis=1)
    new_na_k = jnp.stack(new_nk, axis=1)
    new_na_v = jnp.stack(new_nv, axis=1)
    return (y_prompt, y_sample, new_attn_k, new_attn_v, new_na_k, new_na_v)
```

```python
import functools

import numpy as np
import jax
import jax.numpy as jnp
from jax import lax
from jax.experimental import pallas as pl
from jax.experimental.pallas import tpu as pltpu

F32 = jnp.float32
BF16 = jnp.bfloat16

D_MODEL = 1024
N_CTX_SETS, CTX_LEN = 16, 256
N_LAT_SETS, LAT_LEN = 2, 1024
T_CTX = N_CTX_SETS * CTX_LEN
T_LAT = N_LAT_SETS * LAT_LEN
T_ALL = T_CTX + T_LAT
DEPTH = 2
PAST_LEN = 512
GRID_W = 64
GRID_ROWS = LAT_LEN // GRID_W
HEAD_DIM = 64
N_HEADS_A, N_KV_A, N_HEADS_B = 8, 2, 8
WIN_ROWS, WIN_COLS = 8, 16
N_EXPERTS = 16
EXPERT_FF = 2048
CAP_CTX = 2 * CTX_LEN // N_EXPERTS
CAP_LAT = 2 * LAT_LEN // N_EXPERTS
ROPE_THETA = 10000.0
EPS = 1e-6
NEG = -1e30
IN_DIM = 4352
ATT_SCALE = HEAD_DIM ** -0.5

LANES = 128
VMEM_LIMIT = 56 * 1024 * 1024

PROJ_TN = 256
N_PROJ_TILES = IN_DIM // PROJ_TN
PROJ_PERM = np.array([0, 1, 16, 2, 3, 4, 5, 6, 7, 8, 9, 10, 11, 12, 13, 14, 15], np.int32)
MOD_SH1, MOD_SC1, MOD_GT1, MOD_SH2, MOD_SC2, MOD_GT2 = range(6)


def _params(n_grid_dims, vmem=VMEM_LIMIT):
    return pltpu.CompilerParams(dimension_semantics=("arbitrary",) * n_grid_dims, vmem_limit_bytes=vmem)


def _bdot(a, b):
    return jnp.dot(a, b, preferred_element_type=F32)


def _bdot_nt(a, b):
    return lax.dot_general(a, b, (((1,), (1,)), ((), ())), preferred_element_type=F32)


def _split2(x):
    hi = x.astype(BF16)
    lo = (x - hi.astype(F32)).astype(BF16)
    return hi, lo


def _split3(x):
    hi = x.astype(BF16)
    r = x - hi.astype(F32)
    mid = r.astype(BF16)
    lo = (r - mid.astype(F32)).astype(BF16)
    return hi, mid, lo


def _rms(x):
    return x * lax.rsqrt(jnp.mean(x * x, axis=-1, keepdims=True) + EPS)


def _modnorm(x, g, sh, sc):
    return (_rms(x) * g) * (1.0 + sc) + sh


def _lat_mod_row(tile_rows):
    return lambda i: 1 + (i * tile_rows) // LAT_LEN


def _mod_spec(layer, row_fn, chunk):
    return pl.BlockSpec((None, None, 1, D_MODEL), lambda *g: (layer, row_fn(*g), 0, chunk))


MOD_TN = 1536


def _mod_kernel(ct_ref, w_ref, b_ref, o_ref):
    ct = ct_ref[...]
    act = ct * jax.nn.sigmoid(ct)
    w = w_ref[...]
    for m in range(3):
        o_ref[m:m + 1, :] = jnp.sum(w * act[:, m:m + 1], axis=0, keepdims=True) + b_ref[...]
    o_ref[3:8, :] = jnp.zeros((5, MOD_TN), F32)


def _modulation(c, c_ctx, w_ada, b_ada):
    cond = jnp.concatenate([c_ctx[None, :], c, jnp.zeros((5, D_MODEL), F32)], axis=0)
    mod = pl.pallas_call(
        _mod_kernel,
        grid=(DEPTH, 6 * D_MODEL // MOD_TN),
        in_specs=[
            pl.BlockSpec((D_MODEL, 8), lambda l, j: (0, 0)),
            pl.BlockSpec((None, D_MODEL, MOD_TN), lambda l, j: (l, 0, j)),
            pl.BlockSpec((None, 1, MOD_TN), lambda l, j: (l, 0, j)),
        ],
        out_specs=pl.BlockSpec((None, 8, MOD_TN), lambda l, j: (l, 0, j)),
        out_shape=jax.ShapeDtypeStruct((DEPTH, 8, 6 * D_MODEL), F32),
        compiler_params=_params(2),
        name="modulation",
    )(cond.T, w_ada, b_ada.reshape(DEPTH, 1, 6 * D_MODEL))
    return mod.reshape(DEPTH, 8, 1, 6 * D_MODEL)


NORM_TM = 512


def _prenorm_kernel(x_ref, g_ref, sh_ref, sc_ref, h_ref):
    h_ref[...] = _modnorm(x_ref[...], g_ref[...], sh_ref[...], sc_ref[...]).astype(BF16)


def _prenorm(x, mod, norm_g, layer, row_fn):
    n = x.shape[0]
    return pl.pallas_call(
        _prenorm_kernel,
        grid=(n // NORM_TM,),
        in_specs=[
            pl.BlockSpec((NORM_TM, D_MODEL), lambda i: (i, 0)),
            pl.BlockSpec((None, 1, D_MODEL), lambda i: (layer, 0, 0)),
            _mod_spec(layer, row_fn, MOD_SH1),
            _mod_spec(layer, row_fn, MOD_SC1),
        ],
        out_specs=pl.BlockSpec((NORM_TM, D_MODEL), lambda i: (i, 0)),
        out_shape=jax.ShapeDtypeStruct((n, D_MODEL), BF16),
        compiler_params=_params(1),
        name="prenorm",
    )(x, norm_g, mod, mod)


PROJ_CH = 1024


def _rope_tables():
    t = np.arange(LAT_LEN)
    lane = np.arange(LANES) % HEAD_DIM
    pos = np.where(lane < HEAD_DIM // 2, (t // GRID_W)[:, None], (t % GRID_W)[:, None]).astype(np.float64)
    freq = ROPE_THETA ** (-(lane % 16).astype(np.float64) / 16.0)
    ang = pos * freq[None, :]
    sign = np.where((lane & 16) == 0, -1.0, 1.0)[None, :]
    return np.cos(ang).astype(np.float32), (np.sin(ang) * sign).astype(np.float32)


def _head_norm_rope(y, gain, cos, sin):
    r = lax.broadcasted_iota(jnp.int32, (PROJ_TN, PROJ_TN), 0) // HEAD_DIM
    c = lax.broadcasted_iota(jnp.int32, (PROJ_TN, PROJ_TN), 1) // HEAD_DIM
    seg = jnp.where(r == c, 1.0, 0.0).astype(BF16)
    hi, lo = _split2(y * y)
    ss = _bdot(hi, seg) + _bdot(lo, seg)
    yn = y * lax.rsqrt(ss * (1.0 / HEAD_DIM) + EPS) * gain
    if cos is None:
        return yn
    lane = lax.broadcasted_iota(jnp.int32, yn.shape, 1)
    partner = jnp.where((lane & 16) == 0, pltpu.roll(yn, PROJ_TN - 16, 1), pltpu.roll(yn, 16, 1))
    cos2 = jnp.concatenate([cos, cos], axis=1)
    sin2 = jnp.concatenate([sin, sin], axis=1)
    return yn * cos2 + partner * sin2


def _proj_kernel(perm_ref, hc_ref, hl_ref, w_ref, gain_ref, apply_ref, cos_ref, sin_ref, o_ref, wb_ref):
    del perm_ref
    j = pl.program_id(0)
    wb_ref[...] = w_ref[...].astype(BF16)
    chunks = [(hc_ref, k * PROJ_CH, k * PROJ_CH, False) for k in range(T_CTX // PROJ_CH)]
    chunks += [(hl_ref, k * PROJ_CH, T_CTX + k * PROJ_CH, True) for k in range(T_LAT // PROJ_CH)]

    @pl.when(j >= 3)
    def _():
        for h_ref, r0, o0, _ in chunks:
            o_ref[o0:o0 + PROJ_CH, :] = _bdot(h_ref[r0:r0 + PROJ_CH, :], wb_ref[...])

    @pl.when(j < 3)
    def _():
        for h_ref, r0, o0, is_lat in chunks:
            y = _bdot(h_ref[r0:r0 + PROJ_CH, :], wb_ref[...])
            cos = cos_ref[...] if is_lat else None
            sin = sin_ref[...] if is_lat else None
            yn = _head_norm_rope(y, gain_ref[...], cos, sin)
            o_ref[o0:o0 + PROJ_CH, :] = jnp.where(apply_ref[...] > 0.0, yn, y)


def _projection(h_ctx, h_lat, w_in, q_norm, k_norm, layer):
    ones = jnp.ones((2 * HEAD_DIM,), F32)
    gain = jnp.stack([jnp.tile(q_norm[layer], 4), jnp.tile(q_norm[layer], 4),
                      jnp.concatenate([jnp.tile(k_norm[layer], 2), ones])])[:, None, :]
    apply = np.ones((3, 1, PROJ_TN), np.float32)
    apply[2, 0, 2 * HEAD_DIM:] = 0.0
    cos, sin = _rope_tables()
    sel = lambda j, p: (jnp.minimum(j, 2), 0, 0)
    grid_spec = pltpu.PrefetchScalarGridSpec(
        num_scalar_prefetch=1,
        grid=(N_PROJ_TILES,),
        in_specs=[
            pl.BlockSpec((T_CTX, D_MODEL), lambda j, p: (0, 0)),
            pl.BlockSpec((T_LAT, D_MODEL), lambda j, p: (0, 0)),
            pl.BlockSpec((None, D_MODEL, PROJ_TN), lambda j, p: (layer, 0, j)),
            pl.BlockSpec((None, 1, PROJ_TN), sel),
            pl.BlockSpec((None, 1, PROJ_TN), sel),
            pl.BlockSpec((LAT_LEN, LANES), lambda j, p: (0, 0)),
            pl.BlockSpec((LAT_LEN, LANES), lambda j, p: (0, 0)),
        ],
        out_specs=pl.BlockSpec((T_ALL, PROJ_TN), lambda j, p: (0, p[j])),
        scratch_shapes=[pltpu.VMEM((D_MODEL, PROJ_TN), BF16)],
    )
    return pl.pallas_call(
        _proj_kernel,
        grid_spec=grid_spec,
        out_shape=jax.ShapeDtypeStruct((T_ALL, IN_DIM), F32),
        compiler_params=_params(1),
        name="projection",
    )(jnp.asarray(PROJ_PERM), h_ctx, h_lat, w_in, gain, jnp.asarray(apply), jnp.asarray(cos), jnp.asarray(sin))


COL_QA, COL_QB, COL_KB, COL_VB, COL_KA, COL_VA = 0, 4, 8, 12, 32, 33


def _lane_is_low(shape):
    return lax.broadcasted_iota(jnp.int32, shape, 1) < HEAD_DIM


def _pair_halves(x):
    low = _lane_is_low(x.shape)
    xb = x.astype(BF16)
    zero = jnp.zeros_like(xb)
    return jnp.where(low, xb, zero), jnp.where(low, zero, xb)


def _attend(q, keys, values, biases):
    scores = []
    for k, b in zip(keys, biases):
        s = _bdot_nt(q, k) * ATT_SCALE
        scores.append(s if b is None else s + b)
    m = scores[0].max(axis=-1, keepdims=True)
    for s in scores[1:]:
        m = jnp.maximum(m, s.max(axis=-1, keepdims=True))
    acc = None
    den = None
    for s, v in zip(scores, values):
        e = jnp.exp(s - m)
        d = e.sum(axis=-1, keepdims=True)
        o = _bdot(e.astype(BF16), v)
        acc = o if acc is None else acc + o
        den = d if den is None else den + d
    return acc / den


def _gqa_variants(x):
    lo, hi = _pair_halves(x)
    sw_lo, sw_hi = _pair_halves(pltpu.roll(x, HEAD_DIM, 1))
    return {(0, 0): lo, (1, 1): hi, (1, 0): sw_lo, (0, 1): sw_hi}


def _gqa_attention(q_ref, key_blocks, value_blocks, o_ref):
    kvar = [_gqa_variants(k) for k in key_blocks]
    vvar = [_gqa_variants(v) for v in value_blocks]
    for pair in range(N_HEADS_A // 2):
        q = q_ref[:, pair * LANES:(pair + 1) * LANES].astype(BF16)
        kvh = (2 * pair) // (N_HEADS_A // N_KV_A)
        out = None
        for half in range(2):
            o = _attend(q, [kv[(kvh, half)] for kv in kvar], [vv[(kvh, half)] for vv in vvar],
                        [None] * len(kvar))
            out = o if out is None else out + o
        o_ref[:, pair * LANES:(pair + 1) * LANES] = out


def _ctx_attn_kernel(qa_ref, qb_ref, kb_ref, vb_ref, kava_ref, y_ref, nak_ref, nav_ref, nbk_ref, nbv_ref):
    ka = kava_ref[:, :LANES]
    va = kava_ref[:, LANES:]
    nak_ref[...] = ka
    nav_ref[...] = va
    nbk_ref[...] = kb_ref[...]
    nbv_ref[...] = vb_ref[...]
    _gqa_attention(qa_ref, [ka], [va], y_ref.at[:, :N_HEADS_A * HEAD_DIM])
    for pair in range(N_HEADS_B // 2):
        cols = slice(pair * LANES, (pair + 1) * LANES)
        q = qb_ref[:, cols].astype(BF16)
        k_lo, k_hi = _pair_halves(kb_ref[:, cols])
        v_lo, v_hi = _pair_halves(vb_ref[:, cols])
        out = _attend(q, [k_lo], [v_lo], [None]) + _attend(q, [k_hi], [v_hi], [None])
        y_ref[:, N_HEADS_A * HEAD_DIM + pair * LANES:N_HEADS_A * HEAD_DIM + (pair + 1) * LANES] = out


def _ctx_attention(y):
    wide = 4 * LANES
    row = lambda b: (b, 0)
    f32 = lambda *s: jax.ShapeDtypeStruct(s, F32)
    return pl.pallas_call(
        _ctx_attn_kernel,
        grid=(N_CTX_SETS,),
        in_specs=[
            pl.BlockSpec((CTX_LEN, wide), lambda b: (b, 0)),
            pl.BlockSpec((CTX_LEN, wide), lambda b: (b, 1)),
            pl.BlockSpec((CTX_LEN, wide), lambda b: (b, 2)),
            pl.BlockSpec((CTX_LEN, wide), lambda b: (b, 3)),
            pl.BlockSpec((CTX_LEN, 2 * LANES), lambda b: (b, COL_KA // 2)),
        ],
        out_specs=[
            pl.BlockSpec((CTX_LEN, 2 * wide), row),
            pl.BlockSpec((CTX_LEN, LANES), row),
            pl.BlockSpec((CTX_LEN, LANES), row),
            pl.BlockSpec((CTX_LEN, wide), row),
            pl.BlockSpec((CTX_LEN, wide), row),
        ],
        out_shape=[f32(T_CTX, 2 * wide), f32(T_CTX, LANES), f32(T_CTX, LANES), f32(T_CTX, wide), f32(T_CTX, wide)],
        compiler_params=_params(1),
        name="ctx_attention",
    )(y, y, y, y, y)


LAT_TQ = 256
LAT_QT = LAT_LEN // LAT_TQ


def _lat_gqa_kernel(qa_ref, kava_ref, ck_ref, cv_ref, o_ref):
    _gqa_attention(qa_ref, [ck_ref[...], kava_ref[:, :LANES]], [cv_ref[...], kava_ref[:, LANES:]], o_ref)


def _lat_gqa_attention(y, cache_k, cache_v, layer):
    wide = 4 * LANES
    first = T_CTX // LAT_TQ
    cache = pl.BlockSpec((None, None, PAST_LEN, LANES), lambda b, t: (b, layer, 0, 0))
    return pl.pallas_call(
        _lat_gqa_kernel,
        grid=(N_LAT_SETS, LAT_QT),
        in_specs=[
            pl.BlockSpec((LAT_TQ, wide), lambda b, t: (first + b * LAT_QT + t, 0)),
            pl.BlockSpec((LAT_LEN, 2 * LANES), lambda b, t: (T_CTX // LAT_LEN + b, COL_KA // 2)),
            cache, cache,
        ],
        out_specs=pl.BlockSpec((LAT_TQ, wide), lambda b, t: (b * LAT_QT + t, 0)),
        out_shape=jax.ShapeDtypeStruct((T_LAT, wide), F32),
        compiler_params=_params(2),
        name="lat_gqa_attention",
    )(y, y, cache_k.reshape(N_LAT_SETS, DEPTH, PAST_LEN, LANES), cache_v.reshape(N_LAT_SETS, DEPTH, PAST_LEN, LANES))


N_DR = 2 * WIN_ROWS - 1
N_DC = 2 * WIN_COLS - 1
ROWS_PER_TQ = LAT_TQ // GRID_W


def _window_mask():
    r = np.arange(GRID_ROWS)
    row_start = np.clip(r - WIN_ROWS // 2, 0, GRID_ROWS - WIN_ROWS)
    in_rows = (r[None, :] >= row_start[:, None]) & (r[None, :] < row_start[:, None] + WIN_ROWS)
    cq = np.arange(GRID_W)
    col_start = np.clip(cq - WIN_COLS // 2, 0, GRID_W - WIN_COLS)
    in_cols = (cq[None, :] >= col_start[:, None]) & (cq[None, :] < col_start[:, None] + WIN_COLS)
    valid = in_rows[:, None, :, None] & in_cols[None, :, None, :]
    return np.where(valid, 0.0, NEG).astype(np.float32).reshape(LAT_LEN, LAT_LEN)


def _na_kernel(rpb_ref, q_ref, k_ref, v_ref, ck_ref, cv_ref, mask_ref, o_ref, toe_ref, bias_ref, *, layer):
    hp, qt, b = pl.program_id(0), pl.program_id(1), pl.program_id(2)

    @pl.when((qt == 0) & (b == 0))
    def _():
        qc = lax.broadcasted_iota(jnp.int32, (GRID_W, LANES), 0)
        lane = lax.broadcasted_iota(jnp.int32, (GRID_W, LANES), 1)
        dc = jnp.clip((lane & (GRID_W - 1)) - qc, -(WIN_COLS - 1), WIN_COLS - 1) + (WIN_COLS - 1)
        low = lane < GRID_W
        for hh in range(2):
            base = ((layer * N_HEADS_B + 2 * hp + hh) * N_DR) * N_DC
            for dd in range(N_DR + 1):
                d_lo, d_hi = dd - 1, dd
                acc = jnp.zeros((GRID_W, LANES), F32)
                for c in range(N_DC):
                    s_lo = rpb_ref[base + d_lo * N_DC + c] if d_lo >= 0 else 0.0
                    s_hi = rpb_ref[base + d_hi * N_DC + c] if d_hi < N_DR else 0.0
                    acc = jnp.where(dc == c, jnp.where(low, s_lo, s_hi), acc)
                toe_ref[hh, dd] = acc

    q = q_ref[...].astype(BF16)
    k_halves = _pair_halves(k_ref[...])
    v_halves = _pair_halves(v_ref[...])
    ck_halves = _pair_halves(ck_ref[...])
    cv_halves = _pair_halves(cv_ref[...])
    out = None
    for hh in range(2):
        for rr in range(ROWS_PER_TQ):
            for kp in range(GRID_ROWS // 2):
                d = 2 * kp - (qt * ROWS_PER_TQ + rr) + (WIN_ROWS - 1)
                dd = jnp.clip(d, -1, N_DR - 1) + 1
                bias_ref[rr * GRID_W:(rr + 1) * GRID_W, kp * LANES:(kp + 1) * LANES] = toe_ref[hh, dd]
        bias = bias_ref[...] + mask_ref[...]
        o = _attend(q, [k_halves[hh], ck_halves[hh]], [v_halves[hh], cv_halves[hh]], [bias, None])
        out = o if out is None else out + o
    o_ref[...] = out


def _lat_na_attention(y, cache_k, cache_v, rpb, layer):
    first = T_CTX // LAT_TQ
    kv_row = lambda hp, t, b: T_CTX // LAT_LEN + b
    grid_spec = pltpu.PrefetchScalarGridSpec(
        num_scalar_prefetch=1,
        grid=(N_HEADS_B // 2, LAT_QT, N_LAT_SETS),
        in_specs=[
            pl.BlockSpec((LAT_TQ, LANES), lambda hp, t, b, r: (first + b * LAT_QT + t, COL_QB + hp)),
            pl.BlockSpec((LAT_LEN, LANES), lambda hp, t, b, r: (kv_row(hp, t, b), COL_KB + hp)),
            pl.BlockSpec((LAT_LEN, LANES), lambda hp, t, b, r: (kv_row(hp, t, b), COL_VB + hp)),
            pl.BlockSpec((None, None, PAST_LEN, LANES), lambda hp, t, b, r: (b, layer, 0, hp)),
            pl.BlockSpec((None, None, PAST_LEN, LANES), lambda hp, t, b, r: (b, layer, 0, hp)),
            pl.BlockSpec((LAT_TQ, LAT_LEN), lambda hp, t, b, r: (t, 0)),
        ],
        out_specs=pl.BlockSpec((LAT_TQ, LANES), lambda hp, t, b, r: (b * LAT_QT + t, hp)),
        scratch_shapes=[pltpu.VMEM((2, N_DR + 1, GRID_W, LANES), F32), pltpu.VMEM((LAT_TQ, LAT_LEN), F32)],
    )
    wide = N_HEADS_B * HEAD_DIM
    return pl.pallas_call(
        functools.partial(_na_kernel, layer=layer),
        grid_spec=grid_spec,
        out_shape=jax.ShapeDtypeStruct((T_LAT, wide), F32),
        compiler_params=_params(3),
        name="lat_na_attention",
    )(rpb.reshape(-1), y, y, y, cache_k.reshape(N_LAT_SETS, DEPTH, PAST_LEN, wide),
      cache_v.reshape(N_LAT_SETS, DEPTH, PAST_LEN, wide), jnp.asarray(_window_mask()))


MERGE_TM = 512
N_MERGE_CTX = T_CTX // MERGE_TM


def _merge_kernel(yc_ref, yla_ref, ylb_ref, g_ref, xc_ref, xl_ref, gt1_ref, sh2_ref, sc2_ref, nf_ref,
                  wa_ref, wb_ref, wo_ref, wr_ref, x1_ref, h2_ref, lg_ref, wab, wbb, wob, wrh, wrl):
    i = pl.program_id(0)

    @pl.when(i == 0)
    def _():
        wab[...] = wa_ref[...].astype(BF16)
        wbb[...] = wb_ref[...].astype(BF16)
        wob[...] = wo_ref[...].astype(BF16)
        hi, lo = _split2(wr_ref[...])
        wrh[...] = hi
        wrl[...] = lo

    is_ctx = i < N_MERGE_CTX
    half = N_HEADS_A * HEAD_DIM
    ya = jnp.where(is_ctx, yc_ref[:, :half], yla_ref[...]).astype(BF16)
    yb = jnp.where(is_ctx, yc_ref[:, half:], ylb_ref[...]).astype(BF16)
    x = jnp.where(is_ctx, xc_ref[...], xl_ref[...])
    za = _bdot(ya, wab[...])
    zb = _bdot(yb, wbb[...])
    m = jax.nn.sigmoid(g_ref[:, :D_MODEL]) * za + jax.nn.sigmoid(g_ref[:, D_MODEL:]) * zb
    x1 = x + gt1_ref[...] * _bdot(m.astype(BF16), wob[...])
    x1_ref[...] = x1
    h2 = _modnorm(x1, nf_ref[...], sh2_ref[...], sc2_ref[...])
    h2_ref[...] = h2.astype(BF16)
    hh, hl = _split2(h2)
    lg_ref[...] = _bdot(hh, wrh[...]) + _bdot(hh, wrl[...]) + _bdot(hl, wrh[...])


def _merge(y, yab_ctx, ya_lat, yb_lat, x_ctx, x_lat, mod, norm_ffn, w_ba, w_bb, w_out, w_router_pad, layer):
    row_fn = lambda i: jnp.where(i < N_MERGE_CTX, 0, 1 + ((i - N_MERGE_CTX) * MERGE_TM) // LAT_LEN)
    ctx_row = lambda i: (jnp.minimum(i, N_MERGE_CTX - 1), 0)
    lat_row = lambda i: (jnp.maximum(i - N_MERGE_CTX, 0), 0)
    half = N_HEADS_A * HEAD_DIM
    weight = lambda k: pl.BlockSpec((None, k, D_MODEL), lambda i: (layer, 0, 0))
    return pl.pallas_call(
        _merge_kernel,
        grid=(T_ALL // MERGE_TM,),
        in_specs=[
            pl.BlockSpec((MERGE_TM, 2 * half), ctx_row),
            pl.BlockSpec((MERGE_TM, half), lat_row),
            pl.BlockSpec((MERGE_TM, half), lat_row),
            pl.BlockSpec((MERGE_TM, 2 * D_MODEL), lambda i: (i, 1)),
            pl.BlockSpec((MERGE_TM, D_MODEL), ctx_row),
            pl.BlockSpec((MERGE_TM, D_MODEL), lat_row),
            _mod_spec(layer, row_fn, MOD_GT1),
            _mod_spec(layer, row_fn, MOD_SH2),
            _mod_spec(layer, row_fn, MOD_SC2),
            pl.BlockSpec((None, 1, D_MODEL), lambda i: (layer, 0, 0)),
            weight(half), weight(half), weight(D_MODEL),
            pl.BlockSpec((None, D_MODEL, LANES), lambda i: (layer, 0, 0)),
        ],
        out_specs=[
            pl.BlockSpec((MERGE_TM, D_MODEL), lambda i: (i, 0)),
            pl.BlockSpec((MERGE_TM, D_MODEL), lambda i: (i, 0)),
            pl.BlockSpec((MERGE_TM, LANES), lambda i: (i, 0)),
        ],
        out_shape=[jax.ShapeDtypeStruct((T_ALL, D_MODEL), F32), jax.ShapeDtypeStruct((T_ALL, D_MODEL), BF16),
                   jax.ShapeDtypeStruct((T_ALL, LANES), F32)],
        scratch_shapes=[pltpu.VMEM((half, D_MODEL), BF16), pltpu.VMEM((half, D_MODEL), BF16),
                        pltpu.VMEM((D_MODEL, D_MODEL), BF16), pltpu.VMEM((D_MODEL, LANES), BF16),
                        pltpu.VMEM((D_MODEL, LANES), BF16)],
        compiler_params=_params(1),
        name="merge",
    )(yab_ctx, ya_lat, yb_lat, y, x_ctx, x_lat, mod, mod, mod, norm_ffn, w_ba, w_bb, w_out, w_router_pad)


RANK_CH = 256
GATHER_M = 512


def _route_kernel(lg_ref, h_ref, xg_ref, g_ref, rc_ref, p_ref, rt_ref, *, n, cap):
    lane = lax.broadcasted_iota(jnp.int32, (n, LANES), 1)
    lg = jnp.where(lane < N_EXPERTS, lg_ref[...], -jnp.inf)
    ex = jnp.exp(lg - lg.max(axis=-1, keepdims=True))
    aff = ex / ex.sum(axis=-1, keepdims=True)
    aff_t = aff.T
    rt_ref[...] = jnp.full((LANES, n), float(cap), F32)
    t_row = lax.broadcasted_iota(jnp.int32, (RANK_CH, n), 1)
    t_col = lax.broadcasted_iota(jnp.int32, (RANK_CH, n), 0)
    slot = lax.broadcasted_iota(jnp.int32, (cap, n), 0).astype(F32)
    for e in range(N_EXPERTS):
        a_row = aff_t[e:e + 1, :]
        rank = jnp.zeros((1, n), F32)
        for c in range(n // RANK_CH):
            a_col = aff[c * RANK_CH:(c + 1) * RANK_CH, e:e + 1]
            earlier = jnp.where(t_col + c * RANK_CH < t_row, 1.0, 0.0)
            beats = jnp.where(a_col > a_row, 1.0, 0.0) + jnp.where(a_col == a_row, earlier, 0.0)
            rank = rank + beats.sum(axis=0, keepdims=True)
        rt_ref[e:e + 1, :] = jnp.minimum(rank, float(cap))
        p_ref[e * cap:(e + 1) * cap, :] = jnp.where(rank == slot, 1.0, 0.0).astype(BF16)

    h = h_ref[...]
    a1, a2, a3 = _split3(aff)
    per = GATHER_M // cap
    for grp in range(N_EXPERTS * cap // GATHER_M):
        p = p_ref[grp * GATHER_M:(grp + 1) * GATHER_M, :]
        xg = _bdot(p, h).astype(BF16)
        gg = _bdot(p, a1) + _bdot(p, a2) + _bdot(p, a3)
        glane = lax.broadcasted_iota(jnp.int32, (cap, LANES), 1)
        for k in range(per):
            e = grp * per + k
            xg_ref[e] = xg[k * cap:(k + 1) * cap, :]
            ge = jnp.where(glane == e, gg[k * cap:(k + 1) * cap, :], 0.0).sum(axis=-1, keepdims=True)
            g_ref[e] = jnp.broadcast_to(ge, (cap, LANES))
    rc_ref[...] = rt_ref[...].T


def _route(logits, h2, n, cap, n_sets, first_block):
    return pl.pallas_call(
        functools.partial(_route_kernel, n=n, cap=cap),
        grid=(n_sets,),
        in_specs=[
            pl.BlockSpec((n, LANES), lambda s: (first_block + s, 0)),
            pl.BlockSpec((n, D_MODEL), lambda s: (first_block + s, 0)),
        ],
        out_specs=[
            pl.BlockSpec((N_EXPERTS, cap, D_MODEL), lambda s: (0, s, 0)),
            pl.BlockSpec((N_EXPERTS, cap, LANES), lambda s: (0, s, 0)),
            pl.BlockSpec((n, LANES), lambda s: (s, 0)),
        ],
        out_shape=[jax.ShapeDtypeStruct((N_EXPERTS, n_sets * cap, D_MODEL), BF16),
                   jax.ShapeDtypeStruct((N_EXPERTS, n_sets * cap, LANES), F32),
                   jax.ShapeDtypeStruct((n_sets * n, LANES), F32)],
        scratch_shapes=[pltpu.VMEM((N_EXPERTS * cap, n), BF16), pltpu.VMEM((LANES, n), F32)],
        compiler_params=_params(1),
        name=f"route_n{n}",
    )(logits, h2)


EXPERT_TF = 512
N_FF_TILES = EXPERT_FF // EXPERT_TF


def _expert_kernel(xc_ref, xl_ref, gc_ref, gl_ref, wg_ref, wu_ref, wd_ref, oc_ref, ol_ref):
    f = pl.program_id(1)
    wg = wg_ref[...].astype(BF16)
    wu = wu_ref[...].astype(BF16)
    wd = wd_ref[...].astype(BF16)
    for x_ref, g_ref, o_ref in ((xc_ref, gc_ref, oc_ref), (xl_ref, gl_ref, ol_ref)):
        x = x_ref[...]
        gate = _bdot(x, wg)
        hid = (gate * jax.nn.sigmoid(gate)) * _bdot(x, wu)
        part = _bdot(hid.astype(BF16), wd)

        @pl.when(f == 0)
        def _():
            o_ref[...] = part

        @pl.when(f > 0)
        def _():
            o_ref[...] += part

        @pl.when(f == N_FF_TILES - 1)
        def _():
            o_ref[...] = o_ref[...] * g_ref[:, :1]


def _experts(xg_ctx, xg_lat, g_ctx, g_lat, w_gate, w_up, w_down, layer):
    sc, sl = xg_ctx.shape[1], xg_lat.shape[1]
    slots = lambda s, w: pl.BlockSpec((None, s, w), lambda e, f: (e, 0, 0))
    return pl.pallas_call(
        _expert_kernel,
        grid=(N_EXPERTS, N_FF_TILES),
        in_specs=[
            slots(sc, D_MODEL), slots(sl, D_MODEL), slots(sc, LANES), slots(sl, LANES),
            pl.BlockSpec((None, None, D_MODEL, EXPERT_TF), lambda e, f: (layer, e, 0, f)),
            pl.BlockSpec((None, None, D_MODEL, EXPERT_TF), lambda e, f: (layer, e, 0, f)),
            pl.BlockSpec((None, None, EXPERT_TF, D_MODEL), lambda e, f: (layer, e, f, 0)),
        ],
        out_specs=[slots(sc, D_MODEL), slots(sl, D_MODEL)],
        out_shape=[jax.ShapeDtypeStruct((N_EXPERTS, sc, D_MODEL), F32),
                   jax.ShapeDtypeStruct((N_EXPERTS, sl, D_MODEL), F32)],
        compiler_params=_params(2),
        name="experts",
    )(xg_ctx, xg_lat, g_ctx, g_lat, w_gate, w_up, w_down)


COMB_TM = 256


def _combine_kernel(o_ref, rc_ref, x_ref, gt2_ref, ng_ref, *rest, cap, final):
    if final:
        y_ref, hi_ref, lo_ref = rest
    else:
        sh_ref, sc_ref, xn_ref, hn_ref, hi_ref, lo_ref = rest
    t = pl.program_id(1)
    slots = N_EXPERTS * cap

    @pl.when(t == 0)
    def _():
        hi, lo = _split2(o_ref[...].reshape(slots, D_MODEL))
        hi_ref[...] = hi
        lo_ref[...] = lo

    j = lax.broadcasted_iota(jnp.int32, (LANES, slots), 1)
    e = lax.broadcasted_iota(jnp.int32, (LANES, slots), 0)
    expand = jnp.where(j // cap == e, 1.0, 0.0).astype(BF16)
    rank = _bdot(rc_ref[...].astype(BF16), expand)
    slot = (lax.broadcasted_iota(jnp.int32, (1, slots), 1) % cap).astype(F32)
    pt = jnp.where(rank == slot, 1.0, 0.0).astype(BF16)
    ffn = _bdot(pt, hi_ref[...]) + _bdot(pt, lo_ref[...])
    x = x_ref[...] + gt2_ref[...] * ffn
    if final:
        y_ref[...] = _rms(x) * ng_ref[...]
    else:
        xn_ref[...] = x
        hn_ref[...] = _modnorm(x, ng_ref[...], sh_ref[...], sc_ref[...]).astype(BF16)


def _combine(out, rc, x1, mod, norm_g, layer, n, cap, n_sets, first_row_block, row_fn, final):
    tiles = n // COMB_TM
    rows = lambda s, t: (s * tiles + t, 0)
    in_specs = [
        pl.BlockSpec((N_EXPERTS, cap, D_MODEL), lambda s, t: (0, s, 0)),
        pl.BlockSpec((COMB_TM, LANES), rows),
        pl.BlockSpec((COMB_TM, D_MODEL), lambda s, t: (first_row_block + s * tiles + t, 0)),
        _mod_spec(layer, row_fn, MOD_GT2),
    ]
    args = [out, rc, x1, mod]
    if final:
        in_specs.append(pl.BlockSpec((1, D_MODEL), lambda s, t: (0, 0)))
        args.append(norm_g.reshape(1, D_MODEL))
        out_specs = pl.BlockSpec((COMB_TM, D_MODEL), rows)
        out_shape = jax.ShapeDtypeStruct((n_sets * n, D_MODEL), F32)
    else:
        in_specs += [pl.BlockSpec((None, 1, D_MODEL), lambda s, t: (layer + 1, 0, 0)),
                     _mod_spec(layer + 1, row_fn, MOD_SH1), _mod_spec(layer + 1, row_fn, MOD_SC1)]
        args += [norm_g, mod, mod]
        out_specs = [pl.BlockSpec((COMB_TM, D_MODEL), rows), pl.BlockSpec((COMB_TM, D_MODEL), rows)]
        out_shape = [jax.ShapeDtypeStruct((n_sets * n, D_MODEL), F32),
                     jax.ShapeDtypeStruct((n_sets * n, D_MODEL), BF16)]
    return pl.pallas_call(
        functools.partial(_combine_kernel, cap=cap, final=final),
        grid=(n_sets, tiles),
        in_specs=in_specs,
        out_specs=out_specs,
        out_shape=out_shape,
        scratch_shapes=[pltpu.VMEM((N_EXPERTS * cap, D_MODEL), BF16), pltpu.VMEM((N_EXPERTS * cap, D_MODEL), BF16)],
        compiler_params=_params(2),
        name=f"combine_n{n}",
    )(*args)


def kernel(x_prompt, x_sample, cache_attn_k, cache_attn_v, cache_na_k, cache_na_v, c, c_ctx, w_ada, b_ada,
           norm_mix, norm_ffn, w_in, q_norm, k_norm, rpb, w_branch_a, w_branch_b, w_out, w_router, w_gate,
           w_up, w_down, final_norm):
    x_ctx = x_prompt.reshape(T_CTX, D_MODEL)
    x_lat = x_sample.reshape(T_LAT, D_MODEL)
    mod = _modulation(c, c_ctx, w_ada, b_ada)
    norm_mix3 = norm_mix.reshape(DEPTH, 1, D_MODEL)
    norm_ffn3 = norm_ffn.reshape(DEPTH, 1, D_MODEL)
    w_router_pad = jnp.pad(w_router, ((0, 0), (0, 0), (0, LANES - N_EXPERTS)))
    ctx_row = lambda *g: 0
    lat_comb_row = lambda s, t: 1 + s

    h_ctx = _prenorm(x_ctx, mod, norm_mix3, 0, ctx_row)
    h_lat = _prenorm(x_lat, mod, norm_mix3, 0, _lat_mod_row(NORM_TM))
    caches = [[], [], [], []]
    for layer in range(DEPTH):
        y = _projection(h_ctx, h_lat, w_in, q_norm, k_norm, layer)
        yab_ctx, nak, nav, nbk, nbv = _ctx_attention(y)
        for store, new in zip(caches, (nak, nav, nbk, nbv)):
            store.append(new)
        ya_lat = _lat_gqa_attention(y, cache_attn_k, cache_attn_v, layer)
        yb_lat = _lat_na_attention(y, cache_na_k, cache_na_v, rpb, layer)
        x1, h2, logits = _merge(y, yab_ctx, ya_lat, yb_lat, x_ctx, x_lat, mod, norm_ffn3,
                                w_branch_a, w_branch_b, w_out, w_router_pad, layer)
        xg_ctx, g_ctx, rc_ctx = _route(logits, h2, CTX_LEN, CAP_CTX, N_CTX_SETS, 0)
        xg_lat, g_lat, rc_lat = _route(logits, h2, LAT_LEN, CAP_LAT, N_LAT_SETS, T_CTX // LAT_LEN)
        out_ctx, out_lat = _experts(xg_ctx, xg_lat, g_ctx, g_lat, w_gate, w_up, w_down, layer)
        final = layer == DEPTH - 1
        norm_g = final_norm if final else norm_mix3
        res_ctx = _combine(out_ctx, rc_ctx, x1, mod, norm_g, layer, CTX_LEN, CAP_CTX, N_CTX_SETS, 0, ctx_row, final)
        res_lat = _combine(out_lat, rc_lat, x1, mod, norm_g, layer, LAT_LEN, CAP_LAT, N_LAT_SETS,
                           T_CTX // COMB_TM, lat_comb_row, final)
        if final:
            y_ctx, y_lat = res_ctx, res_lat
        else:
            (x_ctx, h_ctx), (x_lat, h_lat) = res_ctx, res_lat

    def stack(parts, heads):
        return jnp.stack([p.reshape(N_CTX_SETS, CTX_LEN, heads, HEAD_DIM) for p in parts], axis=1)

    return (y_ctx.reshape(N_CTX_SETS, CTX_LEN, D_MODEL), y_lat.reshape(N_LAT_SETS, LAT_LEN, D_MODEL),
            stack(caches[0], N_KV_A), stack(caches[1], N_KV_A), stack(caches[2], N_HEADS_B), stack(caches[3], N_HEADS_B))
```

```python
import functools

import numpy as np
import jax
import jax.numpy as jnp
from jax import lax
from jax.experimental import pallas as pl
from jax.experimental.pallas import tpu as pltpu

F32 = jnp.float32
BF16 = jnp.bfloat16

D_MODEL = 1024
N_CTX_SETS, CTX_LEN = 16, 256
N_LAT_SETS, LAT_LEN = 2, 1024
T_CTX = N_CTX_SETS * CTX_LEN
T_LAT = N_LAT_SETS * LAT_LEN
T_ALL = T_CTX + T_LAT
DEPTH = 2
PAST_LEN = 512
GRID_W = 64
GRID_ROWS = LAT_LEN // GRID_W
HEAD_DIM = 64
N_HEADS_A, N_KV_A, N_HEADS_B = 8, 2, 8
WIN_ROWS, WIN_COLS = 8, 16
N_EXPERTS = 16
EXPERT_FF = 2048
CAP_CTX = 2 * CTX_LEN // N_EXPERTS
CAP_LAT = 2 * LAT_LEN // N_EXPERTS
ROPE_THETA = 10000.0
EPS = 1e-6
NEG = -1e30
IN_DIM = 4352
ATT_SCALE = HEAD_DIM ** -0.5

LANES = 128
VMEM_LIMIT = 56 * 1024 * 1024

PROJ_TN = 256
N_PROJ_TILES = IN_DIM // PROJ_TN
PROJ_PERM = np.array([0, 1, 16, 2, 3, 4, 5, 6, 7, 8, 9, 10, 11, 12, 13, 14, 15], np.int32)
MOD_SH1, MOD_SC1, MOD_GT1, MOD_SH2, MOD_SC2, MOD_GT2 = range(6)


def _params(n_grid_dims, vmem=VMEM_LIMIT):
    return pltpu.CompilerParams(dimension_semantics=("arbitrary",) * n_grid_dims, vmem_limit_bytes=vmem)


def _bdot(a, b):
    return jnp.dot(a, b, preferred_element_type=F32)


def _bdot_nt(a, b):
    return lax.dot_general(a, b, (((1,), (1,)), ((), ())), preferred_element_type=F32)


def _split2(x):
    hi = x.astype(BF16)
    lo = (x - hi.astype(F32)).astype(BF16)
    return hi, lo


def _split3(x):
    hi = x.astype(BF16)
    r = x - hi.astype(F32)
    mid = r.astype(BF16)
    lo = (r - mid.astype(F32)).astype(BF16)
    return hi, mid, lo


def _rms(x):
    return x * lax.rsqrt(jnp.mean(x * x, axis=-1, keepdims=True) + EPS)


def _modnorm(x, g, sh, sc):
    return (_rms(x) * g) * (1.0 + sc) + sh


def _lat_mod_row(tile_rows):
    return lambda i: 1 + (i * tile_rows) // LAT_LEN


def _mod_spec(layer, row_fn, chunk):
    return pl.BlockSpec((None, None, 1, D_MODEL), lambda *g: (layer, row_fn(*g), 0, chunk))


MOD_TN = 1536


def _mod_kernel(ct_ref, w_ref, b_ref, o_ref):
    ct = ct_ref[...]
    act = ct * jax.nn.sigmoid(ct)
    w = w_ref[...]
    for m in range(3):
        o_ref[m:m + 1, :] = jnp.sum(w * act[:, m:m + 1], axis=0, keepdims=True) + b_ref[...]
    o_ref[3:8, :] = jnp.zeros((5, MOD_TN), F32)


def _modulation(c, c_ctx, w_ada, b_ada):
    cond = jnp.concatenate([c_ctx[None, :], c, jnp.zeros((5, D_MODEL), F32)], axis=0)
    mod = pl.pallas_call(
        _mod_kernel,
        grid=(DEPTH, 6 * D_MODEL // MOD_TN),
        in_specs=[
            pl.BlockSpec((D_MODEL, 8), lambda l, j: (0, 0)),
            pl.BlockSpec((None, D_MODEL, MOD_TN), lambda l, j: (l, 0, j)),
            pl.BlockSpec((None, 1, MOD_TN), lambda l, j: (l, 0, j)),
        ],
        out_specs=pl.BlockSpec((None, 8, MOD_TN), lambda l, j: (l, 0, j)),
        out_shape=jax.ShapeDtypeStruct((DEPTH, 8, 6 * D_MODEL), F32),
        compiler_params=_params(2),
        name="modulation",
    )(cond.T, w_ada, b_ada.reshape(DEPTH, 1, 6 * D_MODEL))
    return mod.reshape(DEPTH, 8, 1, 6 * D_MODEL)


NORM_TM = 512


def _prenorm_kernel(x_ref, g_ref, sh_ref, sc_ref, h_ref):
    h_ref[...] = _modnorm(x_ref[...], g_ref[...], sh_ref[...], sc_ref[...]).astype(BF16)


def _prenorm(x, mod, norm_g, layer, row_fn):
    n = x.shape[0]
    return pl.pallas_call(
        _prenorm_kernel,
        grid=(n // NORM_TM,),
        in_specs=[
            pl.BlockSpec((NORM_TM, D_MODEL), lambda i: (i, 0)),
            pl.BlockSpec((None, 1, D_MODEL), lambda i: (layer, 0, 0)),
            _mod_spec(layer, row_fn, MOD_SH1),
            _mod_spec(layer, row_fn, MOD_SC1),
        ],
        out_specs=pl.BlockSpec((NORM_TM, D_MODEL), lambda i: (i, 0)),
        out_shape=jax.ShapeDtypeStruct((n, D_MODEL), BF16),
        compiler_params=_params(1),
        name="prenorm",
    )(x, norm_g, mod, mod)


PROJ_CH = 1024


def _rope_tables():
    t = np.arange(LAT_LEN)
    lane = np.arange(LANES) % HEAD_DIM
    pos = np.where(lane < HEAD_DIM // 2, (t // GRID_W)[:, None], (t % GRID_W)[:, None]).astype(np.float64)
    freq = ROPE_THETA ** (-(lane % 16).astype(np.float64) / 16.0)
    ang = pos * freq[None, :]
    sign = np.where((lane & 16) == 0, -1.0, 1.0)[None, :]
    return np.cos(ang).astype(np.float32), (np.sin(ang) * sign).astype(np.float32)


def _head_norm_rope(y, gain, cos, sin):
    w = y.shape[1]
    r = lax.broadcasted_iota(jnp.int32, (w, w), 0) // HEAD_DIM
    c = lax.broadcasted_iota(jnp.int32, (w, w), 1) // HEAD_DIM
    seg = jnp.where(r == c, 1.0, 0.0).astype(BF16)
    hi, lo = _split2(y * y)
    ss = _bdot(hi, seg) + _bdot(lo, seg)
    yn = y * lax.rsqrt(ss * (1.0 / HEAD_DIM) + EPS) * gain
    if cos is None:
        return yn
    lane = lax.broadcasted_iota(jnp.int32, yn.shape, 1)
    partner = jnp.where((lane & 16) == 0, pltpu.roll(yn, w - 16, 1), pltpu.roll(yn, 16, 1))
    if w > LANES:
        cos = jnp.concatenate([cos] * (w // LANES), axis=1)
        sin = jnp.concatenate([sin] * (w // LANES), axis=1)
    return yn * cos + partner * sin


def _proj_kernel(perm_ref, hc_ref, hl_ref, w_ref, gain_ref, cos_ref, sin_ref, o_ref, wb_ref):
    del perm_ref
    j = pl.program_id(0)
    wb_ref[...] = w_ref[...].astype(BF16)
    chunks = [(hc_ref, k * PROJ_CH, k * PROJ_CH, False) for k in range(T_CTX // PROJ_CH)]
    chunks += [(hl_ref, k * PROJ_CH, T_CTX + k * PROJ_CH, True) for k in range(T_LAT // PROJ_CH)]

    def tables(is_lat):
        return (cos_ref[...], sin_ref[...]) if is_lat else (None, None)

    @pl.when(j >= 3)
    def _():
        for h_ref, r0, o0, _ in chunks:
            o_ref[o0:o0 + PROJ_CH, :] = _bdot(h_ref[r0:r0 + PROJ_CH, :], wb_ref[...])

    @pl.when(j < 2)
    def _():
        for h_ref, r0, o0, is_lat in chunks:
            y = _bdot(h_ref[r0:r0 + PROJ_CH, :], wb_ref[...])
            o_ref[o0:o0 + PROJ_CH, :] = _head_norm_rope(y, gain_ref[...], *tables(is_lat))

    @pl.when(j == 2)
    def _():
        for h_ref, r0, o0, is_lat in chunks:
            y = _bdot(h_ref[r0:r0 + PROJ_CH, :], wb_ref[...])
            o_ref[o0:o0 + PROJ_CH, :LANES] = _head_norm_rope(y[:, :LANES], gain_ref[:, :LANES], *tables(is_lat))
            o_ref[o0:o0 + PROJ_CH, LANES:] = y[:, LANES:]


def _projection(h_ctx, h_lat, w_in, q_norm, k_norm, layer):
    ones = jnp.ones((2 * HEAD_DIM,), F32)
    gain = jnp.stack([jnp.tile(q_norm[layer], 4), jnp.tile(q_norm[layer], 4),
                      jnp.concatenate([jnp.tile(k_norm[layer], 2), ones])])[:, None, :]
    cos, sin = _rope_tables()
    grid_spec = pltpu.PrefetchScalarGridSpec(
        num_scalar_prefetch=1,
        grid=(N_PROJ_TILES,),
        in_specs=[
            pl.BlockSpec((T_CTX, D_MODEL), lambda j, p: (0, 0)),
            pl.BlockSpec((T_LAT, D_MODEL), lambda j, p: (0, 0)),
            pl.BlockSpec((None, D_MODEL, PROJ_TN), lambda j, p: (layer, 0, j)),
            pl.BlockSpec((None, 1, PROJ_TN), lambda j, p: (jnp.minimum(j, 2), 0, 0)),
            pl.BlockSpec((LAT_LEN, LANES), lambda j, p: (0, 0)),
            pl.BlockSpec((LAT_LEN, LANES), lambda j, p: (0, 0)),
        ],
        out_specs=pl.BlockSpec((T_ALL, PROJ_TN), lambda j, p: (0, p[j])),
        scratch_shapes=[pltpu.VMEM((D_MODEL, PROJ_TN), BF16)],
    )
    return pl.pallas_call(
        _proj_kernel,
        grid_spec=grid_spec,
        out_shape=jax.ShapeDtypeStruct((T_ALL, IN_DIM), F32),
        compiler_params=_params(1),
        name="projection",
    )(jnp.asarray(PROJ_PERM), h_ctx, h_lat, w_in, gain, jnp.asarray(cos), jnp.asarray(sin))


COL_QA, COL_QB, COL_KB, COL_VB, COL_KA, COL_VA = 0, 4, 8, 12, 32, 33


def _lane_is_low(shape):
    return lax.broadcasted_iota(jnp.int32, shape, 1) < HEAD_DIM


def _pair_halves(x):
    low = _lane_is_low(x.shape)
    xb = x.astype(BF16)
    zero = jnp.zeros_like(xb)
    return jnp.where(low, xb, zero), jnp.where(low, zero, xb)


def _attend(q, keys, values, biases):
    scores = []
    for k, b in zip(keys, biases):
        s = _bdot_nt(q, k) * ATT_SCALE
        scores.append(s if b is None else s + b)
    m = scores[0].max(axis=-1, keepdims=True)
    for s in scores[1:]:
        m = jnp.maximum(m, s.max(axis=-1, keepdims=True))
    acc = None
    den = None
    for s, v in zip(scores, values):
        e = jnp.exp(s - m)
        d = e.sum(axis=-1, keepdims=True)
        o = _bdot(e.astype(BF16), v)
        acc = o if acc is None else acc + o
        den = d if den is None else den + d
    return acc / den


def _gqa_variants(x):
    lo, hi = _pair_halves(x)
    sw_lo, sw_hi = _pair_halves(pltpu.roll(x, HEAD_DIM, 1))
    return {(0, 0): lo, (1, 1): hi, (1, 0): sw_lo, (0, 1): sw_hi}


def _gqa_attention(q_ref, key_blocks, value_blocks, o_ref):
    kvar = [_gqa_variants(k) for k in key_blocks]
    vvar = [_gqa_variants(v) for v in value_blocks]
    for pair in range(N_HEADS_A // 2):
        q = q_ref[:, pair * LANES:(pair + 1) * LANES].astype(BF16)
        kvh = (2 * pair) // (N_HEADS_A // N_KV_A)
        out = None
        for half in range(2):
            o = _attend(q, [kv[(kvh, half)] for kv in kvar], [vv[(kvh, half)] for vv in vvar],
                        [None] * len(kvar))
            out = o if out is None else out + o
        o_ref[:, pair * LANES:(pair + 1) * LANES] = out


def _ctx_attn_kernel(qa_ref, qb_ref, kb_ref, vb_ref, kava_ref, *refs, n_prev):
    prev = refs[:4 * n_prev]
    y_ref, nak_ref, nav_ref, nbk_ref, nbv_ref = refs[4 * n_prev:]
    ka = kava_ref[:, :LANES]
    va = kava_ref[:, LANES:]
    new = (ka, va, kb_ref[...], vb_ref[...])
    for c, (o_ref, val) in enumerate(zip((nak_ref, nav_ref, nbk_ref, nbv_ref), new)):
        if n_prev:
            for p in range(n_prev):
                o_ref[p] = prev[4 * p + c][...]
            o_ref[n_prev] = val
        else:
            o_ref[...] = val
    _gqa_attention(qa_ref, [ka], [va], y_ref.at[:, :N_HEADS_A * HEAD_DIM])
    for pair in range(N_HEADS_B // 2):
        cols = slice(pair * LANES, (pair + 1) * LANES)
        q = qb_ref[:, cols].astype(BF16)
        k_lo, k_hi = _pair_halves(kb_ref[:, cols])
        v_lo, v_hi = _pair_halves(vb_ref[:, cols])
        out = _attend(q, [k_lo], [v_lo], [None]) + _attend(q, [k_hi], [v_hi], [None])
        y_ref[:, N_HEADS_A * HEAD_DIM + pair * LANES:N_HEADS_A * HEAD_DIM + (pair + 1) * LANES] = out


def _ctx_attention(y, prev_caches):
    wide = 4 * LANES
    widths = (LANES, LANES, wide, wide)
    n_prev = len(prev_caches)
    row = lambda b: (b, 0)
    f32 = lambda *s: jax.ShapeDtypeStruct(s, F32)
    in_specs = [
        pl.BlockSpec((CTX_LEN, wide), lambda b: (b, 0)),
        pl.BlockSpec((CTX_LEN, wide), lambda b: (b, 1)),
        pl.BlockSpec((CTX_LEN, wide), lambda b: (b, 2)),
        pl.BlockSpec((CTX_LEN, wide), lambda b: (b, 3)),
        pl.BlockSpec((CTX_LEN, 2 * LANES), lambda b: (b, COL_KA // 2)),
    ]
    args = [y, y, y, y, y]
    for layer_caches in prev_caches:
        in_specs += [pl.BlockSpec((CTX_LEN, w), row) for w in widths]
        args += list(layer_caches)
    if n_prev:
        cache_specs = [pl.BlockSpec((None, n_prev + 1, CTX_LEN, w), lambda b: (b, 0, 0, 0)) for w in widths]
        cache_shapes = [f32(N_CTX_SETS, n_prev + 1, CTX_LEN, w) for w in widths]
    else:
        cache_specs = [pl.BlockSpec((CTX_LEN, w), row) for w in widths]
        cache_shapes = [f32(T_CTX, w) for w in widths]
    return pl.pallas_call(
        functools.partial(_ctx_attn_kernel, n_prev=n_prev),
        grid=(N_CTX_SETS,),
        in_specs=in_specs,
        out_specs=[pl.BlockSpec((CTX_LEN, 2 * wide), row)] + cache_specs,
        out_shape=[f32(T_CTX, 2 * wide)] + cache_shapes,
        compiler_params=_params(1),
        name="ctx_attention",
    )(*args)


LAT_TQ = 256
LAT_QT = LAT_LEN // LAT_TQ


def _lat_gqa_kernel(qa_ref, kava_ref, ck_ref, cv_ref, o_ref):
    _gqa_attention(qa_ref, [ck_ref[...], kava_ref[:, :LANES]], [cv_ref[...], kava_ref[:, LANES:]], o_ref)


def _lat_gqa_attention(y, cache_k, cache_v, layer):
    wide = 4 * LANES
    first = T_CTX // LAT_TQ
    cache = pl.BlockSpec((None, None, PAST_LEN, LANES), lambda b, t: (b, layer, 0, 0))
    return pl.pallas_call(
        _lat_gqa_kernel,
        grid=(N_LAT_SETS, LAT_QT),
        in_specs=[
            pl.BlockSpec((LAT_TQ, wide), lambda b, t: (first + b * LAT_QT + t, 0)),
            pl.BlockSpec((LAT_LEN, 2 * LANES), lambda b, t: (T_CTX // LAT_LEN + b, COL_KA // 2)),
            cache, cache,
        ],
        out_specs=pl.BlockSpec((LAT_TQ, wide), lambda b, t: (b * LAT_QT + t, 0)),
        out_shape=jax.ShapeDtypeStruct((T_LAT, wide), F32),
        compiler_params=_params(2),
        name="lat_gqa_attention",
    )(y, y, cache_k.reshape(N_LAT_SETS, DEPTH, PAST_LEN, LANES), cache_v.reshape(N_LAT_SETS, DEPTH, PAST_LEN, LANES))


N_DR = 2 * WIN_ROWS - 1
N_DC = 2 * WIN_COLS - 1
ROWS_PER_TQ = LAT_TQ // GRID_W


def _window_mask():
    r = np.arange(GRID_ROWS)
    row_start = np.clip(r - WIN_ROWS // 2, 0, GRID_ROWS - WIN_ROWS)
    in_rows = (r[None, :] >= row_start[:, None]) & (r[None, :] < row_start[:, None] + WIN_ROWS)
    cq = np.arange(GRID_W)
    col_start = np.clip(cq - WIN_COLS // 2, 0, GRID_W - WIN_COLS)
    in_cols = (cq[None, :] >= col_start[:, None]) & (cq[None, :] < col_start[:, None] + WIN_COLS)
    valid = in_rows[:, None, :, None] & in_cols[None, :, None, :]
    return np.where(valid, 0.0, NEG).astype(np.float32).reshape(LAT_LEN, LAT_LEN)


def _na_kernel(rpb_ref, q_ref, k_ref, v_ref, ck_ref, cv_ref, mask_ref, o_ref, toe_ref, bias_ref, *, layer):
    hp, qt, b = pl.program_id(0), pl.program_id(1), pl.program_id(2)

    @pl.when((qt == 0) & (b == 0))
    def _():
        qc = lax.broadcasted_iota(jnp.int32, (GRID_W, LANES), 0)
        lane = lax.broadcasted_iota(jnp.int32, (GRID_W, LANES), 1)
        dc = jnp.clip((lane & (GRID_W - 1)) - qc, -(WIN_COLS - 1), WIN_COLS - 1) + (WIN_COLS - 1)
        low = lane < GRID_W
        for hh in range(2):
            base = ((layer * N_HEADS_B + 2 * hp + hh) * N_DR) * N_DC
            for dd in range(N_DR + 1):
                d_lo, d_hi = dd - 1, dd
                acc = jnp.zeros((GRID_W, LANES), F32)
                for c in range(N_DC):
                    s_lo = rpb_ref[base + d_lo * N_DC + c] if d_lo >= 0 else 0.0
                    s_hi = rpb_ref[base + d_hi * N_DC + c] if d_hi < N_DR else 0.0
                    acc = jnp.where(dc == c, jnp.where(low, s_lo, s_hi), acc)
                toe_ref[hh, dd] = acc

    q = q_ref[...].astype(BF16)
    k_halves = _pair_halves(k_ref[...])
    v_halves = _pair_halves(v_ref[...])
    ck_halves = _pair_halves(ck_ref[...])
    cv_halves = _pair_halves(cv_ref[...])
    out = None
    for hh in range(2):
        for rr in range(ROWS_PER_TQ):
            for kp in range(GRID_ROWS // 2):
                d = 2 * kp - (qt * ROWS_PER_TQ + rr) + (WIN_ROWS - 1)
                dd = jnp.clip(d, -1, N_DR - 1) + 1
                bias_ref[rr * GRID_W:(rr + 1) * GRID_W, kp * LANES:(kp + 1) * LANES] = toe_ref[hh, dd]
        bias = bias_ref[...] + mask_ref[...]
        o = _attend(q, [k_halves[hh], ck_halves[hh]], [v_halves[hh], cv_halves[hh]], [bias, None])
        out = o if out is None else out + o
    o_ref[...] = out


def _lat_na_attention(y, cache_k, cache_v, rpb, layer):
    first = T_CTX // LAT_TQ
    kv_row = lambda hp, t, b: T_CTX // LAT_LEN + b
    grid_spec = pltpu.PrefetchScalarGridSpec(
        num_scalar_prefetch=1,
        grid=(N_HEADS_B // 2, LAT_QT, N_LAT_SETS),
        in_specs=[
            pl.BlockSpec((LAT_TQ, LANES), lambda hp, t, b, r: (first + b * LAT_QT + t, COL_QB + hp)),
            pl.BlockSpec((LAT_LEN, LANES), lambda hp, t, b, r: (kv_row(hp, t, b), COL_KB + hp)),
            pl.BlockSpec((LAT_LEN, LANES), lambda hp, t, b, r: (kv_row(hp, t, b), COL_VB + hp)),
            pl.BlockSpec((None, None, PAST_LEN, LANES), lambda hp, t, b, r: (b, layer, 0, hp)),
            pl.BlockSpec((None, None, PAST_LEN, LANES), lambda hp, t, b, r: (b, layer, 0, hp)),
            pl.BlockSpec((LAT_TQ, LAT_LEN), lambda hp, t, b, r: (t, 0)),
        ],
        out_specs=pl.BlockSpec((LAT_TQ, LANES), lambda hp, t, b, r: (b * LAT_QT + t, hp)),
        scratch_shapes=[pltpu.VMEM((2, N_DR + 1, GRID_W, LANES), F32), pltpu.VMEM((LAT_TQ, LAT_LEN), F32)],
    )
    wide = N_HEADS_B * HEAD_DIM
    return pl.pallas_call(
        functools.partial(_na_kernel, layer=layer),
        grid_spec=grid_spec,
        out_shape=jax.ShapeDtypeStruct((T_LAT, wide), F32),
        compiler_params=_params(3),
        name="lat_na_attention",
    )(rpb.reshape(-1), y, y, y, cache_k.reshape(N_LAT_SETS, DEPTH, PAST_LEN, wide),
      cache_v.reshape(N_LAT_SETS, DEPTH, PAST_LEN, wide), jnp.asarray(_window_mask()))


MERGE_TM = 512
MERGE_SUB = 256
N_MERGE_CTX = T_CTX // MERGE_TM


def _merge_kernel(yc_ref, yla_ref, ylb_ref, g_ref, xc_ref, xl_ref, gt1_ref, sh2_ref, sc2_ref, nf_ref,
                  wa_ref, wb_ref, wo_ref, wr_ref, x1_ref, h2_ref, lg_ref, wab, wbb, wob, wrh, wr2):
    i = pl.program_id(0)

    @pl.when(i == 0)
    def _():
        wab[...] = wa_ref[...].astype(BF16)
        wbb[...] = wb_ref[...].astype(BF16)
        wob[...] = wo_ref[...].astype(BF16)
        hi, lo = _split2(wr_ref[...])
        wrh[...] = hi
        wr2[:, :LANES] = hi
        wr2[:, LANES:] = lo

    is_ctx = i < N_MERGE_CTX
    half = N_HEADS_A * HEAD_DIM
    for r in range(MERGE_TM // MERGE_SUB):
        rows = slice(r * MERGE_SUB, (r + 1) * MERGE_SUB)
        ya = jnp.where(is_ctx, yc_ref[rows, :half], yla_ref[rows, :]).astype(BF16)
        yb = jnp.where(is_ctx, yc_ref[rows, half:], ylb_ref[rows, :]).astype(BF16)
        x = jnp.where(is_ctx, xc_ref[rows, :], xl_ref[rows, :])
        za = _bdot(ya, wab[...])
        zb = _bdot(yb, wbb[...])
        m = jax.nn.sigmoid(g_ref[rows, :D_MODEL]) * za + jax.nn.sigmoid(g_ref[rows, D_MODEL:]) * zb
        x1 = x + gt1_ref[...] * _bdot(m.astype(BF16), wob[...])
        x1_ref[rows, :] = x1
        h2 = _modnorm(x1, nf_ref[...], sh2_ref[...], sc2_ref[...])
        h2_ref[rows, :] = h2.astype(BF16)
        hh, hl = _split2(h2)
        both = _bdot(hh, wr2[...])
        lg_ref[rows, :] = both[:, :LANES] + both[:, LANES:] + _bdot(hl, wrh[...])


def _merge(y, yab_ctx, ya_lat, yb_lat, x_ctx, x_lat, mod, norm_ffn, w_ba, w_bb, w_out, w_router_pad, layer):
    row_fn = lambda i: jnp.where(i < N_MERGE_CTX, 0, 1 + ((i - N_MERGE_CTX) * MERGE_TM) // LAT_LEN)
    ctx_row = lambda i: (jnp.minimum(i, N_MERGE_CTX - 1), 0)
    lat_row = lambda i: (jnp.maximum(i - N_MERGE_CTX, 0), 0)
    half = N_HEADS_A * HEAD_DIM
    weight = lambda k: pl.BlockSpec((None, k, D_MODEL), lambda i: (layer, 0, 0))
    return pl.pallas_call(
        _merge_kernel,
        grid=(T_ALL // MERGE_TM,),
        in_specs=[
            pl.BlockSpec((MERGE_TM, 2 * half), ctx_row),
            pl.BlockSpec((MERGE_TM, half), lat_row),
            pl.BlockSpec((MERGE_TM, half), lat_row),
            pl.BlockSpec((MERGE_TM, 2 * D_MODEL), lambda i: (i, 1)),
            pl.BlockSpec((MERGE_TM, D_MODEL), ctx_row),
            pl.BlockSpec((MERGE_TM, D_MODEL), lat_row),
            _mod_spec(layer, row_fn, MOD_GT1),
            _mod_spec(layer, row_fn, MOD_SH2),
            _mod_spec(layer, row_fn, MOD_SC2),
            pl.BlockSpec((None, 1, D_MODEL), lambda i: (layer, 0, 0)),
            weight(half), weight(half), weight(D_MODEL),
            pl.BlockSpec((None, D_MODEL, LANES), lambda i: (layer, 0, 0)),
        ],
        out_specs=[
            pl.BlockSpec((MERGE_TM, D_MODEL), lambda i: (i, 0)),
            pl.BlockSpec((MERGE_TM, D_MODEL), lambda i: (i, 0)),
            pl.BlockSpec((MERGE_TM, LANES), lambda i: (i, 0)),
        ],
        out_shape=[jax.ShapeDtypeStruct((T_ALL, D_MODEL), F32), jax.ShapeDtypeStruct((T_ALL, D_MODEL), BF16),
                   jax.ShapeDtypeStruct((T_ALL, LANES), F32)],
        scratch_shapes=[pltpu.VMEM((half, D_MODEL), BF16), pltpu.VMEM((half, D_MODEL), BF16),
                        pltpu.VMEM((D_MODEL, D_MODEL), BF16), pltpu.VMEM((D_MODEL, LANES), BF16),
                        pltpu.VMEM((D_MODEL, 2 * LANES), BF16)],
        compiler_params=_params(1),
        name="merge",
    )(yab_ctx, ya_lat, yb_lat, y, x_ctx, x_lat, mod, mod, mod, norm_ffn, w_ba, w_bb, w_out, w_router_pad)


GATHER_M = 512
AFF_BITS_MAX = 0x3F800000
SEARCH_STEPS = 31


def _route_kernel(lg_ref, h_ref, xg_ref, g_ref, rc_ref, aff_s, thr_s, p_ref, *, n, cap, n_sets):
    s = pl.program_id(0)

    @pl.when(s == 0)
    def _():
        lane = lax.broadcasted_iota(jnp.int32, (n_sets * n, LANES), 1)
        lg = jnp.where(lane < N_EXPERTS, lg_ref[...], -jnp.inf)
        ex = jnp.exp(lg - lg.max(axis=-1, keepdims=True))
        aff_s[...] = ex / ex.sum(axis=-1, keepdims=True)

        def body(_, carry):
            lo, hi = carry
            mid = lo + ((hi - lo + 1) >> 1)
            bits = lax.bitcast_convert_type(aff_s[...], jnp.int32).reshape(n_sets, n, LANES)
            cnt = jnp.sum(jnp.where(bits >= mid, 1.0, 0.0), axis=1, keepdims=True)
            ok = cnt >= float(cap)
            return jnp.where(ok, mid, lo), jnp.where(ok, hi, mid - 1)

        lo0 = jnp.zeros((n_sets, 1, LANES), jnp.int32)
        hi0 = jnp.full((n_sets, 1, LANES), AFF_BITS_MAX, jnp.int32)
        thr_s[...] = lax.fori_loop(0, SEARCH_STEPS, body, (lo0, hi0))[0]

    aff = aff_s[pl.ds(pl.multiple_of(s * n, n), n), :]
    bits = lax.bitcast_convert_type(aff, jnp.int32)
    thr = thr_s[s]
    above = jnp.where(bits > thr, 1.0, 0.0)
    equal = jnp.where(bits == thr, 1.0, 0.0)
    need = float(cap) - above.sum(axis=0, keepdims=True)
    t_out = lax.broadcasted_iota(jnp.int32, (n, n), 0)
    t_in = lax.broadcasted_iota(jnp.int32, (n, n), 1)
    earlier = jnp.where(t_in < t_out, 1.0, 0.0).astype(BF16)
    equal_rank = _bdot(earlier, equal.astype(BF16))
    chosen = above + equal * jnp.where(equal_rank < need, 1.0, 0.0)
    slot_of = _bdot(earlier, chosen.astype(BF16))
    rc = jnp.where(chosen > 0.0, slot_of, float(cap))
    rc_ref[...] = rc
    rc_t = rc.T
    slot = lax.broadcasted_iota(jnp.int32, (cap, n), 0).astype(F32)
    for e in range(N_EXPERTS):
        p_ref[e * cap:(e + 1) * cap, :] = jnp.where(rc_t[e:e + 1, :] == slot, 1.0, 0.0).astype(BF16)

    h = h_ref[...]
    a1, a2, a3 = _split3(aff)
    per = GATHER_M // cap
    for grp in range(N_EXPERTS * cap // GATHER_M):
        p = p_ref[grp * GATHER_M:(grp + 1) * GATHER_M, :]
        xg = _bdot(p, h).astype(BF16)
        gg = _bdot(p, a1) + _bdot(p, a2) + _bdot(p, a3)
        glane = lax.broadcasted_iota(jnp.int32, (cap, LANES), 1)
        for k in range(per):
            e = grp * per + k
            xg_ref[e] = xg[k * cap:(k + 1) * cap, :]
            ge = jnp.where(glane == e, gg[k * cap:(k + 1) * cap, :], 0.0).sum(axis=-1, keepdims=True)
            g_ref[e] = jnp.broadcast_to(ge, (cap, LANES))


def _route(logits, h2, n, cap, n_sets, first_block):
    rows = n_sets * n
    return pl.pallas_call(
        functools.partial(_route_kernel, n=n, cap=cap, n_sets=n_sets),
        grid=(n_sets,),
        in_specs=[
            pl.BlockSpec((rows, LANES), lambda s: (first_block * n // rows, 0)),
            pl.BlockSpec((n, D_MODEL), lambda s: (first_block + s, 0)),
        ],
        out_specs=[
            pl.BlockSpec((N_EXPERTS, cap, D_MODEL), lambda s: (0, s, 0)),
            pl.BlockSpec((N_EXPERTS, cap, LANES), lambda s: (0, s, 0)),
            pl.BlockSpec((n, LANES), lambda s: (s, 0)),
        ],
        out_shape=[jax.ShapeDtypeStruct((N_EXPERTS, n_sets * cap, D_MODEL), BF16),
                   jax.ShapeDtypeStruct((N_EXPERTS, n_sets * cap, LANES), F32),
                   jax.ShapeDtypeStruct((n_sets * n, LANES), F32)],
        scratch_shapes=[pltpu.VMEM((rows, LANES), F32), pltpu.VMEM((n_sets, 1, LANES), jnp.int32),
                        pltpu.VMEM((N_EXPERTS * cap, n), BF16)],
        compiler_params=_params(1),
        name=f"route_n{n}",
    )(logits, h2)


EXPERT_TF = 512
N_FF_TILES = EXPERT_FF // EXPERT_TF


def _expert_kernel(xc_ref, xl_ref, gc_ref, gl_ref, wg_ref, wu_ref, wd_ref, o_ref, x_s, acc_s):
    f = pl.program_id(1)
    n_ctx = xc_ref.shape[0]

    @pl.when(f == 0)
    def _():
        x_s[:n_ctx, :] = xc_ref[...]
        x_s[n_ctx:, :] = xl_ref[...]
        acc_s[...] = jnp.zeros_like(acc_s)

    x = x_s[...]
    gate = _bdot(x, wg_ref[...].astype(BF16))
    up = _bdot(x, wu_ref[...].astype(BF16))
    hid = (gate * jax.nn.sigmoid(gate)) * up
    acc_s[...] += _bdot(hid.astype(BF16), wd_ref[...].astype(BF16))

    @pl.when(f == N_FF_TILES - 1)
    def _():
        o_ref[:n_ctx, :] = acc_s[:n_ctx, :] * gc_ref[:, :1]
        o_ref[n_ctx:, :] = acc_s[n_ctx:, :] * gl_ref[:, :1]


def _experts(xg_ctx, xg_lat, g_ctx, g_lat, w_gate, w_up, w_down, layer):
    sc, sl = xg_ctx.shape[1], xg_lat.shape[1]
    slots = lambda s, w: pl.BlockSpec((None, s, w), lambda e, f: (e, 0, 0))
    return pl.pallas_call(
        _expert_kernel,
        grid=(N_EXPERTS, N_FF_TILES),
        in_specs=[
            slots(sc, D_MODEL), slots(sl, D_MODEL), slots(sc, LANES), slots(sl, LANES),
            pl.BlockSpec((None, None, D_MODEL, EXPERT_TF), lambda e, f: (layer, e, 0, f)),
            pl.BlockSpec((None, None, D_MODEL, EXPERT_TF), lambda e, f: (layer, e, 0, f)),
            pl.BlockSpec((None, None, EXPERT_TF, D_MODEL), lambda e, f: (layer, e, f, 0)),
        ],
        out_specs=slots(sc + sl, D_MODEL),
        out_shape=jax.ShapeDtypeStruct((N_EXPERTS, sc + sl, D_MODEL), F32),
        scratch_shapes=[pltpu.VMEM((sc + sl, D_MODEL), BF16), pltpu.VMEM((sc + sl, D_MODEL), F32)],
        compiler_params=_params(2),
        name="experts",
    )(xg_ctx, xg_lat, g_ctx, g_lat, w_gate, w_up, w_down)


COMB_TM = 256


def _combine_kernel(o_ref, rc_ref, x_ref, gt2_ref, ng_ref, *rest, cap, final):
    if final:
        y_ref, hi_ref, lo_ref = rest
    else:
        sh_ref, sc_ref, xn_ref, hn_ref, hi_ref, lo_ref = rest
    t = pl.program_id(1)
    slots = N_EXPERTS * cap

    @pl.when(t == 0)
    def _():
        hi, lo = _split2(o_ref[...].reshape(slots, D_MODEL))
        hi_ref[...] = hi
        lo_ref[...] = lo

    j = lax.broadcasted_iota(jnp.int32, (LANES, slots), 1)
    e = lax.broadcasted_iota(jnp.int32, (LANES, slots), 0)
    expand = jnp.where(j // cap == e, 1.0, 0.0).astype(BF16)
    rank = _bdot(rc_ref[...].astype(BF16), expand)
    slot = (lax.broadcasted_iota(jnp.int32, (1, slots), 1) % cap).astype(F32)
    pt = jnp.where(rank == slot, 1.0, 0.0).astype(BF16)
    ffn = _bdot(pt, hi_ref[...]) + _bdot(pt, lo_ref[...])
    x = x_ref[...] + gt2_ref[...] * ffn
    if final:
        y_ref[...] = _rms(x) * ng_ref[...]
    else:
        xn_ref[...] = x
        hn_ref[...] = _modnorm(x, ng_ref[...], sh_ref[...], sc_ref[...]).astype(BF16)


def _combine(out, rc, x1, mod, norm_g, layer, n, cap, n_sets, first_slot_block, first_row_block, row_fn, final):
    tiles = n // COMB_TM
    rows = lambda s, t: (s * tiles + t, 0)
    in_specs = [
        pl.BlockSpec((N_EXPERTS, cap, D_MODEL), lambda s, t: (0, first_slot_block + s, 0)),
        pl.BlockSpec((COMB_TM, LANES), rows),
        pl.BlockSpec((COMB_TM, D_MODEL), lambda s, t: (first_row_block + s * tiles + t, 0)),
        _mod_spec(layer, row_fn, MOD_GT2),
    ]
    args = [out, rc, x1, mod]
    if final:
        in_specs.append(pl.BlockSpec((1, D_MODEL), lambda s, t: (0, 0)))
        args.append(norm_g.reshape(1, D_MODEL))
        out_specs = pl.BlockSpec((COMB_TM, D_MODEL), rows)
        out_shape = jax.ShapeDtypeStruct((n_sets * n, D_MODEL), F32)
    else:
        in_specs += [pl.BlockSpec((None, 1, D_MODEL), lambda s, t: (layer + 1, 0, 0)),
                     _mod_spec(layer + 1, row_fn, MOD_SH1), _mod_spec(layer + 1, row_fn, MOD_SC1)]
        args += [norm_g, mod, mod]
        out_specs = [pl.BlockSpec((COMB_TM, D_MODEL), rows), pl.BlockSpec((COMB_TM, D_MODEL), rows)]
        out_shape = [jax.ShapeDtypeStruct((n_sets * n, D_MODEL), F32),
                     jax.ShapeDtypeStruct((n_sets * n, D_MODEL), BF16)]
    return pl.pallas_call(
        functools.partial(_combine_kernel, cap=cap, final=final),
        grid=(n_sets, tiles),
        in_specs=in_specs,
        out_specs=out_specs,
        out_shape=out_shape,
        scratch_shapes=[pltpu.VMEM((N_EXPERTS * cap, D_MODEL), BF16), pltpu.VMEM((N_EXPERTS * cap, D_MODEL), BF16)],
        compiler_params=_params(2),
        name=f"combine_n{n}",
    )(*args)


def kernel(x_prompt, x_sample, cache_attn_k, cache_attn_v, cache_na_k, cache_na_v, c, c_ctx, w_ada, b_ada,
           norm_mix, norm_ffn, w_in, q_norm, k_norm, rpb, w_branch_a, w_branch_b, w_out, w_router, w_gate,
           w_up, w_down, final_norm):
    x_ctx = x_prompt.reshape(T_CTX, D_MODEL)
    x_lat = x_sample.reshape(T_LAT, D_MODEL)
    mod = _modulation(c, c_ctx, w_ada, b_ada)
    norm_mix3 = norm_mix.reshape(DEPTH, 1, D_MODEL)
    norm_ffn3 = norm_ffn.reshape(DEPTH, 1, D_MODEL)
    w_router_pad = jnp.pad(w_router, ((0, 0), (0, 0), (0, LANES - N_EXPERTS)))
    ctx_row = lambda *g: 0
    lat_comb_row = lambda s, t: 1 + s

    h_ctx = _prenorm(x_ctx, mod, norm_mix3, 0, ctx_row)
    h_lat = _prenorm(x_lat, mod, norm_mix3, 0, _lat_mod_row(NORM_TM))
    layer_caches = []
    for layer in range(DEPTH):
        last = layer == DEPTH - 1
        y = _projection(h_ctx, h_lat, w_in, q_norm, k_norm, layer)
        yab_ctx, *new_caches = _ctx_attention(y, layer_caches if last else [])
        layer_caches.append(tuple(new_caches))
        ya_lat = _lat_gqa_attention(y, cache_attn_k, cache_attn_v, layer)
        yb_lat = _lat_na_attention(y, cache_na_k, cache_na_v, rpb, layer)
        x1, h2, logits = _merge(y, yab_ctx, ya_lat, yb_lat, x_ctx, x_lat, mod, norm_ffn3,
                                w_branch_a, w_branch_b, w_out, w_router_pad, layer)
        xg_ctx, g_ctx, rc_ctx = _route(logits, h2, CTX_LEN, CAP_CTX, N_CTX_SETS, 0)
        xg_lat, g_lat, rc_lat = _route(logits, h2, LAT_LEN, CAP_LAT, N_LAT_SETS, T_CTX // LAT_LEN)
        out = _experts(xg_ctx, xg_lat, g_ctx, g_lat, w_gate, w_up, w_down, layer)
        final = layer == DEPTH - 1
        norm_g = final_norm if final else norm_mix3
        res_ctx = _combine(out, rc_ctx, x1, mod, norm_g, layer, CTX_LEN, CAP_CTX, N_CTX_SETS, 0, 0, ctx_row, final)
        res_lat = _combine(out, rc_lat, x1, mod, norm_g, layer, LAT_LEN, CAP_LAT, N_LAT_SETS,
                           N_CTX_SETS * CAP_CTX // CAP_LAT, T_CTX // COMB_TM, lat_comb_row, final)
        if final:
            y_ctx, y_lat = res_ctx, res_lat
        else:
            (x_ctx, h_ctx), (x_lat, h_lat) = res_ctx, res_lat

    heads = (N_KV_A, N_KV_A, N_HEADS_B, N_HEADS_B)
    new_caches = [a.reshape(N_CTX_SETS, DEPTH, CTX_LEN, h, HEAD_DIM) for a, h in zip(layer_caches[-1], heads)]
    return (y_ctx.reshape(N_CTX_SETS, CTX_LEN, D_MODEL), y_lat.reshape(N_LAT_SETS, LAT_LEN, D_MODEL), *new_caches)
```

```python
import functools

import numpy as np
import jax
import jax.numpy as jnp
from jax import lax
from jax.experimental import pallas as pl
from jax.experimental.pallas import tpu as pltpu

F32 = jnp.float32
BF16 = jnp.bfloat16

D_MODEL = 1024
N_CTX_SETS, CTX_LEN = 16, 256
N_LAT_SETS, LAT_LEN = 2, 1024
T_CTX = N_CTX_SETS * CTX_LEN
T_LAT = N_LAT_SETS * LAT_LEN
T_ALL = T_CTX + T_LAT
DEPTH = 2
PAST_LEN = 512
GRID_W = 64
GRID_ROWS = LAT_LEN // GRID_W
HEAD_DIM = 64
N_HEADS_A, N_KV_A, N_HEADS_B = 8, 2, 8
WIN_ROWS, WIN_COLS = 8, 16
N_EXPERTS = 16
EXPERT_FF = 2048
CAP_CTX = 2 * CTX_LEN // N_EXPERTS
CAP_LAT = 2 * LAT_LEN // N_EXPERTS
ROPE_THETA = 10000.0
EPS = 1e-6
NEG = -1e30
IN_DIM = 4352
ATT_SCALE = HEAD_DIM ** -0.5

LANES = 128
VMEM_LIMIT = 56 * 1024 * 1024

PROJ_TN = 256
N_PROJ_TILES = IN_DIM // PROJ_TN
PROJ_PERM = np.array([0, 1, 16, 2, 3, 4, 5, 6, 7, 8, 9, 10, 11, 12, 13, 14, 15], np.int32)
MOD_SH1, MOD_SC1, MOD_GT1, MOD_SH2, MOD_SC2, MOD_GT2 = range(6)


def _params(n_grid_dims, vmem=VMEM_LIMIT):
    return pltpu.CompilerParams(dimension_semantics=("arbitrary",) * n_grid_dims, vmem_limit_bytes=vmem)


def _bdot(a, b):
    return jnp.dot(a, b, preferred_element_type=F32)


def _bdot_nt(a, b):
    return lax.dot_general(a, b, (((1,), (1,)), ((), ())), preferred_element_type=F32)


def _split2(x):
    hi = x.astype(BF16)
    lo = (x - hi.astype(F32)).astype(BF16)
    return hi, lo


def _split3(x):
    hi = x.astype(BF16)
    r = x - hi.astype(F32)
    mid = r.astype(BF16)
    lo = (r - mid.astype(F32)).astype(BF16)
    return hi, mid, lo


def _rms(x):
    return x * lax.rsqrt(jnp.mean(x * x, axis=-1, keepdims=True) + EPS)


def _modnorm(x, g, sh, sc):
    return (_rms(x) * g) * (1.0 + sc) + sh


def _lat_mod_row(tile_rows):
    return lambda i: 1 + (i * tile_rows) // LAT_LEN


def _mod_spec(layer, row_fn, chunk):
    return pl.BlockSpec((None, None, 1, D_MODEL), lambda *g: (layer, row_fn(*g), 0, chunk))


MOD_TN = 1536


def _mod_kernel(ct_ref, w_ref, b_ref, o_ref):
    ct = ct_ref[...]
    act = ct * jax.nn.sigmoid(ct)
    w = w_ref[...]
    for m in range(3):
        o_ref[m:m + 1, :] = jnp.sum(w * act[:, m:m + 1], axis=0, keepdims=True) + b_ref[...]
    o_ref[3:8, :] = jnp.zeros((5, MOD_TN), F32)


def _modulation(c, c_ctx, w_ada, b_ada):
    cond = jnp.concatenate([c_ctx[None, :], c, jnp.zeros((5, D_MODEL), F32)], axis=0)
    mod = pl.pallas_call(
        _mod_kernel,
        grid=(DEPTH, 6 * D_MODEL // MOD_TN),
        in_specs=[
            pl.BlockSpec((D_MODEL, 8), lambda l, j: (0, 0)),
            pl.BlockSpec((None, D_MODEL, MOD_TN), lambda l, j: (l, 0, j)),
            pl.BlockSpec((None, 1, MOD_TN), lambda l, j: (l, 0, j)),
        ],
        out_specs=pl.BlockSpec((None, 8, MOD_TN), lambda l, j: (l, 0, j)),
        out_shape=jax.ShapeDtypeStruct((DEPTH, 8, 6 * D_MODEL), F32),
        compiler_params=_params(2),
        name="modulation",
    )(cond.T, w_ada, b_ada.reshape(DEPTH, 1, 6 * D_MODEL))
    return mod.reshape(DEPTH, 8, 1, 6 * D_MODEL)


NORM_TM = 512


def _prenorm_kernel(x_ref, g_ref, sh_ref, sc_ref, h_ref):
    h_ref[...] = _modnorm(x_ref[...], g_ref[...], sh_ref[...], sc_ref[...]).astype(BF16)


def _prenorm(x, mod, norm_g, layer, row_fn):
    n = x.shape[0]
    return pl.pallas_call(
        _prenorm_kernel,
        grid=(n // NORM_TM,),
        in_specs=[
            pl.BlockSpec((NORM_TM, D_MODEL), lambda i: (i, 0)),
            pl.BlockSpec((None, 1, D_MODEL), lambda i: (layer, 0, 0)),
            _mod_spec(layer, row_fn, MOD_SH1),
            _mod_spec(layer, row_fn, MOD_SC1),
        ],
        out_specs=pl.BlockSpec((NORM_TM, D_MODEL), lambda i: (i, 0)),
        out_shape=jax.ShapeDtypeStruct((n, D_MODEL), BF16),
        compiler_params=_params(1),
        name="prenorm",
    )(x, norm_g, mod, mod)


PROJ_CH = 1024


def _rope_tables():
    t = np.arange(LAT_LEN)
    lane = np.arange(LANES) % HEAD_DIM
    pos = np.where(lane < HEAD_DIM // 2, (t // GRID_W)[:, None], (t % GRID_W)[:, None]).astype(np.float64)
    freq = ROPE_THETA ** (-(lane % 16).astype(np.float64) / 16.0)
    ang = pos * freq[None, :]
    sign = np.where((lane & 16) == 0, -1.0, 1.0)[None, :]
    return np.cos(ang).astype(np.float32), (np.sin(ang) * sign).astype(np.float32)


def _head_norm_rope(y, gain, cos, sin):
    w = y.shape[1]
    r = lax.broadcasted_iota(jnp.int32, (w, w), 0) // HEAD_DIM
    c = lax.broadcasted_iota(jnp.int32, (w, w), 1) // HEAD_DIM
    seg = jnp.where(r == c, 1.0, 0.0).astype(BF16)
    hi, lo = _split2(y * y)
    ss = _bdot(hi, seg) + _bdot(lo, seg)
    yn = y * lax.rsqrt(ss * (1.0 / HEAD_DIM) + EPS) * gain
    if cos is None:
        return yn
    lane = lax.broadcasted_iota(jnp.int32, yn.shape, 1)
    partner = jnp.where((lane & 16) == 0, pltpu.roll(yn, w - 16, 1), pltpu.roll(yn, 16, 1))
    if w > LANES:
        cos = jnp.concatenate([cos] * (w // LANES), axis=1)
        sin = jnp.concatenate([sin] * (w // LANES), axis=1)
    return yn * cos + partner * sin


def _proj_kernel(perm_ref, hc_ref, hl_ref, w_ref, gain_ref, cos_ref, sin_ref, o_ref, wb_ref):
    del perm_ref
    j = pl.program_id(0)
    wb_ref[...] = w_ref[...].astype(BF16)
    chunks = [(hc_ref, k * PROJ_CH, k * PROJ_CH, False) for k in range(T_CTX // PROJ_CH)]
    chunks += [(hl_ref, k * PROJ_CH, T_CTX + k * PROJ_CH, True) for k in range(T_LAT // PROJ_CH)]

    def tables(is_lat):
        return (cos_ref[...], sin_ref[...]) if is_lat else (None, None)

    @pl.when(j >= 3)
    def _():
        for h_ref, r0, o0, _ in chunks:
            o_ref[o0:o0 + PROJ_CH, :] = _bdot(h_ref[r0:r0 + PROJ_CH, :], wb_ref[...])

    @pl.when(j < 2)
    def _():
        for h_ref, r0, o0, is_lat in chunks:
            y = _bdot(h_ref[r0:r0 + PROJ_CH, :], wb_ref[...])
            o_ref[o0:o0 + PROJ_CH, :] = _head_norm_rope(y, gain_ref[...], *tables(is_lat))

    @pl.when(j == 2)
    def _():
        for h_ref, r0, o0, is_lat in chunks:
            y = _bdot(h_ref[r0:r0 + PROJ_CH, :], wb_ref[...])
            o_ref[o0:o0 + PROJ_CH, :LANES] = _head_norm_rope(y[:, :LANES], gain_ref[:, :LANES], *tables(is_lat))
            o_ref[o0:o0 + PROJ_CH, LANES:] = y[:, LANES:]


def _projection(h_ctx, h_lat, w_in, q_norm, k_norm, layer):
    ones = jnp.ones((2 * HEAD_DIM,), F32)
    gain = jnp.stack([jnp.tile(q_norm[layer], 4), jnp.tile(q_norm[layer], 4),
                      jnp.concatenate([jnp.tile(k_norm[layer], 2), ones])])[:, None, :]
    cos, sin = _rope_tables()
    grid_spec = pltpu.PrefetchScalarGridSpec(
        num_scalar_prefetch=1,
        grid=(N_PROJ_TILES,),
        in_specs=[
            pl.BlockSpec((T_CTX, D_MODEL), lambda j, p: (0, 0)),
            pl.BlockSpec((T_LAT, D_MODEL), lambda j, p: (0, 0)),
            pl.BlockSpec((None, D_MODEL, PROJ_TN), lambda j, p: (layer, 0, j)),
            pl.BlockSpec((None, 1, PROJ_TN), lambda j, p: (jnp.minimum(j, 2), 0, 0)),
            pl.BlockSpec((LAT_LEN, LANES), lambda j, p: (0, 0)),
            pl.BlockSpec((LAT_LEN, LANES), lambda j, p: (0, 0)),
        ],
        out_specs=pl.BlockSpec((T_ALL, PROJ_TN), lambda j, p: (0, p[j])),
        scratch_shapes=[pltpu.VMEM((D_MODEL, PROJ_TN), BF16)],
    )
    return pl.pallas_call(
        _proj_kernel,
        grid_spec=grid_spec,
        out_shape=jax.ShapeDtypeStruct((T_ALL, IN_DIM), F32),
        compiler_params=_params(1),
        name="projection",
    )(jnp.asarray(PROJ_PERM), h_ctx, h_lat, w_in, gain, jnp.asarray(cos), jnp.asarray(sin))


COL_QA, COL_QB, COL_KB, COL_VB, COL_KA, COL_VA = 0, 4, 8, 12, 32, 33


def _lane_is_low(shape):
    return lax.broadcasted_iota(jnp.int32, shape, 1) < HEAD_DIM


def _pair_halves(x):
    low = _lane_is_low(x.shape)
    xb = x.astype(BF16)
    zero = jnp.zeros_like(xb)
    return jnp.where(low, xb, zero), jnp.where(low, zero, xb)


def _attend(q, keys, values, biases):
    scores = []
    for k, b in zip(keys, biases):
        s = _bdot_nt(q, k) * ATT_SCALE
        scores.append(s if b is None else s + b)
    m = scores[0].max(axis=-1, keepdims=True)
    for s in scores[1:]:
        m = jnp.maximum(m, s.max(axis=-1, keepdims=True))
    acc = None
    den = None
    for s, v in zip(scores, values):
        e = jnp.exp(s - m)
        d = e.sum(axis=-1, keepdims=True)
        o = _bdot(e.astype(BF16), v)
        acc = o if acc is None else acc + o
        den = d if den is None else den + d
    return acc / den


def _gqa_variants(x):
    lo, hi = _pair_halves(x)
    sw_lo, sw_hi = _pair_halves(pltpu.roll(x, HEAD_DIM, 1))
    return [lo, sw_hi, sw_lo, hi]


def _gqa_attention(q_ref, kvar, vvar, o_ref):
    for pair in range(N_HEADS_A // 2):
        q = q_ref[:, pair * LANES:(pair + 1) * LANES].astype(BF16)
        kvh = (2 * pair) // (N_HEADS_A // N_KV_A)
        out = None
        for half in range(2):
            o = _attend(q, [kvar[2 * kvh + half]], [vvar[2 * kvh + half]], [None])
            out = o if out is None else out + o
        o_ref[:, pair * LANES:(pair + 1) * LANES] = out


def _ctx_attn_kernel(qa_ref, qb_ref, kb_ref, vb_ref, kava_ref, *refs, n_prev):
    prev = refs[:4 * n_prev]
    y_ref, nak_ref, nav_ref, nbk_ref, nbv_ref = refs[4 * n_prev:]
    ka = kava_ref[:, :LANES]
    va = kava_ref[:, LANES:]
    new = (ka, va, kb_ref[...], vb_ref[...])
    for c, (o_ref, val) in enumerate(zip((nak_ref, nav_ref, nbk_ref, nbv_ref), new)):
        if n_prev:
            for p in range(n_prev):
                o_ref[p] = prev[4 * p + c][...]
            o_ref[n_prev] = val
        else:
            o_ref[...] = val
    _gqa_attention(qa_ref, _gqa_variants(ka), _gqa_variants(va), y_ref.at[:, :N_HEADS_A * HEAD_DIM])
    for pair in range(N_HEADS_B // 2):
        cols = slice(pair * LANES, (pair + 1) * LANES)
        q = qb_ref[:, cols].astype(BF16)
        k_lo, k_hi = _pair_halves(kb_ref[:, cols])
        v_lo, v_hi = _pair_halves(vb_ref[:, cols])
        out = _attend(q, [k_lo], [v_lo], [None]) + _attend(q, [k_hi], [v_hi], [None])
        y_ref[:, N_HEADS_A * HEAD_DIM + pair * LANES:N_HEADS_A * HEAD_DIM + (pair + 1) * LANES] = out


def _ctx_attention(y, prev_caches):
    wide = 4 * LANES
    widths = (LANES, LANES, wide, wide)
    n_prev = len(prev_caches)
    row = lambda b: (b, 0)
    f32 = lambda *s: jax.ShapeDtypeStruct(s, F32)
    in_specs = [
        pl.BlockSpec((CTX_LEN, wide), lambda b: (b, 0)),
        pl.BlockSpec((CTX_LEN, wide), lambda b: (b, 1)),
        pl.BlockSpec((CTX_LEN, wide), lambda b: (b, 2)),
        pl.BlockSpec((CTX_LEN, wide), lambda b: (b, 3)),
        pl.BlockSpec((CTX_LEN, 2 * LANES), lambda b: (b, COL_KA // 2)),
    ]
    args = [y, y, y, y, y]
    for layer_caches in prev_caches:
        in_specs += [pl.BlockSpec((CTX_LEN, w), row) for w in widths]
        args += list(layer_caches)
    if n_prev:
        cache_specs = [pl.BlockSpec((None, n_prev + 1, CTX_LEN, w), lambda b: (b, 0, 0, 0)) for w in widths]
        cache_shapes = [f32(N_CTX_SETS, n_prev + 1, CTX_LEN, w) for w in widths]
    else:
        cache_specs = [pl.BlockSpec((CTX_LEN, w), row) for w in widths]
        cache_shapes = [f32(T_CTX, w) for w in widths]
    return pl.pallas_call(
        functools.partial(_ctx_attn_kernel, n_prev=n_prev),
        grid=(N_CTX_SETS,),
        in_specs=in_specs,
        out_specs=[pl.BlockSpec((CTX_LEN, 2 * wide), row)] + cache_specs,
        out_shape=[f32(T_CTX, 2 * wide)] + cache_shapes,
        compiler_params=_params(1),
        name="ctx_attention",
    )(*args)


LAT_TQ = 256
LAT_QT = LAT_LEN // LAT_TQ


def _lat_gqa_kernel(qa_ref, kava_ref, ck_ref, cv_ref, o_ref, k_s, v_s):
    @pl.when(pl.program_id(1) == 0)
    def _():
        for dst, cached, new in ((k_s, ck_ref[...], kava_ref[:, :LANES]), (v_s, cv_ref[...], kava_ref[:, LANES:])):
            for i, (c, x) in enumerate(zip(_gqa_variants(cached), _gqa_variants(new))):
                dst[i, :PAST_LEN, :] = c
                dst[i, PAST_LEN:, :] = x

    _gqa_attention(qa_ref, [k_s[i] for i in range(4)], [v_s[i] for i in range(4)], o_ref)


def _lat_gqa_attention(y, cache_k, cache_v, layer):
    wide = 4 * LANES
    first = T_CTX // LAT_TQ
    cache = pl.BlockSpec((None, None, PAST_LEN, LANES), lambda b, t: (b, layer, 0, 0))
    return pl.pallas_call(
        _lat_gqa_kernel,
        grid=(N_LAT_SETS, LAT_QT),
        in_specs=[
            pl.BlockSpec((LAT_TQ, wide), lambda b, t: (first + b * LAT_QT + t, 0)),
            pl.BlockSpec((LAT_LEN, 2 * LANES), lambda b, t: (T_CTX // LAT_LEN + b, COL_KA // 2)),
            cache, cache,
        ],
        out_specs=pl.BlockSpec((LAT_TQ, wide), lambda b, t: (b * LAT_QT + t, 0)),
        out_shape=jax.ShapeDtypeStruct((T_LAT, wide), F32),
        scratch_shapes=[pltpu.VMEM((4, PAST_LEN + LAT_LEN, LANES), BF16)] * 2,
        compiler_params=_params(2),
        name="lat_gqa_attention",
    )(y, y, cache_k.reshape(N_LAT_SETS, DEPTH, PAST_LEN, LANES), cache_v.reshape(N_LAT_SETS, DEPTH, PAST_LEN, LANES))


N_DR = 2 * WIN_ROWS - 1
N_DC = 2 * WIN_COLS - 1
ROWS_PER_TQ = LAT_TQ // GRID_W
NA_LOCAL = 768
NA_LOCAL_BLOCKS = NA_LOCAL // LAT_TQ
NA_TOE_ROWS = 32


def _na_key_base(qt):
    return jnp.where(qt < LAT_QT // 2, 0, LAT_QT - NA_LOCAL_BLOCKS)


def _window_mask():
    r = np.arange(GRID_ROWS)
    row_start = np.clip(r - WIN_ROWS // 2, 0, GRID_ROWS - WIN_ROWS)
    in_rows = (r[None, :] >= row_start[:, None]) & (r[None, :] < row_start[:, None] + WIN_ROWS)
    cq = np.arange(GRID_W)
    col_start = np.clip(cq - WIN_COLS // 2, 0, GRID_W - WIN_COLS)
    in_cols = (cq[None, :] >= col_start[:, None]) & (cq[None, :] < col_start[:, None] + WIN_COLS)
    valid = (in_rows[:, None, :, None] & in_cols[None, :, None, :]).reshape(LAT_LEN, LAT_LEN)
    tiles = []
    for qt in range(LAT_QT):
        base = (0 if qt < LAT_QT // 2 else LAT_QT - NA_LOCAL_BLOCKS) * LAT_TQ
        tile = valid[qt * LAT_TQ:(qt + 1) * LAT_TQ]
        assert not tile[:, :base].any() and not tile[:, base + NA_LOCAL:].any()
        tiles.append(tile[:, base:base + NA_LOCAL])
    return np.where(np.stack(tiles), 0.0, NEG).astype(np.float32)


def _toeplitz_select():
    j = np.arange(LANES)
    c = np.clip(j - (GRID_W - 1), -(WIN_COLS - 1), WIN_COLS - 1) + (WIN_COLS - 1)
    return (np.arange(LANES)[:, None] == c[None, :]).astype(np.float32)


def _na_kernel(q_ref, k_ref, v_ref, ck_ref, cv_ref, mask_ref, rpb_ref, sel_ref, o_ref,
               toe_ref, bias_ref, k_s, v_s, ck_s, cv_s):
    hp, b, qt = pl.program_id(0), pl.program_id(1), pl.program_id(2)

    @pl.when((qt == 0) & (b == 0))
    def _():
        low = lax.broadcasted_iota(jnp.int32, (GRID_W, LANES), 1) < GRID_W
        sel = sel_ref[...].astype(BF16)
        for hh in range(2):
            gen = sum(_bdot(piece, sel) for piece in _split3(rpb_ref[hh]))
            for dd in range(N_DR + 1):
                lo = jnp.broadcast_to(gen[dd:dd + 1, :], (GRID_W, LANES))
                hi = jnp.broadcast_to(gen[dd + 1:dd + 2, :], (GRID_W, LANES))
                lo = pltpu.roll(lo, LANES - (GRID_W - 1), 1, stride=1, stride_axis=0)
                hi = pltpu.roll(hi, 1, 1, stride=1, stride_axis=0)
                toe_ref[hh, dd] = jnp.where(low, lo, hi)

    @pl.when(qt == 0)
    def _():
        for src, dst in ((k_ref, k_s), (v_ref, v_s), (ck_ref, ck_s), (cv_ref, cv_s)):
            lo, hi = _pair_halves(src[...])
            dst[0] = lo
            dst[1] = hi

    q = q_ref[...].astype(BF16)
    base = _na_key_base(qt)
    local = pl.ds(pl.multiple_of(base * LAT_TQ, LAT_TQ), NA_LOCAL)
    out = None
    for hh in range(2):
        for rr in range(ROWS_PER_TQ):
            for kp in range(NA_LOCAL // LANES):
                d = 2 * (kp + base * (ROWS_PER_TQ // 2)) - (qt * ROWS_PER_TQ + rr) + (WIN_ROWS - 1)
                dd = jnp.clip(d, -1, N_DR - 1) + 1
                bias_ref[hh, rr * GRID_W:(rr + 1) * GRID_W, kp * LANES:(kp + 1) * LANES] = toe_ref[hh, dd]
        bias = bias_ref[hh] + mask_ref[...]
        o = _attend(q, [k_s[hh, local, :], ck_s[hh]], [v_s[hh, local, :], cv_s[hh]], [bias, None])
        out = o if out is None else out + o
    o_ref[...] = out


def _lat_na_attention(y, cache_k, cache_v, rpb, layer):
    first = T_CTX // LAT_TQ
    kv_row = T_CTX // LAT_LEN
    wide = N_HEADS_B * HEAD_DIM
    gen = jnp.pad(rpb[layer], ((0, 0), (1, NA_TOE_ROWS - N_DR - 1), (0, LANES - N_DC)))
    return pl.pallas_call(
        _na_kernel,
        grid=(N_HEADS_B // 2, N_LAT_SETS, LAT_QT),
        in_specs=[
            pl.BlockSpec((LAT_TQ, LANES), lambda hp, b, t: (first + b * LAT_QT + t, COL_QB + hp)),
            pl.BlockSpec((LAT_LEN, LANES), lambda hp, b, t: (kv_row + b, COL_KB + hp)),
            pl.BlockSpec((LAT_LEN, LANES), lambda hp, b, t: (kv_row + b, COL_VB + hp)),
            pl.BlockSpec((None, None, PAST_LEN, LANES), lambda hp, b, t: (b, layer, 0, hp)),
            pl.BlockSpec((None, None, PAST_LEN, LANES), lambda hp, b, t: (b, layer, 0, hp)),
            pl.BlockSpec((None, LAT_TQ, NA_LOCAL), lambda hp, b, t: (t, 0, 0)),
            pl.BlockSpec((2, NA_TOE_ROWS, LANES), lambda hp, b, t: (hp, 0, 0)),
            pl.BlockSpec((LANES, LANES), lambda hp, b, t: (0, 0)),
        ],
        out_specs=pl.BlockSpec((LAT_TQ, LANES), lambda hp, b, t: (b * LAT_QT + t, hp)),
        out_shape=jax.ShapeDtypeStruct((T_LAT, wide), F32),
        scratch_shapes=[pltpu.VMEM((2, N_DR + 1, GRID_W, LANES), F32), pltpu.VMEM((2, LAT_TQ, NA_LOCAL), F32),
                        pltpu.VMEM((2, LAT_LEN, LANES), BF16), pltpu.VMEM((2, LAT_LEN, LANES), BF16),
                        pltpu.VMEM((2, PAST_LEN, LANES), BF16), pltpu.VMEM((2, PAST_LEN, LANES), BF16)],
        compiler_params=_params(3),
        name="lat_na_attention",
    )(y, y, y, cache_k.reshape(N_LAT_SETS, DEPTH, PAST_LEN, wide), cache_v.reshape(N_LAT_SETS, DEPTH, PAST_LEN, wide),
      jnp.asarray(_window_mask()), gen, jnp.asarray(_toeplitz_select()))


MERGE_TM = 512
MERGE_SUB = 256
N_MERGE_CTX = T_CTX // MERGE_TM


def _merge_kernel(yc_ref, yla_ref, ylb_ref, g_ref, xc_ref, xl_ref, gt1_ref, sh2_ref, sc2_ref, nf_ref,
                  wa_ref, wb_ref, wo_ref, wr_ref, x1_ref, h2_ref, lg_ref, wab, wbb, wob, wrh, wr2):
    i = pl.program_id(0)

    @pl.when(i == 0)
    def _():
        wab[...] = wa_ref[...].astype(BF16)
        wbb[...] = wb_ref[...].astype(BF16)
        wob[...] = wo_ref[...].astype(BF16)
        hi, lo = _split2(wr_ref[...])
        wrh[...] = hi
        wr2[:, :LANES] = hi
        wr2[:, LANES:] = lo

    is_ctx = i < N_MERGE_CTX
    half = N_HEADS_A * HEAD_DIM
    for r in range(MERGE_TM // MERGE_SUB):
        rows = slice(r * MERGE_SUB, (r + 1) * MERGE_SUB)
        ya = jnp.where(is_ctx, yc_ref[rows, :half], yla_ref[rows, :]).astype(BF16)
        yb = jnp.where(is_ctx, yc_ref[rows, half:], ylb_ref[rows, :]).astype(BF16)
        x = jnp.where(is_ctx, xc_ref[rows, :], xl_ref[rows, :])
        za = _bdot(ya, wab[...])
        zb = _bdot(yb, wbb[...])
        m = jax.nn.sigmoid(g_ref[rows, :D_MODEL]) * za + jax.nn.sigmoid(g_ref[rows, D_MODEL:]) * zb
        x1 = x + gt1_ref[...] * _bdot(m.astype(BF16), wob[...])
        x1_ref[rows, :] = x1
        h2 = _modnorm(x1, nf_ref[...], sh2_ref[...], sc2_ref[...])
        h2_ref[rows, :] = h2.astype(BF16)
        hh, hl = _split2(h2)
        both = _bdot(hh, wr2[...])
        lg_ref[rows, :] = both[:, :LANES] + both[:, LANES:] + _bdot(hl, wrh[...])


def _merge(y, yab_ctx, ya_lat, yb_lat, x_ctx, x_lat, mod, norm_ffn, w_ba, w_bb, w_out, w_router_pad, layer):
    row_fn = lambda i: jnp.where(i < N_MERGE_CTX, 0, 1 + ((i - N_MERGE_CTX) * MERGE_TM) // LAT_LEN)
    ctx_row = lambda i: (jnp.minimum(i, N_MERGE_CTX - 1), 0)
    lat_row = lambda i: (jnp.maximum(i - N_MERGE_CTX, 0), 0)
    half = N_HEADS_A * HEAD_DIM
    weight = lambda k: pl.BlockSpec((None, k, D_MODEL), lambda i: (layer, 0, 0))
    return pl.pallas_call(
        _merge_kernel,
        grid=(T_ALL // MERGE_TM,),
        in_specs=[
            pl.BlockSpec((MERGE_TM, 2 * half), ctx_row),
            pl.BlockSpec((MERGE_TM, half), lat_row),
            pl.BlockSpec((MERGE_TM, half), lat_row),
            pl.BlockSpec((MERGE_TM, 2 * D_MODEL), lambda i: (i, 1)),
            pl.BlockSpec((MERGE_TM, D_MODEL), ctx_row),
            pl.BlockSpec((MERGE_TM, D_MODEL), lat_row),
            _mod_spec(layer, row_fn, MOD_GT1),
            _mod_spec(layer, row_fn, MOD_SH2),
            _mod_spec(layer, row_fn, MOD_SC2),
            pl.BlockSpec((None, 1, D_MODEL), lambda i: (layer, 0, 0)),
            weight(half), weight(half), weight(D_MODEL),
            pl.BlockSpec((None, D_MODEL, LANES), lambda i: (layer, 0, 0)),
        ],
        out_specs=[
            pl.BlockSpec((MERGE_TM, D_MODEL), lambda i: (i, 0)),
            pl.BlockSpec((MERGE_TM, D_MODEL), lambda i: (i, 0)),
            pl.BlockSpec((MERGE_TM, LANES), lambda i: (i, 0)),
        ],
        out_shape=[jax.ShapeDtypeStruct((T_ALL, D_MODEL), F32), jax.ShapeDtypeStruct((T_ALL, D_MODEL), BF16),
                   jax.ShapeDtypeStruct((T_ALL, LANES), F32)],
        scratch_shapes=[pltpu.VMEM((half, D_MODEL), BF16), pltpu.VMEM((half, D_MODEL), BF16),
                        pltpu.VMEM((D_MODEL, D_MODEL), BF16), pltpu.VMEM((D_MODEL, LANES), BF16),
                        pltpu.VMEM((D_MODEL, 2 * LANES), BF16)],
        compiler_params=_params(1),
        name="merge",
    )(yab_ctx, ya_lat, yb_lat, y, x_ctx, x_lat, mod, mod, mod, norm_ffn, w_ba, w_bb, w_out, w_router_pad)


GATHER_M = 512
RANK_TILE = 128


def _rank_row(aff, a_row, e, n):
    tiles = n // RANK_TILE
    sub = lax.broadcasted_iota(jnp.int32, (RANK_TILE, RANK_TILE), 0)
    lane = lax.broadcasted_iota(jnp.int32, (RANK_TILE, RANK_TILE), 1)
    earlier = jnp.where(sub < lane, 1.0, 0.0)
    acc = [jnp.zeros((8, RANK_TILE), F32) for _ in range(tiles)]
    for c in range(tiles):
        a_col = jnp.broadcast_to(aff[c * RANK_TILE:(c + 1) * RANK_TILE, e:e + 1], (RANK_TILE, RANK_TILE))
        for j in range(tiles):
            a_rj = a_row[:, j * RANK_TILE:(j + 1) * RANK_TILE]
            if c < j:
                beats = jnp.where(a_col >= a_rj, 1.0, 0.0)
            elif c > j:
                beats = jnp.where(a_col > a_rj, 1.0, 0.0)
            else:
                beats = jnp.where(a_col > a_rj, 1.0, jnp.where(a_col == a_rj, earlier, 0.0))
            acc[j] = acc[j] + beats.reshape(RANK_TILE // 8, 8, RANK_TILE).sum(axis=0)
    return jnp.concatenate([a.sum(axis=0, keepdims=True) for a in acc], axis=1)


def _route_kernel(lg_ref, h_ref, xg_ref, g_ref, rc_ref, p_ref, rt_ref, *, n, cap):
    lane = lax.broadcasted_iota(jnp.int32, (n, LANES), 1)
    lg = jnp.where(lane < N_EXPERTS, lg_ref[...], -jnp.inf)
    ex = jnp.exp(lg - lg.max(axis=-1, keepdims=True))
    aff = ex / ex.sum(axis=-1, keepdims=True)
    aff_t = aff.T
    rt_ref[...] = jnp.full((LANES, n), float(cap), F32)
    slot = lax.broadcasted_iota(jnp.int32, (cap, n), 0).astype(F32)
    for e in range(N_EXPERTS):
        rank = _rank_row(aff, aff_t[e:e + 1, :], e, n)
        rt_ref[e:e + 1, :] = jnp.minimum(rank, float(cap))
        p_ref[e * cap:(e + 1) * cap, :] = jnp.where(rank == slot, 1.0, 0.0).astype(BF16)

    h = h_ref[...]
    a1, a2, a3 = (p.astype(F32) for p in _split3(aff))
    packed = (a1 + pltpu.roll(a2, N_EXPERTS, 1) + pltpu.roll(a3, 2 * N_EXPERTS, 1)).astype(BF16)
    per = GATHER_M // cap
    glane = lax.broadcasted_iota(jnp.int32, (cap, LANES), 1)
    for grp in range(N_EXPERTS * cap // GATHER_M):
        p = p_ref[grp * GATHER_M:(grp + 1) * GATHER_M, :]
        xg = _bdot(p, h).astype(BF16)
        gg = _bdot(p, packed)
        for k in range(per):
            e = grp * per + k
            xg_ref[e] = xg[k * cap:(k + 1) * cap, :]
            mine = (glane < 3 * N_EXPERTS) & ((glane & (N_EXPERTS - 1)) == e)
            ge = jnp.where(mine, gg[k * cap:(k + 1) * cap, :], 0.0).sum(axis=-1, keepdims=True)
            g_ref[e] = jnp.broadcast_to(ge, (cap, LANES))
    rc_ref[...] = rt_ref[...].T


def _route(logits, h2, n, cap, n_sets, first_block):
    return pl.pallas_call(
        functools.partial(_route_kernel, n=n, cap=cap),
        grid=(n_sets,),
        in_specs=[
            pl.BlockSpec((n, LANES), lambda s: (first_block + s, 0)),
            pl.BlockSpec((n, D_MODEL), lambda s: (first_block + s, 0)),
        ],
        out_specs=[
            pl.BlockSpec((N_EXPERTS, cap, D_MODEL), lambda s: (0, s, 0)),
            pl.BlockSpec((N_EXPERTS, cap, LANES), lambda s: (0, s, 0)),
            pl.BlockSpec((n, LANES), lambda s: (s, 0)),
        ],
        out_shape=[jax.ShapeDtypeStruct((N_EXPERTS, n_sets * cap, D_MODEL), BF16),
                   jax.ShapeDtypeStruct((N_EXPERTS, n_sets * cap, LANES), F32),
                   jax.ShapeDtypeStruct((n_sets * n, LANES), F32)],
        scratch_shapes=[pltpu.VMEM((N_EXPERTS * cap, n), BF16), pltpu.VMEM((LANES, n), F32)],
        compiler_params=_params(1),
        name=f"route_n{n}",
    )(logits, h2)


EXPERT_TF = 512
N_FF_TILES = EXPERT_FF // EXPERT_TF


def _expert_kernel(xc_ref, xl_ref, gc_ref, gl_ref, wg_ref, wu_ref, wd_ref, o_ref, x_s, acc_s):
    f = pl.program_id(1)
    n_ctx = xc_ref.shape[0]

    @pl.when(f == 0)
    def _():
        x_s[:n_ctx, :] = xc_ref[...]
        x_s[n_ctx:, :] = xl_ref[...]
        acc_s[...] = jnp.zeros_like(acc_s)

    x = x_s[...]
    gate = _bdot(x, wg_ref[...].astype(BF16))
    up = _bdot(x, wu_ref[...].astype(BF16))
    hid = (gate * jax.nn.sigmoid(gate)) * up
    acc_s[...] += _bdot(hid.astype(BF16), wd_ref[...].astype(BF16))

    @pl.when(f == N_FF_TILES - 1)
    def _():
        o_ref[:n_ctx, :] = (acc_s[:n_ctx, :] * gc_ref[:, :1]).astype(o_ref.dtype)
        o_ref[n_ctx:, :] = (acc_s[n_ctx:, :] * gl_ref[:, :1]).astype(o_ref.dtype)


def _experts(xg_ctx, xg_lat, g_ctx, g_lat, w_gate, w_up, w_down, layer):
    sc, sl = xg_ctx.shape[1], xg_lat.shape[1]
    slots = lambda s, w: pl.BlockSpec((None, s, w), lambda e, f: (e, 0, 0))
    return pl.pallas_call(
        _expert_kernel,
        grid=(N_EXPERTS, N_FF_TILES),
        in_specs=[
            slots(sc, D_MODEL), slots(sl, D_MODEL), slots(sc, LANES), slots(sl, LANES),
            pl.BlockSpec((None, None, D_MODEL, EXPERT_TF), lambda e, f: (layer, e, 0, f)),
            pl.BlockSpec((None, None, D_MODEL, EXPERT_TF), lambda e, f: (layer, e, 0, f)),
            pl.BlockSpec((None, None, EXPERT_TF, D_MODEL), lambda e, f: (layer, e, f, 0)),
        ],
        out_specs=slots(sc + sl, D_MODEL),
        out_shape=jax.ShapeDtypeStruct((N_EXPERTS, sc + sl, D_MODEL), BF16),
        scratch_shapes=[pltpu.VMEM((sc + sl, D_MODEL), BF16), pltpu.VMEM((sc + sl, D_MODEL), F32)],
        compiler_params=_params(2),
        name="experts",
    )(xg_ctx, xg_lat, g_ctx, g_lat, w_gate, w_up, w_down)


COMB_TM = 256


def _combine_kernel(o_ref, rc_ref, x_ref, gt2_ref, ng_ref, *rest, cap, final):
    if final:
        (y_ref,) = rest
    else:
        sh_ref, sc_ref, xn_ref, hn_ref = rest
    slots = N_EXPERTS * cap

    j = lax.broadcasted_iota(jnp.int32, (LANES, slots), 1)
    e = lax.broadcasted_iota(jnp.int32, (LANES, slots), 0)
    expand = jnp.where(j // cap == e, 1.0, 0.0).astype(BF16)
    rank = _bdot(rc_ref[...].astype(BF16), expand)
    slot = (lax.broadcasted_iota(jnp.int32, (1, slots), 1) % cap).astype(F32)
    pt = jnp.where(rank == slot, 1.0, 0.0).astype(BF16)
    ffn = _bdot(pt, o_ref[...].reshape(slots, D_MODEL))
    x = x_ref[...] + gt2_ref[...] * ffn
    if final:
        y_ref[...] = _rms(x) * ng_ref[...]
    else:
        xn_ref[...] = x
        hn_ref[...] = _modnorm(x, ng_ref[...], sh_ref[...], sc_ref[...]).astype(BF16)


def _combine(out, rc, x1, mod, norm_g, layer, n, cap, n_sets, first_slot_block, first_row_block, row_fn, final):
    tiles = n // COMB_TM
    rows = lambda s, t: (s * tiles + t, 0)
    in_specs = [
        pl.BlockSpec((N_EXPERTS, cap, D_MODEL), lambda s, t: (0, first_slot_block + s, 0)),
        pl.BlockSpec((COMB_TM, LANES), rows),
        pl.BlockSpec((COMB_TM, D_MODEL), lambda s, t: (first_row_block + s * tiles + t, 0)),
        _mod_spec(layer, row_fn, MOD_GT2),
    ]
    args = [out, rc, x1, mod]
    if final:
        in_specs.append(pl.BlockSpec((1, D_MODEL), lambda s, t: (0, 0)))
        args.append(norm_g.reshape(1, D_MODEL))
        out_specs = pl.BlockSpec((COMB_TM, D_MODEL), rows)
        out_shape = jax.ShapeDtypeStruct((n_sets * n, D_MODEL), F32)
    else:
        in_specs += [pl.BlockSpec((None, 1, D_MODEL), lambda s, t: (layer + 1, 0, 0)),
                     _mod_spec(layer + 1, row_fn, MOD_SH1), _mod_spec(layer + 1, row_fn, MOD_SC1)]
        args += [norm_g, mod, mod]
        out_specs = [pl.BlockSpec((COMB_TM, D_MODEL), rows), pl.BlockSpec((COMB_TM, D_MODEL), rows)]
        out_shape = [jax.ShapeDtypeStruct((n_sets * n, D_MODEL), F32),
                     jax.ShapeDtypeStruct((n_sets * n, D_MODEL), BF16)]
    return pl.pallas_call(
        functools.partial(_combine_kernel, cap=cap, final=final),
        grid=(n_sets, tiles),
        in_specs=in_specs,
        out_specs=out_specs,
        out_shape=out_shape,
        compiler_params=_params(2),
        name=f"combine_n{n}",
    )(*args)


def kernel(x_prompt, x_sample, cache_attn_k, cache_attn_v, cache_na_k, cache_na_v, c, c_ctx, w_ada, b_ada,
           norm_mix, norm_ffn, w_in, q_norm, k_norm, rpb, w_branch_a, w_branch_b, w_out, w_router, w_gate,
           w_up, w_down, final_norm):
    x_ctx = x_prompt.reshape(T_CTX, D_MODEL)
    x_lat = x_sample.reshape(T_LAT, D_MODEL)
    mod = _modulation(c, c_ctx, w_ada, b_ada)
    norm_mix3 = norm_mix.reshape(DEPTH, 1, D_MODEL)
    norm_ffn3 = norm_ffn.reshape(DEPTH, 1, D_MODEL)
    w_router_pad = jnp.pad(w_router, ((0, 0), (0, 0), (0, LANES - N_EXPERTS)))
    ctx_row = lambda *g: 0
    lat_comb_row = lambda s, t: 1 + s

    h_ctx = _prenorm(x_ctx, mod, norm_mix3, 0, ctx_row)
    h_lat = _prenorm(x_lat, mod, norm_mix3, 0, _lat_mod_row(NORM_TM))
    layer_caches = []
    for layer in range(DEPTH):
        last = layer == DEPTH - 1
        y = _projection(h_ctx, h_lat, w_in, q_norm, k_norm, layer)
        yab_ctx, *new_caches = _ctx_attention(y, layer_caches if last else [])
        layer_caches.append(tuple(new_caches))
        ya_lat = _lat_gqa_attention(y, cache_attn_k, cache_attn_v, layer)
        yb_lat = _lat_na_attention(y, cache_na_k, cache_na_v, rpb, layer)
        x1, h2, logits = _merge(y, yab_ctx, ya_lat, yb_lat, x_ctx, x_lat, mod, norm_ffn3,
                                w_branch_a, w_branch_b, w_out, w_router_pad, layer)
        xg_ctx, g_ctx, rc_ctx = _route(logits, h2, CTX_LEN, CAP_CTX, N_CTX_SETS, 0)
        xg_lat, g_lat, rc_lat = _route(logits, h2, LAT_LEN, CAP_LAT, N_LAT_SETS, T_CTX // LAT_LEN)
        out = _experts(xg_ctx, xg_lat, g_ctx, g_lat, w_gate, w_up, w_down, layer)
        final = layer == DEPTH - 1
        norm_g = final_norm if final else norm_mix3
        res_ctx = _combine(out, rc_ctx, x1, mod, norm_g, layer, CTX_LEN, CAP_CTX, N_CTX_SETS, 0, 0, ctx_row, final)
        res_lat = _combine(out, rc_lat, x1, mod, norm_g, layer, LAT_LEN, CAP_LAT, N_LAT_SETS,
                           N_CTX_SETS * CAP_CTX // CAP_LAT, T_CTX // COMB_TM, lat_comb_row, final)
        if final:
            y_ctx, y_lat = res_ctx, res_lat
        else:
            (x_ctx, h_ctx), (x_lat, h_lat) = res_ctx, res_lat

    heads = (N_KV_A, N_KV_A, N_HEADS_B, N_HEADS_B)
    new_caches = [a.reshape(N_CTX_SETS, DEPTH, CTX_LEN, h, HEAD_DIM) for a, h in zip(layer_caches[-1], heads)]
    return (y_ctx.reshape(N_CTX_SETS, CTX_LEN, D_MODEL), y_lat.reshape(N_LAT_SETS, LAT_LEN, D_MODEL), *new_caches)
```

```python
import functools

import numpy as np
import jax
import jax.numpy as jnp
from jax import lax
from jax.experimental import pallas as pl
from jax.experimental.pallas import tpu as pltpu

F32 = jnp.float32
BF16 = jnp.bfloat16

D_MODEL = 1024
N_CTX_SETS, CTX_LEN = 16, 256
N_LAT_SETS, LAT_LEN = 2, 1024
T_CTX = N_CTX_SETS * CTX_LEN
T_LAT = N_LAT_SETS * LAT_LEN
T_ALL = T_CTX + T_LAT
DEPTH = 2
PAST_LEN = 512
GRID_W = 64
GRID_ROWS = LAT_LEN // GRID_W
HEAD_DIM = 64
N_HEADS_A, N_KV_A, N_HEADS_B = 8, 2, 8
WIN_ROWS, WIN_COLS = 8, 16
N_EXPERTS = 16
EXPERT_FF = 2048
CAP_CTX = 2 * CTX_LEN // N_EXPERTS
CAP_LAT = 2 * LAT_LEN // N_EXPERTS
ROPE_THETA = 10000.0
EPS = 1e-6
NEG = -1e30
QKV_DIM = 2304
ATT_SCALE = HEAD_DIM ** -0.5

LANES = 128
VMEM_LIMIT = 56 * 1024 * 1024

PROJ_TN = 256
N_PROJ_TILES = QKV_DIM // PROJ_TN
PROJ_PERM = np.array([0, 1, 8, 2, 3, 4, 5, 6, 7], np.int32)
MOD_SH1, MOD_SC1, MOD_GT1, MOD_SH2, MOD_SC2, MOD_GT2 = range(6)


def _params(n_grid_dims, vmem=VMEM_LIMIT):
    return pltpu.CompilerParams(dimension_semantics=("arbitrary",) * n_grid_dims, vmem_limit_bytes=vmem)


def _bdot(a, b):
    return jnp.dot(a, b, preferred_element_type=F32)


def _bdot_nt(a, b):
    return lax.dot_general(a, b, (((1,), (1,)), ((), ())), preferred_element_type=F32)


def _split2(x):
    hi = x.astype(BF16)
    lo = (x - hi.astype(F32)).astype(BF16)
    return hi, lo


def _split3(x):
    hi = x.astype(BF16)
    r = x - hi.astype(F32)
    mid = r.astype(BF16)
    lo = (r - mid.astype(F32)).astype(BF16)
    return hi, mid, lo


def _rms(x):
    return x * lax.rsqrt(jnp.mean(x * x, axis=-1, keepdims=True) + EPS)


def _modnorm(x, g, sh, sc):
    return (_rms(x) * g) * (1.0 + sc) + sh


def _lat_mod_row(tile_rows):
    return lambda i: 1 + (i * tile_rows) // LAT_LEN


def _mod_spec(layer, row_fn, chunk):
    return pl.BlockSpec((None, None, 1, D_MODEL), lambda *g: (layer, row_fn(*g), 0, chunk))


MOD_TN = 1536


def _mod_kernel(ct_ref, w_ref, b_ref, o_ref):
    ct = ct_ref[...]
    act = ct * jax.nn.sigmoid(ct)
    w = w_ref[...]
    for m in range(3):
        o_ref[m:m + 1, :] = jnp.sum(w * act[:, m:m + 1], axis=0, keepdims=True) + b_ref[...]
    o_ref[3:8, :] = jnp.zeros((5, MOD_TN), F32)


def _modulation(c, c_ctx, w_ada, b_ada):
    cond = jnp.concatenate([c_ctx[None, :], c, jnp.zeros((5, D_MODEL), F32)], axis=0)
    mod = pl.pallas_call(
        _mod_kernel,
        grid=(DEPTH, 6 * D_MODEL // MOD_TN),
        in_specs=[
            pl.BlockSpec((D_MODEL, 8), lambda l, j: (0, 0)),
            pl.BlockSpec((None, D_MODEL, MOD_TN), lambda l, j: (l, 0, j)),
            pl.BlockSpec((None, 1, MOD_TN), lambda l, j: (l, 0, j)),
        ],
        out_specs=pl.BlockSpec((None, 8, MOD_TN), lambda l, j: (l, 0, j)),
        out_shape=jax.ShapeDtypeStruct((DEPTH, 8, 6 * D_MODEL), F32),
        compiler_params=_params(2),
        name="modulation",
    )(cond.T, w_ada, b_ada.reshape(DEPTH, 1, 6 * D_MODEL))
    return mod.reshape(DEPTH, 8, 1, 6 * D_MODEL)


NORM_TM = 512


def _prenorm_kernel(x_ref, g_ref, sh_ref, sc_ref, h_ref):
    h_ref[...] = _modnorm(x_ref[...], g_ref[...], sh_ref[...], sc_ref[...]).astype(BF16)


def _prenorm(x, mod, norm_g, layer, row_fn):
    n = x.shape[0]
    return pl.pallas_call(
        _prenorm_kernel,
        grid=(n // NORM_TM,),
        in_specs=[
            pl.BlockSpec((NORM_TM, D_MODEL), lambda i: (i, 0)),
            pl.BlockSpec((None, 1, D_MODEL), lambda i: (layer, 0, 0)),
            _mod_spec(layer, row_fn, MOD_SH1),
            _mod_spec(layer, row_fn, MOD_SC1),
        ],
        out_specs=pl.BlockSpec((NORM_TM, D_MODEL), lambda i: (i, 0)),
        out_shape=jax.ShapeDtypeStruct((n, D_MODEL), BF16),
        compiler_params=_params(1),
        name="prenorm",
    )(x, norm_g, mod, mod)


PROJ_CH = 1024


def _rope_tables():
    t = np.arange(LAT_LEN)
    lane = np.arange(LANES) % HEAD_DIM
    pos = np.where(lane < HEAD_DIM // 2, (t // GRID_W)[:, None], (t % GRID_W)[:, None]).astype(np.float64)
    freq = ROPE_THETA ** (-(lane % 16).astype(np.float64) / 16.0)
    ang = pos * freq[None, :]
    sign = np.where((lane & 16) == 0, -1.0, 1.0)[None, :]
    return np.cos(ang).astype(np.float32), (np.sin(ang) * sign).astype(np.float32)


def _head_norm_rope(y, gain, cos, sin):
    w = y.shape[1]
    r = lax.broadcasted_iota(jnp.int32, (w, w), 0) // HEAD_DIM
    c = lax.broadcasted_iota(jnp.int32, (w, w), 1) // HEAD_DIM
    seg = jnp.where(r == c, 1.0, 0.0).astype(BF16)
    hi, lo = _split2(y * y)
    ss = _bdot(hi, seg) + _bdot(lo, seg)
    yn = y * lax.rsqrt(ss * (1.0 / HEAD_DIM) + EPS) * gain
    if cos is None:
        return yn
    lane = lax.broadcasted_iota(jnp.int32, yn.shape, 1)
    partner = jnp.where((lane & 16) == 0, pltpu.roll(yn, w - 16, 1), pltpu.roll(yn, 16, 1))
    if w > LANES:
        cos = jnp.concatenate([cos] * (w // LANES), axis=1)
        sin = jnp.concatenate([sin] * (w // LANES), axis=1)
    return yn * cos + partner * sin


def _proj_kernel(perm_ref, hc_ref, hl_ref, w_ref, gain_ref, cos_ref, sin_ref, o_ref, wb_ref):
    del perm_ref
    j = pl.program_id(0)
    wb_ref[...] = w_ref[...].astype(BF16)
    chunks = [(hc_ref, k * PROJ_CH, k * PROJ_CH, False) for k in range(T_CTX // PROJ_CH)]
    chunks += [(hl_ref, k * PROJ_CH, T_CTX + k * PROJ_CH, True) for k in range(T_LAT // PROJ_CH)]

    def tables(is_lat):
        return (cos_ref[...], sin_ref[...]) if is_lat else (None, None)

    @pl.when(j >= 3)
    def _():
        for h_ref, r0, o0, _ in chunks:
            o_ref[o0:o0 + PROJ_CH, :] = _bdot(h_ref[r0:r0 + PROJ_CH, :], wb_ref[...])

    @pl.when(j < 2)
    def _():
        for h_ref, r0, o0, is_lat in chunks:
            y = _bdot(h_ref[r0:r0 + PROJ_CH, :], wb_ref[...])
            o_ref[o0:o0 + PROJ_CH, :] = _head_norm_rope(y, gain_ref[...], *tables(is_lat))

    @pl.when(j == 2)
    def _():
        for h_ref, r0, o0, is_lat in chunks:
            y = _bdot(h_ref[r0:r0 + PROJ_CH, :], wb_ref[...])
            o_ref[o0:o0 + PROJ_CH, :LANES] = _head_norm_rope(y[:, :LANES], gain_ref[:, :LANES], *tables(is_lat))
            o_ref[o0:o0 + PROJ_CH, LANES:] = y[:, LANES:]


def _projection(h_ctx, h_lat, w_in, q_norm, k_norm, layer):
    ones = jnp.ones((2 * HEAD_DIM,), F32)
    gain = jnp.stack([jnp.tile(q_norm[layer], 4), jnp.tile(q_norm[layer], 4),
                      jnp.concatenate([jnp.tile(k_norm[layer], 2), ones])])[:, None, :]
    cos, sin = _rope_tables()
    grid_spec = pltpu.PrefetchScalarGridSpec(
        num_scalar_prefetch=1,
        grid=(N_PROJ_TILES,),
        in_specs=[
            pl.BlockSpec((T_CTX, D_MODEL), lambda j, p: (0, 0)),
            pl.BlockSpec((T_LAT, D_MODEL), lambda j, p: (0, 0)),
            pl.BlockSpec((None, D_MODEL, PROJ_TN), lambda j, p: (layer, 0, j)),
            pl.BlockSpec((None, 1, PROJ_TN), lambda j, p: (jnp.minimum(j, 2), 0, 0)),
            pl.BlockSpec((LAT_LEN, LANES), lambda j, p: (0, 0)),
            pl.BlockSpec((LAT_LEN, LANES), lambda j, p: (0, 0)),
        ],
        out_specs=pl.BlockSpec((T_ALL, PROJ_TN), lambda j, p: (0, p[j])),
        scratch_shapes=[pltpu.VMEM((D_MODEL, PROJ_TN), BF16)],
    )
    return pl.pallas_call(
        _proj_kernel,
        grid_spec=grid_spec,
        out_shape=jax.ShapeDtypeStruct((T_ALL, QKV_DIM), F32),
        compiler_params=_params(1),
        name="projection",
    )(jnp.asarray(PROJ_PERM), h_ctx, h_lat, w_in, gain, jnp.asarray(cos), jnp.asarray(sin))


COL_QA, COL_QB, COL_KB, COL_VB, COL_KA, COL_VA = 0, 4, 8, 12, 16, 17


def _lane_is_low(shape):
    return lax.broadcasted_iota(jnp.int32, shape, 1) < HEAD_DIM


def _pair_halves(x):
    low = _lane_is_low(x.shape)
    xb = x.astype(BF16)
    zero = jnp.zeros_like(xb)
    return jnp.where(low, xb, zero), jnp.where(low, zero, xb)


def _attend(q, keys, values, biases):
    scores = []
    for k, b in zip(keys, biases):
        s = _bdot_nt(q, k) * ATT_SCALE
        scores.append(s if b is None else s + b)
    m = scores[0].max(axis=-1, keepdims=True)
    for s in scores[1:]:
        m = jnp.maximum(m, s.max(axis=-1, keepdims=True))
    acc = None
    den = None
    for s, v in zip(scores, values):
        e = jnp.exp(s - m)
        d = e.sum(axis=-1, keepdims=True)
        o = _bdot(e.astype(BF16), v)
        acc = o if acc is None else acc + o
        den = d if den is None else den + d
    return acc / den


def _gqa_variants(x):
    lo, hi = _pair_halves(x)
    sw_lo, sw_hi = _pair_halves(pltpu.roll(x, HEAD_DIM, 1))
    return [lo, sw_hi, sw_lo, hi]


def _gqa_attention(q_ref, kvar, vvar, o_ref):
    for pair in range(N_HEADS_A // 2):
        q = q_ref[:, pair * LANES:(pair + 1) * LANES].astype(BF16)
        kvh = (2 * pair) // (N_HEADS_A // N_KV_A)
        out = None
        for half in range(2):
            o = _attend(q, [kvar[2 * kvh + half]], [vvar[2 * kvh + half]], [None])
            out = o if out is None else out + o
        o_ref[:, pair * LANES:(pair + 1) * LANES] = out.astype(o_ref.dtype)


def _ctx_attn_kernel(qa_ref, qb_ref, kb_ref, vb_ref, kava_ref, *refs, n_prev):
    prev = refs[:4 * n_prev]
    y_ref, nak_ref, nav_ref, nbk_ref, nbv_ref = refs[4 * n_prev:]
    ka = kava_ref[:, :LANES]
    va = kava_ref[:, LANES:]
    new = (ka, va, kb_ref[...], vb_ref[...])
    for c, (o_ref, val) in enumerate(zip((nak_ref, nav_ref, nbk_ref, nbv_ref), new)):
        if n_prev:
            for p in range(n_prev):
                o_ref[p] = prev[4 * p + c][...]
            o_ref[n_prev] = val
        else:
            o_ref[...] = val
    _gqa_attention(qa_ref, _gqa_variants(ka), _gqa_variants(va), y_ref.at[:, :N_HEADS_A * HEAD_DIM])
    for pair in range(N_HEADS_B // 2):
        cols = slice(pair * LANES, (pair + 1) * LANES)
        q = qb_ref[:, cols].astype(BF16)
        k_lo, k_hi = _pair_halves(kb_ref[:, cols])
        v_lo, v_hi = _pair_halves(vb_ref[:, cols])
        out = _attend(q, [k_lo], [v_lo], [None]) + _attend(q, [k_hi], [v_hi], [None])
        y_ref[:, N_HEADS_A * HEAD_DIM + pair * LANES:N_HEADS_A * HEAD_DIM + (pair + 1) * LANES] = out.astype(BF16)


def _ctx_attention(y, prev_caches):
    wide = 4 * LANES
    widths = (LANES, LANES, wide, wide)
    n_prev = len(prev_caches)
    row = lambda b: (b, 0)
    f32 = lambda *s: jax.ShapeDtypeStruct(s, F32)
    in_specs = [
        pl.BlockSpec((CTX_LEN, wide), lambda b: (b, 0)),
        pl.BlockSpec((CTX_LEN, wide), lambda b: (b, 1)),
        pl.BlockSpec((CTX_LEN, wide), lambda b: (b, 2)),
        pl.BlockSpec((CTX_LEN, wide), lambda b: (b, 3)),
        pl.BlockSpec((CTX_LEN, 2 * LANES), lambda b: (b, COL_KA // 2)),
    ]
    args = [y, y, y, y, y]
    for layer_caches in prev_caches:
        in_specs += [pl.BlockSpec((CTX_LEN, w), row) for w in widths]
        args += list(layer_caches)
    if n_prev:
        cache_specs = [pl.BlockSpec((None, n_prev + 1, CTX_LEN, w), lambda b: (b, 0, 0, 0)) for w in widths]
        cache_shapes = [f32(N_CTX_SETS, n_prev + 1, CTX_LEN, w) for w in widths]
    else:
        cache_specs = [pl.BlockSpec((CTX_LEN, w), row) for w in widths]
        cache_shapes = [f32(T_CTX, w) for w in widths]
    return pl.pallas_call(
        functools.partial(_ctx_attn_kernel, n_prev=n_prev),
        grid=(N_CTX_SETS,),
        in_specs=in_specs,
        out_specs=[pl.BlockSpec((CTX_LEN, 2 * wide), row)] + cache_specs,
        out_shape=[jax.ShapeDtypeStruct((T_CTX, 2 * wide), BF16)] + cache_shapes,
        compiler_params=_params(1),
        name="ctx_attention",
    )(*args)


LAT_TQ = 256
LAT_QT = LAT_LEN // LAT_TQ


def _lat_gqa_kernel(qa_ref, kava_ref, ck_ref, cv_ref, o_ref, k_s, v_s):
    @pl.when(pl.program_id(1) == 0)
    def _():
        for dst, cached, new in ((k_s, ck_ref[...], kava_ref[:, :LANES]), (v_s, cv_ref[...], kava_ref[:, LANES:])):
            for i, (c, x) in enumerate(zip(_gqa_variants(cached), _gqa_variants(new))):
                dst[i, :PAST_LEN, :] = c
                dst[i, PAST_LEN:, :] = x

    _gqa_attention(qa_ref, [k_s[i] for i in range(4)], [v_s[i] for i in range(4)], o_ref)


def _lat_gqa_attention(y, cache_k, cache_v, layer):
    wide = 4 * LANES
    first = T_CTX // LAT_TQ
    cache = pl.BlockSpec((None, None, PAST_LEN, LANES), lambda b, t: (b, layer, 0, 0))
    return pl.pallas_call(
        _lat_gqa_kernel,
        grid=(N_LAT_SETS, LAT_QT),
        in_specs=[
            pl.BlockSpec((LAT_TQ, wide), lambda b, t: (first + b * LAT_QT + t, 0)),
            pl.BlockSpec((LAT_LEN, 2 * LANES), lambda b, t: (T_CTX // LAT_LEN + b, COL_KA // 2)),
            cache, cache,
        ],
        out_specs=pl.BlockSpec((LAT_TQ, wide), lambda b, t: (b * LAT_QT + t, 0)),
        out_shape=jax.ShapeDtypeStruct((T_LAT, wide), BF16),
        scratch_shapes=[pltpu.VMEM((4, PAST_LEN + LAT_LEN, LANES), BF16)] * 2,
        compiler_params=_params(2),
        name="lat_gqa_attention",
    )(y, y, cache_k.reshape(N_LAT_SETS, DEPTH, PAST_LEN, LANES), cache_v.reshape(N_LAT_SETS, DEPTH, PAST_LEN, LANES))


N_DR = 2 * WIN_ROWS - 1
N_DC = 2 * WIN_COLS - 1
ROWS_PER_TQ = LAT_TQ // GRID_W
NA_LOCAL = 768
NA_LOCAL_BLOCKS = NA_LOCAL // LAT_TQ
NA_TOE_ROWS = 32


def _na_key_base(qt):
    return jnp.where(qt < LAT_QT // 2, 0, LAT_QT - NA_LOCAL_BLOCKS)


def _window_mask():
    r = np.arange(GRID_ROWS)
    row_start = np.clip(r - WIN_ROWS // 2, 0, GRID_ROWS - WIN_ROWS)
    in_rows = (r[None, :] >= row_start[:, None]) & (r[None, :] < row_start[:, None] + WIN_ROWS)
    cq = np.arange(GRID_W)
    col_start = np.clip(cq - WIN_COLS // 2, 0, GRID_W - WIN_COLS)
    in_cols = (cq[None, :] >= col_start[:, None]) & (cq[None, :] < col_start[:, None] + WIN_COLS)
    valid = (in_rows[:, None, :, None] & in_cols[None, :, None, :]).reshape(LAT_LEN, LAT_LEN)
    tiles = []
    for qt in range(LAT_QT):
        base = (0 if qt < LAT_QT // 2 else LAT_QT - NA_LOCAL_BLOCKS) * LAT_TQ
        tile = valid[qt * LAT_TQ:(qt + 1) * LAT_TQ]
        assert not tile[:, :base].any() and not tile[:, base + NA_LOCAL:].any()
        tiles.append(tile[:, base:base + NA_LOCAL])
    return np.where(np.stack(tiles), 0.0, NEG).astype(np.float32)


def _toeplitz_select():
    j = np.arange(LANES)
    c = np.clip(j - (GRID_W - 1), -(WIN_COLS - 1), WIN_COLS - 1) + (WIN_COLS - 1)
    return (np.arange(LANES)[:, None] == c[None, :]).astype(np.float32)


def _na_kernel(q_ref, k_ref, v_ref, ck_ref, cv_ref, mask_ref, rpb_ref, sel_ref, o_ref,
               toe_ref, bias_ref, k_s, v_s, ck_s, cv_s):
    hp, b, qt = pl.program_id(0), pl.program_id(1), pl.program_id(2)

    @pl.when((qt == 0) & (b == 0))
    def _():
        low = lax.broadcasted_iota(jnp.int32, (GRID_W, LANES), 1) < GRID_W
        sel = sel_ref[...].astype(BF16)
        for hh in range(2):
            gen = sum(_bdot(piece, sel) for piece in _split3(rpb_ref[hh]))
            for dd in range(N_DR + 1):
                lo = jnp.broadcast_to(gen[dd:dd + 1, :], (GRID_W, LANES))
                hi = jnp.broadcast_to(gen[dd + 1:dd + 2, :], (GRID_W, LANES))
                lo = pltpu.roll(lo, LANES - (GRID_W - 1), 1, stride=1, stride_axis=0)
                hi = pltpu.roll(hi, 1, 1, stride=1, stride_axis=0)
                toe_ref[hh, dd] = jnp.where(low, lo, hi)

    @pl.when(qt == 0)
    def _():
        for src, dst in ((k_ref, k_s), (v_ref, v_s), (ck_ref, ck_s), (cv_ref, cv_s)):
            lo, hi = _pair_halves(src[...])
            dst[0] = lo
            dst[1] = hi

    q = q_ref[...].astype(BF16)
    base = _na_key_base(qt)
    local = pl.ds(pl.multiple_of(base * LAT_TQ, LAT_TQ), NA_LOCAL)
    out = None
    for hh in range(2):
        for rr in range(ROWS_PER_TQ):
            for kp in range(NA_LOCAL // LANES):
                d = 2 * (kp + base * (ROWS_PER_TQ // 2)) - (qt * ROWS_PER_TQ + rr) + (WIN_ROWS - 1)
                dd = jnp.clip(d, -1, N_DR - 1) + 1
                bias_ref[hh, rr * GRID_W:(rr + 1) * GRID_W, kp * LANES:(kp + 1) * LANES] = toe_ref[hh, dd]
        bias = bias_ref[hh] + mask_ref[...]
        o = _attend(q, [k_s[hh, local, :], ck_s[hh]], [v_s[hh, local, :], cv_s[hh]], [bias, None])
        out = o if out is None else out + o
    o_ref[...] = out.astype(o_ref.dtype)


def _lat_na_attention(y, cache_k, cache_v, rpb, layer):
    first = T_CTX // LAT_TQ
    kv_row = T_CTX // LAT_LEN
    wide = N_HEADS_B * HEAD_DIM
    gen = jnp.pad(rpb[layer], ((0, 0), (1, NA_TOE_ROWS - N_DR - 1), (0, LANES - N_DC)))
    return pl.pallas_call(
        _na_kernel,
        grid=(N_HEADS_B // 2, N_LAT_SETS, LAT_QT),
        in_specs=[
            pl.BlockSpec((LAT_TQ, LANES), lambda hp, b, t: (first + b * LAT_QT + t, COL_QB + hp)),
            pl.BlockSpec((LAT_LEN, LANES), lambda hp, b, t: (kv_row + b, COL_KB + hp)),
            pl.BlockSpec((LAT_LEN, LANES), lambda hp, b, t: (kv_row + b, COL_VB + hp)),
            pl.BlockSpec((None, None, PAST_LEN, LANES), lambda hp, b, t: (b, layer, 0, hp)),
            pl.BlockSpec((None, None, PAST_LEN, LANES), lambda hp, b, t: (b, layer, 0, hp)),
            pl.BlockSpec((None, LAT_TQ, NA_LOCAL), lambda hp, b, t: (t, 0, 0)),
            pl.BlockSpec((2, NA_TOE_ROWS, LANES), lambda hp, b, t: (hp, 0, 0)),
            pl.BlockSpec((LANES, LANES), lambda hp, b, t: (0, 0)),
        ],
        out_specs=pl.BlockSpec((LAT_TQ, LANES), lambda hp, b, t: (b * LAT_QT + t, hp)),
        out_shape=jax.ShapeDtypeStruct((T_LAT, wide), BF16),
        scratch_shapes=[pltpu.VMEM((2, N_DR + 1, GRID_W, LANES), F32), pltpu.VMEM((2, LAT_TQ, NA_LOCAL), F32),
                        pltpu.VMEM((2, LAT_LEN, LANES), BF16), pltpu.VMEM((2, LAT_LEN, LANES), BF16),
                        pltpu.VMEM((2, PAST_LEN, LANES), BF16), pltpu.VMEM((2, PAST_LEN, LANES), BF16)],
        compiler_params=_params(3),
        name="lat_na_attention",
    )(y, y, y, cache_k.reshape(N_LAT_SETS, DEPTH, PAST_LEN, wide), cache_v.reshape(N_LAT_SETS, DEPTH, PAST_LEN, wide),
      jnp.asarray(_window_mask()), gen, jnp.asarray(_toeplitz_select()))


MERGE_TM = 512
MERGE_SUB = 256
N_MERGE_CTX = T_CTX // MERGE_TM


def _merge_kernel(yc_ref, yla_ref, ylb_ref, hc_ref, hl_ref, xc_ref, xl_ref, gt1_ref, sh2_ref, sc2_ref, nf_ref,
                  wa_ref, wb_ref, wo_ref, wr_ref, wg_ref, x1_ref, h2_ref, lg_ref, wab, wbb, wob, wrh, wr2, wgb):
    i = pl.program_id(0)

    @pl.when(i == 0)
    def _():
        wab[...] = wa_ref[...].astype(BF16)
        wbb[...] = wb_ref[...].astype(BF16)
        wob[...] = wo_ref[...].astype(BF16)
        wgb[...] = wg_ref[0].astype(BF16)
        hi, lo = _split2(wr_ref[...])
        wrh[...] = hi
        wr2[:, :LANES] = hi
        wr2[:, LANES:] = lo

    is_ctx = i < N_MERGE_CTX
    half = N_HEADS_A * HEAD_DIM
    for r in range(MERGE_TM // MERGE_SUB):
        rows = slice(r * MERGE_SUB, (r + 1) * MERGE_SUB)
        ya = jnp.where(is_ctx, yc_ref[rows, :half], yla_ref[rows, :])
        yb = jnp.where(is_ctx, yc_ref[rows, half:], ylb_ref[rows, :])
        h = jnp.where(is_ctx, hc_ref[rows, :], hl_ref[rows, :])
        x = jnp.where(is_ctx, xc_ref[rows, :], xl_ref[rows, :])
        gates = _bdot(h, wgb[...])
        za = _bdot(ya, wab[...])
        zb = _bdot(yb, wbb[...])
        m = jax.nn.sigmoid(gates[:, :D_MODEL]) * za + jax.nn.sigmoid(gates[:, D_MODEL:]) * zb
        x1 = x + gt1_ref[...] * _bdot(m.astype(BF16), wob[...])
        x1_ref[rows, :] = x1
        h2 = _modnorm(x1, nf_ref[...], sh2_ref[...], sc2_ref[...])
        h2_ref[rows, :] = h2.astype(BF16)
        hh, hl = _split2(h2)
        both = _bdot(hh, wr2[...])
        lg_ref[rows, :] = both[:, :LANES] + both[:, LANES:] + _bdot(hl, wrh[...])


def _merge(h_ctx, h_lat, yab_ctx, ya_lat, yb_lat, x_ctx, x_lat, mod, norm_ffn, w_ba, w_bb, w_out, w_router_pad,
           w_in, layer):
    row_fn = lambda i: jnp.where(i < N_MERGE_CTX, 0, 1 + ((i - N_MERGE_CTX) * MERGE_TM) // LAT_LEN)
    ctx_row = lambda i: (jnp.minimum(i, N_MERGE_CTX - 1), 0)
    lat_row = lambda i: (jnp.maximum(i - N_MERGE_CTX, 0), 0)
    half = N_HEADS_A * HEAD_DIM
    once = pl.Buffered(1)
    weight = lambda k: pl.BlockSpec((None, k, D_MODEL), lambda i: (layer, 0, 0), pipeline_mode=once)
    return pl.pallas_call(
        _merge_kernel,
        grid=(T_ALL // MERGE_TM,),
        in_specs=[
            pl.BlockSpec((MERGE_TM, 2 * half), ctx_row),
            pl.BlockSpec((MERGE_TM, half), lat_row),
            pl.BlockSpec((MERGE_TM, half), lat_row),
            pl.BlockSpec((MERGE_TM, D_MODEL), ctx_row),
            pl.BlockSpec((MERGE_TM, D_MODEL), lat_row),
            pl.BlockSpec((MERGE_TM, D_MODEL), ctx_row),
            pl.BlockSpec((MERGE_TM, D_MODEL), lat_row),
            _mod_spec(layer, row_fn, MOD_GT1),
            _mod_spec(layer, row_fn, MOD_SH2),
            _mod_spec(layer, row_fn, MOD_SC2),
            pl.BlockSpec((None, 1, D_MODEL), lambda i: (layer, 0, 0)),
            weight(half), weight(half), weight(D_MODEL),
            pl.BlockSpec((None, D_MODEL, LANES), lambda i: (layer, 0, 0), pipeline_mode=once),
            pl.BlockSpec((pl.Element(1), pl.Element(D_MODEL), pl.Element(2 * D_MODEL)),
                         lambda i: (layer, 0, QKV_DIM), pipeline_mode=once),
        ],
        out_specs=[
            pl.BlockSpec((MERGE_TM, D_MODEL), lambda i: (i, 0)),
            pl.BlockSpec((MERGE_TM, D_MODEL), lambda i: (i, 0)),
            pl.BlockSpec((MERGE_TM, LANES), lambda i: (i, 0)),
        ],
        out_shape=[jax.ShapeDtypeStruct((T_ALL, D_MODEL), F32), jax.ShapeDtypeStruct((T_ALL, D_MODEL), BF16),
                   jax.ShapeDtypeStruct((T_ALL, LANES), F32)],
        scratch_shapes=[pltpu.VMEM((half, D_MODEL), BF16), pltpu.VMEM((half, D_MODEL), BF16),
                        pltpu.VMEM((D_MODEL, D_MODEL), BF16), pltpu.VMEM((D_MODEL, LANES), BF16),
                        pltpu.VMEM((D_MODEL, 2 * LANES), BF16), pltpu.VMEM((D_MODEL, 2 * D_MODEL), BF16)],
        compiler_params=_params(1),
        name="merge",
    )(yab_ctx, ya_lat, yb_lat, h_ctx, h_lat, x_ctx, x_lat, mod, mod, mod, norm_ffn, w_ba, w_bb, w_out, w_router_pad,
      w_in)


GATHER_M = 512
RANK_TILE = 128


def _rank_row(aff, a_row, e, n):
    tiles = n // RANK_TILE
    sub = lax.broadcasted_iota(jnp.int32, (RANK_TILE, RANK_TILE), 0)
    lane = lax.broadcasted_iota(jnp.int32, (RANK_TILE, RANK_TILE), 1)
    earlier = jnp.where(sub < lane, 1.0, 0.0)
    acc = [jnp.zeros((8, RANK_TILE), F32) for _ in range(tiles)]
    for c in range(tiles):
        a_col = jnp.broadcast_to(aff[c * RANK_TILE:(c + 1) * RANK_TILE, e:e + 1], (RANK_TILE, RANK_TILE))
        for j in range(tiles):
            a_rj = a_row[:, j * RANK_TILE:(j + 1) * RANK_TILE]
            if c < j:
                beats = jnp.where(a_col >= a_rj, 1.0, 0.0)
            elif c > j:
                beats = jnp.where(a_col > a_rj, 1.0, 0.0)
            else:
                beats = jnp.where(a_col > a_rj, 1.0, jnp.where(a_col == a_rj, earlier, 0.0))
            acc[j] = acc[j] + beats.reshape(RANK_TILE // 8, 8, RANK_TILE).sum(axis=0)
    return jnp.concatenate([a.sum(axis=0, keepdims=True) for a in acc], axis=1)


def _route_kernel(lg_ref, h_ref, xg_ref, g_ref, rc_ref, p_ref, rt_ref, *, n, cap):
    lane = lax.broadcasted_iota(jnp.int32, (n, LANES), 1)
    lg = jnp.where(lane < N_EXPERTS, lg_ref[...], -jnp.inf)
    ex = jnp.exp(lg - lg.max(axis=-1, keepdims=True))
    aff = ex / ex.sum(axis=-1, keepdims=True)
    aff_t = aff.T
    rt_ref[...] = jnp.full((LANES, n), float(cap), F32)
    slot = lax.broadcasted_iota(jnp.int32, (cap, n), 0).astype(F32)
    for e in range(N_EXPERTS):
        rank = _rank_row(aff, aff_t[e:e + 1, :], e, n)
        rt_ref[e:e + 1, :] = jnp.minimum(rank, float(cap))
        p_ref[e * cap:(e + 1) * cap, :] = jnp.where(rank == slot, 1.0, 0.0).astype(BF16)

    h = h_ref[...]
    a1, a2, a3 = (p.astype(F32) for p in _split3(aff))
    packed = (a1 + pltpu.roll(a2, N_EXPERTS, 1) + pltpu.roll(a3, 2 * N_EXPERTS, 1)).astype(BF16)
    per = GATHER_M // cap
    glane = lax.broadcasted_iota(jnp.int32, (cap, LANES), 1)
    for grp in range(N_EXPERTS * cap // GATHER_M):
        p = p_ref[grp * GATHER_M:(grp + 1) * GATHER_M, :]
        xg = _bdot(p, h).astype(BF16)
        gg = _bdot(p, packed)
        for k in range(per):
            e = grp * per + k
            xg_ref[e] = xg[k * cap:(k + 1) * cap, :]
            mine = (glane < 3 * N_EXPERTS) & ((glane & (N_EXPERTS - 1)) == e)
            ge = jnp.where(mine, gg[k * cap:(k + 1) * cap, :], 0.0).sum(axis=-1, keepdims=True)
            g_ref[e] = jnp.broadcast_to(ge, (cap, LANES))
    rc_ref[...] = rt_ref[...].T


def _route(logits, h2, n, cap, n_sets, first_block):
    return pl.pallas_call(
        functools.partial(_route_kernel, n=n, cap=cap),
        grid=(n_sets,),
        in_specs=[
            pl.BlockSpec((n, LANES), lambda s: (first_block + s, 0)),
            pl.BlockSpec((n, D_MODEL), lambda s: (first_block + s, 0)),
        ],
        out_specs=[
            pl.BlockSpec((N_EXPERTS, cap, D_MODEL), lambda s: (0, s, 0)),
            pl.BlockSpec((N_EXPERTS, cap, LANES), lambda s: (0, s, 0)),
            pl.BlockSpec((n, LANES), lambda s: (s, 0)),
        ],
        out_shape=[jax.ShapeDtypeStruct((N_EXPERTS, n_sets * cap, D_MODEL), BF16),
                   jax.ShapeDtypeStruct((N_EXPERTS, n_sets * cap, LANES), F32),
                   jax.ShapeDtypeStruct((n_sets * n, LANES), F32)],
        scratch_shapes=[pltpu.VMEM((N_EXPERTS * cap, n), BF16), pltpu.VMEM((LANES, n), F32)],
        compiler_params=_params(1),
        name=f"route_n{n}",
    )(logits, h2)


EXPERT_TF = 1024
N_FF_TILES = EXPERT_FF // EXPERT_TF


def _expert_kernel(xc_ref, xl_ref, gc_ref, gl_ref, wg_ref, wu_ref, wd_ref, o_ref, x_s, acc_s):
    f = pl.program_id(1)
    n_ctx = xc_ref.shape[0]

    @pl.when(f == 0)
    def _():
        x_s[:n_ctx, :] = xc_ref[...]
        x_s[n_ctx:, :] = xl_ref[...]
        acc_s[...] = jnp.zeros_like(acc_s)

    x = x_s[...]
    gate = _bdot(x, wg_ref[...].astype(BF16))
    up = _bdot(x, wu_ref[...].astype(BF16))
    hid = (gate * jax.nn.sigmoid(gate)) * up
    acc_s[...] += _bdot(hid.astype(BF16), wd_ref[...].astype(BF16))

    @pl.when(f == N_FF_TILES - 1)
    def _():
        o_ref[:n_ctx, :] = (acc_s[:n_ctx, :] * gc_ref[:, :1]).astype(o_ref.dtype)
        o_ref[n_ctx:, :] = (acc_s[n_ctx:, :] * gl_ref[:, :1]).astype(o_ref.dtype)


def _experts(xg_ctx, xg_lat, g_ctx, g_lat, w_gate, w_up, w_down, layer):
    sc, sl = xg_ctx.shape[1], xg_lat.shape[1]
    slots = lambda s, w: pl.BlockSpec((None, s, w), lambda e, f: (e, 0, 0))
    return pl.pallas_call(
        _expert_kernel,
        grid=(N_EXPERTS, N_FF_TILES),
        in_specs=[
            slots(sc, D_MODEL), slots(sl, D_MODEL), slots(sc, LANES), slots(sl, LANES),
            pl.BlockSpec((None, None, D_MODEL, EXPERT_TF), lambda e, f: (layer, e, 0, f)),
            pl.BlockSpec((None, None, D_MODEL, EXPERT_TF), lambda e, f: (layer, e, 0, f)),
            pl.BlockSpec((None, None, EXPERT_TF, D_MODEL), lambda e, f: (layer, e, f, 0)),
        ],
        out_specs=slots(sc + sl, D_MODEL),
        out_shape=jax.ShapeDtypeStruct((N_EXPERTS, sc + sl, D_MODEL), BF16),
        scratch_shapes=[pltpu.VMEM((sc + sl, D_MODEL), BF16), pltpu.VMEM((sc + sl, D_MODEL), F32)],
        compiler_params=_params(2),
        name="experts",
    )(xg_ctx, xg_lat, g_ctx, g_lat, w_gate, w_up, w_down)


COMB_TM = 256


def _combine_kernel(o_ref, rc_ref, x_ref, gt2_ref, ng_ref, *rest, cap, final):
    if final:
        (y_ref,) = rest
    else:
        sh_ref, sc_ref, xn_ref, hn_ref = rest
    slots = N_EXPERTS * cap

    j = lax.broadcasted_iota(jnp.int32, (LANES, slots), 1)
    e = lax.broadcasted_iota(jnp.int32, (LANES, slots), 0)
    expand = jnp.where(j // cap == e, 1.0, 0.0).astype(BF16)
    rank = _bdot(rc_ref[...].astype(BF16), expand)
    slot = (lax.broadcasted_iota(jnp.int32, (1, slots), 1) % cap).astype(F32)
    pt = jnp.where(rank == slot, 1.0, 0.0).astype(BF16)
    ffn = _bdot(pt, o_ref[...].reshape(slots, D_MODEL))
    x = x_ref[...] + gt2_ref[...] * ffn
    if final:
        y_ref[...] = _rms(x) * ng_ref[...]
    else:
        xn_ref[...] = x
        hn_ref[...] = _modnorm(x, ng_ref[...], sh_ref[...], sc_ref[...]).astype(BF16)


def _combine(out, rc, x1, mod, norm_g, layer, n, cap, n_sets, first_slot_block, first_row_block, row_fn, final):
    tiles = n // COMB_TM
    rows = lambda s, t: (s * tiles + t, 0)
    in_specs = [
        pl.BlockSpec((N_EXPERTS, cap, D_MODEL), lambda s, t: (0, first_slot_block + s, 0)),
        pl.BlockSpec((COMB_TM, LANES), rows),
        pl.BlockSpec((COMB_TM, D_MODEL), lambda s, t: (first_row_block + s * tiles + t, 0)),
        _mod_spec(layer, row_fn, MOD_GT2),
    ]
    args = [out, rc, x1, mod]
    if final:
        in_specs.append(pl.BlockSpec((1, D_MODEL), lambda s, t: (0, 0)))
        args.append(norm_g.reshape(1, D_MODEL))
        out_specs = pl.BlockSpec((COMB_TM, D_MODEL), rows)
        out_shape = jax.ShapeDtypeStruct((n_sets * n, D_MODEL), F32)
    else:
        in_specs += [pl.BlockSpec((None, 1, D_MODEL), lambda s, t: (layer + 1, 0, 0)),
                     _mod_spec(layer + 1, row_fn, MOD_SH1), _mod_spec(layer + 1, row_fn, MOD_SC1)]
        args += [norm_g, mod, mod]
        out_specs = [pl.BlockSpec((COMB_TM, D_MODEL), rows), pl.BlockSpec((COMB_TM, D_MODEL), rows)]
        out_shape = [jax.ShapeDtypeStruct((n_sets * n, D_MODEL), F32),
                     jax.ShapeDtypeStruct((n_sets * n, D_MODEL), BF16)]
    return pl.pallas_call(
        functools.partial(_combine_kernel, cap=cap, final=final),
        grid=(n_sets, tiles),
        in_specs=in_specs,
        out_specs=out_specs,
        out_shape=out_shape,
        compiler_params=_params(2),
        name=f"combine_n{n}",
    )(*args)


def kernel(x_prompt, x_sample, cache_attn_k, cache_attn_v, cache_na_k, cache_na_v, c, c_ctx, w_ada, b_ada,
           norm_mix, norm_ffn, w_in, q_norm, k_norm, rpb, w_branch_a, w_branch_b, w_out, w_router, w_gate,
           w_up, w_down, final_norm):
    x_ctx = x_prompt.reshape(T_CTX, D_MODEL)
    x_lat = x_sample.reshape(T_LAT, D_MODEL)
    mod = _modulation(c, c_ctx, w_ada, b_ada)
    norm_mix3 = norm_mix.reshape(DEPTH, 1, D_MODEL)
    norm_ffn3 = norm_ffn.reshape(DEPTH, 1, D_MODEL)
    w_router_pad = jnp.pad(w_router, ((0, 0), (0, 0), (0, LANES - N_EXPERTS)))
    ctx_row = lambda *g: 0
    lat_comb_row = lambda s, t: 1 + s

    h_ctx = _prenorm(x_ctx, mod, norm_mix3, 0, ctx_row)
    h_lat = _prenorm(x_lat, mod, norm_mix3, 0, _lat_mod_row(NORM_TM))
    layer_caches = []
    for layer in range(DEPTH):
        last = layer == DEPTH - 1
        y = _projection(h_ctx, h_lat, w_in, q_norm, k_norm, layer)
        yab_ctx, *new_caches = _ctx_attention(y, layer_caches if last else [])
        layer_caches.append(tuple(new_caches))
        ya_lat = _lat_gqa_attention(y, cache_attn_k, cache_attn_v, layer)
        yb_lat = _lat_na_attention(y, cache_na_k, cache_na_v, rpb, layer)
        x1, h2, logits = _merge(h_ctx, h_lat, yab_ctx, ya_lat, yb_lat, x_ctx, x_lat, mod, norm_ffn3,
                                w_branch_a, w_branch_b, w_out, w_router_pad, w_in, layer)
        xg_ctx, g_ctx, rc_ctx = _route(logits, h2, CTX_LEN, CAP_CTX, N_CTX_SETS, 0)
        xg_lat, g_lat, rc_lat = _route(logits, h2, LAT_LEN, CAP_LAT, N_LAT_SETS, T_CTX // LAT_LEN)
        out = _experts(xg_ctx, xg_lat, g_ctx, g_lat, w_gate, w_up, w_down, layer)
        final = layer == DEPTH - 1
        norm_g = final_norm if final else norm_mix3
        res_ctx = _combine(out, rc_ctx, x1, mod, norm_g, layer, CTX_LEN, CAP_CTX, N_CTX_SETS, 0, 0, ctx_row, final)
        res_lat = _combine(out, rc_lat, x1, mod, norm_g, layer, LAT_LEN, CAP_LAT, N_LAT_SETS,
                           N_CTX_SETS * CAP_CTX // CAP_LAT, T_CTX // COMB_TM, lat_comb_row, final)
        if final:
            y_ctx, y_lat = res_ctx, res_lat
        else:
            (x_ctx, h_ctx), (x_lat, h_lat) = res_ctx, res_lat

    heads = (N_KV_A, N_KV_A, N_HEADS_B, N_HEADS_B)
    new_caches = [a.reshape(N_CTX_SETS, DEPTH, CTX_LEN, h, HEAD_DIM) for a, h in zip(layer_caches[-1], heads)]
    return (y_ctx.reshape(N_CTX_SETS, CTX_LEN, D_MODEL), y_lat.reshape(N_LAT_SETS, LAT_LEN, D_MODEL), *new_caches)
```

```python
import functools

import numpy as np
import jax
import jax.numpy as jnp
from jax import lax
from jax.experimental import pallas as pl
from jax.experimental.pallas import tpu as pltpu

F32 = jnp.float32
BF16 = jnp.bfloat16

D_MODEL = 1024
N_CTX_SETS, CTX_LEN = 16, 256
N_LAT_SETS, LAT_LEN = 2, 1024
T_CTX = N_CTX_SETS * CTX_LEN
T_LAT = N_LAT_SETS * LAT_LEN
T_ALL = T_CTX + T_LAT
DEPTH = 2
PAST_LEN = 512
GRID_W = 64
GRID_ROWS = LAT_LEN // GRID_W
HEAD_DIM = 64
N_HEADS_A, N_KV_A, N_HEADS_B = 8, 2, 8
WIN_ROWS, WIN_COLS = 8, 16
N_EXPERTS = 16
EXPERT_FF = 2048
CAP_CTX = 2 * CTX_LEN // N_EXPERTS
CAP_LAT = 2 * LAT_LEN // N_EXPERTS
ROPE_THETA = 10000.0
EPS = 1e-6
NEG = -1e30
QKV_DIM = 2304
ATT_SCALE = HEAD_DIM ** -0.5

LANES = 128
VMEM_LIMIT = 56 * 1024 * 1024

PROJ_TN = 256
N_PROJ_TILES = QKV_DIM // PROJ_TN
PROJ_PERM = np.array([0, 1, 8, 2, 3, 4, 5, 6, 7], np.int32)
MOD_SH1, MOD_SC1, MOD_GT1, MOD_SH2, MOD_SC2, MOD_GT2 = range(6)


def _params(n_grid_dims, vmem=VMEM_LIMIT):
    return pltpu.CompilerParams(dimension_semantics=("arbitrary",) * n_grid_dims, vmem_limit_bytes=vmem)


def _bdot(a, b):
    return jnp.dot(a, b, preferred_element_type=F32)


def _bdot_nt(a, b):
    return lax.dot_general(a, b, (((1,), (1,)), ((), ())), preferred_element_type=F32)


def _split2(x):
    hi = x.astype(BF16)
    lo = (x - hi.astype(F32)).astype(BF16)
    return hi, lo


def _split3(x):
    hi = x.astype(BF16)
    r = x - hi.astype(F32)
    mid = r.astype(BF16)
    lo = (r - mid.astype(F32)).astype(BF16)
    return hi, mid, lo


def _rms(x):
    return x * lax.rsqrt(jnp.mean(x * x, axis=-1, keepdims=True) + EPS)


def _modnorm(x, g, sh, sc):
    return (_rms(x) * g) * (1.0 + sc) + sh


def _lat_mod_row(tile_rows):
    return lambda i: 1 + (i * tile_rows) // LAT_LEN


def _mod_spec(layer, row_fn, chunk):
    return pl.BlockSpec((None, None, 1, D_MODEL), lambda *g: (layer, row_fn(*g), 0, chunk))


MOD_TN = 1536


def _mod_kernel(ct_ref, w_ref, b_ref, o_ref):
    ct = ct_ref[...]
    act = ct * jax.nn.sigmoid(ct)
    w = w_ref[...]
    for m in range(3):
        o_ref[m:m + 1, :] = jnp.sum(w * act[:, m:m + 1], axis=0, keepdims=True) + b_ref[...]
    o_ref[3:8, :] = jnp.zeros((5, MOD_TN), F32)


def _modulation(c, c_ctx, w_ada, b_ada):
    cond = jnp.concatenate([c_ctx[None, :], c, jnp.zeros((5, D_MODEL), F32)], axis=0)
    mod = pl.pallas_call(
        _mod_kernel,
        grid=(DEPTH, 6 * D_MODEL // MOD_TN),
        in_specs=[
            pl.BlockSpec((D_MODEL, 8), lambda l, j: (0, 0)),
            pl.BlockSpec((None, D_MODEL, MOD_TN), lambda l, j: (l, 0, j)),
            pl.BlockSpec((None, 1, MOD_TN), lambda l, j: (l, 0, j)),
        ],
        out_specs=pl.BlockSpec((None, 8, MOD_TN), lambda l, j: (l, 0, j)),
        out_shape=jax.ShapeDtypeStruct((DEPTH, 8, 6 * D_MODEL), F32),
        compiler_params=_params(2),
        name="modulation",
    )(cond.T, w_ada, b_ada.reshape(DEPTH, 1, 6 * D_MODEL))
    return mod.reshape(DEPTH, 8, 1, 6 * D_MODEL)


NORM_TM = 512


def _prenorm_kernel(x_ref, g_ref, sh_ref, sc_ref, h_ref):
    h_ref[...] = _modnorm(x_ref[...], g_ref[...], sh_ref[...], sc_ref[...]).astype(BF16)


def _prenorm(x, mod, norm_g, layer, row_fn):
    n = x.shape[0]
    return pl.pallas_call(
        _prenorm_kernel,
        grid=(n // NORM_TM,),
        in_specs=[
            pl.BlockSpec((NORM_TM, D_MODEL), lambda i: (i, 0)),
            pl.BlockSpec((None, 1, D_MODEL), lambda i: (layer, 0, 0)),
            _mod_spec(layer, row_fn, MOD_SH1),
            _mod_spec(layer, row_fn, MOD_SC1),
        ],
        out_specs=pl.BlockSpec((NORM_TM, D_MODEL), lambda i: (i, 0)),
        out_shape=jax.ShapeDtypeStruct((n, D_MODEL), BF16),
        compiler_params=_params(1),
        name="prenorm",
    )(x, norm_g, mod, mod)


PROJ_CH = 1024


def _rope_tables():
    t = np.arange(LAT_LEN)
    lane = np.arange(LANES) % HEAD_DIM
    pos = np.where(lane < HEAD_DIM // 2, (t // GRID_W)[:, None], (t % GRID_W)[:, None]).astype(np.float64)
    freq = ROPE_THETA ** (-(lane % 16).astype(np.float64) / 16.0)
    ang = pos * freq[None, :]
    sign = np.where((lane & 16) == 0, -1.0, 1.0)[None, :]
    return np.cos(ang).astype(np.float32), (np.sin(ang) * sign).astype(np.float32)


def _head_norm_rope(y, gain, cos, sin):
    w = y.shape[1]
    r = (lax.broadcasted_iota(jnp.int32, (2 * w, w), 0) % w) // HEAD_DIM
    c = lax.broadcasted_iota(jnp.int32, (2 * w, w), 1) // HEAD_DIM
    seg = jnp.where(r == c, 1.0 / HEAD_DIM, 0.0).astype(BF16)
    ms = _bdot(jnp.concatenate(_split2(y * y), axis=1), seg)
    yn = y * lax.rsqrt(ms + EPS) * gain
    if cos is None:
        return yn
    lane = lax.broadcasted_iota(jnp.int32, yn.shape, 1)
    partner = jnp.where((lane & 16) == 0, pltpu.roll(yn, w - 16, 1), pltpu.roll(yn, 16, 1))
    if w > LANES:
        cos = jnp.concatenate([cos] * (w // LANES), axis=1)
        sin = jnp.concatenate([sin] * (w // LANES), axis=1)
    return yn * cos + partner * sin


def _proj_kernel(perm_ref, hc_ref, hl_ref, w_ref, gain_ref, cos_ref, sin_ref, o_ref, wb_ref):
    del perm_ref
    j = pl.program_id(0)
    wb_ref[...] = w_ref[...].astype(BF16)
    chunks = [(hc_ref, k * PROJ_CH, k * PROJ_CH, False) for k in range(T_CTX // PROJ_CH)]
    chunks += [(hl_ref, k * PROJ_CH, T_CTX + k * PROJ_CH, True) for k in range(T_LAT // PROJ_CH)]

    for h_ref, r0, o0, _ in chunks:
        o_ref[o0:o0 + PROJ_CH, :] = _bdot(h_ref[r0:r0 + PROJ_CH, :], wb_ref[...])

    def finish(width):
        for _, _, o0, is_lat in chunks:
            cos, sin = (cos_ref[...], sin_ref[...]) if is_lat else (None, None)
            y = o_ref[o0:o0 + PROJ_CH, :width]
            o_ref[o0:o0 + PROJ_CH, :width] = _head_norm_rope(y, gain_ref[:, :width], cos, sin)

    pl.when(j < 2)(lambda: finish(PROJ_TN))
    pl.when(j == 2)(lambda: finish(LANES))


def _projection(h_ctx, h_lat, w_in, q_norm, k_norm, layer):
    ones = jnp.ones((2 * HEAD_DIM,), F32)
    gain = jnp.stack([jnp.tile(q_norm[layer], 4), jnp.tile(q_norm[layer], 4),
                      jnp.concatenate([jnp.tile(k_norm[layer], 2), ones])])[:, None, :]
    cos, sin = _rope_tables()
    grid_spec = pltpu.PrefetchScalarGridSpec(
        num_scalar_prefetch=1,
        grid=(N_PROJ_TILES,),
        in_specs=[
            pl.BlockSpec((T_CTX, D_MODEL), lambda j, p: (0, 0)),
            pl.BlockSpec((T_LAT, D_MODEL), lambda j, p: (0, 0)),
            pl.BlockSpec((None, D_MODEL, PROJ_TN), lambda j, p: (layer, 0, j)),
            pl.BlockSpec((None, 1, PROJ_TN), lambda j, p: (jnp.minimum(j, 2), 0, 0)),
            pl.BlockSpec((LAT_LEN, LANES), lambda j, p: (0, 0)),
            pl.BlockSpec((LAT_LEN, LANES), lambda j, p: (0, 0)),
        ],
        out_specs=pl.BlockSpec((T_ALL, PROJ_TN), lambda j, p: (0, p[j])),
        scratch_shapes=[pltpu.VMEM((D_MODEL, PROJ_TN), BF16)],
    )
    return pl.pallas_call(
        _proj_kernel,
        grid_spec=grid_spec,
        out_shape=jax.ShapeDtypeStruct((T_ALL, QKV_DIM), F32),
        compiler_params=_params(1),
        name="projection",
    )(jnp.asarray(PROJ_PERM), h_ctx, h_lat, w_in, gain, jnp.asarray(cos), jnp.asarray(sin))


COL_QA, COL_QB, COL_KB, COL_VB, COL_KA, COL_VA = 0, 4, 8, 12, 16, 17


def _lane_is_low(shape):
    return lax.broadcasted_iota(jnp.int32, shape, 1) < HEAD_DIM


def _pair_halves(x):
    low = _lane_is_low(x.shape)
    xb = x.astype(BF16)
    zero = jnp.zeros_like(xb)
    return jnp.where(low, xb, zero), jnp.where(low, zero, xb)


def _scaled_q(q):
    assert ATT_SCALE == 0.125
    return (q * ATT_SCALE).astype(BF16)


def _attend_pair(q, keys, values, biases, joint_pv=True):
    probs, dens, out = [], [], None
    for h in range(2):
        scores = []
        for k, b in zip(keys[h], biases[h]):
            s = _bdot_nt(q, k)
            scores.append(s if b is None else s + b)
        m = scores[0].max(axis=-1, keepdims=True)
        for s in scores[1:]:
            m = jnp.maximum(m, s.max(axis=-1, keepdims=True))
        den, num = None, None
        for s, v in zip(scores, values[h]):
            e = jnp.exp(s - m)
            d = e.sum(axis=-1, keepdims=True)
            den = d if den is None else den + d
            if joint_pv:
                probs.append(e.astype(BF16))
            else:
                o = _bdot(e.astype(BF16), v)
                num = o if num is None else num + o
        dens.append(den)
        if not joint_pv:
            out = num / den if out is None else out + num / den
    if not joint_pv:
        return out
    num = _bdot(jnp.concatenate(probs, axis=1), jnp.concatenate(values[0] + values[1], axis=0))
    return num / jnp.where(_lane_is_low(num.shape), dens[0], dens[1])


def _gqa_variants(x):
    lo, hi = _pair_halves(x)
    sw_lo, sw_hi = _pair_halves(pltpu.roll(x, HEAD_DIM, 1))
    return [lo, sw_hi, sw_lo, hi]


def _gqa_attention(q_ref, kvar, vvar, o_ref, joint_pv):
    for pair in range(N_HEADS_A // 2):
        q = _scaled_q(q_ref[:, pair * LANES:(pair + 1) * LANES])
        kvh = (2 * pair) // (N_HEADS_A // N_KV_A)
        out = _attend_pair(q, [[kvar[2 * kvh]], [kvar[2 * kvh + 1]]], [[vvar[2 * kvh]], [vvar[2 * kvh + 1]]],
                           [[None], [None]], joint_pv)
        o_ref[:, pair * LANES:(pair + 1) * LANES] = out.astype(o_ref.dtype)


def _ctx_attn_kernel(qa_ref, qb_ref, kb_ref, vb_ref, kava_ref, *refs, n_prev):
    prev = refs[:4 * n_prev]
    y_ref, nak_ref, nav_ref, nbk_ref, nbv_ref = refs[4 * n_prev:]
    ka = kava_ref[:, :LANES]
    va = kava_ref[:, LANES:]
    new = (ka, va, kb_ref[...], vb_ref[...])
    for c, (o_ref, val) in enumerate(zip((nak_ref, nav_ref, nbk_ref, nbv_ref), new)):
        if n_prev:
            for p in range(n_prev):
                o_ref[p] = prev[4 * p + c][...]
            o_ref[n_prev] = val
        else:
            o_ref[...] = val
    _gqa_attention(qa_ref, _gqa_variants(ka), _gqa_variants(va), y_ref.at[:, :N_HEADS_A * HEAD_DIM], joint_pv=False)
    for pair in range(N_HEADS_B // 2):
        cols = slice(pair * LANES, (pair + 1) * LANES)
        q = _scaled_q(qb_ref[:, cols])
        k_lo, k_hi = _pair_halves(kb_ref[:, cols])
        v_lo, v_hi = _pair_halves(vb_ref[:, cols])
        out = _attend_pair(q, [[k_lo], [k_hi]], [[v_lo], [v_hi]], [[None], [None]], joint_pv=False)
        y_ref[:, N_HEADS_A * HEAD_DIM + pair * LANES:N_HEADS_A * HEAD_DIM + (pair + 1) * LANES] = out.astype(BF16)


def _ctx_attention(y, prev_caches):
    wide = 4 * LANES
    widths = (LANES, LANES, wide, wide)
    n_prev = len(prev_caches)
    row = lambda b: (b, 0)
    f32 = lambda *s: jax.ShapeDtypeStruct(s, F32)
    in_specs = [
        pl.BlockSpec((CTX_LEN, wide), lambda b: (b, 0)),
        pl.BlockSpec((CTX_LEN, wide), lambda b: (b, 1)),
        pl.BlockSpec((CTX_LEN, wide), lambda b: (b, 2)),
        pl.BlockSpec((CTX_LEN, wide), lambda b: (b, 3)),
        pl.BlockSpec((CTX_LEN, 2 * LANES), lambda b: (b, COL_KA // 2)),
    ]
    args = [y, y, y, y, y]
    for layer_caches in prev_caches:
        in_specs += [pl.BlockSpec((CTX_LEN, w), row) for w in widths]
        args += list(layer_caches)
    if n_prev:
        cache_specs = [pl.BlockSpec((None, n_prev + 1, CTX_LEN, w), lambda b: (b, 0, 0, 0)) for w in widths]
        cache_shapes = [f32(N_CTX_SETS, n_prev + 1, CTX_LEN, w) for w in widths]
    else:
        cache_specs = [pl.BlockSpec((CTX_LEN, w), row) for w in widths]
        cache_shapes = [f32(T_CTX, w) for w in widths]
    return pl.pallas_call(
        functools.partial(_ctx_attn_kernel, n_prev=n_prev),
        grid=(N_CTX_SETS,),
        in_specs=in_specs,
        out_specs=[pl.BlockSpec((CTX_LEN, 2 * wide), row)] + cache_specs,
        out_shape=[jax.ShapeDtypeStruct((T_CTX, 2 * wide), BF16)] + cache_shapes,
        compiler_params=_params(1),
        name="ctx_attention",
    )(*args)


LAT_TQ = 256
LAT_QT = LAT_LEN // LAT_TQ


def _lat_gqa_kernel(qa_ref, kava_ref, ck_ref, cv_ref, o_ref, k_s, v_s):
    @pl.when(pl.program_id(1) == 0)
    def _():
        for dst, cached, new in ((k_s, ck_ref[...], kava_ref[:, :LANES]), (v_s, cv_ref[...], kava_ref[:, LANES:])):
            for i, (c, x) in enumerate(zip(_gqa_variants(cached), _gqa_variants(new))):
                dst[i, :PAST_LEN, :] = c
                dst[i, PAST_LEN:, :] = x

    _gqa_attention(qa_ref, [k_s[i] for i in range(4)], [v_s[i] for i in range(4)], o_ref, joint_pv=True)


def _lat_gqa_attention(y, cache_k, cache_v, layer):
    wide = 4 * LANES
    first = T_CTX // LAT_TQ
    cache = pl.BlockSpec((None, None, PAST_LEN, LANES), lambda b, t: (b, layer, 0, 0))
    return pl.pallas_call(
        _lat_gqa_kernel,
        grid=(N_LAT_SETS, LAT_QT),
        in_specs=[
            pl.BlockSpec((LAT_TQ, wide), lambda b, t: (first + b * LAT_QT + t, 0)),
            pl.BlockSpec((LAT_LEN, 2 * LANES), lambda b, t: (T_CTX // LAT_LEN + b, COL_KA // 2)),
            cache, cache,
        ],
        out_specs=pl.BlockSpec((LAT_TQ, wide), lambda b, t: (b * LAT_QT + t, 0)),
        out_shape=jax.ShapeDtypeStruct((T_LAT, wide), BF16),
        scratch_shapes=[pltpu.VMEM((4, PAST_LEN + LAT_LEN, LANES), BF16)] * 2,
        compiler_params=_params(2),
        name="lat_gqa_attention",
    )(y, y, cache_k.reshape(N_LAT_SETS, DEPTH, PAST_LEN, LANES), cache_v.reshape(N_LAT_SETS, DEPTH, PAST_LEN, LANES))


N_DR = 2 * WIN_ROWS - 1
N_DC = 2 * WIN_COLS - 1
ROWS_PER_TQ = LAT_TQ // GRID_W
NA_LOCAL = 768
NA_LOCAL_BLOCKS = NA_LOCAL // LAT_TQ
NA_TOE_ROWS = 32
NA_PAIRS = 2


def _na_key_base(qt):
    return jnp.where(qt < LAT_QT // 2, 0, LAT_QT - NA_LOCAL_BLOCKS)


def _window_mask():
    r = np.arange(GRID_ROWS)
    row_start = np.clip(r - WIN_ROWS // 2, 0, GRID_ROWS - WIN_ROWS)
    in_rows = (r[None, :] >= row_start[:, None]) & (r[None, :] < row_start[:, None] + WIN_ROWS)
    cq = np.arange(GRID_W)
    col_start = np.clip(cq - WIN_COLS // 2, 0, GRID_W - WIN_COLS)
    in_cols = (cq[None, :] >= col_start[:, None]) & (cq[None, :] < col_start[:, None] + WIN_COLS)
    valid = (in_rows[:, None, :, None] & in_cols[None, :, None, :]).reshape(LAT_LEN, LAT_LEN)
    tiles = []
    for qt in range(LAT_QT):
        base = (0 if qt < LAT_QT // 2 else LAT_QT - NA_LOCAL_BLOCKS) * LAT_TQ
        tile = valid[qt * LAT_TQ:(qt + 1) * LAT_TQ]
        assert not tile[:, :base].any() and not tile[:, base + NA_LOCAL:].any()
        tiles.append(tile[:, base:base + NA_LOCAL])
    return np.where(np.stack(tiles), 0.0, NEG).astype(np.float32)


def _toeplitz_select():
    j = np.arange(LANES)
    c = np.clip(j - (GRID_W - 1), -(WIN_COLS - 1), WIN_COLS - 1) + (WIN_COLS - 1)
    return (np.arange(LANES)[:, None] == c[None, :]).astype(np.float32)


def _na_kernel(q_ref, k_ref, v_ref, ck_ref, cv_ref, mask_ref, rpb_ref, sel_ref, o_ref,
               toe_ref, bias_ref, k_s, v_s, ck_s, cv_s):
    b, qt = pl.program_id(1), pl.program_id(2)
    n_heads = 2 * NA_PAIRS

    @pl.when((qt == 0) & (b == 0))
    def _():
        low = lax.broadcasted_iota(jnp.int32, (GRID_W, LANES), 1) < GRID_W
        sel = sel_ref[...].astype(BF16)
        for hd in range(n_heads):
            gen = sum(_bdot(piece, sel) for piece in _split3(rpb_ref[hd]))
            for dd in range(N_DR + 1):
                lo = jnp.broadcast_to(gen[dd:dd + 1, :], (GRID_W, LANES))
                hi = jnp.broadcast_to(gen[dd + 1:dd + 2, :], (GRID_W, LANES))
                lo = pltpu.roll(lo, LANES - (GRID_W - 1), 1, stride=1, stride_axis=0)
                hi = pltpu.roll(hi, 1, 1, stride=1, stride_axis=0)
                toe_ref[hd, dd] = jnp.where(low, lo, hi)

    @pl.when(qt == 0)
    def _():
        for src, dst in ((k_ref, k_s), (v_ref, v_s), (ck_ref, ck_s), (cv_ref, cv_s)):
            for pair in range(NA_PAIRS):
                lo, hi = _pair_halves(src[:, pair * LANES:(pair + 1) * LANES])
                dst[2 * pair] = lo
                dst[2 * pair + 1] = hi

    base = _na_key_base(qt)
    local = pl.ds(pl.multiple_of(base * LAT_TQ, LAT_TQ), NA_LOCAL)
    for pair in range(NA_PAIRS):
        q = _scaled_q(q_ref[:, pair * LANES:(pair + 1) * LANES])
        heads = (2 * pair, 2 * pair + 1)
        biases = []
        for hd in heads:
            for rr in range(ROWS_PER_TQ):
                for kp in range(NA_LOCAL // LANES):
                    d = 2 * (kp + base * (ROWS_PER_TQ // 2)) - (qt * ROWS_PER_TQ + rr) + (WIN_ROWS - 1)
                    dd = jnp.clip(d, -1, N_DR - 1) + 1
                    bias_ref[hd, rr * GRID_W:(rr + 1) * GRID_W, kp * LANES:(kp + 1) * LANES] = toe_ref[hd, dd]
            biases.append([bias_ref[hd] + mask_ref[...], None])
        out = _attend_pair(q, [[k_s[hd, local, :], ck_s[hd]] for hd in heads],
                           [[v_s[hd, local, :], cv_s[hd]] for hd in heads], biases)
        o_ref[:, pair * LANES:(pair + 1) * LANES] = out.astype(o_ref.dtype)


def _lat_na_attention(y, cache_k, cache_v, rpb, layer):
    first = T_CTX // LAT_TQ
    kv_row = T_CTX // LAT_LEN
    wide = N_HEADS_B * HEAD_DIM
    gen = jnp.pad(rpb[layer], ((0, 0), (1, NA_TOE_ROWS - N_DR - 1), (0, LANES - N_DC)))
    w = NA_PAIRS * LANES
    nh = 2 * NA_PAIRS
    cq, ck, cv = COL_QB // NA_PAIRS, COL_KB // NA_PAIRS, COL_VB // NA_PAIRS
    return pl.pallas_call(
        _na_kernel,
        grid=(N_HEADS_B // nh, N_LAT_SETS, LAT_QT),
        in_specs=[
            pl.BlockSpec((LAT_TQ, w), lambda g, b, t: (first + b * LAT_QT + t, cq + g)),
            pl.BlockSpec((LAT_LEN, w), lambda g, b, t: (kv_row + b, ck + g)),
            pl.BlockSpec((LAT_LEN, w), lambda g, b, t: (kv_row + b, cv + g)),
            pl.BlockSpec((None, None, PAST_LEN, w), lambda g, b, t: (b, layer, 0, g)),
            pl.BlockSpec((None, None, PAST_LEN, w), lambda g, b, t: (b, layer, 0, g)),
            pl.BlockSpec((None, LAT_TQ, NA_LOCAL), lambda g, b, t: (t, 0, 0)),
            pl.BlockSpec((nh, NA_TOE_ROWS, LANES), lambda g, b, t: (g, 0, 0)),
            pl.BlockSpec((LANES, LANES), lambda g, b, t: (0, 0)),
        ],
        out_specs=pl.BlockSpec((LAT_TQ, w), lambda g, b, t: (b * LAT_QT + t, g)),
        out_shape=jax.ShapeDtypeStruct((T_LAT, wide), BF16),
        scratch_shapes=[pltpu.VMEM((nh, N_DR + 1, GRID_W, LANES), F32), pltpu.VMEM((nh, LAT_TQ, NA_LOCAL), F32),
                        pltpu.VMEM((nh, LAT_LEN, LANES), BF16), pltpu.VMEM((nh, LAT_LEN, LANES), BF16),
                        pltpu.VMEM((nh, PAST_LEN, LANES), BF16), pltpu.VMEM((nh, PAST_LEN, LANES), BF16)],
        compiler_params=_params(3),
        name="lat_na_attention",
    )(y, y, y, cache_k.reshape(N_LAT_SETS, DEPTH, PAST_LEN, wide), cache_v.reshape(N_LAT_SETS, DEPTH, PAST_LEN, wide),
      jnp.asarray(_window_mask()), gen, jnp.asarray(_toeplitz_select()))


MERGE_TM = 512
MERGE_SUB = 256
N_MERGE_CTX = T_CTX // MERGE_TM


def _merge_kernel(yc_ref, yla_ref, ylb_ref, hc_ref, hl_ref, xc_ref, xl_ref, gt1_ref, sh2_ref, sc2_ref, nf_ref,
                  wa_ref, wb_ref, wo_ref, wr_ref, wg_ref, x1_ref, h2_ref, lg_ref, wab, wbb, wob, wrh, wr2, wgb):
    i = pl.program_id(0)

    @pl.when(i == 0)
    def _():
        wab[...] = wa_ref[...].astype(BF16)
        wbb[...] = wb_ref[...].astype(BF16)
        wob[...] = wo_ref[...].astype(BF16)
        wgb[...] = wg_ref[0].astype(BF16)
        hi, lo = _split2(wr_ref[...])
        wrh[...] = hi
        wr2[:, :LANES] = hi
        wr2[:, LANES:] = lo

    is_ctx = i < N_MERGE_CTX
    half = N_HEADS_A * HEAD_DIM
    for r in range(MERGE_TM // MERGE_SUB):
        rows = slice(r * MERGE_SUB, (r + 1) * MERGE_SUB)
        ya = jnp.where(is_ctx, yc_ref[rows, :half], yla_ref[rows, :])
        yb = jnp.where(is_ctx, yc_ref[rows, half:], ylb_ref[rows, :])
        h = jnp.where(is_ctx, hc_ref[rows, :], hl_ref[rows, :])
        x = jnp.where(is_ctx, xc_ref[rows, :], xl_ref[rows, :])
        gates = _bdot(h, wgb[...])
        za = _bdot(ya, wab[...])
        zb = _bdot(yb, wbb[...])
        m = jax.nn.sigmoid(gates[:, :D_MODEL]) * za + jax.nn.sigmoid(gates[:, D_MODEL:]) * zb
        x1 = x + gt1_ref[...] * _bdot(m.astype(BF16), wob[...])
        x1_ref[rows, :] = x1
        h2 = _modnorm(x1, nf_ref[...], sh2_ref[...], sc2_ref[...])
        h2_ref[rows, :] = h2.astype(BF16)
        hh, hl = _split2(h2)
        both = _bdot(hh, wr2[...])
        lg_ref[rows, :] = both[:, :LANES] + both[:, LANES:] + _bdot(hl, wrh[...])


def _merge(h_ctx, h_lat, yab_ctx, ya_lat, yb_lat, x_ctx, x_lat, mod, norm_ffn, w_ba, w_bb, w_out, w_router_pad,
           w_in, layer):
    row_fn = lambda i: jnp.where(i < N_MERGE_CTX, 0, 1 + ((i - N_MERGE_CTX) * MERGE_TM) // LAT_LEN)
    ctx_row = lambda i: (jnp.minimum(i, N_MERGE_CTX - 1), 0)
    lat_row = lambda i: (jnp.maximum(i - N_MERGE_CTX, 0), 0)
    half = N_HEADS_A * HEAD_DIM
    once = pl.Buffered(1)
    weight = lambda k: pl.BlockSpec((None, k, D_MODEL), lambda i: (layer, 0, 0), pipeline_mode=once)
    return pl.pallas_call(
        _merge_kernel,
        grid=(T_ALL // MERGE_TM,),
        in_specs=[
            pl.BlockSpec((MERGE_TM, 2 * half), ctx_row),
            pl.BlockSpec((MERGE_TM, half), lat_row),
            pl.BlockSpec((MERGE_TM, half), lat_row),
            pl.BlockSpec((MERGE_TM, D_MODEL), ctx_row),
            pl.BlockSpec((MERGE_TM, D_MODEL), lat_row),
            pl.BlockSpec((MERGE_TM, D_MODEL), ctx_row),
            pl.BlockSpec((MERGE_TM, D_MODEL), lat_row),
            _mod_spec(layer, row_fn, MOD_GT1),
            _mod_spec(layer, row_fn, MOD_SH2),
            _mod_spec(layer, row_fn, MOD_SC2),
            pl.BlockSpec((None, 1, D_MODEL), lambda i: (layer, 0, 0)),
            weight(half), weight(half), weight(D_MODEL),
            pl.BlockSpec((None, D_MODEL, LANES), lambda i: (layer, 0, 0), pipeline_mode=once),
            pl.BlockSpec((pl.Element(1), pl.Element(D_MODEL), pl.Element(2 * D_MODEL)),
                         lambda i: (layer, 0, QKV_DIM), pipeline_mode=once),
        ],
        out_specs=[
            pl.BlockSpec((MERGE_TM, D_MODEL), lambda i: (i, 0)),
            pl.BlockSpec((MERGE_TM, D_MODEL), lambda i: (i, 0)),
            pl.BlockSpec((MERGE_TM, LANES), lambda i: (i, 0)),
        ],
        out_shape=[jax.ShapeDtypeStruct((T_ALL, D_MODEL), F32), jax.ShapeDtypeStruct((T_ALL, D_MODEL), BF16),
                   jax.ShapeDtypeStruct((T_ALL, LANES), F32)],
        scratch_shapes=[pltpu.VMEM((half, D_MODEL), BF16), pltpu.VMEM((half, D_MODEL), BF16),
                        pltpu.VMEM((D_MODEL, D_MODEL), BF16), pltpu.VMEM((D_MODEL, LANES), BF16),
                        pltpu.VMEM((D_MODEL, 2 * LANES), BF16), pltpu.VMEM((D_MODEL, 2 * D_MODEL), BF16)],
        compiler_params=_params(1),
        name="merge",
    )(yab_ctx, ya_lat, yb_lat, h_ctx, h_lat, x_ctx, x_lat, mod, mod, mod, norm_ffn, w_ba, w_bb, w_out, w_router_pad,
      w_in)


GATHER_M = 512
RANK_TILE = 128


def _rank_row(aff, a_row, e, n):
    tiles = n // RANK_TILE
    sub = lax.broadcasted_iota(jnp.int32, (RANK_TILE, RANK_TILE), 0)
    lane = lax.broadcasted_iota(jnp.int32, (RANK_TILE, RANK_TILE), 1)
    earlier = jnp.where(sub < lane, 1.0, 0.0)
    acc = [jnp.zeros((8, RANK_TILE), F32) for _ in range(tiles)]
    for c in range(tiles):
        a_col = jnp.broadcast_to(aff[c * RANK_TILE:(c + 1) * RANK_TILE, e:e + 1], (RANK_TILE, RANK_TILE))
        for j in range(tiles):
            a_rj = a_row[:, j * RANK_TILE:(j + 1) * RANK_TILE]
            if c < j:
                beats = jnp.where(a_col >= a_rj, 1.0, 0.0)
            elif c > j:
                beats = jnp.where(a_col > a_rj, 1.0, 0.0)
            else:
                beats = jnp.where(a_col > a_rj, 1.0, jnp.where(a_col == a_rj, earlier, 0.0))
            acc[j] = acc[j] + beats.reshape(RANK_TILE // 8, 8, RANK_TILE).sum(axis=0)
    return jnp.concatenate([a.sum(axis=0, keepdims=True) for a in acc], axis=1)


def _route_kernel(lg_ref, h_ref, xg_ref, g_ref, rc_ref, p_ref, rt_ref, *, n, cap):
    lane = lax.broadcasted_iota(jnp.int32, (n, LANES), 1)
    lg = jnp.where(lane < N_EXPERTS, lg_ref[...], -jnp.inf)
    ex = jnp.exp(lg - lg.max(axis=-1, keepdims=True))
    aff = ex / ex.sum(axis=-1, keepdims=True)
    aff_t = aff.T
    rt_ref[...] = jnp.full((LANES, n), float(cap), F32)
    slot = lax.broadcasted_iota(jnp.int32, (cap, n), 0).astype(F32)
    for e in range(N_EXPERTS):
        rank = _rank_row(aff, aff_t[e:e + 1, :], e, n)
        rt_ref[e:e + 1, :] = jnp.minimum(rank, float(cap))
        p_ref[e * cap:(e + 1) * cap, :] = jnp.where(rank == slot, 1.0, 0.0).astype(BF16)

    h = h_ref[...]
    a1, a2, a3 = (p.astype(F32) for p in _split3(aff))
    packed = (a1 + pltpu.roll(a2, N_EXPERTS, 1) + pltpu.roll(a3, 2 * N_EXPERTS, 1)).astype(BF16)
    per = GATHER_M // cap
    glane = lax.broadcasted_iota(jnp.int32, (cap, LANES), 1)
    for grp in range(N_EXPERTS * cap // GATHER_M):
        p = p_ref[grp * GATHER_M:(grp + 1) * GATHER_M, :]
        xg = _bdot(p, h).astype(BF16)
        gg = _bdot(p, packed)
        for k in range(per):
            e = grp * per + k
            xg_ref[e] = xg[k * cap:(k + 1) * cap, :]
            mine = (glane < 3 * N_EXPERTS) & ((glane & (N_EXPERTS - 1)) == e)
            ge = jnp.where(mine, gg[k * cap:(k + 1) * cap, :], 0.0).sum(axis=-1, keepdims=True)
            g_ref[e] = jnp.broadcast_to(ge, (cap, LANES))
    rc_ref[...] = rt_ref[...].T


def _route(logits, h2, n, cap, n_sets, first_block):
    return pl.pallas_call(
        functools.partial(_route_kernel, n=n, cap=cap),
        grid=(n_sets,),
        in_specs=[
            pl.BlockSpec((n, LANES), lambda s: (first_block + s, 0)),
            pl.BlockSpec((n, D_MODEL), lambda s: (first_block + s, 0)),
        ],
        out_specs=[
            pl.BlockSpec((N_EXPERTS, cap, D_MODEL), lambda s: (0, s, 0)),
            pl.BlockSpec((N_EXPERTS, cap, LANES), lambda s: (0, s, 0)),
            pl.BlockSpec((n, LANES), lambda s: (s, 0)),
        ],
        out_shape=[jax.ShapeDtypeStruct((N_EXPERTS, n_sets * cap, D_MODEL), BF16),
                   jax.ShapeDtypeStruct((N_EXPERTS, n_sets * cap, LANES), F32),
                   jax.ShapeDtypeStruct((n_sets * n, LANES), F32)],
        scratch_shapes=[pltpu.VMEM((N_EXPERTS * cap, n), BF16), pltpu.VMEM((LANES, n), F32)],
        compiler_params=_params(1),
        name=f"route_n{n}",
    )(logits, h2)


EXPERT_TF = 1024
N_FF_TILES = EXPERT_FF // EXPERT_TF


def _expert_kernel(xc_ref, xl_ref, gc_ref, gl_ref, wg_ref, wu_ref, wd_ref, o_ref, x_s, acc_s):
    f = pl.program_id(1)
    n_ctx = xc_ref.shape[0]

    @pl.when(f == 0)
    def _():
        x_s[:n_ctx, :] = xc_ref[...]
        x_s[n_ctx:, :] = xl_ref[...]
        acc_s[...] = jnp.zeros_like(acc_s)

    x = x_s[...]
    gate = _bdot(x, wg_ref[...].astype(BF16))
    up = _bdot(x, wu_ref[...].astype(BF16))
    hid = (gate * jax.nn.sigmoid(gate)) * up
    acc_s[...] += _bdot(hid.astype(BF16), wd_ref[...].astype(BF16))

    @pl.when(f == N_FF_TILES - 1)
    def _():
        o_ref[:n_ctx, :] = (acc_s[:n_ctx, :] * gc_ref[:, :1]).astype(o_ref.dtype)
        o_ref[n_ctx:, :] = (acc_s[n_ctx:, :] * gl_ref[:, :1]).astype(o_ref.dtype)


def _experts(xg_ctx, xg_lat, g_ctx, g_lat, w_gate, w_up, w_down, layer):
    sc, sl = xg_ctx.shape[1], xg_lat.shape[1]
    slots = lambda s, w: pl.BlockSpec((None, s, w), lambda e, f: (e, 0, 0))
    return pl.pallas_call(
        _expert_kernel,
        grid=(N_EXPERTS, N_FF_TILES),
        in_specs=[
            slots(sc, D_MODEL), slots(sl, D_MODEL), slots(sc, LANES), slots(sl, LANES),
            pl.BlockSpec((None, None, D_MODEL, EXPERT_TF), lambda e, f: (layer, e, 0, f)),
            pl.BlockSpec((None, None, D_MODEL, EXPERT_TF), lambda e, f: (layer, e, 0, f)),
            pl.BlockSpec((None, None, EXPERT_TF, D_MODEL), lambda e, f: (layer, e, f, 0)),
        ],
        out_specs=slots(sc + sl, D_MODEL),
        out_shape=jax.ShapeDtypeStruct((N_EXPERTS, sc + sl, D_MODEL), BF16),
        scratch_shapes=[pltpu.VMEM((sc + sl, D_MODEL), BF16), pltpu.VMEM((sc + sl, D_MODEL), F32)],
        compiler_params=_params(2),
        name="experts",
    )(xg_ctx, xg_lat, g_ctx, g_lat, w_gate, w_up, w_down)


COMB_TM = 256


def _combine_kernel(o_ref, rc_ref, x_ref, gt2_ref, ng_ref, *rest, cap, final):
    if final:
        (y_ref,) = rest
    else:
        sh_ref, sc_ref, xn_ref, hn_ref = rest
    slots = N_EXPERTS * cap

    j = lax.broadcasted_iota(jnp.int32, (LANES, slots), 1)
    e = lax.broadcasted_iota(jnp.int32, (LANES, slots), 0)
    expand = jnp.where(j // cap == e, 1.0, 0.0).astype(BF16)
    rank = _bdot(rc_ref[...].astype(BF16), expand)
    slot = (lax.broadcasted_iota(jnp.int32, (1, slots), 1) % cap).astype(F32)
    pt = jnp.where(rank == slot, 1.0, 0.0).astype(BF16)
    ffn = _bdot(pt, o_ref[...].reshape(slots, D_MODEL))
    x = x_ref[...] + gt2_ref[...] * ffn
    if final:
        y_ref[...] = _rms(x) * ng_ref[...]
    else:
        xn_ref[...] = x
        hn_ref[...] = _modnorm(x, ng_ref[...], sh_ref[...], sc_ref[...]).astype(BF16)


def _combine(out, rc, x1, mod, norm_g, layer, n, cap, n_sets, first_slot_block, first_row_block, row_fn, final):
    tiles = n // COMB_TM
    rows = lambda s, t: (s * tiles + t, 0)
    in_specs = [
        pl.BlockSpec((N_EXPERTS, cap, D_MODEL), lambda s, t: (0, first_slot_block + s, 0)),
        pl.BlockSpec((COMB_TM, LANES), rows),
        pl.BlockSpec((COMB_TM, D_MODEL), lambda s, t: (first_row_block + s * tiles + t, 0)),
        _mod_spec(layer, row_fn, MOD_GT2),
    ]
    args = [out, rc, x1, mod]
    if final:
        in_specs.append(pl.BlockSpec((1, D_MODEL), lambda s, t: (0, 0)))
        args.append(norm_g.reshape(1, D_MODEL))
        out_specs = pl.BlockSpec((COMB_TM, D_MODEL), rows)
        out_shape = jax.ShapeDtypeStruct((n_sets * n, D_MODEL), F32)
    else:
        in_specs += [pl.BlockSpec((None, 1, D_MODEL), lambda s, t: (layer + 1, 0, 0)),
                     _mod_spec(layer + 1, row_fn, MOD_SH1), _mod_spec(layer + 1, row_fn, MOD_SC1)]
        args += [norm_g, mod, mod]
        out_specs = [pl.BlockSpec((COMB_TM, D_MODEL), rows), pl.BlockSpec((COMB_TM, D_MODEL), rows)]
        out_shape = [jax.ShapeDtypeStruct((n_sets * n, D_MODEL), F32),
                     jax.ShapeDtypeStruct((n_sets * n, D_MODEL), BF16)]
    return pl.pallas_call(
        functools.partial(_combine_kernel, cap=cap, final=final),
        grid=(n_sets, tiles),
        in_specs=in_specs,
        out_specs=out_specs,
        out_shape=out_shape,
        compiler_params=_params(2),
        name=f"combine_n{n}",
    )(*args)


def kernel(x_prompt, x_sample, cache_attn_k, cache_attn_v, cache_na_k, cache_na_v, c, c_ctx, w_ada, b_ada,
           norm_mix, norm_ffn, w_in, q_norm, k_norm, rpb, w_branch_a, w_branch_b, w_out, w_router, w_gate,
           w_up, w_down, final_norm):
    x_ctx = x_prompt.reshape(T_CTX, D_MODEL)
    x_lat = x_sample.reshape(T_LAT, D_MODEL)
    mod = _modulation(c, c_ctx, w_ada, b_ada)
    norm_mix3 = norm_mix.reshape(DEPTH, 1, D_MODEL)
    norm_ffn3 = norm_ffn.reshape(DEPTH, 1, D_MODEL)
    w_router_pad = jnp.pad(w_router, ((0, 0), (0, 0), (0, LANES - N_EXPERTS)))
    ctx_row = lambda *g: 0
    lat_comb_row = lambda s, t: 1 + s

    h_ctx = _prenorm(x_ctx, mod, norm_mix3, 0, ctx_row)
    h_lat = _prenorm(x_lat, mod, norm_mix3, 0, _lat_mod_row(NORM_TM))
    layer_caches = []
    for layer in range(DEPTH):
        last = layer == DEPTH - 1
        y = _projection(h_ctx, h_lat, w_in, q_norm, k_norm, layer)
        yab_ctx, *new_caches = _ctx_attention(y, layer_caches if last else [])
        layer_caches.append(tuple(new_caches))
        ya_lat = _lat_gqa_attention(y, cache_attn_k, cache_attn_v, layer)
        yb_lat = _lat_na_attention(y, cache_na_k, cache_na_v, rpb, layer)
        x1, h2, logits = _merge(h_ctx, h_lat, yab_ctx, ya_lat, yb_lat, x_ctx, x_lat, mod, norm_ffn3,
                                w_branch_a, w_branch_b, w_out, w_router_pad, w_in, layer)
        xg_ctx, g_ctx, rc_ctx = _route(logits, h2, CTX_LEN, CAP_CTX, N_CTX_SETS, 0)
        xg_lat, g_lat, rc_lat = _route(logits, h2, LAT_LEN, CAP_LAT, N_LAT_SETS, T_CTX // LAT_LEN)
        out = _experts(xg_ctx, xg_lat, g_ctx, g_lat, w_gate, w_up, w_down, layer)
        final = layer == DEPTH - 1
        norm_g = final_norm if final else norm_mix3
        res_ctx = _combine(out, rc_ctx, x1, mod, norm_g, layer, CTX_LEN, CAP_CTX, N_CTX_SETS, 0, 0, ctx_row, final)
        res_lat = _combine(out, rc_lat, x1, mod, norm_g, layer, LAT_LEN, CAP_LAT, N_LAT_SETS,
                           N_CTX_SETS * CAP_CTX // CAP_LAT, T_CTX // COMB_TM, lat_comb_row, final)
        if final:
            y_ctx, y_lat = res_ctx, res_lat
        else:
            (x_ctx, h_ctx), (x_lat, h_lat) = res_ctx, res_lat

    heads = (N_KV_A, N_KV_A, N_HEADS_B, N_HEADS_B)
    new_caches = [a.reshape(N_CTX_SETS, DEPTH, CTX_LEN, h, HEAD_DIM) for a, h in zip(layer_caches[-1], heads)]
    return (y_ctx.reshape(N_CTX_SETS, CTX_LEN, D_MODEL), y_lat.reshape(N_LAT_SETS, LAT_LEN, D_MODEL), *new_caches)
```

```python
import functools

import numpy as np
import jax
import jax.numpy as jnp
from jax import lax
from jax.experimental import pallas as pl
from jax.experimental.pallas import tpu as pltpu

F32 = jnp.float32
BF16 = jnp.bfloat16

D_MODEL = 1024
N_CTX_SETS, CTX_LEN = 16, 256
N_LAT_SETS, LAT_LEN = 2, 1024
T_CTX = N_CTX_SETS * CTX_LEN
T_LAT = N_LAT_SETS * LAT_LEN
T_ALL = T_CTX + T_LAT
DEPTH = 2
PAST_LEN = 512
GRID_W = 64
GRID_ROWS = LAT_LEN // GRID_W
HEAD_DIM = 64
N_HEADS_A, N_KV_A, N_HEADS_B = 8, 2, 8
WIN_ROWS, WIN_COLS = 8, 16
N_EXPERTS = 16
EXPERT_FF = 2048
CAP_CTX = 2 * CTX_LEN // N_EXPERTS
CAP_LAT = 2 * LAT_LEN // N_EXPERTS
ROPE_THETA = 10000.0
EPS = 1e-6
NEG = -1e30
QKV_DIM = 2304
ATT_SCALE = HEAD_DIM ** -0.5

LANES = 128
VMEM_LIMIT = 56 * 1024 * 1024

PROJ_TN = 256
N_PROJ_TILES = QKV_DIM // PROJ_TN
PROJ_PERM = np.array([0, 1, 8, 2, 3, 4, 5, 6, 7], np.int32)
MOD_SH1, MOD_SC1, MOD_GT1, MOD_SH2, MOD_SC2, MOD_GT2 = range(6)


def _params(n_grid_dims, vmem=VMEM_LIMIT):
    return pltpu.CompilerParams(dimension_semantics=("arbitrary",) * n_grid_dims, vmem_limit_bytes=vmem)


def _bdot(a, b):
    return jnp.dot(a, b, preferred_element_type=F32)


def _bdot_nt(a, b):
    return lax.dot_general(a, b, (((1,), (1,)), ((), ())), preferred_element_type=F32)


def _split2(x):
    hi = x.astype(BF16)
    lo = (x - hi.astype(F32)).astype(BF16)
    return hi, lo


def _split3(x):
    hi = x.astype(BF16)
    r = x - hi.astype(F32)
    mid = r.astype(BF16)
    lo = (r - mid.astype(F32)).astype(BF16)
    return hi, mid, lo


def _rms(x):
    return x * lax.rsqrt(jnp.mean(x * x, axis=-1, keepdims=True) + EPS)


def _modnorm(x, g, sh, sc):
    return (_rms(x) * g) * (1.0 + sc) + sh


def _lat_mod_row(tile_rows):
    return lambda i: 1 + (i * tile_rows) // LAT_LEN


def _mod_spec(layer, row_fn, chunk):
    return pl.BlockSpec((None, None, 1, D_MODEL), lambda *g: (layer, row_fn(*g), 0, chunk))


MOD_TN = 768


def _mod_kernel(ct_ref, w_ref, b_ref, o_ref):
    ct = ct_ref[...]
    act = ct * jax.nn.sigmoid(ct)
    w = w_ref[...]
    for m in range(3):
        o_ref[m:m + 1, :] = jnp.sum(w * act[:, m:m + 1], axis=0, keepdims=True) + b_ref[...]
    o_ref[3:8, :] = jnp.zeros((5, MOD_TN), F32)


def _modulation(c, c_ctx, w_ada, b_ada):
    cond = jnp.concatenate([c_ctx[None, :], c, jnp.zeros((5, D_MODEL), F32)], axis=0)
    mod = pl.pallas_call(
        _mod_kernel,
        grid=(DEPTH, 6 * D_MODEL // MOD_TN),
        in_specs=[
            pl.BlockSpec((D_MODEL, 8), lambda l, j: (0, 0)),
            pl.BlockSpec((None, D_MODEL, MOD_TN), lambda l, j: (l, 0, j)),
            pl.BlockSpec((None, 1, MOD_TN), lambda l, j: (l, 0, j)),
        ],
        out_specs=pl.BlockSpec((None, 8, MOD_TN), lambda l, j: (l, 0, j)),
        out_shape=jax.ShapeDtypeStruct((DEPTH, 8, 6 * D_MODEL), F32),
        compiler_params=_params(2),
        name="modulation",
    )(cond.T, w_ada, b_ada.reshape(DEPTH, 1, 6 * D_MODEL))
    return mod.reshape(DEPTH, 8, 1, 6 * D_MODEL)


NORM_TM = 512


def _prenorm_kernel(x_ref, g_ref, sh_ref, sc_ref, h_ref):
    h_ref[...] = _modnorm(x_ref[...], g_ref[...], sh_ref[...], sc_ref[...]).astype(BF16)


def _prenorm(x, mod, norm_g, layer, row_fn):
    n = x.shape[0]
    return pl.pallas_call(
        _prenorm_kernel,
        grid=(n // NORM_TM,),
        in_specs=[
            pl.BlockSpec((NORM_TM, D_MODEL), lambda i: (i, 0)),
            pl.BlockSpec((None, 1, D_MODEL), lambda i: (layer, 0, 0)),
            _mod_spec(layer, row_fn, MOD_SH1),
            _mod_spec(layer, row_fn, MOD_SC1),
        ],
        out_specs=pl.BlockSpec((NORM_TM, D_MODEL), lambda i: (i, 0)),
        out_shape=jax.ShapeDtypeStruct((n, D_MODEL), BF16),
        compiler_params=_params(1),
        name="prenorm",
    )(x, norm_g, mod, mod)


PROJ_CH = 1024


def _rope_tables():
    t = np.arange(LAT_LEN)
    lane = np.arange(LANES) % HEAD_DIM
    pos = np.where(lane < HEAD_DIM // 2, (t // GRID_W)[:, None], (t % GRID_W)[:, None]).astype(np.float64)
    freq = ROPE_THETA ** (-(lane % 16).astype(np.float64) / 16.0)
    ang = pos * freq[None, :]
    sign = np.where((lane & 16) == 0, -1.0, 1.0)[None, :]
    return np.cos(ang).astype(np.float32), (np.sin(ang) * sign).astype(np.float32)


def _head_norm_rope(y, gain, cos, sin):
    w = y.shape[1]
    r = (lax.broadcasted_iota(jnp.int32, (2 * w, w), 0) % w) // HEAD_DIM
    c = lax.broadcasted_iota(jnp.int32, (2 * w, w), 1) // HEAD_DIM
    seg = jnp.where(r == c, 1.0 / HEAD_DIM, 0.0).astype(BF16)
    ms = _bdot(jnp.concatenate(_split2(y * y), axis=1), seg)
    yn = y * lax.rsqrt(ms + EPS) * gain
    if cos is None:
        return yn
    lane = lax.broadcasted_iota(jnp.int32, yn.shape, 1)
    partner = jnp.where((lane & 16) == 0, pltpu.roll(yn, w - 16, 1), pltpu.roll(yn, 16, 1))
    if w > LANES:
        cos = jnp.concatenate([cos] * (w // LANES), axis=1)
        sin = jnp.concatenate([sin] * (w // LANES), axis=1)
    return yn * cos + partner * sin


def _proj_kernel(perm_ref, hc_ref, hl_ref, w_ref, gain_ref, cos_ref, sin_ref, o_ref, wb_ref):
    del perm_ref
    j = pl.program_id(0)
    wb_ref[...] = w_ref[...].astype(BF16)
    chunks = [(hc_ref, k * PROJ_CH, k * PROJ_CH, False) for k in range(T_CTX // PROJ_CH)]
    chunks += [(hl_ref, k * PROJ_CH, T_CTX + k * PROJ_CH, True) for k in range(T_LAT // PROJ_CH)]

    for h_ref, r0, o0, _ in chunks:
        o_ref[o0:o0 + PROJ_CH, :] = _bdot(h_ref[r0:r0 + PROJ_CH, :], wb_ref[...])

    def finish(width):
        for _, _, o0, is_lat in chunks:
            cos, sin = (cos_ref[...], sin_ref[...]) if is_lat else (None, None)
            y = o_ref[o0:o0 + PROJ_CH, :width]
            o_ref[o0:o0 + PROJ_CH, :width] = _head_norm_rope(y, gain_ref[:, :width], cos, sin)

    pl.when(j < 2)(lambda: finish(PROJ_TN))
    pl.when(j == 2)(lambda: finish(LANES))


def _projection(h_ctx, h_lat, w_in, q_norm, k_norm, layer):
    ones = jnp.ones((2 * HEAD_DIM,), F32)
    gain = jnp.stack([jnp.tile(q_norm[layer], 4), jnp.tile(q_norm[layer], 4),
                      jnp.concatenate([jnp.tile(k_norm[layer], 2), ones])])[:, None, :]
    cos, sin = _rope_tables()
    grid_spec = pltpu.PrefetchScalarGridSpec(
        num_scalar_prefetch=1,
        grid=(N_PROJ_TILES,),
        in_specs=[
            pl.BlockSpec((T_CTX, D_MODEL), lambda j, p: (0, 0)),
            pl.BlockSpec((T_LAT, D_MODEL), lambda j, p: (0, 0)),
            pl.BlockSpec((None, D_MODEL, PROJ_TN), lambda j, p: (layer, 0, j)),
            pl.BlockSpec((None, 1, PROJ_TN), lambda j, p: (jnp.minimum(j, 2), 0, 0)),
            pl.BlockSpec((LAT_LEN, LANES), lambda j, p: (0, 0)),
            pl.BlockSpec((LAT_LEN, LANES), lambda j, p: (0, 0)),
        ],
        out_specs=pl.BlockSpec((T_ALL, PROJ_TN), lambda j, p: (0, p[j])),
        scratch_shapes=[pltpu.VMEM((D_MODEL, PROJ_TN), BF16)],
    )
    return pl.pallas_call(
        _proj_kernel,
        grid_spec=grid_spec,
        out_shape=jax.ShapeDtypeStruct((T_ALL, QKV_DIM), F32),
        compiler_params=_params(1),
        name="projection",
    )(jnp.asarray(PROJ_PERM), h_ctx, h_lat, w_in, gain, jnp.asarray(cos), jnp.asarray(sin))


COL_QA, COL_QB, COL_KB, COL_VB, COL_KA, COL_VA = 0, 4, 8, 12, 16, 17


def _lane_is_low(shape):
    return lax.broadcasted_iota(jnp.int32, shape, 1) < HEAD_DIM


def _pair_halves(x):
    low = _lane_is_low(x.shape)
    xb = x.astype(BF16)
    zero = jnp.zeros_like(xb)
    return jnp.where(low, xb, zero), jnp.where(low, zero, xb)


def _scaled_q(q):
    assert ATT_SCALE == 0.125
    return (q * ATT_SCALE).astype(BF16)


def _attend_pair(q, keys, values, biases, joint_pv=True):
    probs, dens, out = [], [], None
    for h in range(2):
        scores = []
        for k, b in zip(keys[h], biases[h]):
            s = _bdot_nt(q, k)
            scores.append(s if b is None else s + b)
        m = scores[0].max(axis=-1, keepdims=True)
        for s in scores[1:]:
            m = jnp.maximum(m, s.max(axis=-1, keepdims=True))
        den, num = None, None
        for s, v in zip(scores, values[h]):
            e = jnp.exp(s - m)
            d = e.sum(axis=-1, keepdims=True)
            den = d if den is None else den + d
            if joint_pv:
                probs.append(e.astype(BF16))
            else:
                o = _bdot(e.astype(BF16), v)
                num = o if num is None else num + o
        dens.append(den)
        if not joint_pv:
            out = num / den if out is None else out + num / den
    if not joint_pv:
        return out
    num = _bdot(jnp.concatenate(probs, axis=1), jnp.concatenate(values[0] + values[1], axis=0))
    return num / jnp.where(_lane_is_low(num.shape), dens[0], dens[1])


def _gqa_variants(x):
    lo, hi = _pair_halves(x)
    sw_lo, sw_hi = _pair_halves(pltpu.roll(x, HEAD_DIM, 1))
    return [lo, sw_hi, sw_lo, hi]


def _gqa_attention(q_ref, kvar, vvar, o_ref, joint_pv):
    for pair in range(N_HEADS_A // 2):
        q = _scaled_q(q_ref[:, pair * LANES:(pair + 1) * LANES])
        kvh = (2 * pair) // (N_HEADS_A // N_KV_A)
        out = _attend_pair(q, [[kvar[2 * kvh]], [kvar[2 * kvh + 1]]], [[vvar[2 * kvh]], [vvar[2 * kvh + 1]]],
                           [[None], [None]], joint_pv)
        o_ref[:, pair * LANES:(pair + 1) * LANES] = out.astype(o_ref.dtype)


def _ctx_attn_kernel(qa_ref, qb_ref, kb_ref, vb_ref, kava_ref, *refs, n_prev):
    prev = refs[:4 * n_prev]
    y_ref, nak_ref, nav_ref, nbk_ref, nbv_ref = refs[4 * n_prev:]
    ka = kava_ref[:, :LANES]
    va = kava_ref[:, LANES:]
    new = (ka, va, kb_ref[...], vb_ref[...])
    for c, (o_ref, val) in enumerate(zip((nak_ref, nav_ref, nbk_ref, nbv_ref), new)):
        if n_prev:
            for p in range(n_prev):
                o_ref[p] = prev[4 * p + c][...]
            o_ref[n_prev] = val
        else:
            o_ref[...] = val
    _gqa_attention(qa_ref, _gqa_variants(ka), _gqa_variants(va), y_ref.at[:, :N_HEADS_A * HEAD_DIM], joint_pv=False)
    for pair in range(N_HEADS_B // 2):
        cols = slice(pair * LANES, (pair + 1) * LANES)
        q = _scaled_q(qb_ref[:, cols])
        k_lo, k_hi = _pair_halves(kb_ref[:, cols])
        v_lo, v_hi = _pair_halves(vb_ref[:, cols])
        out = _attend_pair(q, [[k_lo], [k_hi]], [[v_lo], [v_hi]], [[None], [None]], joint_pv=False)
        y_ref[:, N_HEADS_A * HEAD_DIM + pair * LANES:N_HEADS_A * HEAD_DIM + (pair + 1) * LANES] = out.astype(BF16)


def _ctx_attention(y, prev_caches):
    wide = 4 * LANES
    widths = (LANES, LANES, wide, wide)
    n_prev = len(prev_caches)
    row = lambda b: (b, 0)
    f32 = lambda *s: jax.ShapeDtypeStruct(s, F32)
    in_specs = [
        pl.BlockSpec((CTX_LEN, wide), lambda b: (b, 0)),
        pl.BlockSpec((CTX_LEN, wide), lambda b: (b, 1)),
        pl.BlockSpec((CTX_LEN, wide), lambda b: (b, 2)),
        pl.BlockSpec((CTX_LEN, wide), lambda b: (b, 3)),
        pl.BlockSpec((CTX_LEN, 2 * LANES), lambda b: (b, COL_KA // 2)),
    ]
    args = [y, y, y, y, y]
    for layer_caches in prev_caches:
        in_specs += [pl.BlockSpec((CTX_LEN, w), row) for w in widths]
        args += list(layer_caches)
    if n_prev:
        cache_specs = [pl.BlockSpec((None, n_prev + 1, CTX_LEN, w), lambda b: (b, 0, 0, 0)) for w in widths]
        cache_shapes = [f32(N_CTX_SETS, n_prev + 1, CTX_LEN, w) for w in widths]
    else:
        cache_specs = [pl.BlockSpec((CTX_LEN, w), row) for w in widths]
        cache_shapes = [f32(T_CTX, w) for w in widths]
    return pl.pallas_call(
        functools.partial(_ctx_attn_kernel, n_prev=n_prev),
        grid=(N_CTX_SETS,),
        in_specs=in_specs,
        out_specs=[pl.BlockSpec((CTX_LEN, 2 * wide), row)] + cache_specs,
        out_shape=[jax.ShapeDtypeStruct((T_CTX, 2 * wide), BF16)] + cache_shapes,
        compiler_params=_params(1),
        name="ctx_attention",
    )(*args)


LAT_TQ = 256
LAT_QT = LAT_LEN // LAT_TQ


def _lat_gqa_kernel(qa_ref, kava_ref, ck_ref, cv_ref, o_ref, k_s, v_s):
    @pl.when(pl.program_id(1) == 0)
    def _():
        for dst, cached, new in ((k_s, ck_ref[...], kava_ref[:, :LANES]), (v_s, cv_ref[...], kava_ref[:, LANES:])):
            for i, (c, x) in enumerate(zip(_gqa_variants(cached), _gqa_variants(new))):
                dst[i, :PAST_LEN, :] = c
                dst[i, PAST_LEN:, :] = x

    _gqa_attention(qa_ref, [k_s[i] for i in range(4)], [v_s[i] for i in range(4)], o_ref, joint_pv=True)


def _lat_gqa_attention(y, cache_k, cache_v, layer):
    wide = 4 * LANES
    first = T_CTX // LAT_TQ
    cache = pl.BlockSpec((None, None, PAST_LEN, LANES), lambda b, t: (b, layer, 0, 0))
    return pl.pallas_call(
        _lat_gqa_kernel,
        grid=(N_LAT_SETS, LAT_QT),
        in_specs=[
            pl.BlockSpec((LAT_TQ, wide), lambda b, t: (first + b * LAT_QT + t, 0)),
            pl.BlockSpec((LAT_LEN, 2 * LANES), lambda b, t: (T_CTX // LAT_LEN + b, COL_KA // 2)),
            cache, cache,
        ],
        out_specs=pl.BlockSpec((LAT_TQ, wide), lambda b, t: (b * LAT_QT + t, 0)),
        out_shape=jax.ShapeDtypeStruct((T_LAT, wide), BF16),
        scratch_shapes=[pltpu.VMEM((4, PAST_LEN + LAT_LEN, LANES), BF16)] * 2,
        compiler_params=_params(2),
        name="lat_gqa_attention",
    )(y, y, cache_k.reshape(N_LAT_SETS, DEPTH, PAST_LEN, LANES), cache_v.reshape(N_LAT_SETS, DEPTH, PAST_LEN, LANES))


N_DR = 2 * WIN_ROWS - 1
N_DC = 2 * WIN_COLS - 1
ROWS_PER_TQ = LAT_TQ // GRID_W
NA_LOCAL = 768
NA_LOCAL_BLOCKS = NA_LOCAL // LAT_TQ
NA_TOE_ROWS = 32
NA_PAIRS = 2


def _na_key_base(qt):
    return jnp.where(qt < LAT_QT // 2, 0, LAT_QT - NA_LOCAL_BLOCKS)


def _window_mask():
    r = np.arange(GRID_ROWS)
    row_start = np.clip(r - WIN_ROWS // 2, 0, GRID_ROWS - WIN_ROWS)
    in_rows = (r[None, :] >= row_start[:, None]) & (r[None, :] < row_start[:, None] + WIN_ROWS)
    cq = np.arange(GRID_W)
    col_start = np.clip(cq - WIN_COLS // 2, 0, GRID_W - WIN_COLS)
    in_cols = (cq[None, :] >= col_start[:, None]) & (cq[None, :] < col_start[:, None] + WIN_COLS)
    valid = (in_rows[:, None, :, None] & in_cols[None, :, None, :]).reshape(LAT_LEN, LAT_LEN)
    tiles = []
    for qt in range(LAT_QT):
        base = (0 if qt < LAT_QT // 2 else LAT_QT - NA_LOCAL_BLOCKS) * LAT_TQ
        tile = valid[qt * LAT_TQ:(qt + 1) * LAT_TQ]
        assert not tile[:, :base].any() and not tile[:, base + NA_LOCAL:].any()
        tiles.append(tile[:, base:base + NA_LOCAL])
    return np.where(np.stack(tiles), 0.0, NEG).astype(np.float32)


def _toeplitz_select():
    j = np.arange(LANES)
    c = np.clip(j - (GRID_W - 1), -(WIN_COLS - 1), WIN_COLS - 1) + (WIN_COLS - 1)
    return (np.arange(LANES)[:, None] == c[None, :]).astype(np.float32)


def _na_kernel(q_ref, k_ref, v_ref, ck_ref, cv_ref, mask_ref, rpb_ref, sel_ref, o_ref,
               toe_ref, bias_ref, k_s, v_s, ck_s, cv_s):
    b, qt = pl.program_id(1), pl.program_id(2)
    n_heads = 2 * NA_PAIRS

    @pl.when((qt == 0) & (b == 0))
    def _():
        low = lax.broadcasted_iota(jnp.int32, (GRID_W, LANES), 1) < GRID_W
        sel = sel_ref[...].astype(BF16)
        for hd in range(n_heads):
            gen = sum(_bdot(piece, sel) for piece in _split3(rpb_ref[hd]))
            for dd in range(N_DR + 1):
                lo = jnp.broadcast_to(gen[dd:dd + 1, :], (GRID_W, LANES))
                hi = jnp.broadcast_to(gen[dd + 1:dd + 2, :], (GRID_W, LANES))
                lo = pltpu.roll(lo, LANES - (GRID_W - 1), 1, stride=1, stride_axis=0)
                hi = pltpu.roll(hi, 1, 1, stride=1, stride_axis=0)
                toe_ref[hd, dd] = jnp.where(low, lo, hi)

    @pl.when(qt == 0)
    def _():
        for src, dst in ((k_ref, k_s), (v_ref, v_s), (ck_ref, ck_s), (cv_ref, cv_s)):
            for pair in range(NA_PAIRS):
                lo, hi = _pair_halves(src[:, pair * LANES:(pair + 1) * LANES])
                dst[2 * pair] = lo
                dst[2 * pair + 1] = hi

    base = _na_key_base(qt)
    local = pl.ds(pl.multiple_of(base * LAT_TQ, LAT_TQ), NA_LOCAL)
    for pair in range(NA_PAIRS):
        q = _scaled_q(q_ref[:, pair * LANES:(pair + 1) * LANES])
        heads = (2 * pair, 2 * pair + 1)
        biases = []
        for hd in heads:
            for rr in range(ROWS_PER_TQ):
                for kp in range(NA_LOCAL // LANES):
                    d = 2 * (kp + base * (ROWS_PER_TQ // 2)) - (qt * ROWS_PER_TQ + rr) + (WIN_ROWS - 1)
                    dd = jnp.clip(d, -1, N_DR - 1) + 1
                    bias_ref[hd, rr * GRID_W:(rr + 1) * GRID_W, kp * LANES:(kp + 1) * LANES] = toe_ref[hd, dd]
            biases.append([bias_ref[hd] + mask_ref[...], None])
        out = _attend_pair(q, [[k_s[hd, local, :], ck_s[hd]] for hd in heads],
                           [[v_s[hd, local, :], cv_s[hd]] for hd in heads], biases)
        o_ref[:, pair * LANES:(pair + 1) * LANES] = out.astype(o_ref.dtype)


def _lat_na_attention(y, cache_k, cache_v, rpb, layer):
    first = T_CTX // LAT_TQ
    kv_row = T_CTX // LAT_LEN
    wide = N_HEADS_B * HEAD_DIM
    gen = jnp.pad(rpb[layer], ((0, 0), (1, NA_TOE_ROWS - N_DR - 1), (0, LANES - N_DC)))
    w = NA_PAIRS * LANES
    nh = 2 * NA_PAIRS
    cq, ck, cv = COL_QB // NA_PAIRS, COL_KB // NA_PAIRS, COL_VB // NA_PAIRS
    return pl.pallas_call(
        _na_kernel,
        grid=(N_HEADS_B // nh, N_LAT_SETS, LAT_QT),
        in_specs=[
            pl.BlockSpec((LAT_TQ, w), lambda g, b, t: (first + b * LAT_QT + t, cq + g)),
            pl.BlockSpec((LAT_LEN, w), lambda g, b, t: (kv_row + b, ck + g)),
            pl.BlockSpec((LAT_LEN, w), lambda g, b, t: (kv_row + b, cv + g)),
            pl.BlockSpec((None, None, PAST_LEN, w), lambda g, b, t: (b, layer, 0, g)),
            pl.BlockSpec((None, None, PAST_LEN, w), lambda g, b, t: (b, layer, 0, g)),
            pl.BlockSpec((None, LAT_TQ, NA_LOCAL), lambda g, b, t: (t, 0, 0)),
            pl.BlockSpec((nh, NA_TOE_ROWS, LANES), lambda g, b, t: (g, 0, 0)),
            pl.BlockSpec((LANES, LANES), lambda g, b, t: (0, 0)),
        ],
        out_specs=pl.BlockSpec((LAT_TQ, w), lambda g, b, t: (b * LAT_QT + t, g)),
        out_shape=jax.ShapeDtypeStruct((T_LAT, wide), BF16),
        scratch_shapes=[pltpu.VMEM((nh, N_DR + 1, GRID_W, LANES), F32), pltpu.VMEM((nh, LAT_TQ, NA_LOCAL), F32),
                        pltpu.VMEM((nh, LAT_LEN, LANES), BF16), pltpu.VMEM((nh, LAT_LEN, LANES), BF16),
                        pltpu.VMEM((nh, PAST_LEN, LANES), BF16), pltpu.VMEM((nh, PAST_LEN, LANES), BF16)],
        compiler_params=_params(3),
        name="lat_na_attention",
    )(y, y, y, cache_k.reshape(N_LAT_SETS, DEPTH, PAST_LEN, wide), cache_v.reshape(N_LAT_SETS, DEPTH, PAST_LEN, wide),
      jnp.asarray(_window_mask()), gen, jnp.asarray(_toeplitz_select()))


MERGE_TM = 512
MERGE_SUB = 256
MERGE_ROUTE_TM = 256


def _merge_kernel(*refs, n_y, route_sets):
    y_refs, refs = refs[:n_y], refs[n_y:]
    (h_ref, x_ref, gt1_ref, sh2_ref, sc2_ref, nf_ref, wa_ref, wb_ref, wo_ref, wr_ref, wg_ref), refs = refs[:11], refs[11:]
    if route_sets:
        (rlg_ref, rh_ref), refs = refs[:2], refs[2:]
    (x1_ref, h2_ref, lg_ref), refs = refs[:3], refs[3:]
    if route_sets:
        (xg_ref, g_ref, rc_ref), refs = refs[:3], refs[3:]
        wab, wbb, wob, wrs, wgb, p_s, rt_s = refs
    else:
        wab, wbb, wob, wrs, wgb = refs

    @pl.when(pl.program_id(0) == 0)
    def _():
        wab[...] = wa_ref[...].astype(BF16)
        wbb[...] = wb_ref[...].astype(BF16)
        wob[...] = wo_ref[...].astype(BF16)
        wgb[...] = wg_ref[0].astype(BF16)
        hi, lo = _split2(wr_ref[...])
        wrs[:N_EXPERTS, :] = hi
        wrs[N_EXPERTS:, :] = lo

    half = N_HEADS_A * HEAD_DIM
    n_sub = x_ref.shape[0] // MERGE_SUB
    for r in range(n_sub):
        rows = slice(r * MERGE_SUB, (r + 1) * MERGE_SUB)
        if n_y == 1:
            ya, yb = y_refs[0][rows, :half], y_refs[0][rows, half:]
        else:
            ya, yb = y_refs[0][rows, :], y_refs[1][rows, :]
        sets = list(range(r * route_sets // n_sub, (r + 1) * route_sets // n_sub))

        def route(k):
            tok = slice(k * CTX_LEN, (k + 1) * CTX_LEN)
            slots = slice(k * CAP_CTX, (k + 1) * CAP_CTX)
            _route_set(rlg_ref[:, tok], rh_ref[tok, :], xg_ref.at[:, slots, :], g_ref.at[:, slots, :],
                       rc_ref.at[tok, :], p_s.at[k], rt_s.at[k], CTX_LEN, CAP_CTX)

        gates = _bdot(h_ref[rows, :], wgb[...])
        za = _bdot(ya, wab[...])
        zb = _bdot(yb, wbb[...])
        for k in sets[:len(sets) // 2]:
            route(k)
        m = jax.nn.sigmoid(gates[:, :D_MODEL]) * za + jax.nn.sigmoid(gates[:, D_MODEL:]) * zb
        x1 = x_ref[rows, :] + gt1_ref[...] * _bdot(m.astype(BF16), wob[...])
        x1_ref[rows, :] = x1
        h2 = _modnorm(x1, nf_ref[...], sh2_ref[...], sc2_ref[...])
        h2_ref[rows, :] = h2.astype(BF16)
        hh, hl = _split2(h2)
        both = _bdot_nt(wrs[...], hh)
        lg_ref[:, rows] = both[:N_EXPERTS, :] + both[N_EXPERTS:, :] + _bdot_nt(wrs[:N_EXPERTS, :], hl)
        for k in sets[len(sets) // 2:]:
            route(k)


def _merge(y_parts, h, x, mod, norm_ffn, w_ba, w_bb, w_out, w_router_t, w_in, layer, row_fn, tm, route=None):
    n = x.shape[0]
    tiles = n // tm
    half = N_HEADS_A * HEAD_DIM
    once = pl.Buffered(1)
    weight = lambda k: pl.BlockSpec((None, k, D_MODEL), lambda i: (layer, 0, 0), pipeline_mode=once)
    tile = lambda w: pl.BlockSpec((tm, w), lambda i: (i, 0))
    in_specs = [tile(p.shape[1]) for p in y_parts] + [
        tile(D_MODEL), tile(D_MODEL),
        _mod_spec(layer, row_fn, MOD_GT1), _mod_spec(layer, row_fn, MOD_SH2), _mod_spec(layer, row_fn, MOD_SC2),
        pl.BlockSpec((None, 1, D_MODEL), lambda i: (layer, 0, 0)),
        weight(half), weight(half), weight(D_MODEL),
        pl.BlockSpec((None, N_EXPERTS, D_MODEL), lambda i: (layer, 0, 0), pipeline_mode=once),
        pl.BlockSpec((pl.Element(1), pl.Element(D_MODEL), pl.Element(2 * D_MODEL)),
                     lambda i: (layer, 0, QKV_DIM), pipeline_mode=once),
    ]
    args = list(y_parts) + [h, x, mod, mod, mod, norm_ffn, w_ba, w_bb, w_out, w_router_t, w_in]
    out_specs = [tile(D_MODEL), tile(D_MODEL), pl.BlockSpec((N_EXPERTS, tm), lambda i: (0, i))]
    out_shape = [jax.ShapeDtypeStruct((n, D_MODEL), F32), jax.ShapeDtypeStruct((n, D_MODEL), BF16),
                 jax.ShapeDtypeStruct((N_EXPERTS, n), F32)]
    scratch = [pltpu.VMEM((half, D_MODEL), BF16), pltpu.VMEM((half, D_MODEL), BF16),
               pltpu.VMEM((D_MODEL, D_MODEL), BF16), pltpu.VMEM((2 * N_EXPERTS, D_MODEL), BF16),
               pltpu.VMEM((D_MODEL, 2 * D_MODEL), BF16)]
    route_sets = 0
    if route is not None:
        route_sets = N_CTX_SETS // tiles
        rows, slots = route_sets * CTX_LEN, route_sets * CAP_CTX
        in_specs += [pl.BlockSpec((N_EXPERTS, rows), lambda i: (0, i)), pl.BlockSpec((rows, D_MODEL), lambda i: (i, 0))]
        args += list(route)
        out_specs += [pl.BlockSpec((N_EXPERTS, slots, D_MODEL), lambda i: (0, i, 0)),
                      pl.BlockSpec((N_EXPERTS, slots, LANES), lambda i: (0, i, 0)),
                      pl.BlockSpec((rows, LANES), lambda i: (i, 0))]
        out_shape += [jax.ShapeDtypeStruct((N_EXPERTS, N_CTX_SETS * CAP_CTX, D_MODEL), BF16),
                      jax.ShapeDtypeStruct((N_EXPERTS, N_CTX_SETS * CAP_CTX, LANES), F32),
                      jax.ShapeDtypeStruct((T_CTX, LANES), F32)]
        scratch += [pltpu.VMEM((route_sets, N_EXPERTS * CAP_CTX, CTX_LEN), BF16),
                    pltpu.VMEM((route_sets, LANES, CTX_LEN), F32)]
    return pl.pallas_call(
        functools.partial(_merge_kernel, n_y=len(y_parts), route_sets=route_sets),
        grid=(tiles,),
        in_specs=in_specs,
        out_specs=out_specs,
        out_shape=out_shape,
        scratch_shapes=scratch,
        compiler_params=_params(1),
        name="merge_route" if route_sets else "merge",
    )(*args)


GATHER_M = 512
RANK_TILE = 128


def _rank_row(aff, a_row, e, n):
    tiles = n // RANK_TILE
    sub = lax.broadcasted_iota(jnp.int32, (RANK_TILE, RANK_TILE), 0)
    lane = lax.broadcasted_iota(jnp.int32, (RANK_TILE, RANK_TILE), 1)
    earlier = jnp.where(sub < lane, 1.0, 0.0)
    acc = [jnp.zeros((8, RANK_TILE), F32) for _ in range(tiles)]
    for c in range(tiles):
        a_col = jnp.broadcast_to(aff[c * RANK_TILE:(c + 1) * RANK_TILE, e:e + 1], (RANK_TILE, RANK_TILE))
        for j in range(tiles):
            a_rj = a_row[:, j * RANK_TILE:(j + 1) * RANK_TILE]
            if c < j:
                beats = jnp.where(a_col >= a_rj, 1.0, 0.0)
            elif c > j:
                beats = jnp.where(a_col > a_rj, 1.0, 0.0)
            else:
                beats = jnp.where(a_col > a_rj, 1.0, jnp.where(a_col == a_rj, earlier, 0.0))
            acc[j] = acc[j] + beats.reshape(RANK_TILE // 8, 8, RANK_TILE).sum(axis=0)
    return jnp.concatenate([a.sum(axis=0, keepdims=True) for a in acc], axis=1)


def _route_set(lg, h, xg_ref, g_ref, rc_ref, p_ref, rt_ref, n, cap):
    ex = jnp.exp(lg - lg.max(axis=0, keepdims=True))
    aff_t = ex / ex.sum(axis=0, keepdims=True)
    aff = jnp.concatenate([aff_t, jnp.zeros((LANES - N_EXPERTS, n), F32)], axis=0).T
    rt_ref[...] = jnp.full((LANES, n), float(cap), F32)
    slot = lax.broadcasted_iota(jnp.int32, (cap, n), 0).astype(F32)
    for e in range(N_EXPERTS):
        rank = _rank_row(aff, aff_t[e:e + 1, :], e, n)
        rt_ref[e:e + 1, :] = jnp.minimum(rank, float(cap))
        p_ref[e * cap:(e + 1) * cap, :] = jnp.where(rank == slot, 1.0, 0.0).astype(BF16)

    a1, a2, a3 = (p.astype(F32) for p in _split3(aff))
    packed = (a1 + pltpu.roll(a2, N_EXPERTS, 1) + pltpu.roll(a3, 2 * N_EXPERTS, 1)).astype(BF16)
    per = GATHER_M // cap
    glane = lax.broadcasted_iota(jnp.int32, (cap, LANES), 1)
    for grp in range(N_EXPERTS * cap // GATHER_M):
        p = p_ref[grp * GATHER_M:(grp + 1) * GATHER_M, :]
        xg = _bdot(p, h).astype(BF16)
        gg = _bdot(p, packed)
        for k in range(per):
            e = grp * per + k
            xg_ref[e] = xg[k * cap:(k + 1) * cap, :]
            mine = (glane < 3 * N_EXPERTS) & ((glane & (N_EXPERTS - 1)) == e)
            ge = jnp.where(mine, gg[k * cap:(k + 1) * cap, :], 0.0).sum(axis=-1, keepdims=True)
            g_ref[e] = jnp.broadcast_to(ge, (cap, LANES))
    rc_ref[...] = rt_ref[...].T


def _route_kernel(lg_ref, h_ref, xg_ref, g_ref, rc_ref, p_ref, rt_ref, *, n, cap):
    _route_set(lg_ref[...], h_ref[...], xg_ref, g_ref, rc_ref, p_ref, rt_ref, n, cap)


def _route(logits, h2, n, cap, n_sets):
    return pl.pallas_call(
        functools.partial(_route_kernel, n=n, cap=cap),
        grid=(n_sets,),
        in_specs=[
            pl.BlockSpec((N_EXPERTS, n), lambda s: (0, s)),
            pl.BlockSpec((n, D_MODEL), lambda s: (s, 0)),
        ],
        out_specs=[
            pl.BlockSpec((N_EXPERTS, cap, D_MODEL), lambda s: (0, s, 0)),
            pl.BlockSpec((N_EXPERTS, cap, LANES), lambda s: (0, s, 0)),
            pl.BlockSpec((n, LANES), lambda s: (s, 0)),
        ],
        out_shape=[jax.ShapeDtypeStruct((N_EXPERTS, n_sets * cap, D_MODEL), BF16),
                   jax.ShapeDtypeStruct((N_EXPERTS, n_sets * cap, LANES), F32),
                   jax.ShapeDtypeStruct((n_sets * n, LANES), F32)],
        scratch_shapes=[pltpu.VMEM((N_EXPERTS * cap, n), BF16), pltpu.VMEM((LANES, n), F32)],
        compiler_params=_params(1),
        name=f"route_n{n}",
    )(logits, h2)


EXPERT_TF = 1024
N_FF_TILES = EXPERT_FF // EXPERT_TF


def _expert_kernel(xc_ref, xl_ref, gc_ref, gl_ref, wg_ref, wu_ref, wd_ref, o_ref, x_s, acc_s):
    f = pl.program_id(1)
    n_ctx = xc_ref.shape[0]

    @pl.when(f == 0)
    def _():
        x_s[:n_ctx, :] = xc_ref[...]
        x_s[n_ctx:, :] = xl_ref[...]
        acc_s[...] = jnp.zeros_like(acc_s)

    x = x_s[...]
    gate = _bdot(x, wg_ref[...].astype(BF16))
    up = _bdot(x, wu_ref[...].astype(BF16))
    hid = (gate * jax.nn.sigmoid(gate)) * up
    acc_s[...] += _bdot(hid.astype(BF16), wd_ref[...].astype(BF16))

    @pl.when(f == N_FF_TILES - 1)
    def _():
        o_ref[:n_ctx, :] = (acc_s[:n_ctx, :] * gc_ref[:, :1]).astype(o_ref.dtype)
        o_ref[n_ctx:, :] = (acc_s[n_ctx:, :] * gl_ref[:, :1]).astype(o_ref.dtype)


def _experts(xg_ctx, xg_lat, g_ctx, g_lat, w_gate, w_up, w_down, layer):
    sc, sl = xg_ctx.shape[1], xg_lat.shape[1]
    slots = lambda s, w: pl.BlockSpec((None, s, w), lambda e, f: (e, 0, 0))
    return pl.pallas_call(
        _expert_kernel,
        grid=(N_EXPERTS, N_FF_TILES),
        in_specs=[
            slots(sc, D_MODEL), slots(sl, D_MODEL), slots(sc, LANES), slots(sl, LANES),
            pl.BlockSpec((None, None, D_MODEL, EXPERT_TF), lambda e, f: (layer, e, 0, f)),
            pl.BlockSpec((None, None, D_MODEL, EXPERT_TF), lambda e, f: (layer, e, 0, f)),
            pl.BlockSpec((None, None, EXPERT_TF, D_MODEL), lambda e, f: (layer, e, f, 0)),
        ],
        out_specs=slots(sc + sl, D_MODEL),
        out_shape=jax.ShapeDtypeStruct((N_EXPERTS, sc + sl, D_MODEL), BF16),
        scratch_shapes=[pltpu.VMEM((sc + sl, D_MODEL), BF16), pltpu.VMEM((sc + sl, D_MODEL), F32)],
        compiler_params=_params(2),
        name="experts",
    )(xg_ctx, xg_lat, g_ctx, g_lat, w_gate, w_up, w_down)


COMB_TM = 256


def _combine_kernel(o_ref, rc_ref, x_ref, gt2_ref, ng_ref, *rest, cap, final):
    if final:
        (y_ref,) = rest
    else:
        sh_ref, sc_ref, xn_ref, hn_ref = rest
    slots = N_EXPERTS * cap

    j = lax.broadcasted_iota(jnp.int32, (LANES, slots), 1)
    e = lax.broadcasted_iota(jnp.int32, (LANES, slots), 0)
    expand = jnp.where(j // cap == e, 1.0, 0.0).astype(BF16)
    rank = _bdot(rc_ref[...].astype(BF16), expand)
    slot = (lax.broadcasted_iota(jnp.int32, (1, slots), 1) % cap).astype(F32)
    pt = jnp.where(rank == slot, 1.0, 0.0).astype(BF16)
    ffn = _bdot(pt, o_ref[...].reshape(slots, D_MODEL))
    x = x_ref[...] + gt2_ref[...] * ffn
    if final:
        y_ref[...] = _rms(x) * ng_ref[...]
    else:
        xn_ref[...] = x
        hn_ref[...] = _modnorm(x, ng_ref[...], sh_ref[...], sc_ref[...]).astype(BF16)


def _combine(out, rc, x1, mod, norm_g, layer, n, cap, n_sets, first_slot_block, row_fn, final):
    tiles = n // COMB_TM
    rows = lambda s, t: (s * tiles + t, 0)
    in_specs = [
        pl.BlockSpec((N_EXPERTS, cap, D_MODEL), lambda s, t: (0, first_slot_block + s, 0)),
        pl.BlockSpec((COMB_TM, LANES), rows),
        pl.BlockSpec((COMB_TM, D_MODEL), rows),
        _mod_spec(layer, row_fn, MOD_GT2),
    ]
    args = [out, rc, x1, mod]
    if final:
        in_specs.append(pl.BlockSpec((1, D_MODEL), lambda s, t: (0, 0)))
        args.append(norm_g.reshape(1, D_MODEL))
        out_specs = pl.BlockSpec((COMB_TM, D_MODEL), rows)
        out_shape = jax.ShapeDtypeStruct((n_sets * n, D_MODEL), F32)
    else:
        in_specs += [pl.BlockSpec((None, 1, D_MODEL), lambda s, t: (layer + 1, 0, 0)),
                     _mod_spec(layer + 1, row_fn, MOD_SH1), _mod_spec(layer + 1, row_fn, MOD_SC1)]
        args += [norm_g, mod, mod]
        out_specs = [pl.BlockSpec((COMB_TM, D_MODEL), rows), pl.BlockSpec((COMB_TM, D_MODEL), rows)]
        out_shape = [jax.ShapeDtypeStruct((n_sets * n, D_MODEL), F32),
                     jax.ShapeDtypeStruct((n_sets * n, D_MODEL), BF16)]
    return pl.pallas_call(
        functools.partial(_combine_kernel, cap=cap, final=final),
        grid=(n_sets, tiles),
        in_specs=in_specs,
        out_specs=out_specs,
        out_shape=out_shape,
        compiler_params=_params(2),
        name=f"combine_n{n}",
    )(*args)


def kernel(x_prompt, x_sample, cache_attn_k, cache_attn_v, cache_na_k, cache_na_v, c, c_ctx, w_ada, b_ada,
           norm_mix, norm_ffn, w_in, q_norm, k_norm, rpb, w_branch_a, w_branch_b, w_out, w_router, w_gate,
           w_up, w_down, final_norm):
    x_ctx = x_prompt.reshape(T_CTX, D_MODEL)
    x_lat = x_sample.reshape(T_LAT, D_MODEL)
    mod = _modulation(c, c_ctx, w_ada, b_ada)
    norm_mix3 = norm_mix.reshape(DEPTH, 1, D_MODEL)
    norm_ffn3 = norm_ffn.reshape(DEPTH, 1, D_MODEL)
    w_router_t = jnp.swapaxes(w_router, 1, 2)
    ctx_row = lambda *g: 0
    lat_comb_row = lambda s, t: 1 + s

    h_ctx = _prenorm(x_ctx, mod, norm_mix3, 0, ctx_row)
    h_lat = _prenorm(x_lat, mod, norm_mix3, 0, _lat_mod_row(NORM_TM))
    layer_caches = []
    for layer in range(DEPTH):
        last = layer == DEPTH - 1
        y = _projection(h_ctx, h_lat, w_in, q_norm, k_norm, layer)
        yab_ctx, *new_caches = _ctx_attention(y, layer_caches if last else [])
        layer_caches.append(tuple(new_caches))
        ya_lat = _lat_gqa_attention(y, cache_attn_k, cache_attn_v, layer)
        yb_lat = _lat_na_attention(y, cache_na_k, cache_na_v, rpb, layer)
        merge_w = (mod, norm_ffn3, w_branch_a, w_branch_b, w_out, w_router_t, w_in, layer)
        x1_ctx, h2_ctx, lg_ctx = _merge([yab_ctx], h_ctx, x_ctx, *merge_w, ctx_row, MERGE_TM)
        x1_lat, h2_lat, lg_lat, xg_ctx, g_ctx, rc_ctx = _merge(
            [ya_lat, yb_lat], h_lat, x_lat, *merge_w, _lat_mod_row(MERGE_ROUTE_TM), MERGE_ROUTE_TM,
            route=(lg_ctx, h2_ctx))
        xg_lat, g_lat, rc_lat = _route(lg_lat, h2_lat, LAT_LEN, CAP_LAT, N_LAT_SETS)
        out = _experts(xg_ctx, xg_lat, g_ctx, g_lat, w_gate, w_up, w_down, layer)
        final = layer == DEPTH - 1
        norm_g = final_norm if final else norm_mix3
        res_ctx = _combine(out, rc_ctx, x1_ctx, mod, norm_g, layer, CTX_LEN, CAP_CTX, N_CTX_SETS, 0, ctx_row, final)
        res_lat = _combine(out, rc_lat, x1_lat, mod, norm_g, layer, LAT_LEN, CAP_LAT, N_LAT_SETS,
                           N_CTX_SETS * CAP_CTX // CAP_LAT, lat_comb_row, final)
        if final:
            y_ctx, y_lat = res_ctx, res_lat
        else:
            (x_ctx, h_ctx), (x_lat, h_lat) = res_ctx, res_lat

    heads = (N_KV_A, N_KV_A, N_HEADS_B, N_HEADS_B)
    new_caches = [a.reshape(N_CTX_SETS, DEPTH, CTX_LEN, h, HEAD_DIM) for a, h in zip(layer_caches[-1], heads)]
    return (y_ctx.reshape(N_CTX_SETS, CTX_LEN, D_MODEL), y_lat.reshape(N_LAT_SETS, LAT_LEN, D_MODEL), *new_caches)
```

```python
import functools

import numpy as np
import jax
import jax.numpy as jnp
from jax import lax
from jax.experimental import pallas as pl
from jax.experimental.pallas import tpu as pltpu

F32 = jnp.float32
BF16 = jnp.bfloat16

D_MODEL = 1024
N_CTX_SETS, CTX_LEN = 16, 256
N_LAT_SETS, LAT_LEN = 2, 1024
T_CTX = N_CTX_SETS * CTX_LEN
T_LAT = N_LAT_SETS * LAT_LEN
T_ALL = T_CTX + T_LAT
DEPTH = 2
PAST_LEN = 512
GRID_W = 64
GRID_ROWS = LAT_LEN // GRID_W
HEAD_DIM = 64
N_HEADS_A, N_KV_A, N_HEADS_B = 8, 2, 8
WIN_ROWS, WIN_COLS = 8, 16
N_EXPERTS = 16
EXPERT_FF = 2048
CAP_CTX = 2 * CTX_LEN // N_EXPERTS
CAP_LAT = 2 * LAT_LEN // N_EXPERTS
ROPE_THETA = 10000.0
EPS = 1e-6
NEG = -1e30
QKV_DIM = 2304
ATT_SCALE = HEAD_DIM ** -0.5

LANES = 128
VMEM_LIMIT = 56 * 1024 * 1024

PROJ_TN = 256
N_PROJ_TILES = QKV_DIM // PROJ_TN
PROJ_PERM = np.array([0, 1, 8, 2, 3, 4, 5, 6, 7], np.int32)
CAST_STEPS = 8
MOD_SH1, MOD_SC1, MOD_GT1, MOD_SH2, MOD_SC2, MOD_GT2 = range(6)


def _params(n_grid_dims, vmem=VMEM_LIMIT):
    return pltpu.CompilerParams(dimension_semantics=("arbitrary",) * n_grid_dims, vmem_limit_bytes=vmem)


def _bdot(a, b):
    return jnp.dot(a, b, preferred_element_type=F32)


def _bdot_nt(a, b):
    return lax.dot_general(a, b, (((1,), (1,)), ((), ())), preferred_element_type=F32)


def _split2(x):
    hi = x.astype(BF16)
    lo = (x - hi.astype(F32)).astype(BF16)
    return hi, lo


def _split3(x):
    hi = x.astype(BF16)
    r = x - hi.astype(F32)
    mid = r.astype(BF16)
    lo = (r - mid.astype(F32)).astype(BF16)
    return hi, mid, lo


def _rms(x):
    return x * lax.rsqrt(jnp.mean(x * x, axis=-1, keepdims=True) + EPS)


def _modnorm(x, g, sh, sc):
    return (_rms(x) * g) * (1.0 + sc) + sh


def _lat_mod_row(tile_rows):
    return lambda i: 1 + (i * tile_rows) // LAT_LEN


def _mod_spec(layer, row_fn, chunk):
    return pl.BlockSpec((None, None, 1, D_MODEL), lambda *g: (layer, row_fn(*g), 0, chunk))


MOD_TN = 1536


def _mod_kernel(ct_ref, w_ref, b_ref, o_ref):
    ct = ct_ref[...]
    act = ct * jax.nn.sigmoid(ct)
    w = w_ref[...]
    for m in range(3):
        o_ref[m:m + 1, :] = jnp.sum(w * act[:, m:m + 1], axis=0, keepdims=True) + b_ref[...]
    o_ref[3:8, :] = jnp.zeros((5, MOD_TN), F32)


def _modulation(c, c_ctx, w_ada, b_ada):
    cond = jnp.concatenate([c_ctx[None, :], c, jnp.zeros((5, D_MODEL), F32)], axis=0)
    mod = pl.pallas_call(
        _mod_kernel,
        grid=(DEPTH, 6 * D_MODEL // MOD_TN),
        in_specs=[
            pl.BlockSpec((D_MODEL, 8), lambda l, j: (0, 0)),
            pl.BlockSpec((None, D_MODEL, MOD_TN), lambda l, j: (l, 0, j)),
            pl.BlockSpec((None, 1, MOD_TN), lambda l, j: (l, 0, j)),
        ],
        out_specs=pl.BlockSpec((None, 8, MOD_TN), lambda l, j: (l, 0, j)),
        out_shape=jax.ShapeDtypeStruct((DEPTH, 8, 6 * D_MODEL), F32),
        compiler_params=_params(2),
        name="modulation",
    )(cond.T, w_ada, b_ada.reshape(DEPTH, 1, 6 * D_MODEL))
    return mod.reshape(DEPTH, 8, 1, 6 * D_MODEL)


NORM_TM = 512


def _prenorm_kernel(x_ref, g_ref, sh_ref, sc_ref, h_ref):
    h_ref[...] = _modnorm(x_ref[...], g_ref[...], sh_ref[...], sc_ref[...]).astype(BF16)


def _prenorm(x, mod, norm_g, layer, row_fn):
    n = x.shape[0]
    return pl.pallas_call(
        _prenorm_kernel,
        grid=(n // NORM_TM,),
        in_specs=[
            pl.BlockSpec((NORM_TM, D_MODEL), lambda i: (i, 0)),
            pl.BlockSpec((None, 1, D_MODEL), lambda i: (layer, 0, 0)),
            _mod_spec(layer, row_fn, MOD_SH1),
            _mod_spec(layer, row_fn, MOD_SC1),
        ],
        out_specs=pl.BlockSpec((NORM_TM, D_MODEL), lambda i: (i, 0)),
        out_shape=jax.ShapeDtypeStruct((n, D_MODEL), BF16),
        compiler_params=_params(1),
        name="prenorm",
    )(x, norm_g, mod, mod)


PROJ_CH = 1024


def _rope_tables():
    t = np.arange(LAT_LEN)
    lane = np.arange(LANES) % HEAD_DIM
    pos = np.where(lane < HEAD_DIM // 2, (t // GRID_W)[:, None], (t % GRID_W)[:, None]).astype(np.float64)
    freq = ROPE_THETA ** (-(lane % 16).astype(np.float64) / 16.0)
    ang = pos * freq[None, :]
    sign = np.where((lane & 16) == 0, -1.0, 1.0)[None, :]
    return np.cos(ang).astype(np.float32), (np.sin(ang) * sign).astype(np.float32)


def _head_norm_rope(y, gain, cos, sin):
    w = y.shape[1]
    r = (lax.broadcasted_iota(jnp.int32, (2 * w, w), 0) % w) // HEAD_DIM
    c = lax.broadcasted_iota(jnp.int32, (2 * w, w), 1) // HEAD_DIM
    seg = jnp.where(r == c, 1.0 / HEAD_DIM, 0.0).astype(BF16)
    ms = _bdot(jnp.concatenate(_split2(y * y), axis=1), seg)
    yn = y * lax.rsqrt(ms + EPS) * gain
    if cos is None:
        return yn
    lane = lax.broadcasted_iota(jnp.int32, yn.shape, 1)
    partner = jnp.where((lane & 16) == 0, pltpu.roll(yn, w - 16, 1), pltpu.roll(yn, 16, 1))
    if w > LANES:
        cos = jnp.concatenate([cos] * (w // LANES), axis=1)
        sin = jnp.concatenate([sin] * (w // LANES), axis=1)
    return yn * cos + partner * sin


def _proj_kernel(perm_ref, hc_ref, hl_ref, w_ref, gain_ref, cos_ref, sin_ref, wa_ref, wbr_ref, wo_ref, wg_ref,
                 o_ref, oa_ref, ob_ref, oo_ref, og_ref, wb_ref):
    del perm_ref
    j = pl.program_id(0)
    wb_ref[...] = w_ref[...].astype(BF16)
    oa_ref[...] = wa_ref[...].astype(BF16)
    ob_ref[...] = wbr_ref[...].astype(BF16)
    oo_ref[...] = wo_ref[...].astype(BF16)
    og_ref[...] = wg_ref[0].astype(BF16)
    chunks = [(hc_ref, k * PROJ_CH, k * PROJ_CH, False) for k in range(T_CTX // PROJ_CH)]
    chunks += [(hl_ref, k * PROJ_CH, T_CTX + k * PROJ_CH, True) for k in range(T_LAT // PROJ_CH)]

    def matmul(chunk):
        h_ref, r0, o0, _ = chunk
        o_ref[o0:o0 + PROJ_CH, :] = _bdot(h_ref[r0:r0 + PROJ_CH, :], wb_ref[...])

    def finish(chunk, width):
        _, _, o0, is_lat = chunk
        cos, sin = (cos_ref[...], sin_ref[...]) if is_lat else (None, None)
        y = o_ref[o0:o0 + PROJ_CH, :width]
        o_ref[o0:o0 + PROJ_CH, :width] = _head_norm_rope(y, gain_ref[:, :width], cos, sin)

    def tile(width):
        matmul(chunks[0])
        for k in range(1, len(chunks)):
            matmul(chunks[k])
            if width:
                finish(chunks[k - 1], width)
        if width:
            finish(chunks[-1], width)

    pl.when(j < 2)(lambda: tile(PROJ_TN))
    pl.when(j == 2)(lambda: tile(LANES))
    pl.when(j > 2)(lambda: tile(0))


def _projection(h_ctx, h_lat, w_in, q_norm, k_norm, w_ba, w_bb, w_out, layer):
    ones = jnp.ones((2 * HEAD_DIM,), F32)
    gain = jnp.stack([jnp.tile(q_norm[layer], 4), jnp.tile(q_norm[layer], 4),
                      jnp.concatenate([jnp.tile(k_norm[layer], 2), ones])])[:, None, :]
    cos, sin = _rope_tables()
    half = N_HEADS_A * HEAD_DIM
    ra, ro = half // CAST_STEPS, D_MODEL // CAST_STEPS
    part = lambda j: jnp.minimum(j, CAST_STEPS - 1)
    grid_spec = pltpu.PrefetchScalarGridSpec(
        num_scalar_prefetch=1,
        grid=(N_PROJ_TILES,),
        in_specs=[
            pl.BlockSpec((T_CTX, D_MODEL), lambda j, p: (0, 0)),
            pl.BlockSpec((T_LAT, D_MODEL), lambda j, p: (0, 0)),
            pl.BlockSpec((None, D_MODEL, PROJ_TN), lambda j, p: (layer, 0, j)),
            pl.BlockSpec((None, 1, PROJ_TN), lambda j, p: (jnp.minimum(j, 2), 0, 0)),
            pl.BlockSpec((LAT_LEN, LANES), lambda j, p: (0, 0)),
            pl.BlockSpec((LAT_LEN, LANES), lambda j, p: (0, 0)),
            pl.BlockSpec((None, ra, D_MODEL), lambda j, p: (layer, part(j), 0)),
            pl.BlockSpec((None, ra, D_MODEL), lambda j, p: (layer, part(j), 0)),
            pl.BlockSpec((None, ro, D_MODEL), lambda j, p: (layer, part(j), 0)),
            pl.BlockSpec((pl.Element(1), pl.Element(ro), pl.Element(2 * D_MODEL)),
                         lambda j, p: (layer, part(j) * ro, QKV_DIM)),
        ],
        out_specs=[
            pl.BlockSpec((T_ALL, PROJ_TN), lambda j, p: (0, p[j])),
            pl.BlockSpec((ra, D_MODEL), lambda j, p: (part(j), 0)),
            pl.BlockSpec((ra, D_MODEL), lambda j, p: (part(j), 0)),
            pl.BlockSpec((ro, D_MODEL), lambda j, p: (part(j), 0)),
            pl.BlockSpec((ro, 2 * D_MODEL), lambda j, p: (part(j), 0)),
        ],
        scratch_shapes=[pltpu.VMEM((D_MODEL, PROJ_TN), BF16)],
    )
    y, *merge_w = pl.pallas_call(
        _proj_kernel,
        grid_spec=grid_spec,
        out_shape=[jax.ShapeDtypeStruct((T_ALL, QKV_DIM), F32),
                   jax.ShapeDtypeStruct((half, D_MODEL), BF16), jax.ShapeDtypeStruct((half, D_MODEL), BF16),
                   jax.ShapeDtypeStruct((D_MODEL, D_MODEL), BF16), jax.ShapeDtypeStruct((D_MODEL, 2 * D_MODEL), BF16)],
        compiler_params=_params(1),
        name="projection",
    )(jnp.asarray(PROJ_PERM), h_ctx, h_lat, w_in, gain, jnp.asarray(cos), jnp.asarray(sin), w_ba, w_bb, w_out, w_in)
    return y, merge_w


COL_QA, COL_QB, COL_KB, COL_VB, COL_KA, COL_VA = 0, 4, 8, 12, 16, 17


def _lane_is_low(shape):
    return lax.broadcasted_iota(jnp.int32, shape, 1) < HEAD_DIM


def _pair_halves(x):
    low = _lane_is_low(x.shape)
    xb = x.astype(BF16)
    zero = jnp.zeros_like(xb)
    return jnp.where(low, xb, zero), jnp.where(low, zero, xb)


def _scaled_q(q):
    assert ATT_SCALE == 0.125
    return (q * ATT_SCALE).astype(BF16)


def _attend_pair(q, keys, values, biases, joint_pv=True):
    probs, dens, out = [], [], None
    for h in range(2):
        scores = []
        for k, b in zip(keys[h], biases[h]):
            s = _bdot_nt(q, k)
            scores.append(s if b is None else s + b)
        m = scores[0].max(axis=-1, keepdims=True)
        for s in scores[1:]:
            m = jnp.maximum(m, s.max(axis=-1, keepdims=True))
        den, num = None, None
        for s, v in zip(scores, values[h]):
            e = jnp.exp(s - m)
            d = e.sum(axis=-1, keepdims=True)
            den = d if den is None else den + d
            if joint_pv:
                probs.append(e.astype(BF16))
            else:
                o = _bdot(e.astype(BF16), v)
                num = o if num is None else num + o
        dens.append(den)
        if not joint_pv:
            out = num / den if out is None else out + num / den
    if not joint_pv:
        return out
    num = _bdot(jnp.concatenate(probs, axis=1), jnp.concatenate(values[0] + values[1], axis=0))
    return num / jnp.where(_lane_is_low(num.shape), dens[0], dens[1])


def _gqa_variants(x):
    lo, hi = _pair_halves(x)
    sw_lo, sw_hi = _pair_halves(pltpu.roll(x, HEAD_DIM, 1))
    return [lo, sw_hi, sw_lo, hi]


def _gqa_attention(q_ref, kvar, vvar, o_ref, joint_pv):
    for pair in range(N_HEADS_A // 2):
        q = _scaled_q(q_ref[:, pair * LANES:(pair + 1) * LANES])
        kvh = (2 * pair) // (N_HEADS_A // N_KV_A)
        out = _attend_pair(q, [[kvar[2 * kvh]], [kvar[2 * kvh + 1]]], [[vvar[2 * kvh]], [vvar[2 * kvh + 1]]],
                           [[None], [None]], joint_pv)
        o_ref[:, pair * LANES:(pair + 1) * LANES] = out.astype(o_ref.dtype)


def _ctx_attn_kernel(qa_ref, qb_ref, kb_ref, vb_ref, kava_ref, *refs, n_prev):
    prev = refs[:4 * n_prev]
    y_ref, nak_ref, nav_ref, nbk_ref, nbv_ref = refs[4 * n_prev:]
    ka = kava_ref[:, :LANES]
    va = kava_ref[:, LANES:]
    new = (ka, va, kb_ref[...], vb_ref[...])
    for c, (o_ref, val) in enumerate(zip((nak_ref, nav_ref, nbk_ref, nbv_ref), new)):
        if n_prev:
            for p in range(n_prev):
                o_ref[p] = prev[4 * p + c][...]
            o_ref[n_prev] = val
        else:
            o_ref[...] = val
    _gqa_attention(qa_ref, _gqa_variants(ka), _gqa_variants(va), y_ref.at[:, :N_HEADS_A * HEAD_DIM], joint_pv=False)
    for pair in range(N_HEADS_B // 2):
        cols = slice(pair * LANES, (pair + 1) * LANES)
        q = _scaled_q(qb_ref[:, cols])
        k_lo, k_hi = _pair_halves(kb_ref[:, cols])
        v_lo, v_hi = _pair_halves(vb_ref[:, cols])
        out = _attend_pair(q, [[k_lo], [k_hi]], [[v_lo], [v_hi]], [[None], [None]], joint_pv=False)
        y_ref[:, N_HEADS_A * HEAD_DIM + pair * LANES:N_HEADS_A * HEAD_DIM + (pair + 1) * LANES] = out.astype(BF16)


def _ctx_attention(y, prev_caches):
    wide = 4 * LANES
    widths = (LANES, LANES, wide, wide)
    n_prev = len(prev_caches)
    row = lambda b: (b, 0)
    f32 = lambda *s: jax.ShapeDtypeStruct(s, F32)
    in_specs = [
        pl.BlockSpec((CTX_LEN, wide), lambda b: (b, 0)),
        pl.BlockSpec((CTX_LEN, wide), lambda b: (b, 1)),
        pl.BlockSpec((CTX_LEN, wide), lambda b: (b, 2)),
        pl.BlockSpec((CTX_LEN, wide), lambda b: (b, 3)),
        pl.BlockSpec((CTX_LEN, 2 * LANES), lambda b: (b, COL_KA // 2)),
    ]
    args = [y, y, y, y, y]
    for layer_caches in prev_caches:
        in_specs += [pl.BlockSpec((CTX_LEN, w), row) for w in widths]
        args += list(layer_caches)
    if n_prev:
        cache_specs = [pl.BlockSpec((None, n_prev + 1, CTX_LEN, w), lambda b: (b, 0, 0, 0)) for w in widths]
        cache_shapes = [f32(N_CTX_SETS, n_prev + 1, CTX_LEN, w) for w in widths]
    else:
        cache_specs = [pl.BlockSpec((CTX_LEN, w), row) for w in widths]
        cache_shapes = [f32(T_CTX, w) for w in widths]
    return pl.pallas_call(
        functools.partial(_ctx_attn_kernel, n_prev=n_prev),
        grid=(N_CTX_SETS,),
        in_specs=in_specs,
        out_specs=[pl.BlockSpec((CTX_LEN, 2 * wide), row)] + cache_specs,
        out_shape=[jax.ShapeDtypeStruct((T_CTX, 2 * wide), BF16)] + cache_shapes,
        compiler_params=_params(1),
        name="ctx_attention",
    )(*args)


LAT_TQ = 256
LAT_QT = LAT_LEN // LAT_TQ


def _lat_gqa_kernel(qa_ref, kava_ref, ck_ref, cv_ref, o_ref, k_s, v_s):
    @pl.when(pl.program_id(1) == 0)
    def _():
        for dst, cached, new in ((k_s, ck_ref[...], kava_ref[:, :LANES]), (v_s, cv_ref[...], kava_ref[:, LANES:])):
            for i, (c, x) in enumerate(zip(_gqa_variants(cached), _gqa_variants(new))):
                dst[i, :PAST_LEN, :] = c
                dst[i, PAST_LEN:, :] = x

    _gqa_attention(qa_ref, [k_s[i] for i in range(4)], [v_s[i] for i in range(4)], o_ref, joint_pv=True)


def _lat_gqa_attention(y, cache_k, cache_v, layer):
    wide = 4 * LANES
    first = T_CTX // LAT_TQ
    cache = pl.BlockSpec((None, None, PAST_LEN, LANES), lambda b, t: (b, layer, 0, 0))
    return pl.pallas_call(
        _lat_gqa_kernel,
        grid=(N_LAT_SETS, LAT_QT),
        in_specs=[
            pl.BlockSpec((LAT_TQ, wide), lambda b, t: (first + b * LAT_QT + t, 0)),
            pl.BlockSpec((LAT_LEN, 2 * LANES), lambda b, t: (T_CTX // LAT_LEN + b, COL_KA // 2)),
            cache, cache,
        ],
        out_specs=pl.BlockSpec((LAT_TQ, wide), lambda b, t: (b * LAT_QT + t, 0)),
        out_shape=jax.ShapeDtypeStruct((T_LAT, wide), BF16),
        scratch_shapes=[pltpu.VMEM((4, PAST_LEN + LAT_LEN, LANES), BF16)] * 2,
        compiler_params=_params(2),
        name="lat_gqa_attention",
    )(y, y, cache_k.reshape(N_LAT_SETS, DEPTH, PAST_LEN, LANES), cache_v.reshape(N_LAT_SETS, DEPTH, PAST_LEN, LANES))


N_DR = 2 * WIN_ROWS - 1
N_DC = 2 * WIN_COLS - 1
ROWS_PER_TQ = LAT_TQ // GRID_W
NA_LOCAL = 768
NA_LOCAL_BLOCKS = NA_LOCAL // LAT_TQ
NA_TOE_ROWS = 32
NA_PAIRS = 2


def _na_key_base(qt):
    return jnp.where(qt < LAT_QT // 2, 0, LAT_QT - NA_LOCAL_BLOCKS)


def _window_mask():
    r = np.arange(GRID_ROWS)
    row_start = np.clip(r - WIN_ROWS // 2, 0, GRID_ROWS - WIN_ROWS)
    in_rows = (r[None, :] >= row_start[:, None]) & (r[None, :] < row_start[:, None] + WIN_ROWS)
    cq = np.arange(GRID_W)
    col_start = np.clip(cq - WIN_COLS // 2, 0, GRID_W - WIN_COLS)
    in_cols = (cq[None, :] >= col_start[:, None]) & (cq[None, :] < col_start[:, None] + WIN_COLS)
    valid = (in_rows[:, None, :, None] & in_cols[None, :, None, :]).reshape(LAT_LEN, LAT_LEN)
    tiles = []
    for qt in range(LAT_QT):
        base = (0 if qt < LAT_QT // 2 else LAT_QT - NA_LOCAL_BLOCKS) * LAT_TQ
        tile = valid[qt * LAT_TQ:(qt + 1) * LAT_TQ]
        assert not tile[:, :base].any() and not tile[:, base + NA_LOCAL:].any()
        tiles.append(tile[:, base:base + NA_LOCAL])
    return np.where(np.stack(tiles), 0.0, NEG).astype(np.float32)


def _toeplitz_select():
    j = np.arange(LANES)
    c = np.clip(j - (GRID_W - 1), -(WIN_COLS - 1), WIN_COLS - 1) + (WIN_COLS - 1)
    return (np.arange(LANES)[:, None] == c[None, :]).astype(np.float32)


def _na_kernel(q_ref, k_ref, v_ref, ck_ref, cv_ref, mask_ref, rpb_ref, sel_ref, o_ref,
               toe_ref, bias_ref, k_s, v_s, ck_s, cv_s):
    b, qt = pl.program_id(1), pl.program_id(2)
    n_heads = 2 * NA_PAIRS

    @pl.when((qt == 0) & (b == 0))
    def _():
        low = lax.broadcasted_iota(jnp.int32, (GRID_W, LANES), 1) < GRID_W
        sel = sel_ref[...].astype(BF16)
        for hd in range(n_heads):
            gen = sum(_bdot(piece, sel) for piece in _split3(rpb_ref[hd]))
            for dd in range(N_DR + 1):
                lo = jnp.broadcast_to(gen[dd:dd + 1, :], (GRID_W, LANES))
                hi = jnp.broadcast_to(gen[dd + 1:dd + 2, :], (GRID_W, LANES))
                lo = pltpu.roll(lo, LANES - (GRID_W - 1), 1, stride=1, stride_axis=0)
                hi = pltpu.roll(hi, 1, 1, stride=1, stride_axis=0)
                toe_ref[hd, dd] = jnp.where(low, lo, hi)

    @pl.when(qt == 0)
    def _():
        for src, dst in ((k_ref, k_s), (v_ref, v_s), (ck_ref, ck_s), (cv_ref, cv_s)):
            for pair in range(NA_PAIRS):
                lo, hi = _pair_halves(src[:, pair * LANES:(pair + 1) * LANES])
                dst[2 * pair] = lo
                dst[2 * pair + 1] = hi

    base = _na_key_base(qt)
    local = pl.ds(pl.multiple_of(base * LAT_TQ, LAT_TQ), NA_LOCAL)
    for pair in range(NA_PAIRS):
        q = _scaled_q(q_ref[:, pair * LANES:(pair + 1) * LANES])
        heads = (2 * pair, 2 * pair + 1)
        biases = []
        for hd in heads:
            for rr in range(ROWS_PER_TQ):
                for kp in range(NA_LOCAL // LANES):
                    d = 2 * (kp + base * (ROWS_PER_TQ // 2)) - (qt * ROWS_PER_TQ + rr) + (WIN_ROWS - 1)
                    dd = jnp.clip(d, -1, N_DR - 1) + 1
                    bias_ref[hd, rr * GRID_W:(rr + 1) * GRID_W, kp * LANES:(kp + 1) * LANES] = toe_ref[hd, dd]
            biases.append([bias_ref[hd] + mask_ref[...], None])
        out = _attend_pair(q, [[k_s[hd, local, :], ck_s[hd]] for hd in heads],
                           [[v_s[hd, local, :], cv_s[hd]] for hd in heads], biases)
        o_ref[:, pair * LANES:(pair + 1) * LANES] = out.astype(o_ref.dtype)


def _lat_na_attention(y, cache_k, cache_v, rpb, layer):
    first = T_CTX // LAT_TQ
    kv_row = T_CTX // LAT_LEN
    wide = N_HEADS_B * HEAD_DIM
    gen = jnp.pad(rpb[layer], ((0, 0), (1, NA_TOE_ROWS - N_DR - 1), (0, LANES - N_DC)))
    w = NA_PAIRS * LANES
    nh = 2 * NA_PAIRS
    cq, ck, cv = COL_QB // NA_PAIRS, COL_KB // NA_PAIRS, COL_VB // NA_PAIRS
    return pl.pallas_call(
        _na_kernel,
        grid=(N_HEADS_B // nh, N_LAT_SETS, LAT_QT),
        in_specs=[
            pl.BlockSpec((LAT_TQ, w), lambda g, b, t: (first + b * LAT_QT + t, cq + g)),
            pl.BlockSpec((LAT_LEN, w), lambda g, b, t: (kv_row + b, ck + g)),
            pl.BlockSpec((LAT_LEN, w), lambda g, b, t: (kv_row + b, cv + g)),
            pl.BlockSpec((None, None, PAST_LEN, w), lambda g, b, t: (b, layer, 0, g)),
            pl.BlockSpec((None, None, PAST_LEN, w), lambda g, b, t: (b, layer, 0, g)),
            pl.BlockSpec((None, LAT_TQ, NA_LOCAL), lambda g, b, t: (t, 0, 0)),
            pl.BlockSpec((nh, NA_TOE_ROWS, LANES), lambda g, b, t: (g, 0, 0)),
            pl.BlockSpec((LANES, LANES), lambda g, b, t: (0, 0)),
        ],
        out_specs=pl.BlockSpec((LAT_TQ, w), lambda g, b, t: (b * LAT_QT + t, g)),
        out_shape=jax.ShapeDtypeStruct((T_LAT, wide), BF16),
        scratch_shapes=[pltpu.VMEM((nh, N_DR + 1, GRID_W, LANES), F32), pltpu.VMEM((nh, LAT_TQ, NA_LOCAL), F32),
                        pltpu.VMEM((nh, LAT_LEN, LANES), BF16), pltpu.VMEM((nh, LAT_LEN, LANES), BF16),
                        pltpu.VMEM((nh, PAST_LEN, LANES), BF16), pltpu.VMEM((nh, PAST_LEN, LANES), BF16)],
        compiler_params=_params(3),
        name="lat_na_attention",
    )(y, y, y, cache_k.reshape(N_LAT_SETS, DEPTH, PAST_LEN, wide), cache_v.reshape(N_LAT_SETS, DEPTH, PAST_LEN, wide),
      jnp.asarray(_window_mask()), gen, jnp.asarray(_toeplitz_select()))


MERGE_TM = 1024
MERGE_SUB = 1024
MERGE_ROUTE_TM = 512
MERGE_ROUTE_SUB = 512


def _merge_kernel(*refs, n_y, route_sets, sub):
    y_refs, refs = refs[:n_y], refs[n_y:]
    (h_ref, x_ref, gt1_ref, sh2_ref, sc2_ref, nf_ref, wab, wbb, wob, wgb, wr_ref), refs = refs[:11], refs[11:]
    if route_sets:
        (rlg_ref, rh_ref), refs = refs[:2], refs[2:]
    (x1_ref, h2_ref, lg_ref), refs = refs[:3], refs[3:]
    if route_sets:
        xg_ref, g_ref, rc_ref, p_s, rt_s = refs

    wr_hi, wr_lo = _split2(wr_ref[...])
    wr_both = jnp.concatenate([wr_hi, wr_lo], axis=0)
    half = N_HEADS_A * HEAD_DIM
    n_sub = x_ref.shape[0] // sub
    for r in range(n_sub):
        rows = slice(r * sub, (r + 1) * sub)
        if n_y == 1:
            ya, yb = y_refs[0][rows, :half], y_refs[0][rows, half:]
        else:
            ya, yb = y_refs[0][rows, :], y_refs[1][rows, :]
        sets = list(range(r * route_sets // n_sub, (r + 1) * route_sets // n_sub))

        def route(k):
            tok = slice(k * CTX_LEN, (k + 1) * CTX_LEN)
            slots = slice(k * CAP_CTX, (k + 1) * CAP_CTX)
            _route_set(rlg_ref[:, tok], rh_ref[tok, :], xg_ref.at[:, slots, :], g_ref.at[:, slots, :],
                       rc_ref.at[tok, :], p_s.at[k], rt_s.at[k], CTX_LEN, CAP_CTX)

        gates = _bdot(h_ref[rows, :], wgb[...])
        za = _bdot(ya, wab[...])
        zb = _bdot(yb, wbb[...])
        for k in sets[:len(sets) // 2]:
            route(k)
        m = jax.nn.sigmoid(gates[:, :D_MODEL]) * za + jax.nn.sigmoid(gates[:, D_MODEL:]) * zb
        x1 = x_ref[rows, :] + gt1_ref[...] * _bdot(m.astype(BF16), wob[...])
        x1_ref[rows, :] = x1
        h2 = _modnorm(x1, nf_ref[...], sh2_ref[...], sc2_ref[...])
        h2_ref[rows, :] = h2.astype(BF16)
        hh, hl = _split2(h2)
        both = _bdot_nt(wr_both, hh)
        lg_ref[:, rows] = both[:N_EXPERTS, :] + both[N_EXPERTS:, :] + _bdot_nt(wr_hi, hl)
        for k in sets[len(sets) // 2:]:
            route(k)


def _merge(y_parts, h, x, mod, norm_ffn, merge_w, w_router_t, layer, row_fn, tm, sub, route=None):
    n = x.shape[0]
    tiles = n // tm
    once = pl.Buffered(1)
    weight = lambda w: pl.BlockSpec(w.shape, lambda i: (0, 0), pipeline_mode=once)
    tile = lambda w: pl.BlockSpec((tm, w), lambda i: (i, 0))
    in_specs = [tile(p.shape[1]) for p in y_parts] + [
        tile(D_MODEL), tile(D_MODEL),
        _mod_spec(layer, row_fn, MOD_GT1), _mod_spec(layer, row_fn, MOD_SH2), _mod_spec(layer, row_fn, MOD_SC2),
        pl.BlockSpec((None, 1, D_MODEL), lambda i: (layer, 0, 0)),
    ] + [weight(w) for w in merge_w] + [
        pl.BlockSpec((None, N_EXPERTS, D_MODEL), lambda i: (layer, 0, 0), pipeline_mode=once)]
    args = list(y_parts) + [h, x, mod, mod, mod, norm_ffn, *merge_w, w_router_t]
    out_specs = [tile(D_MODEL), tile(D_MODEL), pl.BlockSpec((N_EXPERTS, tm), lambda i: (0, i))]
    out_shape = [jax.ShapeDtypeStruct((n, D_MODEL), F32), jax.ShapeDtypeStruct((n, D_MODEL), BF16),
                 jax.ShapeDtypeStruct((N_EXPERTS, n), F32)]
    scratch = []
    route_sets = 0
    if route is not None:
        route_sets = N_CTX_SETS // tiles
        rows, slots = route_sets * CTX_LEN, route_sets * CAP_CTX
        in_specs += [pl.BlockSpec((N_EXPERTS, rows), lambda i: (0, i)), pl.BlockSpec((rows, D_MODEL), lambda i: (i, 0))]
        args += list(route)
        out_specs += [pl.BlockSpec((N_EXPERTS, slots, D_MODEL), lambda i: (0, i, 0)),
                      pl.BlockSpec((N_EXPERTS, slots, LANES), lambda i: (0, i, 0)),
                      pl.BlockSpec((rows, LANES), lambda i: (i, 0))]
        out_shape += [jax.ShapeDtypeStruct((N_EXPERTS, N_CTX_SETS * CAP_CTX, D_MODEL), BF16),
                      jax.ShapeDtypeStruct((N_EXPERTS, N_CTX_SETS * CAP_CTX, LANES), F32),
                      jax.ShapeDtypeStruct((T_CTX, LANES), F32)]
        scratch += [pltpu.VMEM((route_sets, N_EXPERTS * CAP_CTX, CTX_LEN), BF16),
                    pltpu.VMEM((route_sets, LANES, CTX_LEN), F32)]
    return pl.pallas_call(
        functools.partial(_merge_kernel, n_y=len(y_parts), route_sets=route_sets, sub=sub),
        grid=(tiles,),
        in_specs=in_specs,
        out_specs=out_specs,
        out_shape=out_shape,
        scratch_shapes=scratch,
        compiler_params=_params(1),
        name="merge_route" if route_sets else "merge",
    )(*args)


GATHER_M = 512
RANK_TILE = 128


def _rank_row(aff, a_row, e, n):
    tiles = n // RANK_TILE
    sub = lax.broadcasted_iota(jnp.int32, (RANK_TILE, RANK_TILE), 0)
    lane = lax.broadcasted_iota(jnp.int32, (RANK_TILE, RANK_TILE), 1)
    earlier = jnp.where(sub < lane, 1.0, 0.0)
    acc = [jnp.zeros((8, RANK_TILE), F32) for _ in range(tiles)]
    for c in range(tiles):
        a_col = jnp.broadcast_to(aff[c * RANK_TILE:(c + 1) * RANK_TILE, e:e + 1], (RANK_TILE, RANK_TILE))
        for j in range(tiles):
            a_rj = a_row[:, j * RANK_TILE:(j + 1) * RANK_TILE]
            if c < j:
                beats = jnp.where(a_col >= a_rj, 1.0, 0.0)
            elif c > j:
                beats = jnp.where(a_col > a_rj, 1.0, 0.0)
            else:
                beats = jnp.where(a_col > a_rj, 1.0, jnp.where(a_col == a_rj, earlier, 0.0))
            acc[j] = acc[j] + beats.reshape(RANK_TILE // 8, 8, RANK_TILE).sum(axis=0)
    return jnp.concatenate([a.sum(axis=0, keepdims=True) for a in acc], axis=1)


def _route_set(lg, h, xg_ref, g_ref, rc_ref, p_ref, rt_ref, n, cap):
    ex = jnp.exp(lg - lg.max(axis=0, keepdims=True))
    aff_t = ex / ex.sum(axis=0, keepdims=True)
    aff = jnp.concatenate([aff_t, jnp.zeros((LANES - N_EXPERTS, n), F32)], axis=0).T
    rt_ref[...] = jnp.full((LANES, n), float(cap), F32)
    slot = lax.broadcasted_iota(jnp.int32, (cap, n), 0).astype(F32)
    for e in range(N_EXPERTS):
        rank = _rank_row(aff, aff_t[e:e + 1, :], e, n)
        rt_ref[e:e + 1, :] = jnp.minimum(rank, float(cap))
        p_ref[e * cap:(e + 1) * cap, :] = jnp.where(rank == slot, 1.0, 0.0).astype(BF16)

    a1, a2, a3 = (p.astype(F32) for p in _split3(aff))
    packed = (a1 + pltpu.roll(a2, N_EXPERTS, 1) + pltpu.roll(a3, 2 * N_EXPERTS, 1)).astype(BF16)
    per = GATHER_M // cap
    glane = lax.broadcasted_iota(jnp.int32, (cap, LANES), 1)
    for grp in range(N_EXPERTS * cap // GATHER_M):
        p = p_ref[grp * GATHER_M:(grp + 1) * GATHER_M, :]
        xg = _bdot(p, h).astype(BF16)
        gg = _bdot(p, packed)
        for k in range(per):
            e = grp * per + k
            xg_ref[e] = xg[k * cap:(k + 1) * cap, :]
            mine = (glane < 3 * N_EXPERTS) & ((glane & (N_EXPERTS - 1)) == e)
            ge = jnp.where(mine, gg[k * cap:(k + 1) * cap, :], 0.0).sum(axis=-1, keepdims=True)
            g_ref[e] = jnp.broadcast_to(ge, (cap, LANES))
    rc_ref[...] = rt_ref[...].T


def _route_kernel(lg_ref, h_ref, xg_ref, g_ref, rc_ref, p_ref, rt_ref, *, n, cap):
    _route_set(lg_ref[...], h_ref[...], xg_ref, g_ref, rc_ref, p_ref, rt_ref, n, cap)


def _route(logits, h2, n, cap, n_sets):
    return pl.pallas_call(
        functools.partial(_route_kernel, n=n, cap=cap),
        grid=(n_sets,),
        in_specs=[
            pl.BlockSpec((N_EXPERTS, n), lambda s: (0, s)),
            pl.BlockSpec((n, D_MODEL), lambda s: (s, 0)),
        ],
        out_specs=[
            pl.BlockSpec((N_EXPERTS, cap, D_MODEL), lambda s: (0, s, 0)),
            pl.BlockSpec((N_EXPERTS, cap, LANES), lambda s: (0, s, 0)),
            pl.BlockSpec((n, LANES), lambda s: (s, 0)),
        ],
        out_shape=[jax.ShapeDtypeStruct((N_EXPERTS, n_sets * cap, D_MODEL), BF16),
                   jax.ShapeDtypeStruct((N_EXPERTS, n_sets * cap, LANES), F32),
                   jax.ShapeDtypeStruct((n_sets * n, LANES), F32)],
        scratch_shapes=[pltpu.VMEM((N_EXPERTS * cap, n), BF16), pltpu.VMEM((LANES, n), F32)],
        compiler_params=_params(1),
        name=f"route_n{n}",
    )(logits, h2)


EXPERT_TF = 1024
EXPERT_SUB = 256
N_FF_TILES = EXPERT_FF // EXPERT_TF


def _expert_kernel(xc_ref, xl_ref, gc_ref, gl_ref, wg_ref, wu_ref, wd_ref, o_ref, x_s, acc_s):
    f = pl.program_id(1)
    n_ctx = xc_ref.shape[0]

    @pl.when(f == 0)
    def _():
        x_s[:n_ctx, :] = xc_ref[...]
        x_s[n_ctx:, :] = xl_ref[...]
        acc_s[...] = jnp.zeros_like(acc_s)

    x = x_s[...]
    part = None
    for j in range(EXPERT_TF // EXPERT_SUB):
        cols = slice(j * EXPERT_SUB, (j + 1) * EXPERT_SUB)
        gate = _bdot(x, wg_ref[:, cols].astype(BF16))
        up = _bdot(x, wu_ref[:, cols].astype(BF16))
        hid = (gate * jax.nn.sigmoid(gate)) * up
        down = _bdot(hid.astype(BF16), wd_ref[cols, :].astype(BF16))
        part = down if part is None else part + down
    acc_s[...] += part

    @pl.when(f == N_FF_TILES - 1)
    def _():
        o_ref[:n_ctx, :] = (acc_s[:n_ctx, :] * gc_ref[:, :1]).astype(o_ref.dtype)
        o_ref[n_ctx:, :] = (acc_s[n_ctx:, :] * gl_ref[:, :1]).astype(o_ref.dtype)


def _experts(xg_ctx, xg_lat, g_ctx, g_lat, w_gate, w_up, w_down, layer):
    sc, sl = xg_ctx.shape[1], xg_lat.shape[1]
    slots = lambda s, w: pl.BlockSpec((None, s, w), lambda e, f: (e, 0, 0))
    return pl.pallas_call(
        _expert_kernel,
        grid=(N_EXPERTS, N_FF_TILES),
        in_specs=[
            slots(sc, D_MODEL), slots(sl, D_MODEL), slots(sc, LANES), slots(sl, LANES),
            pl.BlockSpec((None, None, D_MODEL, EXPERT_TF), lambda e, f: (layer, e, 0, f)),
            pl.BlockSpec((None, None, D_MODEL, EXPERT_TF), lambda e, f: (layer, e, 0, f)),
            pl.BlockSpec((None, None, EXPERT_TF, D_MODEL), lambda e, f: (layer, e, f, 0)),
        ],
        out_specs=slots(sc + sl, D_MODEL),
        out_shape=jax.ShapeDtypeStruct((N_EXPERTS, sc + sl, D_MODEL), BF16),
        scratch_shapes=[pltpu.VMEM((sc + sl, D_MODEL), BF16), pltpu.VMEM((sc + sl, D_MODEL), F32)],
        compiler_params=_params(2),
        name="experts",
    )(xg_ctx, xg_lat, g_ctx, g_lat, w_gate, w_up, w_down)


COMB_TM = 256


def _combine_kernel(o_ref, rc_ref, x_ref, gt2_ref, ng_ref, *rest, cap, final):
    if final:
        (y_ref,) = rest
    else:
        sh_ref, sc_ref, xn_ref, hn_ref = rest
    slots = N_EXPERTS * cap

    j = lax.broadcasted_iota(jnp.int32, (LANES, slots), 1)
    e = lax.broadcasted_iota(jnp.int32, (LANES, slots), 0)
    expand = jnp.where(j // cap == e, 1.0, 0.0).astype(BF16)
    rank = _bdot(rc_ref[...].astype(BF16), expand)
    slot = (lax.broadcasted_iota(jnp.int32, (1, slots), 1) % cap).astype(F32)
    pt = jnp.where(rank == slot, 1.0, 0.0).astype(BF16)
    ffn = _bdot(pt, o_ref[...].reshape(slots, D_MODEL))
    x = x_ref[...] + gt2_ref[...] * ffn
    if final:
        y_ref[...] = _rms(x) * ng_ref[...]
    else:
        xn_ref[...] = x
        hn_ref[...] = _modnorm(x, ng_ref[...], sh_ref[...], sc_ref[...]).astype(BF16)


def _combine(out, rc, x1, mod, norm_g, layer, n, cap, n_sets, first_slot_block, row_fn, final):
    tiles = n // COMB_TM
    rows = lambda s, t: (s * tiles + t, 0)
    in_specs = [
        pl.BlockSpec((N_EXPERTS, cap, D_MODEL), lambda s, t: (0, first_slot_block + s, 0)),
        pl.BlockSpec((COMB_TM, LANES), rows),
        pl.BlockSpec((COMB_TM, D_MODEL), rows),
        _mod_spec(layer, row_fn, MOD_GT2),
    ]
    args = [out, rc, x1, mod]
    if final:
        in_specs.append(pl.BlockSpec((1, D_MODEL), lambda s, t: (0, 0)))
        args.append(norm_g.reshape(1, D_MODEL))
        out_specs = pl.BlockSpec((COMB_TM, D_MODEL), rows)
        out_shape = jax.ShapeDtypeStruct((n_sets * n, D_MODEL), F32)
    else:
        in_specs += [pl.BlockSpec((None, 1, D_MODEL), lambda s, t: (layer + 1, 0, 0)),
                     _mod_spec(layer + 1, row_fn, MOD_SH1), _mod_spec(layer + 1, row_fn, MOD_SC1)]
        args += [norm_g, mod, mod]
        out_specs = [pl.BlockSpec((COMB_TM, D_MODEL), rows), pl.BlockSpec((COMB_TM, D_MODEL), rows)]
        out_shape = [jax.ShapeDtypeStruct((n_sets * n, D_MODEL), F32),
                     jax.ShapeDtypeStruct((n_sets * n, D_MODEL), BF16)]
    return pl.pallas_call(
        functools.partial(_combine_kernel, cap=cap, final=final),
        grid=(n_sets, tiles),
        in_specs=in_specs,
        out_specs=out_specs,
        out_shape=out_shape,
        compiler_params=_params(2),
        name=f"combine_n{n}",
    )(*args)


def kernel(x_prompt, x_sample, cache_attn_k, cache_attn_v, cache_na_k, cache_na_v, c, c_ctx, w_ada, b_ada,
           norm_mix, norm_ffn, w_in, q_norm, k_norm, rpb, w_branch_a, w_branch_b, w_out, w_router, w_gate,
           w_up, w_down, final_norm):
    x_ctx = x_prompt.reshape(T_CTX, D_MODEL)
    x_lat = x_sample.reshape(T_LAT, D_MODEL)
    mod = _modulation(c, c_ctx, w_ada, b_ada)
    norm_mix3 = norm_mix.reshape(DEPTH, 1, D_MODEL)
    norm_ffn3 = norm_ffn.reshape(DEPTH, 1, D_MODEL)
    w_router_t = jnp.swapaxes(w_router, 1, 2)
    ctx_row = lambda *g: 0
    lat_comb_row = lambda s, t: 1 + s

    h_ctx = _prenorm(x_ctx, mod, norm_mix3, 0, ctx_row)
    h_lat = _prenorm(x_lat, mod, norm_mix3, 0, _lat_mod_row(NORM_TM))
    layer_caches = []
    for layer in range(DEPTH):
        last = layer == DEPTH - 1
        y, merge_w = _projection(h_ctx, h_lat, w_in, q_norm, k_norm, w_branch_a, w_branch_b, w_out, layer)
        yab_ctx, *new_caches = _ctx_attention(y, layer_caches if last else [])
        layer_caches.append(tuple(new_caches))
        ya_lat = _lat_gqa_attention(y, cache_attn_k, cache_attn_v, layer)
        yb_lat = _lat_na_attention(y, cache_na_k, cache_na_v, rpb, layer)
        merge_args = (mod, norm_ffn3, merge_w, w_router_t, layer)
        x1_ctx, h2_ctx, lg_ctx = _merge([yab_ctx], h_ctx, x_ctx, *merge_args, ctx_row, MERGE_TM, MERGE_SUB)
        x1_lat, h2_lat, lg_lat, xg_ctx, g_ctx, rc_ctx = _merge(
            [ya_lat, yb_lat], h_lat, x_lat, *merge_args, _lat_mod_row(MERGE_ROUTE_TM), MERGE_ROUTE_TM,
            MERGE_ROUTE_SUB, route=(lg_ctx, h2_ctx))
        xg_lat, g_lat, rc_lat = _route(lg_lat, h2_lat, LAT_LEN, CAP_LAT, N_LAT_SETS)
        out = _experts(xg_ctx, xg_lat, g_ctx, g_lat, w_gate, w_up, w_down, layer)
        final = layer == DEPTH - 1
        norm_g = final_norm if final else norm_mix3
        res_ctx = _combine(out, rc_ctx, x1_ctx, mod, norm_g, layer, CTX_LEN, CAP_CTX, N_CTX_SETS, 0, ctx_row, final)
        res_lat = _combine(out, rc_lat, x1_lat, mod, norm_g, layer, LAT_LEN, CAP_LAT, N_LAT_SETS,
                           N_CTX_SETS * CAP_CTX // CAP_LAT, lat_comb_row, final)
        if final:
            y_ctx, y_lat = res_ctx, res_lat
        else:
            (x_ctx, h_ctx), (x_lat, h_lat) = res_ctx, res_lat

    heads = (N_KV_A, N_KV_A, N_HEADS_B, N_HEADS_B)
    new_caches = [a.reshape(N_CTX_SETS, DEPTH, CTX_LEN, h, HEAD_DIM) for a, h in zip(layer_caches[-1], heads)]
    return (y_ctx.reshape(N_CTX_SETS, CTX_LEN, D_MODEL), y_lat.reshape(N_LAT_SETS, LAT_LEN, D_MODEL), *new_caches)
```

```python
import functools

import numpy as np
import jax
import jax.numpy as jnp
from jax import lax
from jax.experimental import pallas as pl
from jax.experimental.pallas import tpu as pltpu

F32 = jnp.float32
BF16 = jnp.bfloat16

D_MODEL = 1024
N_CTX_SETS, CTX_LEN = 16, 256
N_LAT_SETS, LAT_LEN = 2, 1024
T_CTX = N_CTX_SETS * CTX_LEN
T_LAT = N_LAT_SETS * LAT_LEN
T_ALL = T_CTX + T_LAT
DEPTH = 2
PAST_LEN = 512
GRID_W = 64
GRID_ROWS = LAT_LEN // GRID_W
HEAD_DIM = 64
N_HEADS_A, N_KV_A, N_HEADS_B = 8, 2, 8
WIN_ROWS, WIN_COLS = 8, 16
N_EXPERTS = 16
EXPERT_FF = 2048
CAP_CTX = 2 * CTX_LEN // N_EXPERTS
CAP_LAT = 2 * LAT_LEN // N_EXPERTS
ROPE_THETA = 10000.0
EPS = 1e-6
NEG = -1e30
QKV_DIM = 2304
ATT_SCALE = HEAD_DIM ** -0.5

LANES = 128
VMEM_LIMIT = 56 * 1024 * 1024

PROJ_TN = 256
N_PROJ_TILES = QKV_DIM // PROJ_TN
PROJ_PERM = np.array([0, 1, 8, 2, 3, 4, 5, 6, 7], np.int32)
CAST_STEPS = 8
MOD_SH1, MOD_SC1, MOD_GT1, MOD_SH2, MOD_SC2, MOD_GT2 = range(6)


def _params(n_grid_dims, vmem=VMEM_LIMIT):
    return pltpu.CompilerParams(dimension_semantics=("arbitrary",) * n_grid_dims, vmem_limit_bytes=vmem)


def _bdot(a, b):
    return jnp.dot(a, b, preferred_element_type=F32)


def _bdot_nt(a, b):
    return lax.dot_general(a, b, (((1,), (1,)), ((), ())), preferred_element_type=F32)


def _split2(x):
    hi = x.astype(BF16)
    lo = (x - hi.astype(F32)).astype(BF16)
    return hi, lo


def _split3(x):
    hi = x.astype(BF16)
    r = x - hi.astype(F32)
    mid = r.astype(BF16)
    lo = (r - mid.astype(F32)).astype(BF16)
    return hi, mid, lo


def _rms(x):
    return x * lax.rsqrt(jnp.mean(x * x, axis=-1, keepdims=True) + EPS)


def _modnorm(x, g, sh, sc):
    return (_rms(x) * g) * (1.0 + sc) + sh


def _lat_mod_row(tile_rows):
    return lambda i: 1 + (i * tile_rows) // LAT_LEN


def _mod_spec(layer, row_fn, chunk):
    return pl.BlockSpec((None, None, 1, D_MODEL), lambda *g: (layer, row_fn(*g), 0, chunk))


MOD_TN = 1536


def _mod_kernel(ct_ref, w_ref, b_ref, o_ref):
    ct = ct_ref[...]
    act = ct * jax.nn.sigmoid(ct)
    w = w_ref[...]
    for m in range(3):
        o_ref[m:m + 1, :] = jnp.sum(w * act[:, m:m + 1], axis=0, keepdims=True) + b_ref[...]
    o_ref[3:8, :] = jnp.zeros((5, MOD_TN), F32)


def _modulation(c, c_ctx, w_ada, b_ada):
    cond = jnp.concatenate([c_ctx[None, :], c, jnp.zeros((5, D_MODEL), F32)], axis=0)
    mod = pl.pallas_call(
        _mod_kernel,
        grid=(DEPTH, 6 * D_MODEL // MOD_TN),
        in_specs=[
            pl.BlockSpec((D_MODEL, 8), lambda l, j: (0, 0)),
            pl.BlockSpec((None, D_MODEL, MOD_TN), lambda l, j: (l, 0, j)),
            pl.BlockSpec((None, 1, MOD_TN), lambda l, j: (l, 0, j)),
        ],
        out_specs=pl.BlockSpec((None, 8, MOD_TN), lambda l, j: (l, 0, j)),
        out_shape=jax.ShapeDtypeStruct((DEPTH, 8, 6 * D_MODEL), F32),
        compiler_params=_params(2),
        name="modulation",
    )(cond.T, w_ada, b_ada.reshape(DEPTH, 1, 6 * D_MODEL))
    return mod.reshape(DEPTH, 8, 1, 6 * D_MODEL)


NORM_TM = 512


def _prenorm_kernel(x_ref, g_ref, sh_ref, sc_ref, h_ref):
    h_ref[...] = _modnorm(x_ref[...], g_ref[...], sh_ref[...], sc_ref[...]).astype(BF16)


def _prenorm(x, mod, norm_g, layer, row_fn):
    n = x.shape[0]
    return pl.pallas_call(
        _prenorm_kernel,
        grid=(n // NORM_TM,),
        in_specs=[
            pl.BlockSpec((NORM_TM, D_MODEL), lambda i: (i, 0)),
            pl.BlockSpec((None, 1, D_MODEL), lambda i: (layer, 0, 0)),
            _mod_spec(layer, row_fn, MOD_SH1),
            _mod_spec(layer, row_fn, MOD_SC1),
        ],
        out_specs=pl.BlockSpec((NORM_TM, D_MODEL), lambda i: (i, 0)),
        out_shape=jax.ShapeDtypeStruct((n, D_MODEL), BF16),
        compiler_params=_params(1),
        name="prenorm",
    )(x, norm_g, mod, mod)


PROJ_CH = 1024


def _rope_tables():
    t = np.arange(LAT_LEN)
    lane = np.arange(LANES) % HEAD_DIM
    pos = np.where(lane < HEAD_DIM // 2, (t // GRID_W)[:, None], (t % GRID_W)[:, None]).astype(np.float64)
    freq = ROPE_THETA ** (-(lane % 16).astype(np.float64) / 16.0)
    ang = pos * freq[None, :]
    sign = np.where((lane & 16) == 0, -1.0, 1.0)[None, :]
    return np.cos(ang).astype(np.float32), (np.sin(ang) * sign).astype(np.float32)


def _head_norm_rope(y, gain, cos, sin):
    w = y.shape[1]
    r = (lax.broadcasted_iota(jnp.int32, (2 * w, w), 0) % w) // HEAD_DIM
    c = lax.broadcasted_iota(jnp.int32, (2 * w, w), 1) // HEAD_DIM
    seg = jnp.where(r == c, 1.0 / HEAD_DIM, 0.0).astype(BF16)
    ms = _bdot(jnp.concatenate(_split2(y * y), axis=1), seg)
    yn = y * lax.rsqrt(ms + EPS) * gain
    if cos is None:
        return yn
    lane = lax.broadcasted_iota(jnp.int32, yn.shape, 1)
    partner = jnp.where((lane & 16) == 0, pltpu.roll(yn, w - 16, 1), pltpu.roll(yn, 16, 1))
    if w > LANES:
        cos = jnp.concatenate([cos] * (w // LANES), axis=1)
        sin = jnp.concatenate([sin] * (w // LANES), axis=1)
    return yn * cos + partner * sin


def _proj_kernel(perm_ref, hc_ref, hl_ref, w_ref, gain_ref, cos_ref, sin_ref, wa_ref, wbr_ref, wo_ref, wg_ref,
                 o_ref, oa_ref, ob_ref, oo_ref, og_ref, wb_ref):
    del perm_ref
    j = pl.program_id(0)
    wb_ref[...] = w_ref[...].astype(BF16)
    oa_ref[...] = wa_ref[...].astype(BF16)
    ob_ref[...] = wbr_ref[...].astype(BF16)
    oo_ref[...] = wo_ref[...].astype(BF16)
    og_ref[...] = wg_ref[0].astype(BF16)
    chunks = [(hc_ref, k * PROJ_CH, k * PROJ_CH, False) for k in range(T_CTX // PROJ_CH)]
    chunks += [(hl_ref, k * PROJ_CH, T_CTX + k * PROJ_CH, True) for k in range(T_LAT // PROJ_CH)]

    def matmul(chunk):
        h_ref, r0, o0, _ = chunk
        o_ref[o0:o0 + PROJ_CH, :] = _bdot(h_ref[r0:r0 + PROJ_CH, :], wb_ref[...])

    def finish(chunk, width):
        _, _, o0, is_lat = chunk
        cos, sin = (cos_ref[...], sin_ref[...]) if is_lat else (None, None)
        y = o_ref[o0:o0 + PROJ_CH, :width]
        o_ref[o0:o0 + PROJ_CH, :width] = _head_norm_rope(y, gain_ref[:, :width], cos, sin)

    def tile(width):
        matmul(chunks[0])
        for k in range(1, len(chunks)):
            matmul(chunks[k])
            if width:
                finish(chunks[k - 1], width)
        if width:
            finish(chunks[-1], width)

    pl.when(j < 2)(lambda: tile(PROJ_TN))
    pl.when(j == 2)(lambda: tile(LANES))
    pl.when(j > 2)(lambda: tile(0))


def _projection(h_ctx, h_lat, w_in, q_norm, k_norm, w_ba, w_bb, w_out, layer):
    ones = jnp.ones((2 * HEAD_DIM,), F32)
    gain = jnp.stack([jnp.tile(q_norm[layer], 4), jnp.tile(q_norm[layer], 4),
                      jnp.concatenate([jnp.tile(k_norm[layer], 2), ones])])[:, None, :]
    cos, sin = _rope_tables()
    half = N_HEADS_A * HEAD_DIM
    ra, ro = half // CAST_STEPS, D_MODEL // CAST_STEPS
    part = lambda j: jnp.minimum(j, CAST_STEPS - 1)
    grid_spec = pltpu.PrefetchScalarGridSpec(
        num_scalar_prefetch=1,
        grid=(N_PROJ_TILES,),
        in_specs=[
            pl.BlockSpec((T_CTX, D_MODEL), lambda j, p: (0, 0)),
            pl.BlockSpec((T_LAT, D_MODEL), lambda j, p: (0, 0)),
            pl.BlockSpec((None, D_MODEL, PROJ_TN), lambda j, p: (layer, 0, j)),
            pl.BlockSpec((None, 1, PROJ_TN), lambda j, p: (jnp.minimum(j, 2), 0, 0)),
            pl.BlockSpec((LAT_LEN, LANES), lambda j, p: (0, 0)),
            pl.BlockSpec((LAT_LEN, LANES), lambda j, p: (0, 0)),
            pl.BlockSpec((None, ra, D_MODEL), lambda j, p: (layer, part(j), 0)),
            pl.BlockSpec((None, ra, D_MODEL), lambda j, p: (layer, part(j), 0)),
            pl.BlockSpec((None, ro, D_MODEL), lambda j, p: (layer, part(j), 0)),
            pl.BlockSpec((pl.Element(1), pl.Element(ro), pl.Element(2 * D_MODEL)),
                         lambda j, p: (layer, part(j) * ro, QKV_DIM)),
        ],
        out_specs=[
            pl.BlockSpec((T_ALL, PROJ_TN), lambda j, p: (0, p[j])),
            pl.BlockSpec((ra, D_MODEL), lambda j, p: (part(j), 0)),
            pl.BlockSpec((ra, D_MODEL), lambda j, p: (part(j), 0)),
            pl.BlockSpec((ro, D_MODEL), lambda j, p: (part(j), 0)),
            pl.BlockSpec((ro, 2 * D_MODEL), lambda j, p: (part(j), 0)),
        ],
        scratch_shapes=[pltpu.VMEM((D_MODEL, PROJ_TN), BF16)],
    )
    y, *merge_w = pl.pallas_call(
        _proj_kernel,
        grid_spec=grid_spec,
        out_shape=[jax.ShapeDtypeStruct((T_ALL, QKV_DIM), F32),
                   jax.ShapeDtypeStruct((half, D_MODEL), BF16), jax.ShapeDtypeStruct((half, D_MODEL), BF16),
                   jax.ShapeDtypeStruct((D_MODEL, D_MODEL), BF16), jax.ShapeDtypeStruct((D_MODEL, 2 * D_MODEL), BF16)],
        compiler_params=_params(1),
        name="projection",
    )(jnp.asarray(PROJ_PERM), h_ctx, h_lat, w_in, gain, jnp.asarray(cos), jnp.asarray(sin), w_ba, w_bb, w_out, w_in)
    return y, merge_w


COL_QA, COL_QB, COL_KB, COL_VB, COL_KA, COL_VA = 0, 4, 8, 12, 16, 17


def _lane_is_low(shape):
    return lax.broadcasted_iota(jnp.int32, shape, 1) < HEAD_DIM


def _pair_halves(x):
    low = _lane_is_low(x.shape)
    xb = x.astype(BF16)
    zero = jnp.zeros_like(xb)
    return jnp.where(low, xb, zero), jnp.where(low, zero, xb)


def _scaled_q(q):
    assert ATT_SCALE == 0.125
    return (q * ATT_SCALE).astype(BF16)


def _attend_pair(q, keys, values, biases, joint_pv=True):
    probs, dens, out = [], [], None
    for h in range(2):
        scores = []
        for k, b in zip(keys[h], biases[h]):
            s = _bdot_nt(q, k)
            scores.append(s if b is None else s + b)
        m = scores[0].max(axis=-1, keepdims=True)
        for s in scores[1:]:
            m = jnp.maximum(m, s.max(axis=-1, keepdims=True))
        den, num = None, None
        for s, v in zip(scores, values[h]):
            e = jnp.exp(s - m)
            d = e.sum(axis=-1, keepdims=True)
            den = d if den is None else den + d
            if joint_pv:
                probs.append(e.astype(BF16))
            else:
                o = _bdot(e.astype(BF16), v)
                num = o if num is None else num + o
        dens.append(den)
        if not joint_pv:
            out = num / den if out is None else out + num / den
    if not joint_pv:
        return out
    num = _bdot(jnp.concatenate(probs, axis=1), jnp.concatenate(values[0] + values[1], axis=0))
    return num / jnp.where(_lane_is_low(num.shape), dens[0], dens[1])


def _gqa_variants(x):
    lo, hi = _pair_halves(x)
    sw_lo, sw_hi = _pair_halves(pltpu.roll(x, HEAD_DIM, 1))
    return [lo, sw_hi, sw_lo, hi]


def _gqa_attention(q_ref, kvar, vvar, o_ref, joint_pv):
    for pair in range(N_HEADS_A // 2):
        q = _scaled_q(q_ref[:, pair * LANES:(pair + 1) * LANES])
        kvh = (2 * pair) // (N_HEADS_A // N_KV_A)
        out = _attend_pair(q, [[kvar[2 * kvh]], [kvar[2 * kvh + 1]]], [[vvar[2 * kvh]], [vvar[2 * kvh + 1]]],
                           [[None], [None]], joint_pv)
        o_ref[:, pair * LANES:(pair + 1) * LANES] = out.astype(o_ref.dtype)


def _ctx_attn_kernel(qa_ref, qb_ref, kb_ref, vb_ref, kava_ref, *refs, n_prev):
    prev = refs[:4 * n_prev]
    y_ref, nak_ref, nav_ref, nbk_ref, nbv_ref = refs[4 * n_prev:]
    ka = kava_ref[:, :LANES]
    va = kava_ref[:, LANES:]
    new = (ka, va, kb_ref[...], vb_ref[...])
    for c, (o_ref, val) in enumerate(zip((nak_ref, nav_ref, nbk_ref, nbv_ref), new)):
        if n_prev:
            for p in range(n_prev):
                o_ref[p] = prev[4 * p + c][...]
            o_ref[n_prev] = val
        else:
            o_ref[...] = val
    _gqa_attention(qa_ref, _gqa_variants(ka), _gqa_variants(va), y_ref.at[:, :N_HEADS_A * HEAD_DIM], joint_pv=False)
    for pair in range(N_HEADS_B // 2):
        cols = slice(pair * LANES, (pair + 1) * LANES)
        q = _scaled_q(qb_ref[:, cols])
        k_lo, k_hi = _pair_halves(kb_ref[:, cols])
        v_lo, v_hi = _pair_halves(vb_ref[:, cols])
        out = _attend_pair(q, [[k_lo], [k_hi]], [[v_lo], [v_hi]], [[None], [None]], joint_pv=False)
        y_ref[:, N_HEADS_A * HEAD_DIM + pair * LANES:N_HEADS_A * HEAD_DIM + (pair + 1) * LANES] = out.astype(BF16)


def _ctx_attention(y, prev_caches):
    wide = 4 * LANES
    widths = (LANES, LANES, wide, wide)
    n_prev = len(prev_caches)
    row = lambda b: (b, 0)
    f32 = lambda *s: jax.ShapeDtypeStruct(s, F32)
    in_specs = [
        pl.BlockSpec((CTX_LEN, wide), lambda b: (b, 0)),
        pl.BlockSpec((CTX_LEN, wide), lambda b: (b, 1)),
        pl.BlockSpec((CTX_LEN, wide), lambda b: (b, 2)),
        pl.BlockSpec((CTX_LEN, wide), lambda b: (b, 3)),
        pl.BlockSpec((CTX_LEN, 2 * LANES), lambda b: (b, COL_KA // 2)),
    ]
    args = [y, y, y, y, y]
    for layer_caches in prev_caches:
        in_specs += [pl.BlockSpec((CTX_LEN, w), row) for w in widths]
        args += list(layer_caches)
    if n_prev:
        cache_specs = [pl.BlockSpec((None, n_prev + 1, CTX_LEN, w), lambda b: (b, 0, 0, 0)) for w in widths]
        cache_shapes = [f32(N_CTX_SETS, n_prev + 1, CTX_LEN, w) for w in widths]
    else:
        cache_specs = [pl.BlockSpec((CTX_LEN, w), row) for w in widths]
        cache_shapes = [f32(T_CTX, w) for w in widths]
    return pl.pallas_call(
        functools.partial(_ctx_attn_kernel, n_prev=n_prev),
        grid=(N_CTX_SETS,),
        in_specs=in_specs,
        out_specs=[pl.BlockSpec((CTX_LEN, 2 * wide), row)] + cache_specs,
        out_shape=[jax.ShapeDtypeStruct((T_CTX, 2 * wide), BF16)] + cache_shapes,
        compiler_params=_params(1),
        name="ctx_attention",
    )(*args)


LAT_TQ = 256
LAT_QT = LAT_LEN // LAT_TQ
GQA_TQ = 512


def _lat_gqa_kernel(qa_ref, kava_ref, ck_ref, cv_ref, o_ref, k_s, v_s):
    @pl.when(pl.program_id(1) == 0)
    def _():
        for dst, cached, new in ((k_s, ck_ref[...], kava_ref[:, :LANES]), (v_s, cv_ref[...], kava_ref[:, LANES:])):
            for i, (c, x) in enumerate(zip(_gqa_variants(cached), _gqa_variants(new))):
                dst[i, :PAST_LEN, :] = c
                dst[i, PAST_LEN:, :] = x

    _gqa_attention(qa_ref, [k_s[i] for i in range(4)], [v_s[i] for i in range(4)], o_ref, joint_pv=True)


def _lat_gqa_attention(y, cache_k, cache_v, layer):
    wide = 4 * LANES
    tiles = LAT_LEN // GQA_TQ
    first = T_CTX // GQA_TQ
    cache = pl.BlockSpec((None, None, PAST_LEN, LANES), lambda b, t: (b, layer, 0, 0))
    return pl.pallas_call(
        _lat_gqa_kernel,
        grid=(N_LAT_SETS, tiles),
        in_specs=[
            pl.BlockSpec((GQA_TQ, wide), lambda b, t: (first + b * tiles + t, 0)),
            pl.BlockSpec((LAT_LEN, 2 * LANES), lambda b, t: (T_CTX // LAT_LEN + b, COL_KA // 2)),
            cache, cache,
        ],
        out_specs=pl.BlockSpec((GQA_TQ, wide), lambda b, t: (b * tiles + t, 0)),
        out_shape=jax.ShapeDtypeStruct((T_LAT, wide), BF16),
        scratch_shapes=[pltpu.VMEM((4, PAST_LEN + LAT_LEN, LANES), BF16)] * 2,
        compiler_params=_params(2),
        name="lat_gqa_attention",
    )(y, y, cache_k.reshape(N_LAT_SETS, DEPTH, PAST_LEN, LANES), cache_v.reshape(N_LAT_SETS, DEPTH, PAST_LEN, LANES))


N_DR = 2 * WIN_ROWS - 1
N_DC = 2 * WIN_COLS - 1
ROWS_PER_TQ = LAT_TQ // GRID_W
NA_LOCAL = 768
NA_LOCAL_BLOCKS = NA_LOCAL // LAT_TQ
NA_TOE_ROWS = 32
NA_PAIRS = 2


def _na_key_base(qt):
    return jnp.where(qt < LAT_QT // 2, 0, LAT_QT - NA_LOCAL_BLOCKS)


def _window_mask():
    r = np.arange(GRID_ROWS)
    row_start = np.clip(r - WIN_ROWS // 2, 0, GRID_ROWS - WIN_ROWS)
    in_rows = (r[None, :] >= row_start[:, None]) & (r[None, :] < row_start[:, None] + WIN_ROWS)
    cq = np.arange(GRID_W)
    col_start = np.clip(cq - WIN_COLS // 2, 0, GRID_W - WIN_COLS)
    in_cols = (cq[None, :] >= col_start[:, None]) & (cq[None, :] < col_start[:, None] + WIN_COLS)
    valid = (in_rows[:, None, :, None] & in_cols[None, :, None, :]).reshape(LAT_LEN, LAT_LEN)
    tiles = []
    for qt in range(LAT_QT):
        base = (0 if qt < LAT_QT // 2 else LAT_QT - NA_LOCAL_BLOCKS) * LAT_TQ
        tile = valid[qt * LAT_TQ:(qt + 1) * LAT_TQ]
        assert not tile[:, :base].any() and not tile[:, base + NA_LOCAL:].any()
        tiles.append(tile[:, base:base + NA_LOCAL])
    return np.where(np.stack(tiles), 0.0, NEG).astype(np.float32)


def _toeplitz_select():
    j = np.arange(LANES)
    c = np.clip(j - (GRID_W - 1), -(WIN_COLS - 1), WIN_COLS - 1) + (WIN_COLS - 1)
    return (np.arange(LANES)[:, None] == c[None, :]).astype(np.float32)


def _na_kernel(q_ref, k_ref, v_ref, ck_ref, cv_ref, mask_ref, rpb_ref, sel_ref, o_ref,
               toe_ref, bias_ref, k_s, v_s, ck_s, cv_s):
    b, qt = pl.program_id(1), pl.program_id(2)
    n_heads = 2 * NA_PAIRS

    @pl.when((qt == 0) & (b == 0))
    def _():
        low = lax.broadcasted_iota(jnp.int32, (GRID_W, LANES), 1) < GRID_W
        sel = sel_ref[...].astype(BF16)
        for hd in range(n_heads):
            gen = sum(_bdot(piece, sel) for piece in _split3(rpb_ref[hd]))
            for dd in range(N_DR + 1):
                lo = jnp.broadcast_to(gen[dd:dd + 1, :], (GRID_W, LANES))
                hi = jnp.broadcast_to(gen[dd + 1:dd + 2, :], (GRID_W, LANES))
                lo = pltpu.roll(lo, LANES - (GRID_W - 1), 1, stride=1, stride_axis=0)
                hi = pltpu.roll(hi, 1, 1, stride=1, stride_axis=0)
                toe_ref[hd, dd] = jnp.where(low, lo, hi)

    @pl.when(qt == 0)
    def _():
        for src, dst in ((k_ref, k_s), (v_ref, v_s), (ck_ref, ck_s), (cv_ref, cv_s)):
            for pair in range(NA_PAIRS):
                lo, hi = _pair_halves(src[:, pair * LANES:(pair + 1) * LANES])
                dst[2 * pair] = lo
                dst[2 * pair + 1] = hi

    base = _na_key_base(qt)
    local = pl.ds(pl.multiple_of(base * LAT_TQ, LAT_TQ), NA_LOCAL)
    for pair in range(NA_PAIRS):
        q = _scaled_q(q_ref[:, pair * LANES:(pair + 1) * LANES])
        heads = (2 * pair, 2 * pair + 1)
        biases = []
        for hd in heads:
            for rr in range(ROWS_PER_TQ):
                for kp in range(NA_LOCAL // LANES):
                    d = 2 * (kp + base * (ROWS_PER_TQ // 2)) - (qt * ROWS_PER_TQ + rr) + (WIN_ROWS - 1)
                    dd = jnp.clip(d, -1, N_DR - 1) + 1
                    bias_ref[hd, rr * GRID_W:(rr + 1) * GRID_W, kp * LANES:(kp + 1) * LANES] = toe_ref[hd, dd]
            biases.append([bias_ref[hd] + mask_ref[...], None])
        out = _attend_pair(q, [[k_s[hd, local, :], ck_s[hd]] for hd in heads],
                           [[v_s[hd, local, :], cv_s[hd]] for hd in heads], biases)
        o_ref[:, pair * LANES:(pair + 1) * LANES] = out.astype(o_ref.dtype)


def _lat_na_attention(y, cache_k, cache_v, rpb, layer):
    first = T_CTX // LAT_TQ
    kv_row = T_CTX // LAT_LEN
    wide = N_HEADS_B * HEAD_DIM
    gen = jnp.pad(rpb[layer], ((0, 0), (1, NA_TOE_ROWS - N_DR - 1), (0, LANES - N_DC)))
    w = NA_PAIRS * LANES
    nh = 2 * NA_PAIRS
    cq, ck, cv = COL_QB // NA_PAIRS, COL_KB // NA_PAIRS, COL_VB // NA_PAIRS
    return pl.pallas_call(
        _na_kernel,
        grid=(N_HEADS_B // nh, N_LAT_SETS, LAT_QT),
        in_specs=[
            pl.BlockSpec((LAT_TQ, w), lambda g, b, t: (first + b * LAT_QT + t, cq + g)),
            pl.BlockSpec((LAT_LEN, w), lambda g, b, t: (kv_row + b, ck + g)),
            pl.BlockSpec((LAT_LEN, w), lambda g, b, t: (kv_row + b, cv + g)),
            pl.BlockSpec((None, None, PAST_LEN, w), lambda g, b, t: (b, layer, 0, g)),
            pl.BlockSpec((None, None, PAST_LEN, w), lambda g, b, t: (b, layer, 0, g)),
            pl.BlockSpec((None, LAT_TQ, NA_LOCAL), lambda g, b, t: (t, 0, 0)),
            pl.BlockSpec((nh, NA_TOE_ROWS, LANES), lambda g, b, t: (g, 0, 0)),
            pl.BlockSpec((LANES, LANES), lambda g, b, t: (0, 0)),
        ],
        out_specs=pl.BlockSpec((LAT_TQ, w), lambda g, b, t: (b * LAT_QT + t, g)),
        out_shape=jax.ShapeDtypeStruct((T_LAT, wide), BF16),
        scratch_shapes=[pltpu.VMEM((nh, N_DR + 1, GRID_W, LANES), F32), pltpu.VMEM((nh, LAT_TQ, NA_LOCAL), F32),
                        pltpu.VMEM((nh, LAT_LEN, LANES), BF16), pltpu.VMEM((nh, LAT_LEN, LANES), BF16),
                        pltpu.VMEM((nh, PAST_LEN, LANES), BF16), pltpu.VMEM((nh, PAST_LEN, LANES), BF16)],
        compiler_params=_params(3),
        name="lat_na_attention",
    )(y, y, y, cache_k.reshape(N_LAT_SETS, DEPTH, PAST_LEN, wide), cache_v.reshape(N_LAT_SETS, DEPTH, PAST_LEN, wide),
      jnp.asarray(_window_mask()), gen, jnp.asarray(_toeplitz_select()))


MERGE_TM = 1024
MERGE_SUB = 1024
MERGE_ROUTE_TM = 512
MERGE_ROUTE_SUB = 512


def _merge_kernel(*refs, n_y, route_sets, sub):
    y_refs, refs = refs[:n_y], refs[n_y:]
    (h_ref, x_ref, gt1_ref, sh2_ref, sc2_ref, nf_ref, wab, wbb, wob, wgb, wr_ref), refs = refs[:11], refs[11:]
    if route_sets:
        (rlg_ref, rh_ref), refs = refs[:2], refs[2:]
    (x1_ref, h2_ref, lg_ref), refs = refs[:3], refs[3:]
    if route_sets:
        xg_ref, g_ref, rc_ref, p_s, rt_s = refs

    wr_hi, wr_lo = _split2(wr_ref[...])
    wr_both = jnp.concatenate([wr_hi, wr_lo], axis=0)
    half = N_HEADS_A * HEAD_DIM
    n_sub = x_ref.shape[0] // sub
    for r in range(n_sub):
        rows = slice(r * sub, (r + 1) * sub)
        if n_y == 1:
            ya, yb = y_refs[0][rows, :half], y_refs[0][rows, half:]
        else:
            ya, yb = y_refs[0][rows, :], y_refs[1][rows, :]
        sets = list(range(r * route_sets // n_sub, (r + 1) * route_sets // n_sub))

        def route(k):
            tok = slice(k * CTX_LEN, (k + 1) * CTX_LEN)
            slots = slice(k * CAP_CTX, (k + 1) * CAP_CTX)
            _route_set(rlg_ref[:, tok], rh_ref[tok, :], xg_ref.at[:, slots, :], g_ref.at[:, slots, :],
                       rc_ref.at[tok, :], p_s.at[k], rt_s.at[k], CTX_LEN, CAP_CTX)

        gates = _bdot(h_ref[rows, :], wgb[...])
        za = _bdot(ya, wab[...])
        zb = _bdot(yb, wbb[...])
        for k in sets[:len(sets) // 2]:
            route(k)
        m = jax.nn.sigmoid(gates[:, :D_MODEL]) * za + jax.nn.sigmoid(gates[:, D_MODEL:]) * zb
        x1 = x_ref[rows, :] + gt1_ref[...] * _bdot(m.astype(BF16), wob[...])
        x1_ref[rows, :] = x1
        h2 = _modnorm(x1, nf_ref[...], sh2_ref[...], sc2_ref[...])
        h2_ref[rows, :] = h2.astype(BF16)
        hh, hl = _split2(h2)
        both = _bdot_nt(wr_both, hh)
        lg_ref[:, rows] = both[:N_EXPERTS, :] + both[N_EXPERTS:, :] + _bdot_nt(wr_hi, hl)
        for k in sets[len(sets) // 2:]:
            route(k)


def _merge(y_parts, h, x, mod, norm_ffn, merge_w, w_router_t, layer, row_fn, tm, sub, route=None):
    n = x.shape[0]
    tiles = n // tm
    once = pl.Buffered(1)
    weight = lambda w: pl.BlockSpec(w.shape, lambda i: (0, 0), pipeline_mode=once)
    tile = lambda w: pl.BlockSpec((tm, w), lambda i: (i, 0))
    in_specs = [tile(p.shape[1]) for p in y_parts] + [
        tile(D_MODEL), tile(D_MODEL),
        _mod_spec(layer, row_fn, MOD_GT1), _mod_spec(layer, row_fn, MOD_SH2), _mod_spec(layer, row_fn, MOD_SC2),
        pl.BlockSpec((None, 1, D_MODEL), lambda i: (layer, 0, 0)),
    ] + [weight(w) for w in merge_w] + [
        pl.BlockSpec((None, N_EXPERTS, D_MODEL), lambda i: (layer, 0, 0), pipeline_mode=once)]
    args = list(y_parts) + [h, x, mod, mod, mod, norm_ffn, *merge_w, w_router_t]
    out_specs = [tile(D_MODEL), tile(D_MODEL), pl.BlockSpec((N_EXPERTS, tm), lambda i: (0, i))]
    out_shape = [jax.ShapeDtypeStruct((n, D_MODEL), F32), jax.ShapeDtypeStruct((n, D_MODEL), BF16),
                 jax.ShapeDtypeStruct((N_EXPERTS, n), F32)]
    scratch = []
    route_sets = 0
    if route is not None:
        route_sets = N_CTX_SETS // tiles
        rows, slots = route_sets * CTX_LEN, route_sets * CAP_CTX
        in_specs += [pl.BlockSpec((N_EXPERTS, rows), lambda i: (0, i)), pl.BlockSpec((rows, D_MODEL), lambda i: (i, 0))]
        args += list(route)
        out_specs += [pl.BlockSpec((N_EXPERTS, slots, D_MODEL), lambda i: (0, i, 0)),
                      pl.BlockSpec((N_EXPERTS, slots, LANES), lambda i: (0, i, 0)),
                      pl.BlockSpec((rows, LANES), lambda i: (i, 0))]
        out_shape += [jax.ShapeDtypeStruct((N_EXPERTS, N_CTX_SETS * CAP_CTX, D_MODEL), BF16),
                      jax.ShapeDtypeStruct((N_EXPERTS, N_CTX_SETS * CAP_CTX, LANES), F32),
                      jax.ShapeDtypeStruct((T_CTX, LANES), F32)]
        scratch += [pltpu.VMEM((route_sets, N_EXPERTS * CAP_CTX, CTX_LEN), BF16),
                    pltpu.VMEM((route_sets, LANES, CTX_LEN), F32)]
    return pl.pallas_call(
        functools.partial(_merge_kernel, n_y=len(y_parts), route_sets=route_sets, sub=sub),
        grid=(tiles,),
        in_specs=in_specs,
        out_specs=out_specs,
        out_shape=out_shape,
        scratch_shapes=scratch,
        compiler_params=_params(1),
        name="merge_route" if route_sets else "merge",
    )(*args)


GATHER_M = 512
RANK_TILE = 128


def _rank_row(aff, a_row, e, n):
    tiles = n // RANK_TILE
    sub = lax.broadcasted_iota(jnp.int32, (RANK_TILE, RANK_TILE), 0)
    lane = lax.broadcasted_iota(jnp.int32, (RANK_TILE, RANK_TILE), 1)
    earlier = jnp.where(sub < lane, 1.0, 0.0)
    acc = [jnp.zeros((8, RANK_TILE), F32) for _ in range(tiles)]
    for c in range(tiles):
        a_col = jnp.broadcast_to(aff[c * RANK_TILE:(c + 1) * RANK_TILE, e:e + 1], (RANK_TILE, RANK_TILE))
        for j in range(tiles):
            a_rj = a_row[:, j * RANK_TILE:(j + 1) * RANK_TILE]
            if c < j:
                beats = jnp.where(a_col >= a_rj, 1.0, 0.0)
            elif c > j:
                beats = jnp.where(a_col > a_rj, 1.0, 0.0)
            else:
                beats = jnp.where(a_col > a_rj, 1.0, jnp.where(a_col == a_rj, earlier, 0.0))
            acc[j] = acc[j] + beats.reshape(RANK_TILE // 8, 8, RANK_TILE).sum(axis=0)
    return jnp.concatenate([a.sum(axis=0, keepdims=True) for a in acc], axis=1)


def _route_set(lg, h, xg_ref, g_ref, rc_ref, p_ref, rt_ref, n, cap):
    ex = jnp.exp(lg - lg.max(axis=0, keepdims=True))
    aff_t = ex / ex.sum(axis=0, keepdims=True)
    aff = jnp.concatenate([aff_t, jnp.zeros((LANES - N_EXPERTS, n), F32)], axis=0).T
    rt_ref[...] = jnp.full((LANES, n), float(cap), F32)
    slot = lax.broadcasted_iota(jnp.int32, (cap, n), 0).astype(F32)
    for e in range(N_EXPERTS):
        rank = _rank_row(aff, aff_t[e:e + 1, :], e, n)
        rt_ref[e:e + 1, :] = jnp.minimum(rank, float(cap))
        p_ref[e * cap:(e + 1) * cap, :] = jnp.where(rank == slot, 1.0, 0.0).astype(BF16)

    a1, a2, a3 = (p.astype(F32) for p in _split3(aff))
    packed = (a1 + pltpu.roll(a2, N_EXPERTS, 1) + pltpu.roll(a3, 2 * N_EXPERTS, 1)).astype(BF16)
    per = GATHER_M // cap
    glane = lax.broadcasted_iota(jnp.int32, (cap, LANES), 1)
    for grp in range(N_EXPERTS * cap // GATHER_M):
        p = p_ref[grp * GATHER_M:(grp + 1) * GATHER_M, :]
        xg = _bdot(p, h).astype(BF16)
        gg = _bdot(p, packed)
        for k in range(per):
            e = grp * per + k
            xg_ref[e] = xg[k * cap:(k + 1) * cap, :]
            mine = (glane < 3 * N_EXPERTS) & ((glane & (N_EXPERTS - 1)) == e)
            ge = jnp.where(mine, gg[k * cap:(k + 1) * cap, :], 0.0).sum(axis=-1, keepdims=True)
            g_ref[e] = jnp.broadcast_to(ge, (cap, LANES))
    rc_ref[...] = rt_ref[...].T


def _route_kernel(lg_ref, h_ref, xg_ref, g_ref, rc_ref, p_ref, rt_ref, *, n, cap):
    _route_set(lg_ref[...], h_ref[...], xg_ref, g_ref, rc_ref, p_ref, rt_ref, n, cap)


def _route(logits, h2, n, cap, n_sets):
    return pl.pallas_call(
        functools.partial(_route_kernel, n=n, cap=cap),
        grid=(n_sets,),
        in_specs=[
            pl.BlockSpec((N_EXPERTS, n), lambda s: (0, s)),
            pl.BlockSpec((n, D_MODEL), lambda s: (s, 0)),
        ],
        out_specs=[
            pl.BlockSpec((N_EXPERTS, cap, D_MODEL), lambda s: (0, s, 0)),
            pl.BlockSpec((N_EXPERTS, cap, LANES), lambda s: (0, s, 0)),
            pl.BlockSpec((n, LANES), lambda s: (s, 0)),
        ],
        out_shape=[jax.ShapeDtypeStruct((N_EXPERTS, n_sets * cap, D_MODEL), BF16),
                   jax.ShapeDtypeStruct((N_EXPERTS, n_sets * cap, LANES), F32),
                   jax.ShapeDtypeStruct((n_sets * n, LANES), F32)],
        scratch_shapes=[pltpu.VMEM((N_EXPERTS * cap, n), BF16), pltpu.VMEM((LANES, n), F32)],
        compiler_params=_params(1),
        name=f"route_n{n}",
    )(logits, h2)


EXPERT_TF = 1024
EXPERT_SUB = 256
N_FF_TILES = EXPERT_FF // EXPERT_TF


def _expert_kernel(xc_ref, xl_ref, gc_ref, gl_ref, wg_ref, wu_ref, wd_ref, o_ref, x_s, acc_s):
    f = pl.program_id(1)
    n_ctx = xc_ref.shape[0]

    @pl.when(f == 0)
    def _():
        x_s[:n_ctx, :] = xc_ref[...]
        x_s[n_ctx:, :] = xl_ref[...]
        acc_s[...] = jnp.zeros_like(acc_s)

    x = x_s[...]
    part = None
    for j in range(EXPERT_TF // EXPERT_SUB):
        cols = slice(j * EXPERT_SUB, (j + 1) * EXPERT_SUB)
        gate = _bdot(x, wg_ref[:, cols].astype(BF16))
        up = _bdot(x, wu_ref[:, cols].astype(BF16))
        hid = (gate * jax.nn.sigmoid(gate)) * up
        down = _bdot(hid.astype(BF16), wd_ref[cols, :].astype(BF16))
        part = down if part is None else part + down
    acc_s[...] += part

    @pl.when(f == N_FF_TILES - 1)
    def _():
        o_ref[:n_ctx, :] = (acc_s[:n_ctx, :] * gc_ref[:, :1]).astype(o_ref.dtype)
        o_ref[n_ctx:, :] = (acc_s[n_ctx:, :] * gl_ref[:, :1]).astype(o_ref.dtype)


def _experts(xg_ctx, xg_lat, g_ctx, g_lat, w_gate, w_up, w_down, layer):
    sc, sl = xg_ctx.shape[1], xg_lat.shape[1]
    slots = lambda s, w: pl.BlockSpec((None, s, w), lambda e, f: (e, 0, 0))
    return pl.pallas_call(
        _expert_kernel,
        grid=(N_EXPERTS, N_FF_TILES),
        in_specs=[
            slots(sc, D_MODEL), slots(sl, D_MODEL), slots(sc, LANES), slots(sl, LANES),
            pl.BlockSpec((None, None, D_MODEL, EXPERT_TF), lambda e, f: (layer, e, 0, f)),
            pl.BlockSpec((None, None, D_MODEL, EXPERT_TF), lambda e, f: (layer, e, 0, f)),
            pl.BlockSpec((None, None, EXPERT_TF, D_MODEL), lambda e, f: (layer, e, f, 0)),
        ],
        out_specs=slots(sc + sl, D_MODEL),
        out_shape=jax.ShapeDtypeStruct((N_EXPERTS, sc + sl, D_MODEL), BF16),
        scratch_shapes=[pltpu.VMEM((sc + sl, D_MODEL), BF16), pltpu.VMEM((sc + sl, D_MODEL), F32)],
        compiler_params=_params(2),
        name="experts",
    )(xg_ctx, xg_lat, g_ctx, g_lat, w_gate, w_up, w_down)


COMB_TM = 256
COMB_CTX_SETS = 4


def _combine_kernel(o_ref, rc_ref, x_ref, gt2_ref, ng_ref, *rest, cap, final, sets):
    if final:
        (y_ref,) = rest
    else:
        sh_ref, sc_ref, xn_ref, hn_ref = rest
    slots = N_EXPERTS * cap

    j = lax.broadcasted_iota(jnp.int32, (LANES, slots), 1)
    e = lax.broadcasted_iota(jnp.int32, (LANES, slots), 0)
    expand = jnp.where(j // cap == e, 1.0, 0.0).astype(BF16)
    slot = (lax.broadcasted_iota(jnp.int32, (1, slots), 1) % cap).astype(F32)
    for k in range(sets):
        rows = slice(k * COMB_TM, (k + 1) * COMB_TM)
        rank = _bdot(rc_ref[rows, :].astype(BF16), expand)
        pt = jnp.where(rank == slot, 1.0, 0.0).astype(BF16)
        ffn = _bdot(pt, o_ref[:, k * cap:(k + 1) * cap, :].reshape(slots, D_MODEL))
        x = x_ref[rows, :] + gt2_ref[...] * ffn
        if final:
            y_ref[rows, :] = _rms(x) * ng_ref[...]
        else:
            xn_ref[rows, :] = x
            hn_ref[rows, :] = _modnorm(x, ng_ref[...], sh_ref[...], sc_ref[...]).astype(BF16)


def _combine(out, rc, x1, mod, norm_g, layer, n, cap, n_sets, sets, first_slot_block, row_fn, final):
    tiles = n // COMB_TM
    assert sets == 1 or tiles == 1
    tm = sets * COMB_TM
    rows = lambda s, t: (s * tiles + t, 0)
    in_specs = [
        pl.BlockSpec((N_EXPERTS, sets * cap, D_MODEL), lambda s, t: (0, first_slot_block + s, 0)),
        pl.BlockSpec((tm, LANES), rows),
        pl.BlockSpec((tm, D_MODEL), rows),
        _mod_spec(layer, row_fn, MOD_GT2),
    ]
    args = [out, rc, x1, mod]
    if final:
        in_specs.append(pl.BlockSpec((1, D_MODEL), lambda s, t: (0, 0)))
        args.append(norm_g.reshape(1, D_MODEL))
        out_specs = pl.BlockSpec((tm, D_MODEL), rows)
        out_shape = jax.ShapeDtypeStruct((n_sets * n, D_MODEL), F32)
    else:
        in_specs += [pl.BlockSpec((None, 1, D_MODEL), lambda s, t: (layer + 1, 0, 0)),
                     _mod_spec(layer + 1, row_fn, MOD_SH1), _mod_spec(layer + 1, row_fn, MOD_SC1)]
        args += [norm_g, mod, mod]
        out_specs = [pl.BlockSpec((tm, D_MODEL), rows), pl.BlockSpec((tm, D_MODEL), rows)]
        out_shape = [jax.ShapeDtypeStruct((n_sets * n, D_MODEL), F32),
                     jax.ShapeDtypeStruct((n_sets * n, D_MODEL), BF16)]
    return pl.pallas_call(
        functools.partial(_combine_kernel, cap=cap, final=final, sets=sets),
        grid=(n_sets // sets, tiles),
        in_specs=in_specs,
        out_specs=out_specs,
        out_shape=out_shape,
        compiler_params=_params(2),
        name=f"combine_n{n}",
    )(*args)


def kernel(x_prompt, x_sample, cache_attn_k, cache_attn_v, cache_na_k, cache_na_v, c, c_ctx, w_ada, b_ada,
           norm_mix, norm_ffn, w_in, q_norm, k_norm, rpb, w_branch_a, w_branch_b, w_out, w_router, w_gate,
           w_up, w_down, final_norm):
    x_ctx = x_prompt.reshape(T_CTX, D_MODEL)
    x_lat = x_sample.reshape(T_LAT, D_MODEL)
    mod = _modulation(c, c_ctx, w_ada, b_ada)
    norm_mix3 = norm_mix.reshape(DEPTH, 1, D_MODEL)
    norm_ffn3 = norm_ffn.reshape(DEPTH, 1, D_MODEL)
    w_router_t = jnp.swapaxes(w_router, 1, 2)
    ctx_row = lambda *g: 0
    lat_comb_row = lambda s, t: 1 + s

    h_ctx = _prenorm(x_ctx, mod, norm_mix3, 0, ctx_row)
    h_lat = _prenorm(x_lat, mod, norm_mix3, 0, _lat_mod_row(NORM_TM))
    layer_caches = []
    for layer in range(DEPTH):
        last = layer == DEPTH - 1
        y, merge_w = _projection(h_ctx, h_lat, w_in, q_norm, k_norm, w_branch_a, w_branch_b, w_out, layer)
        yab_ctx, *new_caches = _ctx_attention(y, layer_caches if last else [])
        layer_caches.append(tuple(new_caches))
        ya_lat = _lat_gqa_attention(y, cache_attn_k, cache_attn_v, layer)
        yb_lat = _lat_na_attention(y, cache_na_k, cache_na_v, rpb, layer)
        merge_args = (mod, norm_ffn3, merge_w, w_router_t, layer)
        x1_ctx, h2_ctx, lg_ctx = _merge([yab_ctx], h_ctx, x_ctx, *merge_args, ctx_row, MERGE_TM, MERGE_SUB)
        x1_lat, h2_lat, lg_lat, xg_ctx, g_ctx, rc_ctx = _merge(
            [ya_lat, yb_lat], h_lat, x_lat, *merge_args, _lat_mod_row(MERGE_ROUTE_TM), MERGE_ROUTE_TM,
            MERGE_ROUTE_SUB, route=(lg_ctx, h2_ctx))
        xg_lat, g_lat, rc_lat = _route(lg_lat, h2_lat, LAT_LEN, CAP_LAT, N_LAT_SETS)
        out = _experts(xg_ctx, xg_lat, g_ctx, g_lat, w_gate, w_up, w_down, layer)
        final = layer == DEPTH - 1
        norm_g = final_norm if final else norm_mix3
        res_ctx = _combine(out, rc_ctx, x1_ctx, mod, norm_g, layer, CTX_LEN, CAP_CTX, N_CTX_SETS, COMB_CTX_SETS, 0,
                           ctx_row, final)
        res_lat = _combine(out, rc_lat, x1_lat, mod, norm_g, layer, LAT_LEN, CAP_LAT, N_LAT_SETS, 1,
                           N_CTX_SETS * CAP_CTX // CAP_LAT, lat_comb_row, final)
        if final:
            y_ctx, y_lat = res_ctx, res_lat
        else:
            (x_ctx, h_ctx), (x_lat, h_lat) = res_ctx, res_lat

    heads = (N_KV_A, N_KV_A, N_HEADS_B, N_HEADS_B)
    new_caches = [a.reshape(N_CTX_SETS, DEPTH, CTX_LEN, h, HEAD_DIM) for a, h in zip(layer_caches[-1], heads)]
    return (y_ctx.reshape(N_CTX_SETS, CTX_LEN, D_MODEL), y_lat.reshape(N_LAT_SETS, LAT_LEN, D_MODEL), *new_caches)
```

```python
import functools

import numpy as np
import jax
import jax.numpy as jnp
from jax import lax
from jax.experimental import pallas as pl
from jax.experimental.pallas import tpu as pltpu

F32 = jnp.float32
BF16 = jnp.bfloat16

D_MODEL = 1024
N_CTX_SETS, CTX_LEN = 16, 256
N_LAT_SETS, LAT_LEN = 2, 1024
T_CTX = N_CTX_SETS * CTX_LEN
T_LAT = N_LAT_SETS * LAT_LEN
T_ALL = T_CTX + T_LAT
DEPTH = 2
PAST_LEN = 512
GRID_W = 64
GRID_ROWS = LAT_LEN // GRID_W
HEAD_DIM = 64
N_HEADS_A, N_KV_A, N_HEADS_B = 8, 2, 8
WIN_ROWS, WIN_COLS = 8, 16
N_EXPERTS = 16
EXPERT_FF = 2048
CAP_CTX = 2 * CTX_LEN // N_EXPERTS
CAP_LAT = 2 * LAT_LEN // N_EXPERTS
ROPE_THETA = 10000.0
EPS = 1e-6
NEG = -1e30
QKV_DIM = 2304
ATT_SCALE = HEAD_DIM ** -0.5

LANES = 128
VMEM_LIMIT = 56 * 1024 * 1024

PROJ_TN = 256
N_PROJ_TILES = QKV_DIM // PROJ_TN
PROJ_PERM = np.array([0, 1, 8, 2, 3, 4, 5, 6, 7], np.int32)
CAST_STEPS = 8
MOD_SH1, MOD_SC1, MOD_GT1, MOD_SH2, MOD_SC2, MOD_GT2 = range(6)


def _params(n_grid_dims, vmem=VMEM_LIMIT):
    return pltpu.CompilerParams(dimension_semantics=("arbitrary",) * n_grid_dims, vmem_limit_bytes=vmem)


def _bdot(a, b):
    return jnp.dot(a, b, preferred_element_type=F32)


def _bdot_nt(a, b):
    return lax.dot_general(a, b, (((1,), (1,)), ((), ())), preferred_element_type=F32)


def _split2(x):
    hi = x.astype(BF16)
    lo = (x - hi.astype(F32)).astype(BF16)
    return hi, lo


def _split3(x):
    hi = x.astype(BF16)
    r = x - hi.astype(F32)
    mid = r.astype(BF16)
    lo = (r - mid.astype(F32)).astype(BF16)
    return hi, mid, lo


def _rms(x):
    return x * lax.rsqrt(jnp.mean(x * x, axis=-1, keepdims=True) + EPS)


def _modnorm(x, g, sh, sc):
    return (_rms(x) * g) * (1.0 + sc) + sh


def _lat_mod_row(tile_rows):
    return lambda i: 1 + (i * tile_rows) // LAT_LEN


def _mod_spec(layer, row_fn, chunk):
    return pl.BlockSpec((None, None, 1, D_MODEL), lambda *g: (layer, row_fn(*g), 0, chunk))


MOD_TN = 1536


def _mod_kernel(ct_ref, w_ref, b_ref, o_ref):
    ct = ct_ref[...]
    act = ct * jax.nn.sigmoid(ct)
    w = w_ref[...]
    for m in range(3):
        o_ref[m:m + 1, :] = jnp.sum(w * act[:, m:m + 1], axis=0, keepdims=True) + b_ref[...]
    o_ref[3:8, :] = jnp.zeros((5, MOD_TN), F32)


def _modulation(c, c_ctx, w_ada, b_ada):
    cond = jnp.concatenate([c_ctx[None, :], c, jnp.zeros((5, D_MODEL), F32)], axis=0)
    mod = pl.pallas_call(
        _mod_kernel,
        grid=(DEPTH, 6 * D_MODEL // MOD_TN),
        in_specs=[
            pl.BlockSpec((D_MODEL, 8), lambda l, j: (0, 0)),
            pl.BlockSpec((None, D_MODEL, MOD_TN), lambda l, j: (l, 0, j)),
            pl.BlockSpec((None, 1, MOD_TN), lambda l, j: (l, 0, j)),
        ],
        out_specs=pl.BlockSpec((None, 8, MOD_TN), lambda l, j: (l, 0, j)),
        out_shape=jax.ShapeDtypeStruct((DEPTH, 8, 6 * D_MODEL), F32),
        compiler_params=_params(2),
        name="modulation",
    )(cond.T, w_ada, b_ada.reshape(DEPTH, 1, 6 * D_MODEL))
    return mod.reshape(DEPTH, 8, 1, 6 * D_MODEL)


NORM_TM = 512


def _prenorm_kernel(x_ref, g_ref, sh_ref, sc_ref, h_ref):
    h_ref[...] = _modnorm(x_ref[...], g_ref[...], sh_ref[...], sc_ref[...]).astype(BF16)


def _prenorm(x, mod, norm_g, layer, row_fn):
    n = x.shape[0]
    return pl.pallas_call(
        _prenorm_kernel,
        grid=(n // NORM_TM,),
        in_specs=[
            pl.BlockSpec((NORM_TM, D_MODEL), lambda i: (i, 0)),
            pl.BlockSpec((None, 1, D_MODEL), lambda i: (layer, 0, 0)),
            _mod_spec(layer, row_fn, MOD_SH1),
            _mod_spec(layer, row_fn, MOD_SC1),
        ],
        out_specs=pl.BlockSpec((NORM_TM, D_MODEL), lambda i: (i, 0)),
        out_shape=jax.ShapeDtypeStruct((n, D_MODEL), BF16),
        compiler_params=_params(1),
        name="prenorm",
    )(x, norm_g, mod, mod)


PROJ_CH = 1024


def _rope_tables():
    t = np.arange(LAT_LEN)
    lane = np.arange(LANES) % HEAD_DIM
    pos = np.where(lane < HEAD_DIM // 2, (t // GRID_W)[:, None], (t % GRID_W)[:, None]).astype(np.float64)
    freq = ROPE_THETA ** (-(lane % 16).astype(np.float64) / 16.0)
    ang = pos * freq[None, :]
    sign = np.where((lane & 16) == 0, -1.0, 1.0)[None, :]
    return np.cos(ang).astype(np.float32), (np.sin(ang) * sign).astype(np.float32)


def _head_norm_rope(y, gain, cos, sin):
    w = y.shape[1]
    r = (lax.broadcasted_iota(jnp.int32, (2 * w, w), 0) % w) // HEAD_DIM
    c = lax.broadcasted_iota(jnp.int32, (2 * w, w), 1) // HEAD_DIM
    seg = jnp.where(r == c, 1.0 / HEAD_DIM, 0.0).astype(BF16)
    ms = _bdot(jnp.concatenate(_split2(y * y), axis=1), seg)
    yn = y * lax.rsqrt(ms + EPS) * gain
    if cos is None:
        return yn
    lane = lax.broadcasted_iota(jnp.int32, yn.shape, 1)
    partner = jnp.where((lane & 16) == 0, pltpu.roll(yn, w - 16, 1), pltpu.roll(yn, 16, 1))
    if w > LANES:
        cos = jnp.concatenate([cos] * (w // LANES), axis=1)
        sin = jnp.concatenate([sin] * (w // LANES), axis=1)
    return yn * cos + partner * sin


def _proj_kernel(perm_ref, hc_ref, hl_ref, w_ref, gain_ref, cos_ref, sin_ref, wa_ref, wbr_ref, wo_ref, wg_ref,
                 o_ref, oa_ref, ob_ref, oo_ref, og_ref, wb_ref):
    del perm_ref
    j = pl.program_id(0)
    wb_ref[...] = w_ref[...].astype(BF16)
    oa_ref[...] = wa_ref[...].astype(BF16)
    ob_ref[...] = wbr_ref[...].astype(BF16)
    oo_ref[...] = wo_ref[...].astype(BF16)
    og_ref[...] = wg_ref[0].astype(BF16)
    chunks = [(hc_ref, k * PROJ_CH, k * PROJ_CH, False) for k in range(T_CTX // PROJ_CH)]
    chunks += [(hl_ref, k * PROJ_CH, T_CTX + k * PROJ_CH, True) for k in range(T_LAT // PROJ_CH)]

    def matmul(chunk):
        h_ref, r0, o0, _ = chunk
        o_ref[o0:o0 + PROJ_CH, :] = _bdot(h_ref[r0:r0 + PROJ_CH, :], wb_ref[...])

    def finish(chunk, width):
        _, _, o0, is_lat = chunk
        cos, sin = (cos_ref[...], sin_ref[...]) if is_lat else (None, None)
        y = o_ref[o0:o0 + PROJ_CH, :width]
        o_ref[o0:o0 + PROJ_CH, :width] = _head_norm_rope(y, gain_ref[:, :width], cos, sin)

    def tile(width):
        matmul(chunks[0])
        for k in range(1, len(chunks)):
            matmul(chunks[k])
            if width:
                finish(chunks[k - 1], width)
        if width:
            finish(chunks[-1], width)

    pl.when(j < 2)(lambda: tile(PROJ_TN))
    pl.when(j == 2)(lambda: tile(LANES))
    pl.when(j > 2)(lambda: tile(0))


def _projection(h_ctx, h_lat, w_in, q_norm, k_norm, w_ba, w_bb, w_out, layer):
    ones = jnp.ones((2 * HEAD_DIM,), F32)
    gain = jnp.stack([jnp.tile(q_norm[layer], 4), jnp.tile(q_norm[layer], 4),
                      jnp.concatenate([jnp.tile(k_norm[layer], 2), ones])])[:, None, :]
    cos, sin = _rope_tables()
    half = N_HEADS_A * HEAD_DIM
    ra, ro = half // CAST_STEPS, D_MODEL // CAST_STEPS
    part = lambda j: jnp.minimum(j, CAST_STEPS - 1)
    grid_spec = pltpu.PrefetchScalarGridSpec(
        num_scalar_prefetch=1,
        grid=(N_PROJ_TILES,),
        in_specs=[
            pl.BlockSpec((T_CTX, D_MODEL), lambda j, p: (0, 0)),
            pl.BlockSpec((T_LAT, D_MODEL), lambda j, p: (0, 0)),
            pl.BlockSpec((None, D_MODEL, PROJ_TN), lambda j, p: (layer, 0, j)),
            pl.BlockSpec((None, 1, PROJ_TN), lambda j, p: (jnp.minimum(j, 2), 0, 0)),
            pl.BlockSpec((LAT_LEN, LANES), lambda j, p: (0, 0)),
            pl.BlockSpec((LAT_LEN, LANES), lambda j, p: (0, 0)),
            pl.BlockSpec((None, ra, D_MODEL), lambda j, p: (layer, part(j), 0)),
            pl.BlockSpec((None, ra, D_MODEL), lambda j, p: (layer, part(j), 0)),
            pl.BlockSpec((None, ro, D_MODEL), lambda j, p: (layer, part(j), 0)),
            pl.BlockSpec((pl.Element(1), pl.Element(ro), pl.Element(2 * D_MODEL)),
                         lambda j, p: (layer, part(j) * ro, QKV_DIM)),
        ],
        out_specs=[
            pl.BlockSpec((T_ALL, PROJ_TN), lambda j, p: (0, p[j])),
            pl.BlockSpec((ra, D_MODEL), lambda j, p: (part(j), 0)),
            pl.BlockSpec((ra, D_MODEL), lambda j, p: (part(j), 0)),
            pl.BlockSpec((ro, D_MODEL), lambda j, p: (part(j), 0)),
            pl.BlockSpec((ro, 2 * D_MODEL), lambda j, p: (part(j), 0)),
        ],
        scratch_shapes=[pltpu.VMEM((D_MODEL, PROJ_TN), BF16)],
    )
    y, *merge_w = pl.pallas_call(
        _proj_kernel,
        grid_spec=grid_spec,
        out_shape=[jax.ShapeDtypeStruct((T_ALL, QKV_DIM), F32),
                   jax.ShapeDtypeStruct((half, D_MODEL), BF16), jax.ShapeDtypeStruct((half, D_MODEL), BF16),
                   jax.ShapeDtypeStruct((D_MODEL, D_MODEL), BF16), jax.ShapeDtypeStruct((D_MODEL, 2 * D_MODEL), BF16)],
        compiler_params=_params(1),
        name="projection",
    )(jnp.asarray(PROJ_PERM), h_ctx, h_lat, w_in, gain, jnp.asarray(cos), jnp.asarray(sin), w_ba, w_bb, w_out, w_in)
    return y, merge_w


COL_QA, COL_QB, COL_KB, COL_VB, COL_KA, COL_VA = 0, 4, 8, 12, 16, 17


def _lane_is_low(shape):
    return lax.broadcasted_iota(jnp.int32, shape, 1) < HEAD_DIM


def _pair_halves(x):
    low = _lane_is_low(x.shape)
    xb = x.astype(BF16)
    zero = jnp.zeros_like(xb)
    return jnp.where(low, xb, zero), jnp.where(low, zero, xb)


def _scaled_q(q):
    assert ATT_SCALE == 0.125
    return (q * ATT_SCALE).astype(BF16)


def _attend_pair(q, keys, values, biases, joint_pv=True):
    probs, dens, out = [], [], None
    for h in range(2):
        scores = []
        for k, b in zip(keys[h], biases[h]):
            s = _bdot_nt(q, k)
            scores.append(s if b is None else s + b)
        m = scores[0].max(axis=-1, keepdims=True)
        for s in scores[1:]:
            m = jnp.maximum(m, s.max(axis=-1, keepdims=True))
        den, num = None, None
        for s, v in zip(scores, values[h]):
            e = jnp.exp(s - m)
            d = e.sum(axis=-1, keepdims=True)
            den = d if den is None else den + d
            if joint_pv:
                probs.append(e.astype(BF16))
            else:
                o = _bdot(e.astype(BF16), v)
                num = o if num is None else num + o
        dens.append(den)
        if not joint_pv:
            out = num / den if out is None else out + num / den
    if not joint_pv:
        return out
    num = _bdot(jnp.concatenate(probs, axis=1), jnp.concatenate(values[0] + values[1], axis=0))
    return num / jnp.where(_lane_is_low(num.shape), dens[0], dens[1])


def _attend_pair_t(q, keys, values_t, biases_t):
    probs, dens = [], []
    for h in range(2):
        scores = []
        for k, b in zip(keys[h], biases_t[h]):
            s = _bdot_nt(k, q)
            scores.append(s if b is None else s + b)
        m = scores[0].max(axis=0, keepdims=True)
        for s in scores[1:]:
            m = jnp.maximum(m, s.max(axis=0, keepdims=True))
        den = None
        for s in scores:
            e = jnp.exp(s - m)
            d = e.sum(axis=0, keepdims=True)
            den = d if den is None else den + d
            probs.append(e.astype(BF16))
        dens.append(den)
    num_t = _bdot(values_t, jnp.concatenate(probs, axis=0))
    row = lax.broadcasted_iota(jnp.int32, num_t.shape, 0)
    return (num_t / jnp.where(row < HEAD_DIM, dens[0], dens[1])).T


def _gqa_variants(x):
    lo, hi = _pair_halves(x)
    sw_lo, sw_hi = _pair_halves(pltpu.roll(x, HEAD_DIM, 1))
    return [lo, sw_hi, sw_lo, hi]


def _gqa_attention(q_ref, kvar, vvar, o_ref, joint_pv):
    for pair in range(N_HEADS_A // 2):
        q = _scaled_q(q_ref[:, pair * LANES:(pair + 1) * LANES])
        kvh = (2 * pair) // (N_HEADS_A // N_KV_A)
        out = _attend_pair(q, [[kvar[2 * kvh]], [kvar[2 * kvh + 1]]], [[vvar[2 * kvh]], [vvar[2 * kvh + 1]]],
                           [[None], [None]], joint_pv)
        o_ref[:, pair * LANES:(pair + 1) * LANES] = out.astype(o_ref.dtype)


def _ctx_attn_kernel(qa_ref, qb_ref, kb_ref, vb_ref, kava_ref, *refs, n_prev):
    prev = refs[:4 * n_prev]
    y_ref, nak_ref, nav_ref, nbk_ref, nbv_ref = refs[4 * n_prev:]
    ka = kava_ref[:, :LANES]
    va = kava_ref[:, LANES:]
    new = (ka, va, kb_ref[...], vb_ref[...])
    for c, (o_ref, val) in enumerate(zip((nak_ref, nav_ref, nbk_ref, nbv_ref), new)):
        if n_prev:
            for p in range(n_prev):
                o_ref[p] = prev[4 * p + c][...]
            o_ref[n_prev] = val
        else:
            o_ref[...] = val
    _gqa_attention(qa_ref, _gqa_variants(ka), _gqa_variants(va), y_ref.at[:, :N_HEADS_A * HEAD_DIM], joint_pv=False)
    for pair in range(N_HEADS_B // 2):
        cols = slice(pair * LANES, (pair + 1) * LANES)
        q = _scaled_q(qb_ref[:, cols])
        k_lo, k_hi = _pair_halves(kb_ref[:, cols])
        v_lo, v_hi = _pair_halves(vb_ref[:, cols])
        out = _attend_pair(q, [[k_lo], [k_hi]], [[v_lo], [v_hi]], [[None], [None]], joint_pv=False)
        y_ref[:, N_HEADS_A * HEAD_DIM + pair * LANES:N_HEADS_A * HEAD_DIM + (pair + 1) * LANES] = out.astype(BF16)


def _ctx_attention(y, prev_caches):
    wide = 4 * LANES
    widths = (LANES, LANES, wide, wide)
    n_prev = len(prev_caches)
    row = lambda b: (b, 0)
    f32 = lambda *s: jax.ShapeDtypeStruct(s, F32)
    in_specs = [
        pl.BlockSpec((CTX_LEN, wide), lambda b: (b, 0)),
        pl.BlockSpec((CTX_LEN, wide), lambda b: (b, 1)),
        pl.BlockSpec((CTX_LEN, wide), lambda b: (b, 2)),
        pl.BlockSpec((CTX_LEN, wide), lambda b: (b, 3)),
        pl.BlockSpec((CTX_LEN, 2 * LANES), lambda b: (b, COL_KA // 2)),
    ]
    args = [y, y, y, y, y]
    for layer_caches in prev_caches:
        in_specs += [pl.BlockSpec((CTX_LEN, w), row) for w in widths]
        args += list(layer_caches)
    if n_prev:
        cache_specs = [pl.BlockSpec((None, n_prev + 1, CTX_LEN, w), lambda b: (b, 0, 0, 0)) for w in widths]
        cache_shapes = [f32(N_CTX_SETS, n_prev + 1, CTX_LEN, w) for w in widths]
    else:
        cache_specs = [pl.BlockSpec((CTX_LEN, w), row) for w in widths]
        cache_shapes = [f32(T_CTX, w) for w in widths]
    return pl.pallas_call(
        functools.partial(_ctx_attn_kernel, n_prev=n_prev),
        grid=(N_CTX_SETS,),
        in_specs=in_specs,
        out_specs=[pl.BlockSpec((CTX_LEN, 2 * wide), row)] + cache_specs,
        out_shape=[jax.ShapeDtypeStruct((T_CTX, 2 * wide), BF16)] + cache_shapes,
        compiler_params=_params(1),
        name="ctx_attention",
    )(*args)


LAT_TQ = 256
LAT_QT = LAT_LEN // LAT_TQ
GQA_TQ = 512


def _lat_gqa_kernel(qa_ref, kava_ref, ck_ref, cv_ref, o_ref, k_s, vt_s):
    n_keys = PAST_LEN + LAT_LEN

    @pl.when(pl.program_id(1) == 0)
    def _():
        for i, (c, x) in enumerate(zip(_gqa_variants(ck_ref[...]), _gqa_variants(kava_ref[:, :LANES]))):
            k_s[i, :PAST_LEN, :] = c
            k_s[i, PAST_LEN:, :] = x
        for src, cols in ((cv_ref[...], slice(0, PAST_LEN)), (kava_ref[:, LANES:], slice(PAST_LEN, n_keys))):
            vt = src.T
            swapped = pltpu.roll(vt, HEAD_DIM, 0)
            top = lax.broadcasted_iota(jnp.int32, vt.shape, 0) < HEAD_DIM
            zero = jnp.zeros_like(vt)
            per_head = ((jnp.where(top, vt, zero), jnp.where(top, zero, swapped)),
                        (jnp.where(top, swapped, zero), jnp.where(top, zero, vt)))
            for kvh in range(N_KV_A):
                for half in range(2):
                    off = half * n_keys
                    vt_s[kvh, :, off + cols.start:off + cols.stop] = per_head[kvh][half].astype(BF16)

    for pair in range(N_HEADS_A // 2):
        q = _scaled_q(qa_ref[:, pair * LANES:(pair + 1) * LANES])
        kvh = (2 * pair) // (N_HEADS_A // N_KV_A)
        out = _attend_pair_t(q, [[k_s[2 * kvh]], [k_s[2 * kvh + 1]]], vt_s[kvh], [[None], [None]])
        o_ref[:, pair * LANES:(pair + 1) * LANES] = out.astype(o_ref.dtype)


def _lat_gqa_attention(y, cache_k, cache_v, layer):
    wide = 4 * LANES
    tiles = LAT_LEN // GQA_TQ
    first = T_CTX // GQA_TQ
    cache = pl.BlockSpec((None, None, PAST_LEN, LANES), lambda b, t: (b, layer, 0, 0))
    return pl.pallas_call(
        _lat_gqa_kernel,
        grid=(N_LAT_SETS, tiles),
        in_specs=[
            pl.BlockSpec((GQA_TQ, wide), lambda b, t: (first + b * tiles + t, 0)),
            pl.BlockSpec((LAT_LEN, 2 * LANES), lambda b, t: (T_CTX // LAT_LEN + b, COL_KA // 2)),
            cache, cache,
        ],
        out_specs=pl.BlockSpec((GQA_TQ, wide), lambda b, t: (b * tiles + t, 0)),
        out_shape=jax.ShapeDtypeStruct((T_LAT, wide), BF16),
        scratch_shapes=[pltpu.VMEM((4, PAST_LEN + LAT_LEN, LANES), BF16),
                        pltpu.VMEM((N_KV_A, LANES, 2 * (PAST_LEN + LAT_LEN)), BF16)],
        compiler_params=_params(2),
        name="lat_gqa_attention",
    )(y, y, cache_k.reshape(N_LAT_SETS, DEPTH, PAST_LEN, LANES), cache_v.reshape(N_LAT_SETS, DEPTH, PAST_LEN, LANES))


N_DR = 2 * WIN_ROWS - 1
N_DC = 2 * WIN_COLS - 1
ROWS_PER_TQ = LAT_TQ // GRID_W
NA_LOCAL = 768
NA_LOCAL_BLOCKS = NA_LOCAL // LAT_TQ
NA_TOE_ROWS = 32
NA_PAIRS = 4


def _na_key_base(qt):
    return jnp.where(qt < LAT_QT // 2, 0, LAT_QT - NA_LOCAL_BLOCKS)


def _window_mask():
    r = np.arange(GRID_ROWS)
    row_start = np.clip(r - WIN_ROWS // 2, 0, GRID_ROWS - WIN_ROWS)
    in_rows = (r[None, :] >= row_start[:, None]) & (r[None, :] < row_start[:, None] + WIN_ROWS)
    cq = np.arange(GRID_W)
    col_start = np.clip(cq - WIN_COLS // 2, 0, GRID_W - WIN_COLS)
    in_cols = (cq[None, :] >= col_start[:, None]) & (cq[None, :] < col_start[:, None] + WIN_COLS)
    valid = (in_rows[:, None, :, None] & in_cols[None, :, None, :]).reshape(LAT_LEN, LAT_LEN)
    tiles = []
    for qt in range(LAT_QT):
        base = (0 if qt < LAT_QT // 2 else LAT_QT - NA_LOCAL_BLOCKS) * LAT_TQ
        tile = valid[qt * LAT_TQ:(qt + 1) * LAT_TQ]
        assert not tile[:, :base].any() and not tile[:, base + NA_LOCAL:].any()
        tiles.append(tile[:, base:base + NA_LOCAL])
    return np.where(np.stack(tiles), 0.0, NEG).astype(np.float32)


def _toeplitz_select():
    j = np.arange(LANES)
    c = np.clip(j - (GRID_W - 1), -(WIN_COLS - 1), WIN_COLS - 1) + (WIN_COLS - 1)
    return (np.arange(LANES)[:, None] == c[None, :]).astype(np.float32)


def _na_kernel(q_ref, k_ref, v_ref, ck_ref, cv_ref, mask_ref, rpb_ref, sel_ref, o_ref,
               toe_ref, bias_ref, k_s, v_s, ck_s, cv_s):
    b, qt = pl.program_id(1), pl.program_id(2)
    n_heads = 2 * NA_PAIRS

    @pl.when((qt == 0) & (b == 0))
    def _():
        low = lax.broadcasted_iota(jnp.int32, (GRID_W, LANES), 1) < GRID_W
        sel = sel_ref[...].astype(BF16)
        for hd in range(n_heads):
            gen = sum(_bdot(piece, sel) for piece in _split3(rpb_ref[hd]))
            for dd in range(N_DR + 1):
                lo = jnp.broadcast_to(gen[dd:dd + 1, :], (GRID_W, LANES))
                hi = jnp.broadcast_to(gen[dd + 1:dd + 2, :], (GRID_W, LANES))
                lo = pltpu.roll(lo, LANES - (GRID_W - 1), 1, stride=1, stride_axis=0)
                hi = pltpu.roll(hi, 1, 1, stride=1, stride_axis=0)
                toe_ref[hd, dd] = jnp.where(low, lo, hi)

    @pl.when(qt == 0)
    def _():
        for src, dst in ((k_ref, k_s), (v_ref, v_s), (ck_ref, ck_s), (cv_ref, cv_s)):
            for pair in range(NA_PAIRS):
                lo, hi = _pair_halves(src[:, pair * LANES:(pair + 1) * LANES])
                dst[2 * pair] = lo
                dst[2 * pair + 1] = hi

    base = _na_key_base(qt)
    local = pl.ds(pl.multiple_of(base * LAT_TQ, LAT_TQ), NA_LOCAL)
    for pair in range(NA_PAIRS):
        q = _scaled_q(q_ref[:, pair * LANES:(pair + 1) * LANES])
        heads = (2 * pair, 2 * pair + 1)
        biases = []
        for hd in heads:
            for rr in range(ROWS_PER_TQ):
                for kp in range(NA_LOCAL // LANES):
                    d = 2 * (kp + base * (ROWS_PER_TQ // 2)) - (qt * ROWS_PER_TQ + rr) + (WIN_ROWS - 1)
                    dd = jnp.clip(d, -1, N_DR - 1) + 1
                    bias_ref[hd, rr * GRID_W:(rr + 1) * GRID_W, kp * LANES:(kp + 1) * LANES] = toe_ref[hd, dd]
            biases.append([bias_ref[hd] + mask_ref[...], None])
        out = _attend_pair(q, [[k_s[hd, local, :], ck_s[hd]] for hd in heads],
                           [[v_s[hd, local, :], cv_s[hd]] for hd in heads], biases)
        o_ref[:, pair * LANES:(pair + 1) * LANES] = out.astype(o_ref.dtype)


def _lat_na_attention(y, cache_k, cache_v, rpb, layer):
    first = T_CTX // LAT_TQ
    kv_row = T_CTX // LAT_LEN
    wide = N_HEADS_B * HEAD_DIM
    gen = jnp.pad(rpb[layer], ((0, 0), (1, NA_TOE_ROWS - N_DR - 1), (0, LANES - N_DC)))
    w = NA_PAIRS * LANES
    nh = 2 * NA_PAIRS
    cq, ck, cv = COL_QB // NA_PAIRS, COL_KB // NA_PAIRS, COL_VB // NA_PAIRS
    return pl.pallas_call(
        _na_kernel,
        grid=(N_HEADS_B // nh, N_LAT_SETS, LAT_QT),
        in_specs=[
            pl.BlockSpec((LAT_TQ, w), lambda g, b, t: (first + b * LAT_QT + t, cq + g)),
            pl.BlockSpec((LAT_LEN, w), lambda g, b, t: (kv_row + b, ck + g)),
            pl.BlockSpec((LAT_LEN, w), lambda g, b, t: (kv_row + b, cv + g)),
            pl.BlockSpec((None, None, PAST_LEN, w), lambda g, b, t: (b, layer, 0, g)),
            pl.BlockSpec((None, None, PAST_LEN, w), lambda g, b, t: (b, layer, 0, g)),
            pl.BlockSpec((None, LAT_TQ, NA_LOCAL), lambda g, b, t: (t, 0, 0)),
            pl.BlockSpec((nh, NA_TOE_ROWS, LANES), lambda g, b, t: (g, 0, 0)),
            pl.BlockSpec((LANES, LANES), lambda g, b, t: (0, 0)),
        ],
        out_specs=pl.BlockSpec((LAT_TQ, w), lambda g, b, t: (b * LAT_QT + t, g)),
        out_shape=jax.ShapeDtypeStruct((T_LAT, wide), BF16),
        scratch_shapes=[pltpu.VMEM((nh, N_DR + 1, GRID_W, LANES), F32), pltpu.VMEM((nh, LAT_TQ, NA_LOCAL), F32),
                        pltpu.VMEM((nh, LAT_LEN, LANES), BF16), pltpu.VMEM((nh, LAT_LEN, LANES), BF16),
                        pltpu.VMEM((nh, PAST_LEN, LANES), BF16), pltpu.VMEM((nh, PAST_LEN, LANES), BF16)],
        compiler_params=_params(3),
        name="lat_na_attention",
    )(y, y, y, cache_k.reshape(N_LAT_SETS, DEPTH, PAST_LEN, wide), cache_v.reshape(N_LAT_SETS, DEPTH, PAST_LEN, wide),
      jnp.asarray(_window_mask()), gen, jnp.asarray(_toeplitz_select()))


MERGE_TM = 1024
MERGE_SUB = 1024
MERGE_ROUTE_TM = 512
MERGE_ROUTE_SUB = 512


def _merge_kernel(*refs, n_y, route_sets, sub):
    y_refs, refs = refs[:n_y], refs[n_y:]
    (h_ref, x_ref, gt1_ref, sh2_ref, sc2_ref, nf_ref, wab, wbb, wob, wgb, wr_ref), refs = refs[:11], refs[11:]
    if route_sets:
        (rlg_ref, rh_ref), refs = refs[:2], refs[2:]
    (x1_ref, h2_ref, lg_ref), refs = refs[:3], refs[3:]
    if route_sets:
        xg_ref, g_ref, rc_ref, p_s, rt_s = refs

    wr_hi, wr_lo = _split2(wr_ref[...])
    wr_both = jnp.concatenate([wr_hi, wr_lo], axis=0)
    half = N_HEADS_A * HEAD_DIM
    n_sub = x_ref.shape[0] // sub
    for r in range(n_sub):
        rows = slice(r * sub, (r + 1) * sub)
        if n_y == 1:
            ya, yb = y_refs[0][rows, :half], y_refs[0][rows, half:]
        else:
            ya, yb = y_refs[0][rows, :], y_refs[1][rows, :]
        sets = list(range(r * route_sets // n_sub, (r + 1) * route_sets // n_sub))

        def route(k):
            tok = slice(k * CTX_LEN, (k + 1) * CTX_LEN)
            slots = slice(k * CAP_CTX, (k + 1) * CAP_CTX)
            _route_set(rlg_ref[:, tok], rh_ref[tok, :], xg_ref.at[:, slots, :], g_ref.at[:, slots, :],
                       rc_ref.at[tok, :], p_s.at[k], rt_s.at[k], CTX_LEN, CAP_CTX)

        gates = _bdot(h_ref[rows, :], wgb[...])
        za = _bdot(ya, wab[...])
        zb = _bdot(yb, wbb[...])
        for k in sets[:len(sets) // 2]:
            route(k)
        m = jax.nn.sigmoid(gates[:, :D_MODEL]) * za + jax.nn.sigmoid(gates[:, D_MODEL:]) * zb
        x1 = x_ref[rows, :] + gt1_ref[...] * _bdot(m.astype(BF16), wob[...])
        x1_ref[rows, :] = x1
        h2 = _modnorm(x1, nf_ref[...], sh2_ref[...], sc2_ref[...])
        h2_ref[rows, :] = h2.astype(BF16)
        hh, hl = _split2(h2)
        both = _bdot_nt(wr_both, hh)
        lg_ref[:, rows] = both[:N_EXPERTS, :] + both[N_EXPERTS:, :] + _bdot_nt(wr_hi, hl)
        for k in sets[len(sets) // 2:]:
            route(k)


def _merge(y_parts, h, x, mod, norm_ffn, merge_w, w_router_t, layer, row_fn, tm, sub, route=None):
    n = x.shape[0]
    tiles = n // tm
    once = pl.Buffered(1)
    weight = lambda w: pl.BlockSpec(w.shape, lambda i: (0, 0), pipeline_mode=once)
    tile = lambda w: pl.BlockSpec((tm, w), lambda i: (i, 0))
    in_specs = [tile(p.shape[1]) for p in y_parts] + [
        tile(D_MODEL), tile(D_MODEL),
        _mod_spec(layer, row_fn, MOD_GT1), _mod_spec(layer, row_fn, MOD_SH2), _mod_spec(layer, row_fn, MOD_SC2),
        pl.BlockSpec((None, 1, D_MODEL), lambda i: (layer, 0, 0)),
    ] + [weight(w) for w in merge_w] + [
        pl.BlockSpec((None, N_EXPERTS, D_MODEL), lambda i: (layer, 0, 0), pipeline_mode=once)]
    args = list(y_parts) + [h, x, mod, mod, mod, norm_ffn, *merge_w, w_router_t]
    out_specs = [tile(D_MODEL), tile(D_MODEL), pl.BlockSpec((N_EXPERTS, tm), lambda i: (0, i))]
    out_shape = [jax.ShapeDtypeStruct((n, D_MODEL), F32), jax.ShapeDtypeStruct((n, D_MODEL), BF16),
                 jax.ShapeDtypeStruct((N_EXPERTS, n), F32)]
    scratch = []
    route_sets = 0
    if route is not None:
        route_sets = N_CTX_SETS // tiles
        rows, slots = route_sets * CTX_LEN, route_sets * CAP_CTX
        in_specs += [pl.BlockSpec((N_EXPERTS, rows), lambda i: (0, i)), pl.BlockSpec((rows, D_MODEL), lambda i: (i, 0))]
        args += list(route)
        out_specs += [pl.BlockSpec((N_EXPERTS, slots, D_MODEL), lambda i: (0, i, 0)),
                      pl.BlockSpec((N_EXPERTS, slots, LANES), lambda i: (0, i, 0)),
                      pl.BlockSpec((rows, LANES), lambda i: (i, 0))]
        out_shape += [jax.ShapeDtypeStruct((N_EXPERTS, N_CTX_SETS * CAP_CTX, D_MODEL), BF16),
                      jax.ShapeDtypeStruct((N_EXPERTS, N_CTX_SETS * CAP_CTX, LANES), F32),
                      jax.ShapeDtypeStruct((T_CTX, LANES), F32)]
        scratch += [pltpu.VMEM((route_sets, N_EXPERTS * CAP_CTX, CTX_LEN), BF16),
                    pltpu.VMEM((route_sets, LANES, CTX_LEN), F32)]
    return pl.pallas_call(
        functools.partial(_merge_kernel, n_y=len(y_parts), route_sets=route_sets, sub=sub),
        grid=(tiles,),
        in_specs=in_specs,
        out_specs=out_specs,
        out_shape=out_shape,
        scratch_shapes=scratch,
        compiler_params=_params(1),
        name="merge_route" if route_sets else "merge",
    )(*args)


GATHER_M = 512
RANK_TILE = 128


def _rank_row(aff, a_row, e, n):
    tiles = n // RANK_TILE
    sub = lax.broadcasted_iota(jnp.int32, (RANK_TILE, RANK_TILE), 0)
    lane = lax.broadcasted_iota(jnp.int32, (RANK_TILE, RANK_TILE), 1)
    earlier = jnp.where(sub < lane, 1.0, 0.0)
    acc = [jnp.zeros((8, RANK_TILE), F32) for _ in range(tiles)]
    for c in range(tiles):
        a_col = jnp.broadcast_to(aff[c * RANK_TILE:(c + 1) * RANK_TILE, e:e + 1], (RANK_TILE, RANK_TILE))
        for j in range(tiles):
            a_rj = a_row[:, j * RANK_TILE:(j + 1) * RANK_TILE]
            if c < j:
                beats = jnp.where(a_col >= a_rj, 1.0, 0.0)
            elif c > j:
                beats = jnp.where(a_col > a_rj, 1.0, 0.0)
            else:
                beats = jnp.where(a_col > a_rj, 1.0, jnp.where(a_col == a_rj, earlier, 0.0))
            acc[j] = acc[j] + beats.reshape(RANK_TILE // 8, 8, RANK_TILE).sum(axis=0)
    return jnp.concatenate([a.sum(axis=0, keepdims=True) for a in acc], axis=1)


def _route_set(lg, h, xg_ref, g_ref, rc_ref, p_ref, rt_ref, n, cap):
    ex = jnp.exp(lg - lg.max(axis=0, keepdims=True))
    aff_t = ex / ex.sum(axis=0, keepdims=True)
    aff = jnp.concatenate([aff_t, jnp.zeros((LANES - N_EXPERTS, n), F32)], axis=0).T
    rt_ref[...] = jnp.full((LANES, n), float(cap), F32)
    slot = lax.broadcasted_iota(jnp.int32, (cap, n), 0).astype(F32)
    for e in range(N_EXPERTS):
        rank = _rank_row(aff, aff_t[e:e + 1, :], e, n)
        rt_ref[e:e + 1, :] = jnp.minimum(rank, float(cap))
        p_ref[e * cap:(e + 1) * cap, :] = jnp.where(rank == slot, 1.0, 0.0).astype(BF16)

    a1, a2, a3 = (p.astype(F32) for p in _split3(aff))
    packed = (a1 + pltpu.roll(a2, N_EXPERTS, 1) + pltpu.roll(a3, 2 * N_EXPERTS, 1)).astype(BF16)
    per = GATHER_M // cap
    glane = lax.broadcasted_iota(jnp.int32, (cap, LANES), 1)
    for grp in range(N_EXPERTS * cap // GATHER_M):
        p = p_ref[grp * GATHER_M:(grp + 1) * GATHER_M, :]
        xg = _bdot(p, h).astype(BF16)
        gg = _bdot(p, packed)
        for k in range(per):
            e = grp * per + k
            xg_ref[e] = xg[k * cap:(k + 1) * cap, :]
            mine = (glane < 3 * N_EXPERTS) & ((glane & (N_EXPERTS - 1)) == e)
            ge = jnp.where(mine, gg[k * cap:(k + 1) * cap, :], 0.0).sum(axis=-1, keepdims=True)
            g_ref[e] = jnp.broadcast_to(ge, (cap, LANES))
    rc_ref[...] = rt_ref[...].T


def _route_kernel(lg_ref, h_ref, xg_ref, g_ref, rc_ref, p_ref, rt_ref, *, n, cap):
    _route_set(lg_ref[...], h_ref[...], xg_ref, g_ref, rc_ref, p_ref, rt_ref, n, cap)


def _route(logits, h2, n, cap, n_sets):
    return pl.pallas_call(
        functools.partial(_route_kernel, n=n, cap=cap),
        grid=(n_sets,),
        in_specs=[
            pl.BlockSpec((N_EXPERTS, n), lambda s: (0, s)),
            pl.BlockSpec((n, D_MODEL), lambda s: (s, 0)),
        ],
        out_specs=[
            pl.BlockSpec((N_EXPERTS, cap, D_MODEL), lambda s: (0, s, 0)),
            pl.BlockSpec((N_EXPERTS, cap, LANES), lambda s: (0, s, 0)),
            pl.BlockSpec((n, LANES), lambda s: (s, 0)),
        ],
        out_shape=[jax.ShapeDtypeStruct((N_EXPERTS, n_sets * cap, D_MODEL), BF16),
                   jax.ShapeDtypeStruct((N_EXPERTS, n_sets * cap, LANES), F32),
                   jax.ShapeDtypeStruct((n_sets * n, LANES), F32)],
        scratch_shapes=[pltpu.VMEM((N_EXPERTS * cap, n), BF16), pltpu.VMEM((LANES, n), F32)],
        compiler_params=_params(1),
        name=f"route_n{n}",
    )(logits, h2)


EXPERT_TF = 1024
EXPERT_SUB = 256
N_FF_TILES = EXPERT_FF // EXPERT_TF


def _expert_kernel(xc_ref, xl_ref, gc_ref, gl_ref, wg_ref, wu_ref, wd_ref, o_ref, x_s, acc_s):
    f = pl.program_id(1)
    n_ctx = xc_ref.shape[0]

    @pl.when(f == 0)
    def _():
        x_s[:n_ctx, :] = xc_ref[...]
        x_s[n_ctx:, :] = xl_ref[...]
        acc_s[...] = jnp.zeros_like(acc_s)

    x = x_s[...]
    part = None
    for j in range(EXPERT_TF // EXPERT_SUB):
        cols = slice(j * EXPERT_SUB, (j + 1) * EXPERT_SUB)
        gate = _bdot(x, wg_ref[:, cols].astype(BF16))
        up = _bdot(x, wu_ref[:, cols].astype(BF16))
        hid = (gate * jax.nn.sigmoid(gate)) * up
        down = _bdot(hid.astype(BF16), wd_ref[cols, :].astype(BF16))
        part = down if part is None else part + down
    acc_s[...] += part

    @pl.when(f == N_FF_TILES - 1)
    def _():
        o_ref[:n_ctx, :] = (acc_s[:n_ctx, :] * gc_ref[:, :1]).astype(o_ref.dtype)
        o_ref[n_ctx:, :] = (acc_s[n_ctx:, :] * gl_ref[:, :1]).astype(o_ref.dtype)


def _experts(xg_ctx, xg_lat, g_ctx, g_lat, w_gate, w_up, w_down, layer):
    sc, sl = xg_ctx.shape[1], xg_lat.shape[1]
    slots = lambda s, w: pl.BlockSpec((None, s, w), lambda e, f: (e, 0, 0))
    return pl.pallas_call(
        _expert_kernel,
        grid=(N_EXPERTS, N_FF_TILES),
        in_specs=[
            slots(sc, D_MODEL), slots(sl, D_MODEL), slots(sc, LANES), slots(sl, LANES),
            pl.BlockSpec((None, None, D_MODEL, EXPERT_TF), lambda e, f: (layer, e, 0, f)),
            pl.BlockSpec((None, None, D_MODEL, EXPERT_TF), lambda e, f: (layer, e, 0, f)),
            pl.BlockSpec((None, None, EXPERT_TF, D_MODEL), lambda e, f: (layer, e, f, 0)),
        ],
        out_specs=slots(sc + sl, D_MODEL),
        out_shape=jax.ShapeDtypeStruct((N_EXPERTS, sc + sl, D_MODEL), BF16),
        scratch_shapes=[pltpu.VMEM((sc + sl, D_MODEL), BF16), pltpu.VMEM((sc + sl, D_MODEL), F32)],
        compiler_params=_params(2),
        name="experts",
    )(xg_ctx, xg_lat, g_ctx, g_lat, w_gate, w_up, w_down)


COMB_TM = 256
COMB_CTX_SETS = 4


def _combine_kernel(o_ref, rc_ref, x_ref, gt2_ref, ng_ref, *rest, cap, final, sets):
    if final:
        (y_ref,) = rest
    else:
        sh_ref, sc_ref, xn_ref, hn_ref = rest
    slots = N_EXPERTS * cap

    j = lax.broadcasted_iota(jnp.int32, (LANES, slots), 1)
    e = lax.broadcasted_iota(jnp.int32, (LANES, slots), 0)
    expand = jnp.where(j // cap == e, 1.0, 0.0).astype(BF16)
    slot = (lax.broadcasted_iota(jnp.int32, (1, slots), 1) % cap).astype(F32)
    for k in range(sets):
        rows = slice(k * COMB_TM, (k + 1) * COMB_TM)
        rank = _bdot(rc_ref[rows, :].astype(BF16), expand)
        pt = jnp.where(rank == slot, 1.0, 0.0).astype(BF16)
        ffn = _bdot(pt, o_ref[:, k * cap:(k + 1) * cap, :].reshape(slots, D_MODEL))
        x = x_ref[rows, :] + gt2_ref[...] * ffn
        if final:
            y_ref[rows, :] = _rms(x) * ng_ref[...]
        else:
            xn_ref[rows, :] = x
            hn_ref[rows, :] = _modnorm(x, ng_ref[...], sh_ref[...], sc_ref[...]).astype(BF16)


def _combine(out, rc, x1, mod, norm_g, layer, n, cap, n_sets, sets, first_slot_block, row_fn, final):
    tiles = n // COMB_TM
    assert sets == 1 or tiles == 1
    tm = sets * COMB_TM
    rows = lambda s, t: (s * tiles + t, 0)
    in_specs = [
        pl.BlockSpec((N_EXPERTS, sets * cap, D_MODEL), lambda s, t: (0, first_slot_block + s, 0)),
        pl.BlockSpec((tm, LANES), rows),
        pl.BlockSpec((tm, D_MODEL), rows),
        _mod_spec(layer, row_fn, MOD_GT2),
    ]
    args = [out, rc, x1, mod]
    if final:
        in_specs.append(pl.BlockSpec((1, D_MODEL), lambda s, t: (0, 0)))
        args.append(norm_g.reshape(1, D_MODEL))
        out_specs = pl.BlockSpec((tm, D_MODEL), rows)
        out_shape = jax.ShapeDtypeStruct((n_sets * n, D_MODEL), F32)
    else:
        in_specs += [pl.BlockSpec((None, 1, D_MODEL), lambda s, t: (layer + 1, 0, 0)),
                     _mod_spec(layer + 1, row_fn, MOD_SH1), _mod_spec(layer + 1, row_fn, MOD_SC1)]
        args += [norm_g, mod, mod]
        out_specs = [pl.BlockSpec((tm, D_MODEL), rows), pl.BlockSpec((tm, D_MODEL), rows)]
        out_shape = [jax.ShapeDtypeStruct((n_sets * n, D_MODEL), F32),
                     jax.ShapeDtypeStruct((n_sets * n, D_MODEL), BF16)]
    return pl.pallas_call(
        functools.partial(_combine_kernel, cap=cap, final=final, sets=sets),
        grid=(n_sets // sets, tiles),
        in_specs=in_specs,
        out_specs=out_specs,
        out_shape=out_shape,
        compiler_params=_params(2),
        name=f"combine_n{n}",
    )(*args)


def kernel(x_prompt, x_sample, cache_attn_k, cache_attn_v, cache_na_k, cache_na_v, c, c_ctx, w_ada, b_ada,
           norm_mix, norm_ffn, w_in, q_norm, k_norm, rpb, w_branch_a, w_branch_b, w_out, w_router, w_gate,
           w_up, w_down, final_norm):
    x_ctx = x_prompt.reshape(T_CTX, D_MODEL)
    x_lat = x_sample.reshape(T_LAT, D_MODEL)
    mod = _modulation(c, c_ctx, w_ada, b_ada)
    norm_mix3 = norm_mix.reshape(DEPTH, 1, D_MODEL)
    norm_ffn3 = norm_ffn.reshape(DEPTH, 1, D_MODEL)
    w_router_t = jnp.swapaxes(w_router, 1, 2)
    ctx_row = lambda *g: 0
    lat_comb_row = lambda s, t: 1 + s

    h_ctx = _prenorm(x_ctx, mod, norm_mix3, 0, ctx_row)
    h_lat = _prenorm(x_lat, mod, norm_mix3, 0, _lat_mod_row(NORM_TM))
    layer_caches = []
    for layer in range(DEPTH):
        last = layer == DEPTH - 1
        y, merge_w = _projection(h_ctx, h_lat, w_in, q_norm, k_norm, w_branch_a, w_branch_b, w_out, layer)
        yab_ctx, *new_caches = _ctx_attention(y, layer_caches if last else [])
        layer_caches.append(tuple(new_caches))
        ya_lat = _lat_gqa_attention(y, cache_attn_k, cache_attn_v, layer)
        yb_lat = _lat_na_attention(y, cache_na_k, cache_na_v, rpb, layer)
        merge_args = (mod, norm_ffn3, merge_w, w_router_t, layer)
        x1_ctx, h2_ctx, lg_ctx = _merge([yab_ctx], h_ctx, x_ctx, *merge_args, ctx_row, MERGE_TM, MERGE_SUB)
        x1_lat, h2_lat, lg_lat, xg_ctx, g_ctx, rc_ctx = _merge(
            [ya_lat, yb_lat], h_lat, x_lat, *merge_args, _lat_mod_row(MERGE_ROUTE_TM), MERGE_ROUTE_TM,
            MERGE_ROUTE_SUB, route=(lg_ctx, h2_ctx))
        xg_lat, g_lat, rc_lat = _route(lg_lat, h2_lat, LAT_LEN, CAP_LAT, N_LAT_SETS)
        out = _experts(xg_ctx, xg_lat, g_ctx, g_lat, w_gate, w_up, w_down, layer)
        final = layer == DEPTH - 1
        norm_g = final_norm if final else norm_mix3
        res_ctx = _combine(out, rc_ctx, x1_ctx, mod, norm_g, layer, CTX_LEN, CAP_CTX, N_CTX_SETS, COMB_CTX_SETS, 0,
                           ctx_row, final)
        res_lat = _combine(out, rc_lat, x1_lat, mod, norm_g, layer, LAT_LEN, CAP_LAT, N_LAT_SETS, 1,
                           N_CTX_SETS * CAP_CTX // CAP_LAT, lat_comb_row, final)
        if final:
            y_ctx, y_lat = res_ctx, res_lat
        else:
            (x_ctx, h_ctx), (x_lat, h_lat) = res_ctx, res_lat

    heads = (N_KV_A, N_KV_A, N_HEADS_B, N_HEADS_B)
    new_caches = [a.reshape(N_CTX_SETS, DEPTH, CTX_LEN, h, HEAD_DIM) for a, h in zip(layer_caches[-1], heads)]
    return (y_ctx.reshape(N_CTX_SETS, CTX_LEN, D_MODEL), y_lat.reshape(N_LAT_SETS, LAT_LEN, D_MODEL), *new_caches)
```

```python
import functools

import numpy as np
import jax
import jax.numpy as jnp
from jax import lax
from jax.experimental import pallas as pl
from jax.experimental.pallas import tpu as pltpu

F32 = jnp.float32
BF16 = jnp.bfloat16

D_MODEL = 1024
N_CTX_SETS, CTX_LEN = 16, 256
N_LAT_SETS, LAT_LEN = 2, 1024
T_CTX = N_CTX_SETS * CTX_LEN
T_LAT = N_LAT_SETS * LAT_LEN
T_ALL = T_CTX + T_LAT
DEPTH = 2
PAST_LEN = 512
GRID_W = 64
GRID_ROWS = LAT_LEN // GRID_W
HEAD_DIM = 64
N_HEADS_A, N_KV_A, N_HEADS_B = 8, 2, 8
WIN_ROWS, WIN_COLS = 8, 16
N_EXPERTS = 16
EXPERT_FF = 2048
CAP_CTX = 2 * CTX_LEN // N_EXPERTS
CAP_LAT = 2 * LAT_LEN // N_EXPERTS
ROPE_THETA = 10000.0
EPS = 1e-6
NEG = -1e30
QKV_DIM = 2304
ATT_SCALE = HEAD_DIM ** -0.5

LANES = 128
VMEM_LIMIT = 56 * 1024 * 1024

PROJ_TN = 256
N_PROJ_TILES = QKV_DIM // PROJ_TN
PROJ_PERM = np.array([0, 1, 8, 2, 3, 4, 5, 6, 7], np.int32)
CAST_STEPS = 8
MOD_SH1, MOD_SC1, MOD_GT1, MOD_SH2, MOD_SC2, MOD_GT2 = range(6)


def _params(n_grid_dims, vmem=VMEM_LIMIT):
    return pltpu.CompilerParams(dimension_semantics=("arbitrary",) * n_grid_dims, vmem_limit_bytes=vmem)


def _bdot(a, b):
    return jnp.dot(a, b, preferred_element_type=F32)


def _bdot_nt(a, b):
    return lax.dot_general(a, b, (((1,), (1,)), ((), ())), preferred_element_type=F32)


def _split2(x):
    hi = x.astype(BF16)
    lo = (x - hi.astype(F32)).astype(BF16)
    return hi, lo


def _split3(x):
    hi = x.astype(BF16)
    r = x - hi.astype(F32)
    mid = r.astype(BF16)
    lo = (r - mid.astype(F32)).astype(BF16)
    return hi, mid, lo


def _rms(x):
    return x * lax.rsqrt(jnp.mean(x * x, axis=-1, keepdims=True) + EPS)


def _modnorm(x, g, sh, sc):
    return (_rms(x) * g) * (1.0 + sc) + sh


def _lat_mod_row(tile_rows):
    return lambda i: 1 + (i * tile_rows) // LAT_LEN


def _mod_spec(layer, row_fn, chunk):
    return pl.BlockSpec((None, None, 1, D_MODEL), lambda *g: (layer, row_fn(*g), 0, chunk))


MOD_TN = 1536


def _mod_kernel(ct_ref, w_ref, b_ref, o_ref):
    ct = ct_ref[...]
    act = ct * jax.nn.sigmoid(ct)
    w = w_ref[...]
    for m in range(3):
        o_ref[m:m + 1, :] = jnp.sum(w * act[:, m:m + 1], axis=0, keepdims=True) + b_ref[...]
    o_ref[3:8, :] = jnp.zeros((5, MOD_TN), F32)


def _modulation(c, c_ctx, w_ada, b_ada):
    cond = jnp.concatenate([c_ctx[None, :], c, jnp.zeros((5, D_MODEL), F32)], axis=0)
    mod = pl.pallas_call(
        _mod_kernel,
        grid=(DEPTH, 6 * D_MODEL // MOD_TN),
        in_specs=[
            pl.BlockSpec((D_MODEL, 8), lambda l, j: (0, 0)),
            pl.BlockSpec((None, D_MODEL, MOD_TN), lambda l, j: (l, 0, j)),
            pl.BlockSpec((None, 1, MOD_TN), lambda l, j: (l, 0, j)),
        ],
        out_specs=pl.BlockSpec((None, 8, MOD_TN), lambda l, j: (l, 0, j)),
        out_shape=jax.ShapeDtypeStruct((DEPTH, 8, 6 * D_MODEL), F32),
        compiler_params=_params(2),
        name="modulation",
    )(cond.T, w_ada, b_ada.reshape(DEPTH, 1, 6 * D_MODEL))
    return mod.reshape(DEPTH, 8, 1, 6 * D_MODEL)


NORM_TM = 512


def _prenorm_kernel(x_ref, g_ref, sh_ref, sc_ref, h_ref):
    h_ref[...] = _modnorm(x_ref[...], g_ref[...], sh_ref[...], sc_ref[...]).astype(BF16)


def _prenorm(x, mod, norm_g, layer, row_fn):
    n = x.shape[0]
    return pl.pallas_call(
        _prenorm_kernel,
        grid=(n // NORM_TM,),
        in_specs=[
            pl.BlockSpec((NORM_TM, D_MODEL), lambda i: (i, 0)),
            pl.BlockSpec((None, 1, D_MODEL), lambda i: (layer, 0, 0)),
            _mod_spec(layer, row_fn, MOD_SH1),
            _mod_spec(layer, row_fn, MOD_SC1),
        ],
        out_specs=pl.BlockSpec((NORM_TM, D_MODEL), lambda i: (i, 0)),
        out_shape=jax.ShapeDtypeStruct((n, D_MODEL), BF16),
        compiler_params=_params(1),
        name="prenorm",
    )(x, norm_g, mod, mod)


PROJ_CH = 1024


def _rope_tables():
    t = np.arange(LAT_LEN)
    lane = np.arange(LANES) % HEAD_DIM
    pos = np.where(lane < HEAD_DIM // 2, (t // GRID_W)[:, None], (t % GRID_W)[:, None]).astype(np.float64)
    freq = ROPE_THETA ** (-(lane % 16).astype(np.float64) / 16.0)
    ang = pos * freq[None, :]
    sign = np.where((lane & 16) == 0, -1.0, 1.0)[None, :]
    return np.cos(ang).astype(np.float32), (np.sin(ang) * sign).astype(np.float32)


def _head_norm_rope(y, gain, cos, sin):
    w = y.shape[1]
    r = (lax.broadcasted_iota(jnp.int32, (2 * w, w), 0) % w) // HEAD_DIM
    c = lax.broadcasted_iota(jnp.int32, (2 * w, w), 1) // HEAD_DIM
    seg = jnp.where(r == c, 1.0 / HEAD_DIM, 0.0).astype(BF16)
    ms = _bdot(jnp.concatenate(_split2(y * y), axis=1), seg)
    yn = y * lax.rsqrt(ms + EPS) * gain
    if cos is None:
        return yn
    lane = lax.broadcasted_iota(jnp.int32, yn.shape, 1)
    partner = jnp.where((lane & 16) == 0, pltpu.roll(yn, w - 16, 1), pltpu.roll(yn, 16, 1))
    if w > LANES:
        cos = jnp.concatenate([cos] * (w // LANES), axis=1)
        sin = jnp.concatenate([sin] * (w // LANES), axis=1)
    return yn * cos + partner * sin


def _proj_kernel(perm_ref, hc_ref, hl_ref, w_ref, gain_ref, cos_ref, sin_ref, wa_ref, wbr_ref, wo_ref, wg_ref,
                 o_ref, oa_ref, ob_ref, oo_ref, og_ref, wb_ref):
    del perm_ref
    j = pl.program_id(0)
    wb_ref[...] = w_ref[...].astype(BF16)
    oa_ref[...] = wa_ref[...].astype(BF16)
    ob_ref[...] = wbr_ref[...].astype(BF16)
    oo_ref[...] = wo_ref[...].astype(BF16)
    og_ref[...] = wg_ref[0].astype(BF16)
    chunks = [(hc_ref, k * PROJ_CH, k * PROJ_CH, False) for k in range(T_CTX // PROJ_CH)]
    chunks += [(hl_ref, k * PROJ_CH, T_CTX + k * PROJ_CH, True) for k in range(T_LAT // PROJ_CH)]

    def matmul(chunk):
        h_ref, r0, o0, _ = chunk
        o_ref[o0:o0 + PROJ_CH, :] = _bdot(h_ref[r0:r0 + PROJ_CH, :], wb_ref[...])

    def finish(chunk, width):
        _, _, o0, is_lat = chunk
        cos, sin = (cos_ref[...], sin_ref[...]) if is_lat else (None, None)
        y = o_ref[o0:o0 + PROJ_CH, :width]
        o_ref[o0:o0 + PROJ_CH, :width] = _head_norm_rope(y, gain_ref[:, :width], cos, sin)

    def tile(width):
        matmul(chunks[0])
        for k in range(1, len(chunks)):
            matmul(chunks[k])
            if width:
                finish(chunks[k - 1], width)
        if width:
            finish(chunks[-1], width)

    pl.when(j < 2)(lambda: tile(PROJ_TN))
    pl.when(j == 2)(lambda: tile(LANES))
    pl.when(j > 2)(lambda: tile(0))


def _projection(h_ctx, h_lat, w_in, q_norm, k_norm, w_ba, w_bb, w_out, layer):
    ones = jnp.ones((2 * HEAD_DIM,), F32)
    gain = jnp.stack([jnp.tile(q_norm[layer], 4), jnp.tile(q_norm[layer], 4),
                      jnp.concatenate([jnp.tile(k_norm[layer], 2), ones])])[:, None, :]
    cos, sin = _rope_tables()
    half = N_HEADS_A * HEAD_DIM
    ra, ro = half // CAST_STEPS, D_MODEL // CAST_STEPS
    part = lambda j: jnp.minimum(j, CAST_STEPS - 1)
    grid_spec = pltpu.PrefetchScalarGridSpec(
        num_scalar_prefetch=1,
        grid=(N_PROJ_TILES,),
        in_specs=[
            pl.BlockSpec((T_CTX, D_MODEL), lambda j, p: (0, 0)),
            pl.BlockSpec((T_LAT, D_MODEL), lambda j, p: (0, 0)),
            pl.BlockSpec((None, D_MODEL, PROJ_TN), lambda j, p: (layer, 0, j)),
            pl.BlockSpec((None, 1, PROJ_TN), lambda j, p: (jnp.minimum(j, 2), 0, 0)),
            pl.BlockSpec((LAT_LEN, LANES), lambda j, p: (0, 0)),
            pl.BlockSpec((LAT_LEN, LANES), lambda j, p: (0, 0)),
            pl.BlockSpec((None, ra, D_MODEL), lambda j, p: (layer, part(j), 0)),
            pl.BlockSpec((None, ra, D_MODEL), lambda j, p: (layer, part(j), 0)),
            pl.BlockSpec((None, ro, D_MODEL), lambda j, p: (layer, part(j), 0)),
            pl.BlockSpec((pl.Element(1), pl.Element(ro), pl.Element(2 * D_MODEL)),
                         lambda j, p: (layer, part(j) * ro, QKV_DIM)),
        ],
        out_specs=[
            pl.BlockSpec((T_ALL, PROJ_TN), lambda j, p: (0, p[j])),
            pl.BlockSpec((ra, D_MODEL), lambda j, p: (part(j), 0)),
            pl.BlockSpec((ra, D_MODEL), lambda j, p: (part(j), 0)),
            pl.BlockSpec((ro, D_MODEL), lambda j, p: (part(j), 0)),
            pl.BlockSpec((ro, 2 * D_MODEL), lambda j, p: (part(j), 0)),
        ],
        scratch_shapes=[pltpu.VMEM((D_MODEL, PROJ_TN), BF16)],
    )
    y, *merge_w = pl.pallas_call(
        _proj_kernel,
        grid_spec=grid_spec,
        out_shape=[jax.ShapeDtypeStruct((T_ALL, QKV_DIM), F32),
                   jax.ShapeDtypeStruct((half, D_MODEL), BF16), jax.ShapeDtypeStruct((half, D_MODEL), BF16),
                   jax.ShapeDtypeStruct((D_MODEL, D_MODEL), BF16), jax.ShapeDtypeStruct((D_MODEL, 2 * D_MODEL), BF16)],
        compiler_params=_params(1),
        name="projection",
    )(jnp.asarray(PROJ_PERM), h_ctx, h_lat, w_in, gain, jnp.asarray(cos), jnp.asarray(sin), w_ba, w_bb, w_out, w_in)
    return y, merge_w


COL_QA, COL_QB, COL_KB, COL_VB, COL_KA, COL_VA = 0, 4, 8, 12, 16, 17


def _lane_is_low(shape):
    return lax.broadcasted_iota(jnp.int32, shape, 1) < HEAD_DIM


def _pair_halves(x):
    low = _lane_is_low(x.shape)
    xb = x.astype(BF16)
    zero = jnp.zeros_like(xb)
    return jnp.where(low, xb, zero), jnp.where(low, zero, xb)


def _scaled_q(q):
    assert ATT_SCALE == 0.125
    return (q * ATT_SCALE).astype(BF16)


def _attend_pair(q, keys, values, biases, joint_pv=True):
    probs, dens, out = [], [], None
    for h in range(2):
        scores = []
        for k, b in zip(keys[h], biases[h]):
            s = _bdot_nt(q, k)
            scores.append(s if b is None else s + b)
        m = scores[0].max(axis=-1, keepdims=True)
        for s in scores[1:]:
            m = jnp.maximum(m, s.max(axis=-1, keepdims=True))
        den, num = None, None
        for s, v in zip(scores, values[h]):
            e = jnp.exp(s - m)
            d = e.sum(axis=-1, keepdims=True)
            den = d if den is None else den + d
            if joint_pv:
                probs.append(e.astype(BF16))
            else:
                o = _bdot(e.astype(BF16), v)
                num = o if num is None else num + o
        dens.append(den)
        if not joint_pv:
            out = num / den if out is None else out + num / den
    if not joint_pv:
        return out
    num = _bdot(jnp.concatenate(probs, axis=1), jnp.concatenate(values[0] + values[1], axis=0))
    return num / jnp.where(_lane_is_low(num.shape), dens[0], dens[1])


def _attend_pair_t(q, keys, values_t, biases_t):
    probs, dens = [], []
    for h in range(2):
        scores = []
        for k, b in zip(keys[h], biases_t[h]):
            s = _bdot_nt(k, q)
            scores.append(s if b is None else s + b)
        m = scores[0].max(axis=0, keepdims=True)
        for s in scores[1:]:
            m = jnp.maximum(m, s.max(axis=0, keepdims=True))
        den = None
        for s in scores:
            e = jnp.exp(s - m)
            d = e.sum(axis=0, keepdims=True)
            den = d if den is None else den + d
            probs.append(e.astype(BF16))
        dens.append(den)
    num_t = _bdot(values_t, jnp.concatenate(probs, axis=0))
    row = lax.broadcasted_iota(jnp.int32, num_t.shape, 0)
    return (num_t / jnp.where(row < HEAD_DIM, dens[0], dens[1])).T


def _gqa_variants(x):
    lo, hi = _pair_halves(x)
    sw_lo, sw_hi = _pair_halves(pltpu.roll(x, HEAD_DIM, 1))
    return [lo, sw_hi, sw_lo, hi]


def _gqa_attention(q_ref, kvar, vvar, o_ref, joint_pv):
    for pair in range(N_HEADS_A // 2):
        q = _scaled_q(q_ref[:, pair * LANES:(pair + 1) * LANES])
        kvh = (2 * pair) // (N_HEADS_A // N_KV_A)
        out = _attend_pair(q, [[kvar[2 * kvh]], [kvar[2 * kvh + 1]]], [[vvar[2 * kvh]], [vvar[2 * kvh + 1]]],
                           [[None], [None]], joint_pv)
        o_ref[:, pair * LANES:(pair + 1) * LANES] = out.astype(o_ref.dtype)


def _ctx_attn_kernel(qa_ref, qb_ref, kb_ref, vb_ref, kava_ref, *refs, n_prev):
    prev = refs[:4 * n_prev]
    y_ref, nak_ref, nav_ref, nbk_ref, nbv_ref = refs[4 * n_prev:]
    ka = kava_ref[:, :LANES]
    va = kava_ref[:, LANES:]
    new = (ka, va, kb_ref[...], vb_ref[...])
    for c, (o_ref, val) in enumerate(zip((nak_ref, nav_ref, nbk_ref, nbv_ref), new)):
        if n_prev:
            for p in range(n_prev):
                o_ref[p] = prev[4 * p + c][...]
            o_ref[n_prev] = val
        else:
            o_ref[...] = val
    _gqa_attention(qa_ref, _gqa_variants(ka), _gqa_variants(va), y_ref.at[:, :N_HEADS_A * HEAD_DIM], joint_pv=False)
    for pair in range(N_HEADS_B // 2):
        cols = slice(pair * LANES, (pair + 1) * LANES)
        q = _scaled_q(qb_ref[:, cols])
        k_lo, k_hi = _pair_halves(kb_ref[:, cols])
        v_lo, v_hi = _pair_halves(vb_ref[:, cols])
        out = _attend_pair(q, [[k_lo], [k_hi]], [[v_lo], [v_hi]], [[None], [None]], joint_pv=False)
        y_ref[:, N_HEADS_A * HEAD_DIM + pair * LANES:N_HEADS_A * HEAD_DIM + (pair + 1) * LANES] = out.astype(BF16)


def _ctx_attention(y, prev_caches):
    wide = 4 * LANES
    widths = (LANES, LANES, wide, wide)
    n_prev = len(prev_caches)
    row = lambda b: (b, 0)
    f32 = lambda *s: jax.ShapeDtypeStruct(s, F32)
    in_specs = [
        pl.BlockSpec((CTX_LEN, wide), lambda b: (b, 0)),
        pl.BlockSpec((CTX_LEN, wide), lambda b: (b, 1)),
        pl.BlockSpec((CTX_LEN, wide), lambda b: (b, 2)),
        pl.BlockSpec((CTX_LEN, wide), lambda b: (b, 3)),
        pl.BlockSpec((CTX_LEN, 2 * LANES), lambda b: (b, COL_KA // 2)),
    ]
    args = [y, y, y, y, y]
    for layer_caches in prev_caches:
        in_specs += [pl.BlockSpec((CTX_LEN, w), row) for w in widths]
        args += list(layer_caches)
    if n_prev:
        cache_specs = [pl.BlockSpec((None, n_prev + 1, CTX_LEN, w), lambda b: (b, 0, 0, 0)) for w in widths]
        cache_shapes = [f32(N_CTX_SETS, n_prev + 1, CTX_LEN, w) for w in widths]
    else:
        cache_specs = [pl.BlockSpec((CTX_LEN, w), row) for w in widths]
        cache_shapes = [f32(T_CTX, w) for w in widths]
    return pl.pallas_call(
        functools.partial(_ctx_attn_kernel, n_prev=n_prev),
        grid=(N_CTX_SETS,),
        in_specs=in_specs,
        out_specs=[pl.BlockSpec((CTX_LEN, 2 * wide), row)] + cache_specs,
        out_shape=[jax.ShapeDtypeStruct((T_CTX, 2 * wide), BF16)] + cache_shapes,
        compiler_params=_params(1),
        name="ctx_attention",
    )(*args)


LAT_TQ = 256
LAT_QT = LAT_LEN // LAT_TQ
GQA_TQ = 512


def _lat_gqa_kernel(qa_ref, kava_ref, ck_ref, cv_ref, o_ref, k_s, vt_s):
    n_keys = PAST_LEN + LAT_LEN

    @pl.when(pl.program_id(1) == 0)
    def _():
        for i, (c, x) in enumerate(zip(_gqa_variants(ck_ref[...]), _gqa_variants(kava_ref[:, :LANES]))):
            k_s[i, :PAST_LEN, :] = c
            k_s[i, PAST_LEN:, :] = x
        for src, cols in ((cv_ref[...], slice(0, PAST_LEN)), (kava_ref[:, LANES:], slice(PAST_LEN, n_keys))):
            vt = src.T
            swapped = pltpu.roll(vt, HEAD_DIM, 0)
            top = lax.broadcasted_iota(jnp.int32, vt.shape, 0) < HEAD_DIM
            zero = jnp.zeros_like(vt)
            per_head = ((jnp.where(top, vt, zero), jnp.where(top, zero, swapped)),
                        (jnp.where(top, swapped, zero), jnp.where(top, zero, vt)))
            for kvh in range(N_KV_A):
                for half in range(2):
                    off = half * n_keys
                    vt_s[kvh, :, off + cols.start:off + cols.stop] = per_head[kvh][half].astype(BF16)

    for pair in range(N_HEADS_A // 2):
        q = _scaled_q(qa_ref[:, pair * LANES:(pair + 1) * LANES])
        kvh = (2 * pair) // (N_HEADS_A // N_KV_A)
        out = _attend_pair_t(q, [[k_s[2 * kvh]], [k_s[2 * kvh + 1]]], vt_s[kvh], [[None], [None]])
        o_ref[:, pair * LANES:(pair + 1) * LANES] = out.astype(o_ref.dtype)


def _lat_gqa_attention(y, cache_k, cache_v, layer):
    wide = 4 * LANES
    tiles = LAT_LEN // GQA_TQ
    first = T_CTX // GQA_TQ
    cache = pl.BlockSpec((None, None, PAST_LEN, LANES), lambda b, t: (b, layer, 0, 0))
    return pl.pallas_call(
        _lat_gqa_kernel,
        grid=(N_LAT_SETS, tiles),
        in_specs=[
            pl.BlockSpec((GQA_TQ, wide), lambda b, t: (first + b * tiles + t, 0)),
            pl.BlockSpec((LAT_LEN, 2 * LANES), lambda b, t: (T_CTX // LAT_LEN + b, COL_KA // 2)),
            cache, cache,
        ],
        out_specs=pl.BlockSpec((GQA_TQ, wide), lambda b, t: (b * tiles + t, 0)),
        out_shape=jax.ShapeDtypeStruct((T_LAT, wide), BF16),
        scratch_shapes=[pltpu.VMEM((4, PAST_LEN + LAT_LEN, LANES), BF16),
                        pltpu.VMEM((N_KV_A, LANES, 2 * (PAST_LEN + LAT_LEN)), BF16)],
        compiler_params=_params(2),
        name="lat_gqa_attention",
    )(y, y, cache_k.reshape(N_LAT_SETS, DEPTH, PAST_LEN, LANES), cache_v.reshape(N_LAT_SETS, DEPTH, PAST_LEN, LANES))


N_DR = 2 * WIN_ROWS - 1
N_DC = 2 * WIN_COLS - 1
ROWS_PER_TQ = LAT_TQ // GRID_W
NA_LOCAL = 768
NA_LOCAL_BLOCKS = NA_LOCAL // LAT_TQ
NA_TOE_ROWS = 32
NA_PAIRS = 4


def _na_key_base(qt):
    return jnp.where(qt < LAT_QT // 2, 0, LAT_QT - NA_LOCAL_BLOCKS)


def _window_mask():
    r = np.arange(GRID_ROWS)
    row_start = np.clip(r - WIN_ROWS // 2, 0, GRID_ROWS - WIN_ROWS)
    in_rows = (r[None, :] >= row_start[:, None]) & (r[None, :] < row_start[:, None] + WIN_ROWS)
    cq = np.arange(GRID_W)
    col_start = np.clip(cq - WIN_COLS // 2, 0, GRID_W - WIN_COLS)
    in_cols = (cq[None, :] >= col_start[:, None]) & (cq[None, :] < col_start[:, None] + WIN_COLS)
    valid = (in_rows[:, None, :, None] & in_cols[None, :, None, :]).reshape(LAT_LEN, LAT_LEN)
    tiles = []
    for qt in range(LAT_QT):
        base = (0 if qt < LAT_QT // 2 else LAT_QT - NA_LOCAL_BLOCKS) * LAT_TQ
        tile = valid[qt * LAT_TQ:(qt + 1) * LAT_TQ]
        assert not tile[:, :base].any() and not tile[:, base + NA_LOCAL:].any()
        tiles.append(tile[:, base:base + NA_LOCAL])
    return np.where(np.stack(tiles), 0.0, NEG).astype(np.float32)


def _toeplitz_select():
    j = np.arange(LANES)
    c = np.clip(j - (GRID_W - 1), -(WIN_COLS - 1), WIN_COLS - 1) + (WIN_COLS - 1)
    return (np.arange(LANES)[:, None] == c[None, :]).astype(np.float32)


def _na_kernel(q_ref, k_ref, v_ref, ck_ref, cv_ref, mask_ref, rpb_ref, sel_ref, o_ref,
               toe_ref, bias_ref, k_s, v_s, ck_s, cv_s):
    b, qt = pl.program_id(1), pl.program_id(2)
    n_heads = 2 * NA_PAIRS

    @pl.when((qt == 0) & (b == 0))
    def _():
        low = lax.broadcasted_iota(jnp.int32, (GRID_W, LANES), 1) < GRID_W
        sel = sel_ref[...].astype(BF16)
        for hd in range(n_heads):
            gen = sum(_bdot(piece, sel) for piece in _split3(rpb_ref[hd]))
            for dd in range(N_DR + 1):
                lo = jnp.broadcast_to(gen[dd:dd + 1, :], (GRID_W, LANES))
                hi = jnp.broadcast_to(gen[dd + 1:dd + 2, :], (GRID_W, LANES))
                lo = pltpu.roll(lo, LANES - (GRID_W - 1), 1, stride=1, stride_axis=0)
                hi = pltpu.roll(hi, 1, 1, stride=1, stride_axis=0)
                toe_ref[hd, dd] = jnp.where(low, lo, hi)

    @pl.when(qt == 0)
    def _():
        for src, dst in ((k_ref, k_s), (v_ref, v_s), (ck_ref, ck_s), (cv_ref, cv_s)):
            for pair in range(NA_PAIRS):
                lo, hi = _pair_halves(src[:, pair * LANES:(pair + 1) * LANES])
                dst[2 * pair] = lo
                dst[2 * pair + 1] = hi

    base = _na_key_base(qt)
    local = pl.ds(pl.multiple_of(base * LAT_TQ, LAT_TQ), NA_LOCAL)
    for pair in range(NA_PAIRS):
        q = _scaled_q(q_ref[:, pair * LANES:(pair + 1) * LANES])
        heads = (2 * pair, 2 * pair + 1)
        biases = []
        for hd in heads:
            for rr in range(ROWS_PER_TQ):
                for kp in range(NA_LOCAL // LANES):
                    d = 2 * (kp + base * (ROWS_PER_TQ // 2)) - (qt * ROWS_PER_TQ + rr) + (WIN_ROWS - 1)
                    dd = jnp.clip(d, -1, N_DR - 1) + 1
                    bias_ref[hd, rr * GRID_W:(rr + 1) * GRID_W, kp * LANES:(kp + 1) * LANES] = toe_ref[hd, dd]
            biases.append([bias_ref[hd] + mask_ref[...], None])
        out = _attend_pair(q, [[k_s[hd, local, :], ck_s[hd]] for hd in heads],
                           [[v_s[hd, local, :], cv_s[hd]] for hd in heads], biases)
        o_ref[:, pair * LANES:(pair + 1) * LANES] = out.astype(o_ref.dtype)


def _lat_na_attention(y, cache_k, cache_v, rpb, layer):
    first = T_CTX // LAT_TQ
    kv_row = T_CTX // LAT_LEN
    wide = N_HEADS_B * HEAD_DIM
    gen = jnp.pad(rpb[layer], ((0, 0), (1, NA_TOE_ROWS - N_DR - 1), (0, LANES - N_DC)))
    w = NA_PAIRS * LANES
    nh = 2 * NA_PAIRS
    cq, ck, cv = COL_QB // NA_PAIRS, COL_KB // NA_PAIRS, COL_VB // NA_PAIRS
    return pl.pallas_call(
        _na_kernel,
        grid=(N_HEADS_B // nh, N_LAT_SETS, LAT_QT),
        in_specs=[
            pl.BlockSpec((LAT_TQ, w), lambda g, b, t: (first + b * LAT_QT + t, cq + g)),
            pl.BlockSpec((LAT_LEN, w), lambda g, b, t: (kv_row + b, ck + g)),
            pl.BlockSpec((LAT_LEN, w), lambda g, b, t: (kv_row + b, cv + g)),
            pl.BlockSpec((None, None, PAST_LEN, w), lambda g, b, t: (b, layer, 0, g)),
            pl.BlockSpec((None, None, PAST_LEN, w), lambda g, b, t: (b, layer, 0, g)),
            pl.BlockSpec((None, LAT_TQ, NA_LOCAL), lambda g, b, t: (t, 0, 0)),
            pl.BlockSpec((nh, NA_TOE_ROWS, LANES), lambda g, b, t: (g, 0, 0)),
            pl.BlockSpec((LANES, LANES), lambda g, b, t: (0, 0)),
        ],
        out_specs=pl.BlockSpec((LAT_TQ, w), lambda g, b, t: (b * LAT_QT + t, g)),
        out_shape=jax.ShapeDtypeStruct((T_LAT, wide), BF16),
        scratch_shapes=[pltpu.VMEM((nh, N_DR + 1, GRID_W, LANES), F32), pltpu.VMEM((nh, LAT_TQ, NA_LOCAL), F32),
                        pltpu.VMEM((nh, LAT_LEN, LANES), BF16), pltpu.VMEM((nh, LAT_LEN, LANES), BF16),
                        pltpu.VMEM((nh, PAST_LEN, LANES), BF16), pltpu.VMEM((nh, PAST_LEN, LANES), BF16)],
        compiler_params=_params(3),
        name="lat_na_attention",
    )(y, y, y, cache_k.reshape(N_LAT_SETS, DEPTH, PAST_LEN, wide), cache_v.reshape(N_LAT_SETS, DEPTH, PAST_LEN, wide),
      jnp.asarray(_window_mask()), gen, jnp.asarray(_toeplitz_select()))


MERGE_TM = 1024
MERGE_SUB = 1024
MERGE_ROUTE_TM = 512
MERGE_ROUTE_SUB = 512


def _merge_kernel(*refs, n_y, route_sets, sub):
    y_refs, refs = refs[:n_y], refs[n_y:]
    (h_ref, x_ref, gt1_ref, sh2_ref, sc2_ref, nf_ref, wab, wbb, wob, wgb, wr_ref), refs = refs[:11], refs[11:]
    if route_sets:
        (rlg_ref, rh_ref), refs = refs[:2], refs[2:]
    (x1_ref, h2_ref, lg_ref), refs = refs[:3], refs[3:]
    if route_sets:
        xg_ref, g_ref, rc_ref, p_s, rt_s = refs

    wr_hi, wr_lo = _split2(wr_ref[...])
    wr_both = jnp.concatenate([wr_hi, wr_lo], axis=0)
    half = N_HEADS_A * HEAD_DIM
    n_sub = x_ref.shape[0] // sub
    for r in range(n_sub):
        rows = slice(r * sub, (r + 1) * sub)
        if n_y == 1:
            ya, yb = y_refs[0][rows, :half], y_refs[0][rows, half:]
        else:
            ya, yb = y_refs[0][rows, :], y_refs[1][rows, :]
        sets = list(range(r * route_sets // n_sub, (r + 1) * route_sets // n_sub))

        def route(k):
            tok = slice(k * CTX_LEN, (k + 1) * CTX_LEN)
            slots = slice(k * CAP_CTX, (k + 1) * CAP_CTX)
            _route_set(rlg_ref[:, tok], rh_ref[tok, :], xg_ref.at[:, slots, :], g_ref.at[:, slots, :],
                       rc_ref.at[tok, :], p_s.at[k], rt_s.at[k], CTX_LEN, CAP_CTX)

        gates = _bdot(h_ref[rows, :], wgb[...])
        za = _bdot(ya, wab[...])
        zb = _bdot(yb, wbb[...])
        for k in sets[:len(sets) // 2]:
            route(k)
        m = jax.nn.sigmoid(gates[:, :D_MODEL]) * za + jax.nn.sigmoid(gates[:, D_MODEL:]) * zb
        x1 = x_ref[rows, :] + gt1_ref[...] * _bdot(m.astype(BF16), wob[...])
        x1_ref[rows, :] = x1
        h2 = _modnorm(x1, nf_ref[...], sh2_ref[...], sc2_ref[...])
        h2_ref[rows, :] = h2.astype(BF16)
        hh, hl = _split2(h2)
        both = _bdot_nt(wr_both, hh)
        lg_ref[:, rows] = both[:N_EXPERTS, :] + both[N_EXPERTS:, :] + _bdot_nt(wr_hi, hl)
        for k in sets[len(sets) // 2:]:
            route(k)


def _merge(y_parts, h, x, mod, norm_ffn, merge_w, w_router_t, layer, row_fn, tm, sub, route=None):
    n = x.shape[0]
    tiles = n // tm
    once = pl.Buffered(1)
    weight = lambda w: pl.BlockSpec(w.shape, lambda i: (0, 0), pipeline_mode=once)
    tile = lambda w: pl.BlockSpec((tm, w), lambda i: (i, 0))
    in_specs = [tile(p.shape[1]) for p in y_parts] + [
        tile(D_MODEL), tile(D_MODEL),
        _mod_spec(layer, row_fn, MOD_GT1), _mod_spec(layer, row_fn, MOD_SH2), _mod_spec(layer, row_fn, MOD_SC2),
        pl.BlockSpec((None, 1, D_MODEL), lambda i: (layer, 0, 0)),
    ] + [weight(w) for w in merge_w] + [
        pl.BlockSpec((None, N_EXPERTS, D_MODEL), lambda i: (layer, 0, 0), pipeline_mode=once)]
    args = list(y_parts) + [h, x, mod, mod, mod, norm_ffn, *merge_w, w_router_t]
    out_specs = [tile(D_MODEL), tile(D_MODEL), pl.BlockSpec((N_EXPERTS, tm), lambda i: (0, i))]
    out_shape = [jax.ShapeDtypeStruct((n, D_MODEL), F32), jax.ShapeDtypeStruct((n, D_MODEL), BF16),
                 jax.ShapeDtypeStruct((N_EXPERTS, n), F32)]
    scratch = []
    route_sets = 0
    if route is not None:
        route_sets = N_CTX_SETS // tiles
        rows, slots = route_sets * CTX_LEN, route_sets * CAP_CTX
        in_specs += [pl.BlockSpec((N_EXPERTS, rows), lambda i: (0, i)), pl.BlockSpec((rows, D_MODEL), lambda i: (i, 0))]
        args += list(route)
        out_specs += [pl.BlockSpec((N_EXPERTS, slots, D_MODEL), lambda i: (0, i, 0)),
                      pl.BlockSpec((N_EXPERTS, slots, LANES), lambda i: (0, i, 0)),
                      pl.BlockSpec((rows, LANES), lambda i: (i, 0))]
        out_shape += [jax.ShapeDtypeStruct((N_EXPERTS, N_CTX_SETS * CAP_CTX, D_MODEL), BF16),
                      jax.ShapeDtypeStruct((N_EXPERTS, N_CTX_SETS * CAP_CTX, LANES), F32),
                      jax.ShapeDtypeStruct((T_CTX, LANES), F32)]
        scratch += [pltpu.VMEM((route_sets, N_EXPERTS * CAP_CTX, CTX_LEN), BF16),
                    pltpu.VMEM((route_sets, LANES, CTX_LEN), F32)]
    return pl.pallas_call(
        functools.partial(_merge_kernel, n_y=len(y_parts), route_sets=route_sets, sub=sub),
        grid=(tiles,),
        in_specs=in_specs,
        out_specs=out_specs,
        out_shape=out_shape,
        scratch_shapes=scratch,
        compiler_params=_params(1),
        name="merge_route" if route_sets else "merge",
    )(*args)


GATHER_M = 512
RANK_TILE = 128


def _rank_row(aff, a_row, e, n):
    tiles = n // RANK_TILE
    sub = lax.broadcasted_iota(jnp.int32, (RANK_TILE, RANK_TILE), 0)
    lane = lax.broadcasted_iota(jnp.int32, (RANK_TILE, RANK_TILE), 1)
    earlier = jnp.where(sub < lane, 1.0, 0.0)
    acc = [jnp.zeros((8, RANK_TILE), F32) for _ in range(tiles)]
    for c in range(tiles):
        a_col = jnp.broadcast_to(aff[c * RANK_TILE:(c + 1) * RANK_TILE, e:e + 1], (RANK_TILE, RANK_TILE))
        for j in range(tiles):
            a_rj = a_row[:, j * RANK_TILE:(j + 1) * RANK_TILE]
            if c < j:
                beats = jnp.where(a_col >= a_rj, 1.0, 0.0)
            elif c > j:
                beats = jnp.where(a_col > a_rj, 1.0, 0.0)
            else:
                beats = jnp.where(a_col > a_rj, 1.0, jnp.where(a_col == a_rj, earlier, 0.0))
            acc[j] = acc[j] + beats.reshape(RANK_TILE // 8, 8, RANK_TILE).sum(axis=0)
    return jnp.concatenate([a.sum(axis=0, keepdims=True) for a in acc], axis=1)


def _route_set(lg, h, xg_ref, g_ref, rc_ref, p_ref, rt_ref, n, cap):
    ex = jnp.exp(lg - lg.max(axis=0, keepdims=True))
    aff_t = ex / ex.sum(axis=0, keepdims=True)
    aff = jnp.concatenate([aff_t, jnp.zeros((LANES - N_EXPERTS, n), F32)], axis=0).T
    rt_ref[...] = jnp.full((LANES, n), float(cap), F32)
    slot = lax.broadcasted_iota(jnp.int32, (cap, n), 0).astype(F32)
    for e in range(N_EXPERTS):
        rank = _rank_row(aff, aff_t[e:e + 1, :], e, n)
        rt_ref[e:e + 1, :] = jnp.minimum(rank, float(cap))
        p_ref[e * cap:(e + 1) * cap, :] = jnp.where(rank == slot, 1.0, 0.0).astype(BF16)

    a1, a2, a3 = (p.astype(F32) for p in _split3(aff))
    packed = (a1 + pltpu.roll(a2, N_EXPERTS, 1) + pltpu.roll(a3, 2 * N_EXPERTS, 1)).astype(BF16)
    per = GATHER_M // cap
    glane = lax.broadcasted_iota(jnp.int32, (cap, LANES), 1)
    for grp in range(N_EXPERTS * cap // GATHER_M):
        p = p_ref[grp * GATHER_M:(grp + 1) * GATHER_M, :]
        xg = _bdot(p, h).astype(BF16)
        gg = _bdot(p, packed)
        for k in range(per):
            e = grp * per + k
            xg_ref[e] = xg[k * cap:(k + 1) * cap, :]
            mine = (glane < 3 * N_EXPERTS) & ((glane & (N_EXPERTS - 1)) == e)
            ge = jnp.where(mine, gg[k * cap:(k + 1) * cap, :], 0.0).sum(axis=-1, keepdims=True)
            g_ref[e] = jnp.broadcast_to(ge, (cap, LANES))
    rc_ref[...] = rt_ref[...].T


def _route_kernel(lg_ref, h_ref, xg_ref, g_ref, rc_ref, p_ref, rt_ref, *, n, cap):
    _route_set(lg_ref[...], h_ref[...], xg_ref, g_ref, rc_ref, p_ref, rt_ref, n, cap)


def _route(logits, h2, n, cap, n_sets):
    return pl.pallas_call(
        functools.partial(_route_kernel, n=n, cap=cap),
        grid=(n_sets,),
        in_specs=[
            pl.BlockSpec((N_EXPERTS, n), lambda s: (0, s)),
            pl.BlockSpec((n, D_MODEL), lambda s: (s, 0)),
        ],
        out_specs=[
            pl.BlockSpec((N_EXPERTS, cap, D_MODEL), lambda s: (0, s, 0)),
            pl.BlockSpec((N_EXPERTS, cap, LANES), lambda s: (0, s, 0)),
            pl.BlockSpec((n, LANES), lambda s: (s, 0)),
        ],
        out_shape=[jax.ShapeDtypeStruct((N_EXPERTS, n_sets * cap, D_MODEL), BF16),
                   jax.ShapeDtypeStruct((N_EXPERTS, n_sets * cap, LANES), F32),
                   jax.ShapeDtypeStruct((n_sets * n, LANES), F32)],
        scratch_shapes=[pltpu.VMEM((N_EXPERTS * cap, n), BF16), pltpu.VMEM((LANES, n), F32)],
        compiler_params=_params(1),
        name=f"route_n{n}",
    )(logits, h2)


EXPERT_TF = 1024
EXPERT_SUB = 256
N_FF_TILES = EXPERT_FF // EXPERT_TF


def _expert_kernel(xc_ref, xl_ref, gc_ref, gl_ref, wg_ref, wu_ref, wd_ref, o_ref, x_s, acc_s):
    f = pl.program_id(1)
    n_ctx = xc_ref.shape[0]

    @pl.when(f == 0)
    def _():
        x_s[:n_ctx, :] = xc_ref[...]
        x_s[n_ctx:, :] = xl_ref[...]
        acc_s[...] = jnp.zeros_like(acc_s)

    x = x_s[...]
    part = None
    for j in range(EXPERT_TF // EXPERT_SUB):
        cols = slice(j * EXPERT_SUB, (j + 1) * EXPERT_SUB)
        gate = _bdot(x, wg_ref[:, cols].astype(BF16))
        up = _bdot(x, wu_ref[:, cols].astype(BF16))
        hid = (gate * jax.nn.sigmoid(gate)) * up
        down = _bdot(hid.astype(BF16), wd_ref[cols, :].astype(BF16))
        part = down if part is None else part + down
    acc_s[...] += part

    @pl.when(f == N_FF_TILES - 1)
    def _():
        o_ref[:n_ctx, :] = (acc_s[:n_ctx, :] * gc_ref[:, :1]).astype(o_ref.dtype)
        o_ref[n_ctx:, :] = (acc_s[n_ctx:, :] * gl_ref[:, :1]).astype(o_ref.dtype)


def _experts(xg_ctx, xg_lat, g_ctx, g_lat, w_gate, w_up, w_down, layer):
    sc, sl = xg_ctx.shape[1], xg_lat.shape[1]
    slots = lambda s, w: pl.BlockSpec((None, s, w), lambda e, f: (e, 0, 0))
    return pl.pallas_call(
        _expert_kernel,
        grid=(N_EXPERTS, N_FF_TILES),
        in_specs=[
            slots(sc, D_MODEL), slots(sl, D_MODEL), slots(sc, LANES), slots(sl, LANES),
            pl.BlockSpec((None, None, D_MODEL, EXPERT_TF), lambda e, f: (layer, e, 0, f)),
            pl.BlockSpec((None, None, D_MODEL, EXPERT_TF), lambda e, f: (layer, e, 0, f)),
            pl.BlockSpec((None, None, EXPERT_TF, D_MODEL), lambda e, f: (layer, e, f, 0)),
        ],
        out_specs=slots(sc + sl, D_MODEL),
        out_shape=jax.ShapeDtypeStruct((N_EXPERTS, sc + sl, D_MODEL), BF16),
        scratch_shapes=[pltpu.VMEM((sc + sl, D_MODEL), BF16), pltpu.VMEM((sc + sl, D_MODEL), F32)],
        compiler_params=_params(2),
        name="experts",
    )(xg_ctx, xg_lat, g_ctx, g_lat, w_gate, w_up, w_down)


COMB_TM = 256
COMB_STEPS = T_LAT // COMB_TM
COMB_CTX_SETS = N_CTX_SETS // COMB_STEPS


def _combine_group(o_ref, rc_ref, x_ref, gt2_ref, ng_ref, norm_refs, out_refs, cap, sets, final):
    slots = N_EXPERTS * cap
    j = lax.broadcasted_iota(jnp.int32, (LANES, slots), 1)
    e = lax.broadcasted_iota(jnp.int32, (LANES, slots), 0)
    expand = jnp.where(j // cap == e, 1.0, 0.0).astype(BF16)
    slot = (lax.broadcasted_iota(jnp.int32, (1, slots), 1) % cap).astype(F32)
    for k in range(sets):
        rows = slice(k * COMB_TM, (k + 1) * COMB_TM)
        rank = _bdot(rc_ref[rows, :].astype(BF16), expand)
        pt = jnp.where(rank == slot, 1.0, 0.0).astype(BF16)
        ffn = _bdot(pt, o_ref[:, k * cap:(k + 1) * cap, :].reshape(slots, D_MODEL))
        x = x_ref[rows, :] + gt2_ref[...] * ffn
        if final:
            out_refs[0][rows, :] = _rms(x) * ng_ref[...]
        else:
            out_refs[0][rows, :] = x
            out_refs[1][rows, :] = _modnorm(x, ng_ref[...], norm_refs[0][...], norm_refs[1][...]).astype(BF16)


def _combine_kernel(*refs, final):
    n_in, n_out = (4, 1) if final else (6, 2)
    ng_ref = refs[2 * n_in]
    outs = refs[2 * n_in + 1:]
    for g, (cap, sets) in enumerate(((CAP_CTX, COMB_CTX_SETS), (CAP_LAT, 1))):
        o_ref, rc_ref, x_ref, gt2_ref, *norm_refs = refs[g * n_in:(g + 1) * n_in]
        _combine_group(o_ref, rc_ref, x_ref, gt2_ref, ng_ref, norm_refs, outs[g * n_out:(g + 1) * n_out], cap, sets,
                       final)


def _combine(out, rc_ctx, rc_lat, x1_ctx, x1_lat, mod, norm_g, layer, final):
    lat_tiles = LAT_LEN // COMB_TM
    lat_slot0 = N_CTX_SETS * CAP_CTX // CAP_LAT
    groups = (
        (rc_ctx, x1_ctx, lambda i: 0, COMB_CTX_SETS * CTX_LEN,
         pl.BlockSpec((N_EXPERTS, COMB_CTX_SETS * CAP_CTX, D_MODEL), lambda i: (0, i, 0))),
        (rc_lat, x1_lat, lambda i: 1 + i // lat_tiles, COMB_TM,
         pl.BlockSpec((N_EXPERTS, CAP_LAT, D_MODEL), lambda i: (0, lat_slot0 + i // lat_tiles, 0))),
    )
    in_specs, args, out_specs, out_shape = [], [], [], []
    for rc, x1, row_fn, tm, o_spec in groups:
        tile = lambda w, tm=tm: pl.BlockSpec((tm, w), lambda i: (i, 0))
        in_specs += [o_spec, tile(LANES), tile(D_MODEL), _mod_spec(layer, row_fn, MOD_GT2)]
        args += [out, rc, x1, mod]
        if not final:
            in_specs += [_mod_spec(layer + 1, row_fn, MOD_SH1), _mod_spec(layer + 1, row_fn, MOD_SC1)]
            args += [mod, mod]
        out_specs += [tile(D_MODEL)] * (1 if final else 2)
        out_shape += [jax.ShapeDtypeStruct(x1.shape, F32)] + ([] if final else [jax.ShapeDtypeStruct(x1.shape, BF16)])
    if final:
        in_specs.append(pl.BlockSpec((1, D_MODEL), lambda i: (0, 0)))
        args.append(norm_g.reshape(1, D_MODEL))
    else:
        in_specs.append(pl.BlockSpec((None, 1, D_MODEL), lambda i: (layer + 1, 0, 0)))
        args.append(norm_g)
    res = pl.pallas_call(
        functools.partial(_combine_kernel, final=final),
        grid=(COMB_STEPS,),
        in_specs=in_specs,
        out_specs=out_specs,
        out_shape=out_shape,
        compiler_params=_params(1),
        name="combine",
    )(*args)
    half = len(res) // 2
    return tuple(res[:half]), tuple(res[half:])


def kernel(x_prompt, x_sample, cache_attn_k, cache_attn_v, cache_na_k, cache_na_v, c, c_ctx, w_ada, b_ada,
           norm_mix, norm_ffn, w_in, q_norm, k_norm, rpb, w_branch_a, w_branch_b, w_out, w_router, w_gate,
           w_up, w_down, final_norm):
    x_ctx = x_prompt.reshape(T_CTX, D_MODEL)
    x_lat = x_sample.reshape(T_LAT, D_MODEL)
    mod = _modulation(c, c_ctx, w_ada, b_ada)
    norm_mix3 = norm_mix.reshape(DEPTH, 1, D_MODEL)
    norm_ffn3 = norm_ffn.reshape(DEPTH, 1, D_MODEL)
    w_router_t = jnp.swapaxes(w_router, 1, 2)
    ctx_row = lambda *g: 0

    h_ctx = _prenorm(x_ctx, mod, norm_mix3, 0, ctx_row)
    h_lat = _prenorm(x_lat, mod, norm_mix3, 0, _lat_mod_row(NORM_TM))
    layer_caches = []
    for layer in range(DEPTH):
        last = layer == DEPTH - 1
        y, merge_w = _projection(h_ctx, h_lat, w_in, q_norm, k_norm, w_branch_a, w_branch_b, w_out, layer)
        yab_ctx, *new_caches = _ctx_attention(y, layer_caches if last else [])
        layer_caches.append(tuple(new_caches))
        ya_lat = _lat_gqa_attention(y, cache_attn_k, cache_attn_v, layer)
        yb_lat = _lat_na_attention(y, cache_na_k, cache_na_v, rpb, layer)
        merge_args = (mod, norm_ffn3, merge_w, w_router_t, layer)
        x1_ctx, h2_ctx, lg_ctx = _merge([yab_ctx], h_ctx, x_ctx, *merge_args, ctx_row, MERGE_TM, MERGE_SUB)
        x1_lat, h2_lat, lg_lat, xg_ctx, g_ctx, rc_ctx = _merge(
            [ya_lat, yb_lat], h_lat, x_lat, *merge_args, _lat_mod_row(MERGE_ROUTE_TM), MERGE_ROUTE_TM,
            MERGE_ROUTE_SUB, route=(lg_ctx, h2_ctx))
        xg_lat, g_lat, rc_lat = _route(lg_lat, h2_lat, LAT_LEN, CAP_LAT, N_LAT_SETS)
        out = _experts(xg_ctx, xg_lat, g_ctx, g_lat, w_gate, w_up, w_down, layer)
        final = layer == DEPTH - 1
        norm_g = final_norm if final else norm_mix3
        res_ctx, res_lat = _combine(out, rc_ctx, rc_lat, x1_ctx, x1_lat, mod, norm_g, layer, final)
        if final:
            (y_ctx,), (y_lat,) = res_ctx, res_lat
        else:
            (x_ctx, h_ctx), (x_lat, h_lat) = res_ctx, res_lat

    heads = (N_KV_A, N_KV_A, N_HEADS_B, N_HEADS_B)
    new_caches = [a.reshape(N_CTX_SETS, DEPTH, CTX_LEN, h, HEAD_DIM) for a, h in zip(layer_caches[-1], heads)]
    return (y_ctx.reshape(N_CTX_SETS, CTX_LEN, D_MODEL), y_lat.reshape(N_LAT_SETS, LAT_LEN, D_MODEL), *new_caches)
```

```python
import functools

import numpy as np
import jax
import jax.numpy as jnp
from jax import lax
from jax.experimental import pallas as pl
from jax.experimental.pallas import tpu as pltpu

F32 = jnp.float32
BF16 = jnp.bfloat16

D_MODEL = 1024
N_CTX_SETS, CTX_LEN = 16, 256
N_LAT_SETS, LAT_LEN = 2, 1024
T_CTX = N_CTX_SETS * CTX_LEN
T_LAT = N_LAT_SETS * LAT_LEN
T_ALL = T_CTX + T_LAT
DEPTH = 2
PAST_LEN = 512
GRID_W = 64
GRID_ROWS = LAT_LEN // GRID_W
HEAD_DIM = 64
N_HEADS_A, N_KV_A, N_HEADS_B = 8, 2, 8
WIN_ROWS, WIN_COLS = 8, 16
N_EXPERTS = 16
EXPERT_FF = 2048
CAP_CTX = 2 * CTX_LEN // N_EXPERTS
CAP_LAT = 2 * LAT_LEN // N_EXPERTS
ROPE_THETA = 10000.0
EPS = 1e-6
NEG = -1e30
QKV_DIM = 2304
ATT_SCALE = HEAD_DIM ** -0.5

LANES = 128
VMEM_LIMIT = 56 * 1024 * 1024

PROJ_TN = 256
N_PROJ_TILES = QKV_DIM // PROJ_TN
PROJ_PERM = np.array([0, 1, 8, 2, 3, 4, 5, 6, 7], np.int32)
CAST_STEPS = 8
MOD_SH1, MOD_SC1, MOD_GT1, MOD_SH2, MOD_SC2, MOD_GT2 = range(6)


def _params(n_grid_dims, vmem=VMEM_LIMIT):
    return pltpu.CompilerParams(dimension_semantics=("arbitrary",) * n_grid_dims, vmem_limit_bytes=vmem)


def _bdot(a, b):
    return jnp.dot(a, b, preferred_element_type=F32)


def _bdot_nt(a, b):
    return lax.dot_general(a, b, (((1,), (1,)), ((), ())), preferred_element_type=F32)


def _split2(x):
    hi = x.astype(BF16)
    lo = (x - hi.astype(F32)).astype(BF16)
    return hi, lo


def _split3(x):
    hi = x.astype(BF16)
    r = x - hi.astype(F32)
    mid = r.astype(BF16)
    lo = (r - mid.astype(F32)).astype(BF16)
    return hi, mid, lo


def _rms(x):
    return x * lax.rsqrt(jnp.mean(x * x, axis=-1, keepdims=True) + EPS)


def _modnorm(x, g, sh, sc):
    return (_rms(x) * g) * (1.0 + sc) + sh


def _lat_mod_row(tile_rows):
    return lambda i: 1 + (i * tile_rows) // LAT_LEN


def _mod_spec(layer, row_fn, chunk):
    return pl.BlockSpec((None, None, 1, D_MODEL), lambda *g: (layer, row_fn(*g), 0, chunk))


MOD_TN = 1536


def _mod_kernel(ct_ref, w_ref, b_ref, o_ref):
    ct = ct_ref[...]
    act = ct * jax.nn.sigmoid(ct)
    w = w_ref[...]
    for m in range(3):
        o_ref[m:m + 1, :] = jnp.sum(w * act[:, m:m + 1], axis=0, keepdims=True) + b_ref[...]
    o_ref[3:8, :] = jnp.zeros((5, MOD_TN), F32)


def _modulation(c, c_ctx, w_ada, b_ada):
    cond = jnp.concatenate([c_ctx[None, :], c, jnp.zeros((5, D_MODEL), F32)], axis=0)
    mod = pl.pallas_call(
        _mod_kernel,
        grid=(DEPTH, 6 * D_MODEL // MOD_TN),
        in_specs=[
            pl.BlockSpec((D_MODEL, 8), lambda l, j: (0, 0)),
            pl.BlockSpec((None, D_MODEL, MOD_TN), lambda l, j: (l, 0, j)),
            pl.BlockSpec((None, 1, MOD_TN), lambda l, j: (l, 0, j)),
        ],
        out_specs=pl.BlockSpec((None, 8, MOD_TN), lambda l, j: (l, 0, j)),
        out_shape=jax.ShapeDtypeStruct((DEPTH, 8, 6 * D_MODEL), F32),
        compiler_params=_params(2),
        name="modulation",
    )(cond.T, w_ada, b_ada.reshape(DEPTH, 1, 6 * D_MODEL))
    return mod.reshape(DEPTH, 8, 1, 6 * D_MODEL)


NORM_TM = 512


def _prenorm_kernel(x_ref, g_ref, sh_ref, sc_ref, h_ref):
    h_ref[...] = _modnorm(x_ref[...], g_ref[...], sh_ref[...], sc_ref[...]).astype(BF16)


def _prenorm(x, mod, norm_g, layer, row_fn):
    n = x.shape[0]
    return pl.pallas_call(
        _prenorm_kernel,
        grid=(n // NORM_TM,),
        in_specs=[
            pl.BlockSpec((NORM_TM, D_MODEL), lambda i: (i, 0)),
            pl.BlockSpec((None, 1, D_MODEL), lambda i: (layer, 0, 0)),
            _mod_spec(layer, row_fn, MOD_SH1),
            _mod_spec(layer, row_fn, MOD_SC1),
        ],
        out_specs=pl.BlockSpec((NORM_TM, D_MODEL), lambda i: (i, 0)),
        out_shape=jax.ShapeDtypeStruct((n, D_MODEL), BF16),
        compiler_params=_params(1),
        name="prenorm",
    )(x, norm_g, mod, mod)


PROJ_CH = 1024


def _rope_tables():
    t = np.arange(LAT_LEN)
    lane = np.arange(LANES) % HEAD_DIM
    pos = np.where(lane < HEAD_DIM // 2, (t // GRID_W)[:, None], (t % GRID_W)[:, None]).astype(np.float64)
    freq = ROPE_THETA ** (-(lane % 16).astype(np.float64) / 16.0)
    ang = pos * freq[None, :]
    sign = np.where((lane & 16) == 0, -1.0, 1.0)[None, :]
    return np.cos(ang).astype(np.float32), (np.sin(ang) * sign).astype(np.float32)


def _head_norm_rope(y, gain, cos, sin):
    w = y.shape[1]
    r = (lax.broadcasted_iota(jnp.int32, (2 * w, w), 0) % w) // HEAD_DIM
    c = lax.broadcasted_iota(jnp.int32, (2 * w, w), 1) // HEAD_DIM
    seg = jnp.where(r == c, 1.0 / HEAD_DIM, 0.0).astype(BF16)
    ms = _bdot(jnp.concatenate(_split2(y * y), axis=1), seg)
    yn = y * lax.rsqrt(ms + EPS) * gain
    if cos is None:
        return yn
    lane = lax.broadcasted_iota(jnp.int32, yn.shape, 1)
    partner = jnp.where((lane & 16) == 0, pltpu.roll(yn, w - 16, 1), pltpu.roll(yn, 16, 1))
    if w > LANES:
        cos = jnp.concatenate([cos] * (w // LANES), axis=1)
        sin = jnp.concatenate([sin] * (w // LANES), axis=1)
    return yn * cos + partner * sin


def _proj_kernel(perm_ref, hc_ref, hl_ref, w_ref, gain_ref, cos_ref, sin_ref, wa_ref, wbr_ref, wo_ref, wg_ref,
                 o_ref, oa_ref, ob_ref, oo_ref, og_ref, wb_ref):
    del perm_ref
    j = pl.program_id(0)
    wb_ref[...] = w_ref[...].astype(BF16)
    oa_ref[...] = wa_ref[...].astype(BF16)
    ob_ref[...] = wbr_ref[...].astype(BF16)
    oo_ref[...] = wo_ref[...].astype(BF16)
    og_ref[...] = wg_ref[0].astype(BF16)
    chunks = [(hc_ref, k * PROJ_CH, k * PROJ_CH, False) for k in range(T_CTX // PROJ_CH)]
    chunks += [(hl_ref, k * PROJ_CH, T_CTX + k * PROJ_CH, True) for k in range(T_LAT // PROJ_CH)]

    def matmul(chunk):
        h_ref, r0, o0, _ = chunk
        o_ref[o0:o0 + PROJ_CH, :] = _bdot(h_ref[r0:r0 + PROJ_CH, :], wb_ref[...])

    def finish(chunk, width):
        _, _, o0, is_lat = chunk
        cos, sin = (cos_ref[...], sin_ref[...]) if is_lat else (None, None)
        y = o_ref[o0:o0 + PROJ_CH, :width]
        o_ref[o0:o0 + PROJ_CH, :width] = _head_norm_rope(y, gain_ref[:, :width], cos, sin)

    def tile(width):
        matmul(chunks[0])
        for k in range(1, len(chunks)):
            matmul(chunks[k])
            if width:
                finish(chunks[k - 1], width)
        if width:
            finish(chunks[-1], width)

    pl.when(j < 2)(lambda: tile(PROJ_TN))
    pl.when(j == 2)(lambda: tile(LANES))
    pl.when(j > 2)(lambda: tile(0))


def _projection(h_ctx, h_lat, w_in, q_norm, k_norm, w_ba, w_bb, w_out, layer):
    ones = jnp.ones((2 * HEAD_DIM,), F32)
    gain = jnp.stack([jnp.tile(q_norm[layer], 4), jnp.tile(q_norm[layer], 4),
                      jnp.concatenate([jnp.tile(k_norm[layer], 2), ones])])[:, None, :]
    cos, sin = _rope_tables()
    half = N_HEADS_A * HEAD_DIM
    ra, ro = half // CAST_STEPS, D_MODEL // CAST_STEPS
    part = lambda j: jnp.minimum(j, CAST_STEPS - 1)
    grid_spec = pltpu.PrefetchScalarGridSpec(
        num_scalar_prefetch=1,
        grid=(N_PROJ_TILES,),
        in_specs=[
            pl.BlockSpec((T_CTX, D_MODEL), lambda j, p: (0, 0)),
            pl.BlockSpec((T_LAT, D_MODEL), lambda j, p: (0, 0)),
            pl.BlockSpec((None, D_MODEL, PROJ_TN), lambda j, p: (layer, 0, j)),
            pl.BlockSpec((None, 1, PROJ_TN), lambda j, p: (jnp.minimum(j, 2), 0, 0)),
            pl.BlockSpec((LAT_LEN, LANES), lambda j, p: (0, 0)),
            pl.BlockSpec((LAT_LEN, LANES), lambda j, p: (0, 0)),
            pl.BlockSpec((None, ra, D_MODEL), lambda j, p: (layer, part(j), 0)),
            pl.BlockSpec((None, ra, D_MODEL), lambda j, p: (layer, part(j), 0)),
            pl.BlockSpec((None, ro, D_MODEL), lambda j, p: (layer, part(j), 0)),
            pl.BlockSpec((pl.Element(1), pl.Element(ro), pl.Element(2 * D_MODEL)),
                         lambda j, p: (layer, part(j) * ro, QKV_DIM)),
        ],
        out_specs=[
            pl.BlockSpec((T_ALL, PROJ_TN), lambda j, p: (0, p[j])),
            pl.BlockSpec((ra, D_MODEL), lambda j, p: (part(j), 0)),
            pl.BlockSpec((ra, D_MODEL), lambda j, p: (part(j), 0)),
            pl.BlockSpec((ro, D_MODEL), lambda j, p: (part(j), 0)),
            pl.BlockSpec((ro, 2 * D_MODEL), lambda j, p: (part(j), 0)),
        ],
        scratch_shapes=[pltpu.VMEM((D_MODEL, PROJ_TN), BF16)],
    )
    y, *merge_w = pl.pallas_call(
        _proj_kernel,
        grid_spec=grid_spec,
        out_shape=[jax.ShapeDtypeStruct((T_ALL, QKV_DIM), F32),
                   jax.ShapeDtypeStruct((half, D_MODEL), BF16), jax.ShapeDtypeStruct((half, D_MODEL), BF16),
                   jax.ShapeDtypeStruct((D_MODEL, D_MODEL), BF16), jax.ShapeDtypeStruct((D_MODEL, 2 * D_MODEL), BF16)],
        compiler_params=_params(1),
        name="projection",
    )(jnp.asarray(PROJ_PERM), h_ctx, h_lat, w_in, gain, jnp.asarray(cos), jnp.asarray(sin), w_ba, w_bb, w_out, w_in)
    return y, merge_w


COL_QA, COL_QB, COL_KB, COL_VB, COL_KA, COL_VA = 0, 4, 8, 12, 16, 17


def _lane_is_low(shape):
    return lax.broadcasted_iota(jnp.int32, shape, 1) < HEAD_DIM


def _pair_halves(x):
    low = _lane_is_low(x.shape)
    xb = x.astype(BF16)
    zero = jnp.zeros_like(xb)
    return jnp.where(low, xb, zero), jnp.where(low, zero, xb)


def _scaled_q(q):
    assert ATT_SCALE == 0.125
    return (q * ATT_SCALE).astype(BF16)


def _attend_pair(q, keys, values, biases, joint_pv=True):
    probs, dens, out = [], [], None
    for h in range(2):
        scores = []
        for k, b in zip(keys[h], biases[h]):
            s = _bdot_nt(q, k)
            scores.append(s if b is None else s + b)
        m = scores[0].max(axis=-1, keepdims=True)
        for s in scores[1:]:
            m = jnp.maximum(m, s.max(axis=-1, keepdims=True))
        den, num = None, None
        for s, v in zip(scores, values[h]):
            e = jnp.exp(s - m)
            d = e.sum(axis=-1, keepdims=True)
            den = d if den is None else den + d
            if joint_pv:
                probs.append(e.astype(BF16))
            else:
                o = _bdot(e.astype(BF16), v)
                num = o if num is None else num + o
        dens.append(den)
        if not joint_pv:
            out = num / den if out is None else out + num / den
    if not joint_pv:
        return out
    num = _bdot(jnp.concatenate(probs, axis=1), jnp.concatenate(values[0] + values[1], axis=0))
    return num / jnp.where(_lane_is_low(num.shape), dens[0], dens[1])


def _attend_pair_t(q, keys, values_t, biases_t):
    probs, dens = [], []
    for h in range(2):
        scores = []
        for k, b in zip(keys[h], biases_t[h]):
            s = _bdot_nt(k, q)
            scores.append(s if b is None else s + b)
        m = scores[0].max(axis=0, keepdims=True)
        for s in scores[1:]:
            m = jnp.maximum(m, s.max(axis=0, keepdims=True))
        den = None
        for s in scores:
            e = jnp.exp(s - m)
            d = e.sum(axis=0, keepdims=True)
            den = d if den is None else den + d
            probs.append(e.astype(BF16))
        dens.append(den)
    num_t = _bdot(values_t, jnp.concatenate(probs, axis=0))
    row = lax.broadcasted_iota(jnp.int32, num_t.shape, 0)
    return (num_t / jnp.where(row < HEAD_DIM, dens[0], dens[1])).T


def _gqa_variants(x):
    lo, hi = _pair_halves(x)
    sw_lo, sw_hi = _pair_halves(pltpu.roll(x, HEAD_DIM, 1))
    return [lo, sw_hi, sw_lo, hi]


def _gqa_attention(q_ref, kvar, vvar, o_ref, joint_pv):
    for pair in range(N_HEADS_A // 2):
        q = _scaled_q(q_ref[:, pair * LANES:(pair + 1) * LANES])
        kvh = (2 * pair) // (N_HEADS_A // N_KV_A)
        out = _attend_pair(q, [[kvar[2 * kvh]], [kvar[2 * kvh + 1]]], [[vvar[2 * kvh]], [vvar[2 * kvh + 1]]],
                           [[None], [None]], joint_pv)
        o_ref[:, pair * LANES:(pair + 1) * LANES] = out.astype(o_ref.dtype)


def _ctx_attn_kernel(qa_ref, qb_ref, kb_ref, vb_ref, kava_ref, *refs, n_prev):
    prev = refs[:4 * n_prev]
    y_ref, nak_ref, nav_ref, nbk_ref, nbv_ref = refs[4 * n_prev:]
    ka = kava_ref[:, :LANES]
    va = kava_ref[:, LANES:]
    new = (ka, va, kb_ref[...], vb_ref[...])
    for c, (o_ref, val) in enumerate(zip((nak_ref, nav_ref, nbk_ref, nbv_ref), new)):
        if n_prev:
            for p in range(n_prev):
                o_ref[p] = prev[4 * p + c][...]
            o_ref[n_prev] = val
        else:
            o_ref[...] = val
    _gqa_attention(qa_ref, _gqa_variants(ka), _gqa_variants(va), y_ref.at[:, :N_HEADS_A * HEAD_DIM], joint_pv=False)
    for pair in range(N_HEADS_B // 2):
        cols = slice(pair * LANES, (pair + 1) * LANES)
        q = _scaled_q(qb_ref[:, cols])
        k_lo, k_hi = _pair_halves(kb_ref[:, cols])
        v_lo, v_hi = _pair_halves(vb_ref[:, cols])
        out = _attend_pair(q, [[k_lo], [k_hi]], [[v_lo], [v_hi]], [[None], [None]], joint_pv=False)
        y_ref[:, N_HEADS_A * HEAD_DIM + pair * LANES:N_HEADS_A * HEAD_DIM + (pair + 1) * LANES] = out.astype(BF16)


def _ctx_attention(y, prev_caches):
    wide = 4 * LANES
    widths = (LANES, LANES, wide, wide)
    n_prev = len(prev_caches)
    row = lambda b: (b, 0)
    f32 = lambda *s: jax.ShapeDtypeStruct(s, F32)
    in_specs = [
        pl.BlockSpec((CTX_LEN, wide), lambda b: (b, 0)),
        pl.BlockSpec((CTX_LEN, wide), lambda b: (b, 1)),
        pl.BlockSpec((CTX_LEN, wide), lambda b: (b, 2)),
        pl.BlockSpec((CTX_LEN, wide), lambda b: (b, 3)),
        pl.BlockSpec((CTX_LEN, 2 * LANES), lambda b: (b, COL_KA // 2)),
    ]
    args = [y, y, y, y, y]
    for layer_caches in prev_caches:
        in_specs += [pl.BlockSpec((CTX_LEN, w), row) for w in widths]
        args += list(layer_caches)
    if n_prev:
        cache_specs = [pl.BlockSpec((None, n_prev + 1, CTX_LEN, w), lambda b: (b, 0, 0, 0)) for w in widths]
        cache_shapes = [f32(N_CTX_SETS, n_prev + 1, CTX_LEN, w) for w in widths]
    else:
        cache_specs = [pl.BlockSpec((CTX_LEN, w), row) for w in widths]
        cache_shapes = [f32(T_CTX, w) for w in widths]
    return pl.pallas_call(
        functools.partial(_ctx_attn_kernel, n_prev=n_prev),
        grid=(N_CTX_SETS,),
        in_specs=in_specs,
        out_specs=[pl.BlockSpec((CTX_LEN, 2 * wide), row)] + cache_specs,
        out_shape=[jax.ShapeDtypeStruct((T_CTX, 2 * wide), BF16)] + cache_shapes,
        compiler_params=_params(1),
        name="ctx_attention",
    )(*args)


LAT_TQ = 512
LAT_QT = LAT_LEN // LAT_TQ
GQA_TQ = 512


def _lat_gqa_kernel(qa_ref, kava_ref, ck_ref, cv_ref, o_ref, k_s, vt_s):
    n_keys = PAST_LEN + LAT_LEN

    @pl.when(pl.program_id(1) == 0)
    def _():
        for i, (c, x) in enumerate(zip(_gqa_variants(ck_ref[...]), _gqa_variants(kava_ref[:, :LANES]))):
            k_s[i, :PAST_LEN, :] = c
            k_s[i, PAST_LEN:, :] = x
        for src, cols in ((cv_ref[...], slice(0, PAST_LEN)), (kava_ref[:, LANES:], slice(PAST_LEN, n_keys))):
            vt = src.T
            swapped = pltpu.roll(vt, HEAD_DIM, 0)
            top = lax.broadcasted_iota(jnp.int32, vt.shape, 0) < HEAD_DIM
            zero = jnp.zeros_like(vt)
            per_head = ((jnp.where(top, vt, zero), jnp.where(top, zero, swapped)),
                        (jnp.where(top, swapped, zero), jnp.where(top, zero, vt)))
            for kvh in range(N_KV_A):
                for half in range(2):
                    off = half * n_keys
                    vt_s[kvh, :, off + cols.start:off + cols.stop] = per_head[kvh][half].astype(BF16)

    for pair in range(N_HEADS_A // 2):
        q = _scaled_q(qa_ref[:, pair * LANES:(pair + 1) * LANES])
        kvh = (2 * pair) // (N_HEADS_A // N_KV_A)
        out = _attend_pair_t(q, [[k_s[2 * kvh]], [k_s[2 * kvh + 1]]], vt_s[kvh], [[None], [None]])
        o_ref[:, pair * LANES:(pair + 1) * LANES] = out.astype(o_ref.dtype)


def _lat_gqa_attention(y, cache_k, cache_v, layer):
    wide = 4 * LANES
    tiles = LAT_LEN // GQA_TQ
    first = T_CTX // GQA_TQ
    cache = pl.BlockSpec((None, None, PAST_LEN, LANES), lambda b, t: (b, layer, 0, 0))
    return pl.pallas_call(
        _lat_gqa_kernel,
        grid=(N_LAT_SETS, tiles),
        in_specs=[
            pl.BlockSpec((GQA_TQ, wide), lambda b, t: (first + b * tiles + t, 0)),
            pl.BlockSpec((LAT_LEN, 2 * LANES), lambda b, t: (T_CTX // LAT_LEN + b, COL_KA // 2)),
            cache, cache,
        ],
        out_specs=pl.BlockSpec((GQA_TQ, wide), lambda b, t: (b * tiles + t, 0)),
        out_shape=jax.ShapeDtypeStruct((T_LAT, wide), BF16),
        scratch_shapes=[pltpu.VMEM((4, PAST_LEN + LAT_LEN, LANES), BF16),
                        pltpu.VMEM((N_KV_A, LANES, 2 * (PAST_LEN + LAT_LEN)), BF16)],
        compiler_params=_params(2),
        name="lat_gqa_attention",
    )(y, y, cache_k.reshape(N_LAT_SETS, DEPTH, PAST_LEN, LANES), cache_v.reshape(N_LAT_SETS, DEPTH, PAST_LEN, LANES))


N_DR = 2 * WIN_ROWS - 1
N_DC = 2 * WIN_COLS - 1
ROWS_PER_TQ = LAT_TQ // GRID_W
NA_LOCAL = 768
NA_KEY_STEP = 2 * LANES
NA_LAST_BASE = (LAT_LEN - NA_LOCAL) // NA_KEY_STEP
NA_TOE_ROWS = 32
NA_PAIRS = 4


def _na_key_base(qt):
    return jnp.where(qt < LAT_QT // 2, 0, NA_LAST_BASE)


def _window_mask():
    r = np.arange(GRID_ROWS)
    row_start = np.clip(r - WIN_ROWS // 2, 0, GRID_ROWS - WIN_ROWS)
    in_rows = (r[None, :] >= row_start[:, None]) & (r[None, :] < row_start[:, None] + WIN_ROWS)
    cq = np.arange(GRID_W)
    col_start = np.clip(cq - WIN_COLS // 2, 0, GRID_W - WIN_COLS)
    in_cols = (cq[None, :] >= col_start[:, None]) & (cq[None, :] < col_start[:, None] + WIN_COLS)
    valid = (in_rows[:, None, :, None] & in_cols[None, :, None, :]).reshape(LAT_LEN, LAT_LEN)
    tiles = []
    for qt in range(LAT_QT):
        base = (0 if qt < LAT_QT // 2 else NA_LAST_BASE) * NA_KEY_STEP
        tile = valid[qt * LAT_TQ:(qt + 1) * LAT_TQ]
        assert not tile[:, :base].any() and not tile[:, base + NA_LOCAL:].any()
        tiles.append(tile[:, base:base + NA_LOCAL])
    return np.where(np.stack(tiles), 0.0, NEG).astype(np.float32)


def _toeplitz_select():
    j = np.arange(LANES)
    c = np.clip(j - (GRID_W - 1), -(WIN_COLS - 1), WIN_COLS - 1) + (WIN_COLS - 1)
    return (np.arange(LANES)[:, None] == c[None, :]).astype(np.float32)


def _na_kernel(q_ref, k_ref, v_ref, ck_ref, cv_ref, mask_ref, rpb_ref, sel_ref, o_ref,
               toe_ref, bias_ref, k_s, v_s, ck_s, cv_s):
    b, qt = pl.program_id(1), pl.program_id(2)
    n_heads = 2 * NA_PAIRS

    @pl.when((qt == 0) & (b == 0))
    def _():
        low = lax.broadcasted_iota(jnp.int32, (GRID_W, LANES), 1) < GRID_W
        sel = sel_ref[...].astype(BF16)
        for hd in range(n_heads):
            gen = sum(_bdot(piece, sel) for piece in _split3(rpb_ref[hd]))
            for dd in range(N_DR + 1):
                lo = jnp.broadcast_to(gen[dd:dd + 1, :], (GRID_W, LANES))
                hi = jnp.broadcast_to(gen[dd + 1:dd + 2, :], (GRID_W, LANES))
                lo = pltpu.roll(lo, LANES - (GRID_W - 1), 1, stride=1, stride_axis=0)
                hi = pltpu.roll(hi, 1, 1, stride=1, stride_axis=0)
                toe_ref[hd, dd] = jnp.where(low, lo, hi)

    @pl.when(qt == 0)
    def _():
        for src, dst in ((k_ref, k_s), (v_ref, v_s), (ck_ref, ck_s), (cv_ref, cv_s)):
            for pair in range(NA_PAIRS):
                lo, hi = _pair_halves(src[:, pair * LANES:(pair + 1) * LANES])
                dst[2 * pair] = lo
                dst[2 * pair + 1] = hi

    base = _na_key_base(qt)
    local = pl.ds(pl.multiple_of(base * NA_KEY_STEP, NA_KEY_STEP), NA_LOCAL)
    for pair in range(NA_PAIRS):
        q = _scaled_q(q_ref[:, pair * LANES:(pair + 1) * LANES])
        heads = (2 * pair, 2 * pair + 1)
        biases = []
        for hd in heads:
            for rr in range(ROWS_PER_TQ):
                for kp in range(NA_LOCAL // LANES):
                    d = 2 * (kp + base * (NA_KEY_STEP // LANES)) - (qt * ROWS_PER_TQ + rr) + (WIN_ROWS - 1)
                    dd = jnp.clip(d, -1, N_DR - 1) + 1
                    bias_ref[hd, rr * GRID_W:(rr + 1) * GRID_W, kp * LANES:(kp + 1) * LANES] = toe_ref[hd, dd]
            biases.append([bias_ref[hd] + mask_ref[...], None])
        out = _attend_pair(q, [[k_s[hd, local, :], ck_s[hd]] for hd in heads],
                           [[v_s[hd, local, :], cv_s[hd]] for hd in heads], biases)
        o_ref[:, pair * LANES:(pair + 1) * LANES] = out.astype(o_ref.dtype)


def _lat_na_attention(y, cache_k, cache_v, rpb, layer):
    first = T_CTX // LAT_TQ
    kv_row = T_CTX // LAT_LEN
    wide = N_HEADS_B * HEAD_DIM
    gen = jnp.pad(rpb[layer], ((0, 0), (1, NA_TOE_ROWS - N_DR - 1), (0, LANES - N_DC)))
    w = NA_PAIRS * LANES
    nh = 2 * NA_PAIRS
    cq, ck, cv = COL_QB // NA_PAIRS, COL_KB // NA_PAIRS, COL_VB // NA_PAIRS
    return pl.pallas_call(
        _na_kernel,
        grid=(N_HEADS_B // nh, N_LAT_SETS, LAT_QT),
        in_specs=[
            pl.BlockSpec((LAT_TQ, w), lambda g, b, t: (first + b * LAT_QT + t, cq + g)),
            pl.BlockSpec((LAT_LEN, w), lambda g, b, t: (kv_row + b, ck + g)),
            pl.BlockSpec((LAT_LEN, w), lambda g, b, t: (kv_row + b, cv + g)),
            pl.BlockSpec((None, None, PAST_LEN, w), lambda g, b, t: (b, layer, 0, g)),
            pl.BlockSpec((None, None, PAST_LEN, w), lambda g, b, t: (b, layer, 0, g)),
            pl.BlockSpec((None, LAT_TQ, NA_LOCAL), lambda g, b, t: (t, 0, 0)),
            pl.BlockSpec((nh, NA_TOE_ROWS, LANES), lambda g, b, t: (g, 0, 0)),
            pl.BlockSpec((LANES, LANES), lambda g, b, t: (0, 0)),
        ],
        out_specs=pl.BlockSpec((LAT_TQ, w), lambda g, b, t: (b * LAT_QT + t, g)),
        out_shape=jax.ShapeDtypeStruct((T_LAT, wide), BF16),
        scratch_shapes=[pltpu.VMEM((nh, N_DR + 1, GRID_W, LANES), F32), pltpu.VMEM((nh, LAT_TQ, NA_LOCAL), F32),
                        pltpu.VMEM((nh, LAT_LEN, LANES), BF16), pltpu.VMEM((nh, LAT_LEN, LANES), BF16),
                        pltpu.VMEM((nh, PAST_LEN, LANES), BF16), pltpu.VMEM((nh, PAST_LEN, LANES), BF16)],
        compiler_params=_params(3),
        name="lat_na_attention",
    )(y, y, y, cache_k.reshape(N_LAT_SETS, DEPTH, PAST_LEN, wide), cache_v.reshape(N_LAT_SETS, DEPTH, PAST_LEN, wide),
      jnp.asarray(_window_mask()), gen, jnp.asarray(_toeplitz_select()))


MERGE_TM = 1024
MERGE_SUB = 1024
MERGE_ROUTE_TM = 512
MERGE_ROUTE_SUB = 512


def _merge_kernel(*refs, n_y, route_sets, sub):
    y_refs, refs = refs[:n_y], refs[n_y:]
    (h_ref, x_ref, gt1_ref, sh2_ref, sc2_ref, nf_ref, wab, wbb, wob, wgb, wr_ref), refs = refs[:11], refs[11:]
    if route_sets:
        (rlg_ref, rh_ref), refs = refs[:2], refs[2:]
    (x1_ref, h2_ref, lg_ref), refs = refs[:3], refs[3:]
    if route_sets:
        xg_ref, g_ref, rc_ref, p_s, rt_s = refs

    wr_hi, wr_lo = _split2(wr_ref[...])
    wr_both = jnp.concatenate([wr_hi, wr_lo], axis=0)
    half = N_HEADS_A * HEAD_DIM
    n_sub = x_ref.shape[0] // sub
    for r in range(n_sub):
        rows = slice(r * sub, (r + 1) * sub)
        if n_y == 1:
            ya, yb = y_refs[0][rows, :half], y_refs[0][rows, half:]
        else:
            ya, yb = y_refs[0][rows, :], y_refs[1][rows, :]
        sets = list(range(r * route_sets // n_sub, (r + 1) * route_sets // n_sub))

        def route(k):
            tok = slice(k * CTX_LEN, (k + 1) * CTX_LEN)
            slots = slice(k * CAP_CTX, (k + 1) * CAP_CTX)
            _route_set(rlg_ref[:, tok], rh_ref[tok, :], xg_ref.at[:, slots, :], g_ref.at[:, slots, :],
                       rc_ref.at[tok, :], p_s.at[k], rt_s.at[k], CTX_LEN, CAP_CTX)

        gates = _bdot(h_ref[rows, :], wgb[...])
        za = _bdot(ya, wab[...])
        zb = _bdot(yb, wbb[...])
        for k in sets[:len(sets) // 2]:
            route(k)
        m = jax.nn.sigmoid(gates[:, :D_MODEL]) * za + jax.nn.sigmoid(gates[:, D_MODEL:]) * zb
        x1 = x_ref[rows, :] + gt1_ref[...] * _bdot(m.astype(BF16), wob[...])
        x1_ref[rows, :] = x1
        h2 = _modnorm(x1, nf_ref[...], sh2_ref[...], sc2_ref[...])
        h2_ref[rows, :] = h2.astype(BF16)
        hh, hl = _split2(h2)
        both = _bdot_nt(wr_both, hh)
        lg_ref[:, rows] = both[:N_EXPERTS, :] + both[N_EXPERTS:, :] + _bdot_nt(wr_hi, hl)
        for k in sets[len(sets) // 2:]:
            route(k)


def _merge(y_parts, h, x, mod, norm_ffn, merge_w, w_router_t, layer, row_fn, tm, sub, route=None):
    n = x.shape[0]
    tiles = n // tm
    once = pl.Buffered(1)
    weight = lambda w: pl.BlockSpec(w.shape, lambda i: (0, 0), pipeline_mode=once)
    tile = lambda w: pl.BlockSpec((tm, w), lambda i: (i, 0))
    in_specs = [tile(p.shape[1]) for p in y_parts] + [
        tile(D_MODEL), tile(D_MODEL),
        _mod_spec(layer, row_fn, MOD_GT1), _mod_spec(layer, row_fn, MOD_SH2), _mod_spec(layer, row_fn, MOD_SC2),
        pl.BlockSpec((None, 1, D_MODEL), lambda i: (layer, 0, 0)),
    ] + [weight(w) for w in merge_w] + [
        pl.BlockSpec((None, N_EXPERTS, D_MODEL), lambda i: (layer, 0, 0), pipeline_mode=once)]
    args = list(y_parts) + [h, x, mod, mod, mod, norm_ffn, *merge_w, w_router_t]
    out_specs = [tile(D_MODEL), tile(D_MODEL), pl.BlockSpec((N_EXPERTS, tm), lambda i: (0, i))]
    out_shape = [jax.ShapeDtypeStruct((n, D_MODEL), F32), jax.ShapeDtypeStruct((n, D_MODEL), BF16),
                 jax.ShapeDtypeStruct((N_EXPERTS, n), F32)]
    scratch = []
    route_sets = 0
    if route is not None:
        route_sets = N_CTX_SETS // tiles
        rows, slots = route_sets * CTX_LEN, route_sets * CAP_CTX
        in_specs += [pl.BlockSpec((N_EXPERTS, rows), lambda i: (0, i)), pl.BlockSpec((rows, D_MODEL), lambda i: (i, 0))]
        args += list(route)
        out_specs += [pl.BlockSpec((N_EXPERTS, slots, D_MODEL), lambda i: (0, i, 0)),
                      pl.BlockSpec((N_EXPERTS, slots, LANES), lambda i: (0, i, 0)),
                      pl.BlockSpec((rows, LANES), lambda i: (i, 0))]
        out_shape += [jax.ShapeDtypeStruct((N_EXPERTS, N_CTX_SETS * CAP_CTX, D_MODEL), BF16),
                      jax.ShapeDtypeStruct((N_EXPERTS, N_CTX_SETS * CAP_CTX, LANES), F32),
                      jax.ShapeDtypeStruct((T_CTX, LANES), F32)]
        scratch += [pltpu.VMEM((route_sets, N_EXPERTS * CAP_CTX, CTX_LEN), BF16),
                    pltpu.VMEM((route_sets, LANES, CTX_LEN), F32)]
    return pl.pallas_call(
        functools.partial(_merge_kernel, n_y=len(y_parts), route_sets=route_sets, sub=sub),
        grid=(tiles,),
        in_specs=in_specs,
        out_specs=out_specs,
        out_shape=out_shape,
        scratch_shapes=scratch,
        compiler_params=_params(1),
        name="merge_route" if route_sets else "merge",
    )(*args)


GATHER_M = 512
RANK_TILE = 128


def _rank_row(aff, a_row, e, n):
    tiles = n // RANK_TILE
    sub = lax.broadcasted_iota(jnp.int32, (RANK_TILE, RANK_TILE), 0)
    lane = lax.broadcasted_iota(jnp.int32, (RANK_TILE, RANK_TILE), 1)
    earlier = jnp.where(sub < lane, 1.0, 0.0)
    acc = [jnp.zeros((8, RANK_TILE), F32) for _ in range(tiles)]
    for c in range(tiles):
        a_col = jnp.broadcast_to(aff[c * RANK_TILE:(c + 1) * RANK_TILE, e:e + 1], (RANK_TILE, RANK_TILE))
        for j in range(tiles):
            a_rj = a_row[:, j * RANK_TILE:(j + 1) * RANK_TILE]
            if c < j:
                beats = jnp.where(a_col >= a_rj, 1.0, 0.0)
            elif c > j:
                beats = jnp.where(a_col > a_rj, 1.0, 0.0)
            else:
                beats = jnp.where(a_col > a_rj, 1.0, jnp.where(a_col == a_rj, earlier, 0.0))
            acc[j] = acc[j] + beats.reshape(RANK_TILE // 8, 8, RANK_TILE).sum(axis=0)
    return jnp.concatenate([a.sum(axis=0, keepdims=True) for a in acc], axis=1)


def _route_set(lg, h, xg_ref, g_ref, rc_ref, p_ref, rt_ref, n, cap):
    ex = jnp.exp(lg - lg.max(axis=0, keepdims=True))
    aff_t = ex / ex.sum(axis=0, keepdims=True)
    aff = jnp.concatenate([aff_t, jnp.zeros((LANES - N_EXPERTS, n), F32)], axis=0).T
    rt_ref[...] = jnp.full((LANES, n), float(cap), F32)
    slot = lax.broadcasted_iota(jnp.int32, (cap, n), 0).astype(F32)
    for e in range(N_EXPERTS):
        rank = _rank_row(aff, aff_t[e:e + 1, :], e, n)
        rt_ref[e:e + 1, :] = jnp.minimum(rank, float(cap))
        p_ref[e * cap:(e + 1) * cap, :] = jnp.where(rank == slot, 1.0, 0.0).astype(BF16)

    a1, a2, a3 = (p.astype(F32) for p in _split3(aff))
    packed = (a1 + pltpu.roll(a2, N_EXPERTS, 1) + pltpu.roll(a3, 2 * N_EXPERTS, 1)).astype(BF16)
    per = GATHER_M // cap
    glane = lax.broadcasted_iota(jnp.int32, (cap, LANES), 1)
    for grp in range(N_EXPERTS * cap // GATHER_M):
        p = p_ref[grp * GATHER_M:(grp + 1) * GATHER_M, :]
        xg = _bdot(p, h).astype(BF16)
        gg = _bdot(p, packed)
        for k in range(per):
            e = grp * per + k
            xg_ref[e] = xg[k * cap:(k + 1) * cap, :]
            mine = (glane < 3 * N_EXPERTS) & ((glane & (N_EXPERTS - 1)) == e)
            ge = jnp.where(mine, gg[k * cap:(k + 1) * cap, :], 0.0).sum(axis=-1, keepdims=True)
            g_ref[e] = jnp.broadcast_to(ge, (cap, LANES))
    rc_ref[...] = rt_ref[...].T


def _route_kernel(lg_ref, h_ref, xg_ref, g_ref, rc_ref, p_ref, rt_ref, *, n, cap):
    _route_set(lg_ref[...], h_ref[...], xg_ref, g_ref, rc_ref, p_ref, rt_ref, n, cap)


def _route(logits, h2, n, cap, n_sets):
    return pl.pallas_call(
        functools.partial(_route_kernel, n=n, cap=cap),
        grid=(n_sets,),
        in_specs=[
            pl.BlockSpec((N_EXPERTS, n), lambda s: (0, s)),
            pl.BlockSpec((n, D_MODEL), lambda s: (s, 0)),
        ],
        out_specs=[
            pl.BlockSpec((N_EXPERTS, cap, D_MODEL), lambda s: (0, s, 0)),
            pl.BlockSpec((N_EXPERTS, cap, LANES), lambda s: (0, s, 0)),
            pl.BlockSpec((n, LANES), lambda s: (s, 0)),
        ],
        out_shape=[jax.ShapeDtypeStruct((N_EXPERTS, n_sets * cap, D_MODEL), BF16),
                   jax.ShapeDtypeStruct((N_EXPERTS, n_sets * cap, LANES), F32),
                   jax.ShapeDtypeStruct((n_sets * n, LANES), F32)],
        scratch_shapes=[pltpu.VMEM((N_EXPERTS * cap, n), BF16), pltpu.VMEM((LANES, n), F32)],
        compiler_params=_params(1),
        name=f"route_n{n}",
    )(logits, h2)


EXPERT_TF = 1024
EXPERT_SUB = 256
N_FF_TILES = EXPERT_FF // EXPERT_TF


def _expert_kernel(xc_ref, xl_ref, gc_ref, gl_ref, wg_ref, wu_ref, wd_ref, o_ref, x_s, acc_s):
    f = pl.program_id(1)
    n_ctx = xc_ref.shape[0]

    @pl.when(f == 0)
    def _():
        x_s[:n_ctx, :] = xc_ref[...]
        x_s[n_ctx:, :] = xl_ref[...]
        acc_s[...] = jnp.zeros_like(acc_s)

    x = x_s[...]
    part = None
    for j in range(EXPERT_TF // EXPERT_SUB):
        cols = slice(j * EXPERT_SUB, (j + 1) * EXPERT_SUB)
        gate = _bdot(x, wg_ref[:, cols].astype(BF16))
        up = _bdot(x, wu_ref[:, cols].astype(BF16))
        hid = (gate * jax.nn.sigmoid(gate)) * up
        down = _bdot(hid.astype(BF16), wd_ref[cols, :].astype(BF16))
        part = down if part is None else part + down
    acc_s[...] += part

    @pl.when(f == N_FF_TILES - 1)
    def _():
        o_ref[:n_ctx, :] = (acc_s[:n_ctx, :] * gc_ref[:, :1]).astype(o_ref.dtype)
        o_ref[n_ctx:, :] = (acc_s[n_ctx:, :] * gl_ref[:, :1]).astype(o_ref.dtype)


def _experts(xg_ctx, xg_lat, g_ctx, g_lat, w_gate, w_up, w_down, layer):
    sc, sl = xg_ctx.shape[1], xg_lat.shape[1]
    slots = lambda s, w: pl.BlockSpec((None, s, w), lambda e, f: (e, 0, 0))
    return pl.pallas_call(
        _expert_kernel,
        grid=(N_EXPERTS, N_FF_TILES),
        in_specs=[
            slots(sc, D_MODEL), slots(sl, D_MODEL), slots(sc, LANES), slots(sl, LANES),
            pl.BlockSpec((None, None, D_MODEL, EXPERT_TF), lambda e, f: (layer, e, 0, f)),
            pl.BlockSpec((None, None, D_MODEL, EXPERT_TF), lambda e, f: (layer, e, 0, f)),
            pl.BlockSpec((None, None, EXPERT_TF, D_MODEL), lambda e, f: (layer, e, f, 0)),
        ],
        out_specs=slots(sc + sl, D_MODEL),
        out_shape=jax.ShapeDtypeStruct((N_EXPERTS, sc + sl, D_MODEL), BF16),
        scratch_shapes=[pltpu.VMEM((sc + sl, D_MODEL), BF16), pltpu.VMEM((sc + sl, D_MODEL), F32)],
        compiler_params=_params(2),
        name="experts",
    )(xg_ctx, xg_lat, g_ctx, g_lat, w_gate, w_up, w_down)


COMB_TM = 256
COMB_STEPS = T_LAT // COMB_TM
COMB_CTX_SETS = N_CTX_SETS // COMB_STEPS


def _combine_group(o_ref, rc_ref, x_ref, gt2_ref, ng_ref, norm_refs, out_refs, cap, sets, final):
    slots = N_EXPERTS * cap
    j = lax.broadcasted_iota(jnp.int32, (LANES, slots), 1)
    e = lax.broadcasted_iota(jnp.int32, (LANES, slots), 0)
    expand = jnp.where(j // cap == e, 1.0, 0.0).astype(BF16)
    slot = (lax.broadcasted_iota(jnp.int32, (1, slots), 1) % cap).astype(F32)
    for k in range(sets):
        rows = slice(k * COMB_TM, (k + 1) * COMB_TM)
        rank = _bdot(rc_ref[rows, :].astype(BF16), expand)
        pt = jnp.where(rank == slot, 1.0, 0.0).astype(BF16)
        ffn = _bdot(pt, o_ref[:, k * cap:(k + 1) * cap, :].reshape(slots, D_MODEL))
        x = x_ref[rows, :] + gt2_ref[...] * ffn
        if final:
            out_refs[0][rows, :] = _rms(x) * ng_ref[...]
        else:
            out_refs[0][rows, :] = x
            out_refs[1][rows, :] = _modnorm(x, ng_ref[...], norm_refs[0][...], norm_refs[1][...]).astype(BF16)


def _combine_kernel(*refs, final):
    n_in, n_out = (4, 1) if final else (6, 2)
    ng_ref = refs[2 * n_in]
    outs = refs[2 * n_in + 1:]
    for g, (cap, sets) in enumerate(((CAP_CTX, COMB_CTX_SETS), (CAP_LAT, 1))):
        o_ref, rc_ref, x_ref, gt2_ref, *norm_refs = refs[g * n_in:(g + 1) * n_in]
        _combine_group(o_ref, rc_ref, x_ref, gt2_ref, ng_ref, norm_refs, outs[g * n_out:(g + 1) * n_out], cap, sets,
                       final)


def _combine(out, rc_ctx, rc_lat, x1_ctx, x1_lat, mod, norm_g, layer, final):
    lat_tiles = LAT_LEN // COMB_TM
    lat_slot0 = N_CTX_SETS * CAP_CTX // CAP_LAT
    groups = (
        (rc_ctx, x1_ctx, lambda i: 0, COMB_CTX_SETS * CTX_LEN,
         pl.BlockSpec((N_EXPERTS, COMB_CTX_SETS * CAP_CTX, D_MODEL), lambda i: (0, i, 0))),
        (rc_lat, x1_lat, lambda i: 1 + i // lat_tiles, COMB_TM,
         pl.BlockSpec((N_EXPERTS, CAP_LAT, D_MODEL), lambda i: (0, lat_slot0 + i // lat_tiles, 0))),
    )
    in_specs, args, out_specs, out_shape = [], [], [], []
    for rc, x1, row_fn, tm, o_spec in groups:
        tile = lambda w, tm=tm: pl.BlockSpec((tm, w), lambda i: (i, 0))
        in_specs += [o_spec, tile(LANES), tile(D_MODEL), _mod_spec(layer, row_fn, MOD_GT2)]
        args += [out, rc, x1, mod]
        if not final:
            in_specs += [_mod_spec(layer + 1, row_fn, MOD_SH1), _mod_spec(layer + 1, row_fn, MOD_SC1)]
            args += [mod, mod]
        out_specs += [tile(D_MODEL)] * (1 if final else 2)
        out_shape += [jax.ShapeDtypeStruct(x1.shape, F32)] + ([] if final else [jax.ShapeDtypeStruct(x1.shape, BF16)])
    if final:
        in_specs.append(pl.BlockSpec((1, D_MODEL), lambda i: (0, 0)))
        args.append(norm_g.reshape(1, D_MODEL))
    else:
        in_specs.append(pl.BlockSpec((None, 1, D_MODEL), lambda i: (layer + 1, 0, 0)))
        args.append(norm_g)
    res = pl.pallas_call(
        functools.partial(_combine_kernel, final=final),
        grid=(COMB_STEPS,),
        in_specs=in_specs,
        out_specs=out_specs,
        out_shape=out_shape,
        compiler_params=_params(1),
        name="combine",
    )(*args)
    half = len(res) // 2
    return tuple(res[:half]), tuple(res[half:])


def kernel(x_prompt, x_sample, cache_attn_k, cache_attn_v, cache_na_k, cache_na_v, c, c_ctx, w_ada, b_ada,
           norm_mix, norm_ffn, w_in, q_norm, k_norm, rpb, w_branch_a, w_branch_b, w_out, w_router, w_gate,
           w_up, w_down, final_norm):
    x_ctx = x_prompt.reshape(T_CTX, D_MODEL)
    x_lat = x_sample.reshape(T_LAT, D_MODEL)
    mod = _modulation(c, c_ctx, w_ada, b_ada)
    norm_mix3 = norm_mix.reshape(DEPTH, 1, D_MODEL)
    norm_ffn3 = norm_ffn.reshape(DEPTH, 1, D_MODEL)
    w_router_t = jnp.swapaxes(w_router, 1, 2)
    ctx_row = lambda *g: 0

    h_ctx = _prenorm(x_ctx, mod, norm_mix3, 0, ctx_row)
    h_lat = _prenorm(x_lat, mod, norm_mix3, 0, _lat_mod_row(NORM_TM))
    layer_caches = []
    for layer in range(DEPTH):
        last = layer == DEPTH - 1
        y, merge_w = _projection(h_ctx, h_lat, w_in, q_norm, k_norm, w_branch_a, w_branch_b, w_out, layer)
        yab_ctx, *new_caches = _ctx_attention(y, layer_caches if last else [])
        layer_caches.append(tuple(new_caches))
        ya_lat = _lat_gqa_attention(y, cache_attn_k, cache_attn_v, layer)
        yb_lat = _lat_na_attention(y, cache_na_k, cache_na_v, rpb, layer)
        merge_args = (mod, norm_ffn3, merge_w, w_router_t, layer)
        x1_ctx, h2_ctx, lg_ctx = _merge([yab_ctx], h_ctx, x_ctx, *merge_args, ctx_row, MERGE_TM, MERGE_SUB)
        x1_lat, h2_lat, lg_lat, xg_ctx, g_ctx, rc_ctx = _merge(
            [ya_lat, yb_lat], h_lat, x_lat, *merge_args, _lat_mod_row(MERGE_ROUTE_TM), MERGE_ROUTE_TM,
            MERGE_ROUTE_SUB, route=(lg_ctx, h2_ctx))
        xg_lat, g_lat, rc_lat = _route(lg_lat, h2_lat, LAT_LEN, CAP_LAT, N_LAT_SETS)
        out = _experts(xg_ctx, xg_lat, g_ctx, g_lat, w_gate, w_up, w_down, layer)
        final = layer == DEPTH - 1
        norm_g = final_norm if final else norm_mix3
        res_ctx, res_lat = _combine(out, rc_ctx, rc_lat, x1_ctx, x1_lat, mod, norm_g, layer, final)
        if final:
            (y_ctx,), (y_lat,) = res_ctx, res_lat
        else:
            (x_ctx, h_ctx), (x_lat, h_lat) = res_ctx, res_lat

    heads = (N_KV_A, N_KV_A, N_HEADS_B, N_HEADS_B)
    new_caches = [a.reshape(N_CTX_SETS, DEPTH, CTX_LEN, h, HEAD_DIM) for a, h in zip(layer_caches[-1], heads)]
    return (y_ctx.reshape(N_CTX_SETS, CTX_LEN, D_MODEL), y_lat.reshape(N_LAT_SETS, LAT_LEN, D_MODEL), *new_caches)
```

```python
import functools

import numpy as np
import jax
import jax.numpy as jnp
from jax import lax
from jax.experimental import pallas as pl
from jax.experimental.pallas import tpu as pltpu

F32 = jnp.float32
BF16 = jnp.bfloat16

D_MODEL = 1024
N_CTX_SETS, CTX_LEN = 16, 256
N_LAT_SETS, LAT_LEN = 2, 1024
T_CTX = N_CTX_SETS * CTX_LEN
T_LAT = N_LAT_SETS * LAT_LEN
T_ALL = T_CTX + T_LAT
DEPTH = 2
PAST_LEN = 512
GRID_W = 64
GRID_ROWS = LAT_LEN // GRID_W
HEAD_DIM = 64
N_HEADS_A, N_KV_A, N_HEADS_B = 8, 2, 8
WIN_ROWS, WIN_COLS = 8, 16
N_EXPERTS = 16
EXPERT_FF = 2048
CAP_CTX = 2 * CTX_LEN // N_EXPERTS
CAP_LAT = 2 * LAT_LEN // N_EXPERTS
ROPE_THETA = 10000.0
EPS = 1e-6
NEG = -1e30
QKV_DIM = 2304
ATT_SCALE = HEAD_DIM ** -0.5

LANES = 128
VMEM_LIMIT = 56 * 1024 * 1024

PROJ_TN = 256
N_PROJ_TILES = QKV_DIM // PROJ_TN
PROJ_ORDER = np.array([0, 1, 3, 4, 5, 6, 7, 8, 2], np.int32)
N_Q_TILES = 4
CAST_STEPS = 8
MOD_SH1, MOD_SC1, MOD_GT1, MOD_SH2, MOD_SC2, MOD_GT2 = range(6)


def _params(n_grid_dims, vmem=VMEM_LIMIT):
    return pltpu.CompilerParams(dimension_semantics=("arbitrary",) * n_grid_dims, vmem_limit_bytes=vmem)


def _bdot(a, b):
    return jnp.dot(a, b, preferred_element_type=F32)


def _bdot_nt(a, b):
    return lax.dot_general(a, b, (((1,), (1,)), ((), ())), preferred_element_type=F32)


def _split2(x):
    hi = x.astype(BF16)
    lo = (x - hi.astype(F32)).astype(BF16)
    return hi, lo


def _split3(x):
    hi = x.astype(BF16)
    r = x - hi.astype(F32)
    mid = r.astype(BF16)
    lo = (r - mid.astype(F32)).astype(BF16)
    return hi, mid, lo


def _rms(x):
    return x * lax.rsqrt(jnp.mean(x * x, axis=-1, keepdims=True) + EPS)


def _modnorm(x, g, sh, sc):
    return (_rms(x) * g) * (1.0 + sc) + sh


def _lat_mod_row(tile_rows):
    return lambda i: 1 + (i * tile_rows) // LAT_LEN


def _mod_spec(layer, row_fn, chunk):
    return pl.BlockSpec((None, None, 1, D_MODEL), lambda *g: (layer, row_fn(*g), 0, chunk))


MOD_TN = 1536


def _mod_kernel(ct_ref, w_ref, b_ref, o_ref):
    ct = ct_ref[...]
    act = ct * jax.nn.sigmoid(ct)
    w = w_ref[...]
    for m in range(3):
        o_ref[m:m + 1, :] = jnp.sum(w * act[:, m:m + 1], axis=0, keepdims=True) + b_ref[...]
    o_ref[3:8, :] = jnp.zeros((5, MOD_TN), F32)


def _modulation(c, c_ctx, w_ada, b_ada):
    cond = jnp.concatenate([c_ctx[None, :], c, jnp.zeros((5, D_MODEL), F32)], axis=0)
    mod = pl.pallas_call(
        _mod_kernel,
        grid=(DEPTH, 6 * D_MODEL // MOD_TN),
        in_specs=[
            pl.BlockSpec((D_MODEL, 8), lambda l, j: (0, 0)),
            pl.BlockSpec((None, D_MODEL, MOD_TN), lambda l, j: (l, 0, j)),
            pl.BlockSpec((None, 1, MOD_TN), lambda l, j: (l, 0, j)),
        ],
        out_specs=pl.BlockSpec((None, 8, MOD_TN), lambda l, j: (l, 0, j)),
        out_shape=jax.ShapeDtypeStruct((DEPTH, 8, 6 * D_MODEL), F32),
        compiler_params=_params(2),
        name="modulation",
    )(cond.T, w_ada, b_ada.reshape(DEPTH, 1, 6 * D_MODEL))
    return mod.reshape(DEPTH, 8, 1, 6 * D_MODEL)


NORM_TM = 512


def _prenorm_kernel(x_ref, g_ref, sh_ref, sc_ref, h_ref):
    h_ref[...] = _modnorm(x_ref[...], g_ref[...], sh_ref[...], sc_ref[...]).astype(BF16)


def _prenorm(x, mod, norm_g, layer, row_fn):
    n = x.shape[0]
    return pl.pallas_call(
        _prenorm_kernel,
        grid=(n // NORM_TM,),
        in_specs=[
            pl.BlockSpec((NORM_TM, D_MODEL), lambda i: (i, 0)),
            pl.BlockSpec((None, 1, D_MODEL), lambda i: (layer, 0, 0)),
            _mod_spec(layer, row_fn, MOD_SH1),
            _mod_spec(layer, row_fn, MOD_SC1),
        ],
        out_specs=pl.BlockSpec((NORM_TM, D_MODEL), lambda i: (i, 0)),
        out_shape=jax.ShapeDtypeStruct((n, D_MODEL), BF16),
        compiler_params=_params(1),
        name="prenorm",
    )(x, norm_g, mod, mod)


PROJ_CH = 1024


def _rope_tables():
    t = np.arange(LAT_LEN)
    lane = np.arange(LANES) % HEAD_DIM
    pos = np.where(lane < HEAD_DIM // 2, (t // GRID_W)[:, None], (t % GRID_W)[:, None]).astype(np.float64)
    freq = ROPE_THETA ** (-(lane % 16).astype(np.float64) / 16.0)
    ang = pos * freq[None, :]
    sign = np.where((lane & 16) == 0, -1.0, 1.0)[None, :]
    return np.cos(ang).astype(np.float32), (np.sin(ang) * sign).astype(np.float32)


def _head_norm_rope(y, gain, cos, sin):
    w = y.shape[1]
    r = (lax.broadcasted_iota(jnp.int32, (2 * w, w), 0) % w) // HEAD_DIM
    c = lax.broadcasted_iota(jnp.int32, (2 * w, w), 1) // HEAD_DIM
    seg = jnp.where(r == c, 1.0 / HEAD_DIM, 0.0).astype(BF16)
    ms = _bdot(jnp.concatenate(_split2(y * y), axis=1), seg)
    yn = y * lax.rsqrt(ms + EPS) * gain
    if cos is None:
        return yn
    lane = lax.broadcasted_iota(jnp.int32, yn.shape, 1)
    partner = jnp.where((lane & 16) == 0, pltpu.roll(yn, w - 16, 1), pltpu.roll(yn, 16, 1))
    if w > LANES:
        cos = jnp.concatenate([cos] * (w // LANES), axis=1)
        sin = jnp.concatenate([sin] * (w // LANES), axis=1)
    return yn * cos + partner * sin


def _proj_kernel(order_ref, hc_ref, hl_ref, w_ref, gain_ref, cos_ref, sin_ref, wa_ref, wbr_ref, wo_ref, wg_ref,
                 yq_ref, ykv_ref, oa_ref, ob_ref, oo_ref, og_ref, wb_ref, acc_ref):
    del order_ref
    s = pl.program_id(0)
    wb_ref[...] = w_ref[...].astype(BF16)
    oa_ref[...] = wa_ref[...].astype(BF16)
    ob_ref[...] = wbr_ref[...].astype(BF16)
    oo_ref[...] = wo_ref[...].astype(BF16)
    og_ref[...] = wg_ref[0].astype(BF16)
    chunks = [(hc_ref, k * PROJ_CH, k * PROJ_CH, False) for k in range(T_CTX // PROJ_CH)]
    chunks += [(hl_ref, k * PROJ_CH, T_CTX + k * PROJ_CH, True) for k in range(T_LAT // PROJ_CH)]

    def matmul(chunk, dst):
        h_ref, r0, o0, _ = chunk
        dst[o0:o0 + PROJ_CH, :] = _bdot(h_ref[r0:r0 + PROJ_CH, :], wb_ref[...]).astype(dst.dtype)

    def finish(chunk, width, src, dst):
        _, _, o0, is_lat = chunk
        cos, sin = (cos_ref[...], sin_ref[...]) if is_lat else (None, None)
        y = _head_norm_rope(src[o0:o0 + PROJ_CH, :width], gain_ref[:, :width], cos, sin)
        dst[o0:o0 + PROJ_CH, :width] = y.astype(dst.dtype)

    def tile(width, work, dst):
        matmul(chunks[0], work)
        for k in range(1, len(chunks)):
            matmul(chunks[k], work)
            if width:
                finish(chunks[k - 1], width, work, dst)
        if width:
            finish(chunks[-1], width, work, dst)

    pl.when(s < 2)(lambda: tile(PROJ_TN, acc_ref, yq_ref))
    pl.when((s >= 2) & (s < N_Q_TILES))(lambda: tile(0, yq_ref, yq_ref))
    pl.when((s >= N_Q_TILES) & (s < N_PROJ_TILES - 1))(lambda: tile(0, ykv_ref, ykv_ref))
    pl.when(s == N_PROJ_TILES - 1)(lambda: tile(LANES, ykv_ref, ykv_ref))


def _projection(h_ctx, h_lat, w_in, q_norm, k_norm, w_ba, w_bb, w_out, layer):
    ones = jnp.ones((2 * HEAD_DIM,), F32)
    gain = jnp.stack([jnp.tile(q_norm[layer], 4), jnp.tile(q_norm[layer], 4),
                      jnp.concatenate([jnp.tile(k_norm[layer], 2), ones])])[:, None, :]
    cos, sin = _rope_tables()
    half = N_HEADS_A * HEAD_DIM
    ra, ro = half // CAST_STEPS, D_MODEL // CAST_STEPS
    part = lambda j: jnp.minimum(j, CAST_STEPS - 1)
    gain_row = lambda j: jnp.where(j == N_PROJ_TILES - 1, 2, jnp.minimum(j, 1))
    grid_spec = pltpu.PrefetchScalarGridSpec(
        num_scalar_prefetch=1,
        grid=(N_PROJ_TILES,),
        in_specs=[
            pl.BlockSpec((T_CTX, D_MODEL), lambda j, p: (0, 0)),
            pl.BlockSpec((T_LAT, D_MODEL), lambda j, p: (0, 0)),
            pl.BlockSpec((None, D_MODEL, PROJ_TN), lambda j, p: (layer, 0, p[j])),
            pl.BlockSpec((None, 1, PROJ_TN), lambda j, p: (gain_row(j), 0, 0)),
            pl.BlockSpec((LAT_LEN, LANES), lambda j, p: (0, 0)),
            pl.BlockSpec((LAT_LEN, LANES), lambda j, p: (0, 0)),
            pl.BlockSpec((None, ra, D_MODEL), lambda j, p: (layer, part(j), 0)),
            pl.BlockSpec((None, ra, D_MODEL), lambda j, p: (layer, part(j), 0)),
            pl.BlockSpec((None, ro, D_MODEL), lambda j, p: (layer, part(j), 0)),
            pl.BlockSpec((pl.Element(1), pl.Element(ro), pl.Element(2 * D_MODEL)),
                         lambda j, p: (layer, part(j) * ro, QKV_DIM)),
        ],
        out_specs=[
            pl.BlockSpec((T_ALL, PROJ_TN), lambda j, p: (0, jnp.minimum(j, N_Q_TILES - 1))),
            pl.BlockSpec((T_ALL, PROJ_TN), lambda j, p: (0, jnp.maximum(j - N_Q_TILES, 0))),
            pl.BlockSpec((ra, D_MODEL), lambda j, p: (part(j), 0)),
            pl.BlockSpec((ra, D_MODEL), lambda j, p: (part(j), 0)),
            pl.BlockSpec((ro, D_MODEL), lambda j, p: (part(j), 0)),
            pl.BlockSpec((ro, 2 * D_MODEL), lambda j, p: (part(j), 0)),
        ],
        scratch_shapes=[pltpu.VMEM((D_MODEL, PROJ_TN), BF16), pltpu.VMEM((T_ALL, PROJ_TN), F32)],
    )
    yq, ykv, *merge_w = pl.pallas_call(
        _proj_kernel,
        grid_spec=grid_spec,
        out_shape=[jax.ShapeDtypeStruct((T_ALL, N_Q_TILES * PROJ_TN), BF16),
                   jax.ShapeDtypeStruct((T_ALL, QKV_DIM - N_Q_TILES * PROJ_TN), F32),
                   jax.ShapeDtypeStruct((half, D_MODEL), BF16), jax.ShapeDtypeStruct((half, D_MODEL), BF16),
                   jax.ShapeDtypeStruct((D_MODEL, D_MODEL), BF16), jax.ShapeDtypeStruct((D_MODEL, 2 * D_MODEL), BF16)],
        compiler_params=_params(1),
        name="projection",
    )(jnp.asarray(PROJ_ORDER), h_ctx, h_lat, w_in, gain, jnp.asarray(cos), jnp.asarray(sin), w_ba, w_bb, w_out, w_in)
    return yq, ykv, merge_w


COL_QA, COL_QB = 0, 512
COL_KB, COL_VB, COL_KA = 0, 512, 1024


def _lane_is_low(shape):
    return lax.broadcasted_iota(jnp.int32, shape, 1) < HEAD_DIM


def _pair_halves(x):
    low = _lane_is_low(x.shape)
    xb = x.astype(BF16)
    zero = jnp.zeros_like(xb)
    return jnp.where(low, xb, zero), jnp.where(low, zero, xb)


def _scaled_q(q):
    assert ATT_SCALE == 0.125
    return (q * ATT_SCALE).astype(BF16)


def _attend_pair(q, keys, values, biases, joint_pv=True):
    probs, dens, out = [], [], None
    for h in range(2):
        scores = []
        for k, b in zip(keys[h], biases[h]):
            s = _bdot_nt(q, k)
            scores.append(s if b is None else s + b)
        m = scores[0].max(axis=-1, keepdims=True)
        for s in scores[1:]:
            m = jnp.maximum(m, s.max(axis=-1, keepdims=True))
        den, num = None, None
        for s, v in zip(scores, values[h]):
            e = jnp.exp(s - m)
            d = e.sum(axis=-1, keepdims=True)
            den = d if den is None else den + d
            if joint_pv:
                probs.append(e.astype(BF16))
            else:
                o = _bdot(e.astype(BF16), v)
                num = o if num is None else num + o
        dens.append(den)
        if not joint_pv:
            out = num / den if out is None else out + num / den
    if not joint_pv:
        return out
    num = _bdot(jnp.concatenate(probs, axis=1), jnp.concatenate(values[0] + values[1], axis=0))
    return num / jnp.where(_lane_is_low(num.shape), dens[0], dens[1])


def _attend_pair_t(q, keys, values_t, biases_t):
    probs, dens = [], []
    for h in range(2):
        scores = []
        for k, b in zip(keys[h], biases_t[h]):
            s = _bdot_nt(k, q)
            scores.append(s if b is None else s + b)
        m = scores[0].max(axis=0, keepdims=True)
        for s in scores[1:]:
            m = jnp.maximum(m, s.max(axis=0, keepdims=True))
        den = None
        for s in scores:
            e = jnp.exp(s - m)
            d = e.sum(axis=0, keepdims=True)
            den = d if den is None else den + d
            probs.append(e.astype(BF16))
        dens.append(den)
    num_t = _bdot(values_t, jnp.concatenate(probs, axis=0))
    row = lax.broadcasted_iota(jnp.int32, num_t.shape, 0)
    return (num_t / jnp.where(row < HEAD_DIM, dens[0], dens[1])).T


def _gqa_variants(x):
    lo, hi = _pair_halves(x)
    sw_lo, sw_hi = _pair_halves(pltpu.roll(x, HEAD_DIM, 1))
    return [lo, sw_hi, sw_lo, hi]


def _gqa_attention(q_ref, kvar, vvar, o_ref, joint_pv):
    for pair in range(N_HEADS_A // 2):
        q = _scaled_q(q_ref[:, pair * LANES:(pair + 1) * LANES])
        kvh = (2 * pair) // (N_HEADS_A // N_KV_A)
        out = _attend_pair(q, [[kvar[2 * kvh]], [kvar[2 * kvh + 1]]], [[vvar[2 * kvh]], [vvar[2 * kvh + 1]]],
                           [[None], [None]], joint_pv)
        o_ref[:, pair * LANES:(pair + 1) * LANES] = out.astype(o_ref.dtype)


def _ctx_attn_kernel(qa_ref, qb_ref, kb_ref, vb_ref, kava_ref, *refs, n_prev):
    prev = refs[:4 * n_prev]
    y_ref, nak_ref, nav_ref, nbk_ref, nbv_ref = refs[4 * n_prev:]
    ka = kava_ref[:, :LANES]
    va = kava_ref[:, LANES:]
    new = (ka, va, kb_ref[...], vb_ref[...])
    for c, (o_ref, val) in enumerate(zip((nak_ref, nav_ref, nbk_ref, nbv_ref), new)):
        if n_prev:
            for p in range(n_prev):
                o_ref[p] = prev[4 * p + c][...]
            o_ref[n_prev] = val
        else:
            o_ref[...] = val
    _gqa_attention(qa_ref, _gqa_variants(ka), _gqa_variants(va), y_ref.at[:, :N_HEADS_A * HEAD_DIM], joint_pv=False)
    for pair in range(N_HEADS_B // 2):
        cols = slice(pair * LANES, (pair + 1) * LANES)
        q = _scaled_q(qb_ref[:, cols])
        k_lo, k_hi = _pair_halves(kb_ref[:, cols])
        v_lo, v_hi = _pair_halves(vb_ref[:, cols])
        out = _attend_pair(q, [[k_lo], [k_hi]], [[v_lo], [v_hi]], [[None], [None]], joint_pv=False)
        y_ref[:, N_HEADS_A * HEAD_DIM + pair * LANES:N_HEADS_A * HEAD_DIM + (pair + 1) * LANES] = out.astype(BF16)


def _ctx_attention(yq, ykv, prev_caches):
    wide = 4 * LANES
    widths = (LANES, LANES, wide, wide)
    n_prev = len(prev_caches)
    row = lambda b: (b, 0)
    f32 = lambda *s: jax.ShapeDtypeStruct(s, F32)
    in_specs = [
        pl.BlockSpec((CTX_LEN, wide), lambda b: (b, COL_QA // wide)),
        pl.BlockSpec((CTX_LEN, wide), lambda b: (b, COL_QB // wide)),
        pl.BlockSpec((CTX_LEN, wide), lambda b: (b, COL_KB // wide)),
        pl.BlockSpec((CTX_LEN, wide), lambda b: (b, COL_VB // wide)),
        pl.BlockSpec((CTX_LEN, 2 * LANES), lambda b: (b, COL_KA // (2 * LANES))),
    ]
    args = [yq, yq, ykv, ykv, ykv]
    for layer_caches in prev_caches:
        in_specs += [pl.BlockSpec((CTX_LEN, w), row) for w in widths]
        args += list(layer_caches)
    if n_prev:
        cache_specs = [pl.BlockSpec((None, n_prev + 1, CTX_LEN, w), lambda b: (b, 0, 0, 0)) for w in widths]
        cache_shapes = [f32(N_CTX_SETS, n_prev + 1, CTX_LEN, w) for w in widths]
    else:
        cache_specs = [pl.BlockSpec((CTX_LEN, w), row) for w in widths]
        cache_shapes = [f32(T_CTX, w) for w in widths]
    return pl.pallas_call(
        functools.partial(_ctx_attn_kernel, n_prev=n_prev),
        grid=(N_CTX_SETS,),
        in_specs=in_specs,
        out_specs=[pl.BlockSpec((CTX_LEN, 2 * wide), row)] + cache_specs,
        out_shape=[jax.ShapeDtypeStruct((T_CTX, 2 * wide), BF16)] + cache_shapes,
        compiler_params=_params(1),
        name="ctx_attention",
    )(*args)


LAT_TQ = 512
LAT_QT = LAT_LEN // LAT_TQ
GQA_TQ = 512


def _lat_gqa_kernel(qa_ref, kava_ref, ck_ref, cv_ref, o_ref, k_s, vt_s):
    n_keys = PAST_LEN + LAT_LEN

    @pl.when(pl.program_id(1) == 0)
    def _():
        for i, (c, x) in enumerate(zip(_gqa_variants(ck_ref[...]), _gqa_variants(kava_ref[:, :LANES]))):
            k_s[i, :PAST_LEN, :] = c
            k_s[i, PAST_LEN:, :] = x
        for src, cols in ((cv_ref[...], slice(0, PAST_LEN)), (kava_ref[:, LANES:], slice(PAST_LEN, n_keys))):
            vt = src.T
            swapped = pltpu.roll(vt, HEAD_DIM, 0)
            top = lax.broadcasted_iota(jnp.int32, vt.shape, 0) < HEAD_DIM
            zero = jnp.zeros_like(vt)
            per_head = ((jnp.where(top, vt, zero), jnp.where(top, zero, swapped)),
                        (jnp.where(top, swapped, zero), jnp.where(top, zero, vt)))
            for kvh in range(N_KV_A):
                for half in range(2):
                    off = half * n_keys
                    vt_s[kvh, :, off + cols.start:off + cols.stop] = per_head[kvh][half].astype(BF16)

    for pair in range(N_HEADS_A // 2):
        q = _scaled_q(qa_ref[:, pair * LANES:(pair + 1) * LANES])
        kvh = (2 * pair) // (N_HEADS_A // N_KV_A)
        out = _attend_pair_t(q, [[k_s[2 * kvh]], [k_s[2 * kvh + 1]]], vt_s[kvh], [[None], [None]])
        o_ref[:, pair * LANES:(pair + 1) * LANES] = out.astype(o_ref.dtype)


def _lat_gqa_attention(yq, ykv, cache_k, cache_v, layer):
    wide = 4 * LANES
    tiles = LAT_LEN // GQA_TQ
    first = T_CTX // GQA_TQ
    cache = pl.BlockSpec((None, None, PAST_LEN, LANES), lambda b, t: (b, layer, 0, 0))
    return pl.pallas_call(
        _lat_gqa_kernel,
        grid=(N_LAT_SETS, tiles),
        in_specs=[
            pl.BlockSpec((GQA_TQ, wide), lambda b, t: (first + b * tiles + t, 0)),
            pl.BlockSpec((LAT_LEN, 2 * LANES), lambda b, t: (T_CTX // LAT_LEN + b, COL_KA // (2 * LANES))),
            cache, cache,
        ],
        out_specs=pl.BlockSpec((GQA_TQ, wide), lambda b, t: (b * tiles + t, 0)),
        out_shape=jax.ShapeDtypeStruct((T_LAT, wide), BF16),
        scratch_shapes=[pltpu.VMEM((4, PAST_LEN + LAT_LEN, LANES), BF16),
                        pltpu.VMEM((N_KV_A, LANES, 2 * (PAST_LEN + LAT_LEN)), BF16)],
        compiler_params=_params(2),
        name="lat_gqa_attention",
    )(yq, ykv, cache_k.reshape(N_LAT_SETS, DEPTH, PAST_LEN, LANES), cache_v.reshape(N_LAT_SETS, DEPTH, PAST_LEN, LANES))


N_DR = 2 * WIN_ROWS - 1
N_DC = 2 * WIN_COLS - 1
ROWS_PER_TQ = LAT_TQ // GRID_W
NA_LOCAL = 768
NA_KEY_STEP = 2 * LANES
NA_LAST_BASE = (LAT_LEN - NA_LOCAL) // NA_KEY_STEP
NA_TOE_ROWS = 32
NA_PAIRS = 4


def _na_key_base(qt):
    return jnp.where(qt < LAT_QT // 2, 0, NA_LAST_BASE)


def _window_mask():
    r = np.arange(GRID_ROWS)
    row_start = np.clip(r - WIN_ROWS // 2, 0, GRID_ROWS - WIN_ROWS)
    in_rows = (r[None, :] >= row_start[:, None]) & (r[None, :] < row_start[:, None] + WIN_ROWS)
    cq = np.arange(GRID_W)
    col_start = np.clip(cq - WIN_COLS // 2, 0, GRID_W - WIN_COLS)
    in_cols = (cq[None, :] >= col_start[:, None]) & (cq[None, :] < col_start[:, None] + WIN_COLS)
    valid = (in_rows[:, None, :, None] & in_cols[None, :, None, :]).reshape(LAT_LEN, LAT_LEN)
    tiles = []
    for qt in range(LAT_QT):
        base = (0 if qt < LAT_QT // 2 else NA_LAST_BASE) * NA_KEY_STEP
        tile = valid[qt * LAT_TQ:(qt + 1) * LAT_TQ]
        assert not tile[:, :base].any() and not tile[:, base + NA_LOCAL:].any()
        tiles.append(tile[:, base:base + NA_LOCAL])
    return np.where(np.stack(tiles), 0.0, NEG).astype(np.float32)


def _toeplitz_select():
    j = np.arange(LANES)
    c = np.clip(j - (GRID_W - 1), -(WIN_COLS - 1), WIN_COLS - 1) + (WIN_COLS - 1)
    return (np.arange(LANES)[:, None] == c[None, :]).astype(np.float32)


def _na_kernel(q_ref, k_ref, v_ref, ck_ref, cv_ref, mask_ref, rpb_ref, sel_ref, o_ref,
               toe_ref, bias_ref, k_s, v_s, ck_s, cv_s):
    b, qt = pl.program_id(1), pl.program_id(2)
    n_heads = 2 * NA_PAIRS

    @pl.when((qt == 0) & (b == 0))
    def _():
        low = lax.broadcasted_iota(jnp.int32, (GRID_W, LANES), 1) < GRID_W
        sel = sel_ref[...].astype(BF16)
        for hd in range(n_heads):
            gen = sum(_bdot(piece, sel) for piece in _split3(rpb_ref[hd]))
            for dd in range(N_DR + 1):
                lo = jnp.broadcast_to(gen[dd:dd + 1, :], (GRID_W, LANES))
                hi = jnp.broadcast_to(gen[dd + 1:dd + 2, :], (GRID_W, LANES))
                lo = pltpu.roll(lo, LANES - (GRID_W - 1), 1, stride=1, stride_axis=0)
                hi = pltpu.roll(hi, 1, 1, stride=1, stride_axis=0)
                toe_ref[hd, dd] = jnp.where(low, lo, hi)

    @pl.when(qt == 0)
    def _():
        for src, dst in ((k_ref, k_s), (v_ref, v_s), (ck_ref, ck_s), (cv_ref, cv_s)):
            for pair in range(NA_PAIRS):
                lo, hi = _pair_halves(src[:, pair * LANES:(pair + 1) * LANES])
                dst[2 * pair] = lo
                dst[2 * pair + 1] = hi

    base = _na_key_base(qt)
    local = pl.ds(pl.multiple_of(base * NA_KEY_STEP, NA_KEY_STEP), NA_LOCAL)
    for pair in range(NA_PAIRS):
        q = _scaled_q(q_ref[:, pair * LANES:(pair + 1) * LANES])
        heads = (2 * pair, 2 * pair + 1)
        biases = []
        for hd in heads:
            for rr in range(ROWS_PER_TQ):
                for kp in range(NA_LOCAL // LANES):
                    d = 2 * (kp + base * (NA_KEY_STEP // LANES)) - (qt * ROWS_PER_TQ + rr) + (WIN_ROWS - 1)
                    dd = jnp.clip(d, -1, N_DR - 1) + 1
                    bias_ref[hd, rr * GRID_W:(rr + 1) * GRID_W, kp * LANES:(kp + 1) * LANES] = toe_ref[hd, dd]
            biases.append([bias_ref[hd] + mask_ref[...], None])
        out = _attend_pair(q, [[k_s[hd, local, :], ck_s[hd]] for hd in heads],
                           [[v_s[hd, local, :], cv_s[hd]] for hd in heads], biases)
        o_ref[:, pair * LANES:(pair + 1) * LANES] = out.astype(o_ref.dtype)


def _lat_na_attention(yq, ykv, cache_k, cache_v, rpb, layer):
    first = T_CTX // LAT_TQ
    kv_row = T_CTX // LAT_LEN
    wide = N_HEADS_B * HEAD_DIM
    gen = jnp.pad(rpb[layer], ((0, 0), (1, NA_TOE_ROWS - N_DR - 1), (0, LANES - N_DC)))
    w = NA_PAIRS * LANES
    nh = 2 * NA_PAIRS
    cq, ck, cv = COL_QB // w, COL_KB // w, COL_VB // w
    return pl.pallas_call(
        _na_kernel,
        grid=(N_HEADS_B // nh, N_LAT_SETS, LAT_QT),
        in_specs=[
            pl.BlockSpec((LAT_TQ, w), lambda g, b, t: (first + b * LAT_QT + t, cq + g)),
            pl.BlockSpec((LAT_LEN, w), lambda g, b, t: (kv_row + b, ck + g)),
            pl.BlockSpec((LAT_LEN, w), lambda g, b, t: (kv_row + b, cv + g)),
            pl.BlockSpec((None, None, PAST_LEN, w), lambda g, b, t: (b, layer, 0, g)),
            pl.BlockSpec((None, None, PAST_LEN, w), lambda g, b, t: (b, layer, 0, g)),
            pl.BlockSpec((None, LAT_TQ, NA_LOCAL), lambda g, b, t: (t, 0, 0)),
            pl.BlockSpec((nh, NA_TOE_ROWS, LANES), lambda g, b, t: (g, 0, 0)),
            pl.BlockSpec((LANES, LANES), lambda g, b, t: (0, 0)),
        ],
        out_specs=pl.BlockSpec((LAT_TQ, w), lambda g, b, t: (b * LAT_QT + t, g)),
        out_shape=jax.ShapeDtypeStruct((T_LAT, wide), BF16),
        scratch_shapes=[pltpu.VMEM((nh, N_DR + 1, GRID_W, LANES), F32), pltpu.VMEM((nh, LAT_TQ, NA_LOCAL), F32),
                        pltpu.VMEM((nh, LAT_LEN, LANES), BF16), pltpu.VMEM((nh, LAT_LEN, LANES), BF16),
                        pltpu.VMEM((nh, PAST_LEN, LANES), BF16), pltpu.VMEM((nh, PAST_LEN, LANES), BF16)],
        compiler_params=_params(3),
        name="lat_na_attention",
    )(yq, ykv, ykv, cache_k.reshape(N_LAT_SETS, DEPTH, PAST_LEN, wide), cache_v.reshape(N_LAT_SETS, DEPTH, PAST_LEN, wide),
      jnp.asarray(_window_mask()), gen, jnp.asarray(_toeplitz_select()))


MERGE_TM = 1024
MERGE_SUB = 1024
MERGE_ROUTE_TM = 512
MERGE_ROUTE_SUB = 512


def _merge_kernel(*refs, n_y, route_sets, sub):
    y_refs, refs = refs[:n_y], refs[n_y:]
    (h_ref, x_ref, gt1_ref, sh2_ref, sc2_ref, nf_ref, wab, wbb, wob, wgb, wr_ref), refs = refs[:11], refs[11:]
    if route_sets:
        (rlg_ref, rh_ref), refs = refs[:2], refs[2:]
    (x1_ref, h2_ref, lg_ref), refs = refs[:3], refs[3:]
    if route_sets:
        xg_ref, g_ref, rc_ref, p_s, rt_s = refs

    wr_hi, wr_lo = _split2(wr_ref[...])
    wr_both = jnp.concatenate([wr_hi, wr_lo], axis=0)
    half = N_HEADS_A * HEAD_DIM
    n_sub = x_ref.shape[0] // sub
    for r in range(n_sub):
        rows = slice(r * sub, (r + 1) * sub)
        if n_y == 1:
            ya, yb = y_refs[0][rows, :half], y_refs[0][rows, half:]
        else:
            ya, yb = y_refs[0][rows, :], y_refs[1][rows, :]
        sets = list(range(r * route_sets // n_sub, (r + 1) * route_sets // n_sub))

        def route(k):
            tok = slice(k * CTX_LEN, (k + 1) * CTX_LEN)
            slots = slice(k * CAP_CTX, (k + 1) * CAP_CTX)
            _route_set(rlg_ref[:, tok], rh_ref[tok, :], xg_ref.at[:, slots, :], g_ref.at[:, slots, :],
                       rc_ref.at[tok, :], p_s.at[k], rt_s.at[k], CTX_LEN, CAP_CTX)

        gates = _bdot(h_ref[rows, :], wgb[...])
        za = _bdot(ya, wab[...])
        zb = _bdot(yb, wbb[...])
        for k in sets[:len(sets) // 2]:
            route(k)
        m = jax.nn.sigmoid(gates[:, :D_MODEL]) * za + jax.nn.sigmoid(gates[:, D_MODEL:]) * zb
        x1 = x_ref[rows, :] + gt1_ref[...] * _bdot(m.astype(BF16), wob[...])
        x1_ref[rows, :] = x1
        h2 = _modnorm(x1, nf_ref[...], sh2_ref[...], sc2_ref[...])
        h2_ref[rows, :] = h2.astype(BF16)
        hh, hl = _split2(h2)
        both = _bdot_nt(wr_both, hh)
        lg_ref[:, rows] = both[:N_EXPERTS, :] + both[N_EXPERTS:, :] + _bdot_nt(wr_hi, hl)
        for k in sets[len(sets) // 2:]:
            route(k)


def _merge(y_parts, h, x, mod, norm_ffn, merge_w, w_router_t, layer, row_fn, tm, sub, route=None):
    n = x.shape[0]
    tiles = n // tm
    once = pl.Buffered(1)
    weight = lambda w: pl.BlockSpec(w.shape, lambda i: (0, 0), pipeline_mode=once)
    tile = lambda w: pl.BlockSpec((tm, w), lambda i: (i, 0))
    in_specs = [tile(p.shape[1]) for p in y_parts] + [
        tile(D_MODEL), tile(D_MODEL),
        _mod_spec(layer, row_fn, MOD_GT1), _mod_spec(layer, row_fn, MOD_SH2), _mod_spec(layer, row_fn, MOD_SC2),
        pl.BlockSpec((None, 1, D_MODEL), lambda i: (layer, 0, 0)),
    ] + [weight(w) for w in merge_w] + [
        pl.BlockSpec((None, N_EXPERTS, D_MODEL), lambda i: (layer, 0, 0), pipeline_mode=once)]
    args = list(y_parts) + [h, x, mod, mod, mod, norm_ffn, *merge_w, w_router_t]
    out_specs = [tile(D_MODEL), tile(D_MODEL), pl.BlockSpec((N_EXPERTS, tm), lambda i: (0, i))]
    out_shape = [jax.ShapeDtypeStruct((n, D_MODEL), F32), jax.ShapeDtypeStruct((n, D_MODEL), BF16),
                 jax.ShapeDtypeStruct((N_EXPERTS, n), F32)]
    scratch = []
    route_sets = 0
    if route is not None:
        route_sets = N_CTX_SETS // tiles
        rows, slots = route_sets * CTX_LEN, route_sets * CAP_CTX
        in_specs += [pl.BlockSpec((N_EXPERTS, rows), lambda i: (0, i)), pl.BlockSpec((rows, D_MODEL), lambda i: (i, 0))]
        args += list(route)
        out_specs += [pl.BlockSpec((N_EXPERTS, slots, D_MODEL), lambda i: (0, i, 0)),
                      pl.BlockSpec((N_EXPERTS, slots, LANES), lambda i: (0, i, 0)),
                      pl.BlockSpec((rows, LANES), lambda i: (i, 0))]
        out_shape += [jax.ShapeDtypeStruct((N_EXPERTS, N_CTX_SETS * CAP_CTX, D_MODEL), BF16),
                      jax.ShapeDtypeStruct((N_EXPERTS, N_CTX_SETS * CAP_CTX, LANES), F32),
                      jax.ShapeDtypeStruct((T_CTX, LANES), F32)]
        scratch += [pltpu.VMEM((route_sets, N_EXPERTS * CAP_CTX, CTX_LEN), BF16),
                    pltpu.VMEM((route_sets, LANES, CTX_LEN), F32)]
    return pl.pallas_call(
        functools.partial(_merge_kernel, n_y=len(y_parts), route_sets=route_sets, sub=sub),
        grid=(tiles,),
        in_specs=in_specs,
        out_specs=out_specs,
        out_shape=out_shape,
        scratch_shapes=scratch,
        compiler_params=_params(1),
        name="merge_route" if route_sets else "merge",
    )(*args)


GATHER_M = 512
RANK_TILE = 128


def _rank_row(aff, a_row, e, n):
    tiles = n // RANK_TILE
    sub = lax.broadcasted_iota(jnp.int32, (RANK_TILE, RANK_TILE), 0)
    lane = lax.broadcasted_iota(jnp.int32, (RANK_TILE, RANK_TILE), 1)
    earlier = jnp.where(sub < lane, 1.0, 0.0)
    acc = [jnp.zeros((8, RANK_TILE), F32) for _ in range(tiles)]
    for c in range(tiles):
        a_col = jnp.broadcast_to(aff[c * RANK_TILE:(c + 1) * RANK_TILE, e:e + 1], (RANK_TILE, RANK_TILE))
        for j in range(tiles):
            a_rj = a_row[:, j * RANK_TILE:(j + 1) * RANK_TILE]
            if c < j:
                beats = jnp.where(a_col >= a_rj, 1.0, 0.0)
            elif c > j:
                beats = jnp.where(a_col > a_rj, 1.0, 0.0)
            else:
                beats = jnp.where(a_col > a_rj, 1.0, jnp.where(a_col == a_rj, earlier, 0.0))
            acc[j] = acc[j] + beats.reshape(RANK_TILE // 8, 8, RANK_TILE).sum(axis=0)
    return jnp.concatenate([a.sum(axis=0, keepdims=True) for a in acc], axis=1)


def _route_set(lg, h, xg_ref, g_ref, rc_ref, p_ref, rt_ref, n, cap):
    ex = jnp.exp(lg - lg.max(axis=0, keepdims=True))
    aff_t = ex / ex.sum(axis=0, keepdims=True)
    aff = jnp.concatenate([aff_t, jnp.zeros((LANES - N_EXPERTS, n), F32)], axis=0).T
    rt_ref[...] = jnp.full((LANES, n), float(cap), F32)
    slot = lax.broadcasted_iota(jnp.int32, (cap, n), 0).astype(F32)
    for e in range(N_EXPERTS):
        rank = _rank_row(aff, aff_t[e:e + 1, :], e, n)
        rt_ref[e:e + 1, :] = jnp.minimum(rank, float(cap))
        p_ref[e * cap:(e + 1) * cap, :] = jnp.where(rank == slot, 1.0, 0.0).astype(BF16)

    a1, a2, a3 = (p.astype(F32) for p in _split3(aff))
    packed = (a1 + pltpu.roll(a2, N_EXPERTS, 1) + pltpu.roll(a3, 2 * N_EXPERTS, 1)).astype(BF16)
    per = GATHER_M // cap
    glane = lax.broadcasted_iota(jnp.int32, (cap, LANES), 1)
    for grp in range(N_EXPERTS * cap // GATHER_M):
        p = p_ref[grp * GATHER_M:(grp + 1) * GATHER_M, :]
        xg = _bdot(p, h).astype(BF16)
        gg = _bdot(p, packed)
        for k in range(per):
            e = grp * per + k
            xg_ref[e] = xg[k * cap:(k + 1) * cap, :]
            mine = (glane < 3 * N_EXPERTS) & ((glane & (N_EXPERTS - 1)) == e)
            ge = jnp.where(mine, gg[k * cap:(k + 1) * cap, :], 0.0).sum(axis=-1, keepdims=True)
            g_ref[e] = jnp.broadcast_to(ge, (cap, LANES))
    rc_ref[...] = rt_ref[...].T


def _route_kernel(lg_ref, h_ref, xg_ref, g_ref, rc_ref, p_ref, rt_ref, *, n, cap):
    _route_set(lg_ref[...], h_ref[...], xg_ref, g_ref, rc_ref, p_ref, rt_ref, n, cap)


def _route(logits, h2, n, cap, n_sets):
    return pl.pallas_call(
        functools.partial(_route_kernel, n=n, cap=cap),
        grid=(n_sets,),
        in_specs=[
            pl.BlockSpec((N_EXPERTS, n), lambda s: (0, s)),
            pl.BlockSpec((n, D_MODEL), lambda s: (s, 0)),
        ],
        out_specs=[
            pl.BlockSpec((N_EXPERTS, cap, D_MODEL), lambda s: (0, s, 0)),
            pl.BlockSpec((N_EXPERTS, cap, LANES), lambda s: (0, s, 0)),
            pl.BlockSpec((n, LANES), lambda s: (s, 0)),
        ],
        out_shape=[jax.ShapeDtypeStruct((N_EXPERTS, n_sets * cap, D_MODEL), BF16),
                   jax.ShapeDtypeStruct((N_EXPERTS, n_sets * cap, LANES), F32),
                   jax.ShapeDtypeStruct((n_sets * n, LANES), F32)],
        scratch_shapes=[pltpu.VMEM((N_EXPERTS * cap, n), BF16), pltpu.VMEM((LANES, n), F32)],
        compiler_params=_params(1),
        name=f"route_n{n}",
    )(logits, h2)


EXPERT_TF = 1024
EXPERT_SUB = 256
N_FF_TILES = EXPERT_FF // EXPERT_TF


def _expert_kernel(xc_ref, xl_ref, gc_ref, gl_ref, wg_ref, wu_ref, wd_ref, o_ref, x_s, acc_s):
    f = pl.program_id(1)
    n_ctx = xc_ref.shape[0]

    @pl.when(f == 0)
    def _():
        x_s[:n_ctx, :] = xc_ref[...]
        x_s[n_ctx:, :] = xl_ref[...]
        acc_s[...] = jnp.zeros_like(acc_s)

    x = x_s[...]
    part = None
    for j in range(EXPERT_TF // EXPERT_SUB):
        cols = slice(j * EXPERT_SUB, (j + 1) * EXPERT_SUB)
        gate = _bdot(x, wg_ref[:, cols].astype(BF16))
        up = _bdot(x, wu_ref[:, cols].astype(BF16))
        hid = (gate * jax.nn.sigmoid(gate)) * up
        down = _bdot(hid.astype(BF16), wd_ref[cols, :].astype(BF16))
        part = down if part is None else part + down
    acc_s[...] += part

    @pl.when(f == N_FF_TILES - 1)
    def _():
        o_ref[:n_ctx, :] = (acc_s[:n_ctx, :] * gc_ref[:, :1]).astype(o_ref.dtype)
        o_ref[n_ctx:, :] = (acc_s[n_ctx:, :] * gl_ref[:, :1]).astype(o_ref.dtype)


def _experts(xg_ctx, xg_lat, g_ctx, g_lat, w_gate, w_up, w_down, layer):
    sc, sl = xg_ctx.shape[1], xg_lat.shape[1]
    slots = lambda s, w: pl.BlockSpec((None, s, w), lambda e, f: (e, 0, 0))
    return pl.pallas_call(
        _expert_kernel,
        grid=(N_EXPERTS, N_FF_TILES),
        in_specs=[
            slots(sc, D_MODEL), slots(sl, D_MODEL), slots(sc, LANES), slots(sl, LANES),
            pl.BlockSpec((None, None, D_MODEL, EXPERT_TF), lambda e, f: (layer, e, 0, f)),
            pl.BlockSpec((None, None, D_MODEL, EXPERT_TF), lambda e, f: (layer, e, 0, f)),
            pl.BlockSpec((None, None, EXPERT_TF, D_MODEL), lambda e, f: (layer, e, f, 0)),
        ],
        out_specs=slots(sc + sl, D_MODEL),
        out_shape=jax.ShapeDtypeStruct((N_EXPERTS, sc + sl, D_MODEL), BF16),
        scratch_shapes=[pltpu.VMEM((sc + sl, D_MODEL), BF16), pltpu.VMEM((sc + sl, D_MODEL), F32)],
        compiler_params=_params(2),
        name="experts",
    )(xg_ctx, xg_lat, g_ctx, g_lat, w_gate, w_up, w_down)


COMB_TM = 256
COMB_STEPS = T_LAT // COMB_TM
COMB_CTX_SETS = N_CTX_SETS // COMB_STEPS


def _combine_group(o_ref, rc_ref, x_ref, gt2_ref, ng_ref, norm_refs, out_refs, cap, sets, final):
    slots = N_EXPERTS * cap
    j = lax.broadcasted_iota(jnp.int32, (LANES, slots), 1)
    e = lax.broadcasted_iota(jnp.int32, (LANES, slots), 0)
    expand = jnp.where(j // cap == e, 1.0, 0.0).astype(BF16)
    slot = (lax.broadcasted_iota(jnp.int32, (1, slots), 1) % cap).astype(F32)
    for k in range(sets):
        rows = slice(k * COMB_TM, (k + 1) * COMB_TM)
        rank = _bdot(rc_ref[rows, :].astype(BF16), expand)
        pt = jnp.where(rank == slot, 1.0, 0.0).astype(BF16)
        ffn = _bdot(pt, o_ref[:, k * cap:(k + 1) * cap, :].reshape(slots, D_MODEL))
        x = x_ref[rows, :] + gt2_ref[...] * ffn
        if final:
            out_refs[0][rows, :] = _rms(x) * ng_ref[...]
        else:
            out_refs[0][rows, :] = x
            out_refs[1][rows, :] = _modnorm(x, ng_ref[...], norm_refs[0][...], norm_refs[1][...]).astype(BF16)


def _combine_kernel(*refs, final):
    n_in, n_out = (4, 1) if final else (6, 2)
    ng_ref = refs[2 * n_in]
    outs = refs[2 * n_in + 1:]
    for g, (cap, sets) in enumerate(((CAP_CTX, COMB_CTX_SETS), (CAP_LAT, 1))):
        o_ref, rc_ref, x_ref, gt2_ref, *norm_refs = refs[g * n_in:(g + 1) * n_in]
        _combine_group(o_ref, rc_ref, x_ref, gt2_ref, ng_ref, norm_refs, outs[g * n_out:(g + 1) * n_out], cap, sets,
                       final)


def _combine(out, rc_ctx, rc_lat, x1_ctx, x1_lat, mod, norm_g, layer, final):
    lat_tiles = LAT_LEN // COMB_TM
    lat_slot0 = N_CTX_SETS * CAP_CTX // CAP_LAT
    groups = (
        (rc_ctx, x1_ctx, lambda i: 0, COMB_CTX_SETS * CTX_LEN,
         pl.BlockSpec((N_EXPERTS, COMB_CTX_SETS * CAP_CTX, D_MODEL), lambda i: (0, i, 0))),
        (rc_lat, x1_lat, lambda i: 1 + i // lat_tiles, COMB_TM,
         pl.BlockSpec((N_EXPERTS, CAP_LAT, D_MODEL), lambda i: (0, lat_slot0 + i // lat_tiles, 0))),
    )
    in_specs, args, out_specs, out_shape = [], [], [], []
    for rc, x1, row_fn, tm, o_spec in groups:
        tile = lambda w, tm=tm: pl.BlockSpec((tm, w), lambda i: (i, 0))
        in_specs += [o_spec, tile(LANES), tile(D_MODEL), _mod_spec(layer, row_fn, MOD_GT2)]
        args += [out, rc, x1, mod]
        if not final:
            in_specs += [_mod_spec(layer + 1, row_fn, MOD_SH1), _mod_spec(layer + 1, row_fn, MOD_SC1)]
            args += [mod, mod]
        out_specs += [tile(D_MODEL)] * (1 if final else 2)
        out_shape += [jax.ShapeDtypeStruct(x1.shape, F32)] + ([] if final else [jax.ShapeDtypeStruct(x1.shape, BF16)])
    if final:
        in_specs.append(pl.BlockSpec((1, D_MODEL), lambda i: (0, 0)))
        args.append(norm_g.reshape(1, D_MODEL))
    else:
        in_specs.append(pl.BlockSpec((None, 1, D_MODEL), lambda i: (layer + 1, 0, 0)))
        args.append(norm_g)
    res = pl.pallas_call(
        functools.partial(_combine_kernel, final=final),
        grid=(COMB_STEPS,),
        in_specs=in_specs,
        out_specs=out_specs,
        out_shape=out_shape,
        compiler_params=_params(1),
        name="combine",
    )(*args)
    half = len(res) // 2
    return tuple(res[:half]), tuple(res[half:])


def kernel(x_prompt, x_sample, cache_attn_k, cache_attn_v, cache_na_k, cache_na_v, c, c_ctx, w_ada, b_ada,
           norm_mix, norm_ffn, w_in, q_norm, k_norm, rpb, w_branch_a, w_branch_b, w_out, w_router, w_gate,
           w_up, w_down, final_norm):
    x_ctx = x_prompt.reshape(T_CTX, D_MODEL)
    x_lat = x_sample.reshape(T_LAT, D_MODEL)
    mod = _modulation(c, c_ctx, w_ada, b_ada)
    norm_mix3 = norm_mix.reshape(DEPTH, 1, D_MODEL)
    norm_ffn3 = norm_ffn.reshape(DEPTH, 1, D_MODEL)
    w_router_t = jnp.swapaxes(w_router, 1, 2)
    ctx_row = lambda *g: 0

    h_ctx = _prenorm(x_ctx, mod, norm_mix3, 0, ctx_row)
    h_lat = _prenorm(x_lat, mod, norm_mix3, 0, _lat_mod_row(NORM_TM))
    layer_caches = []
    for layer in range(DEPTH):
        last = layer == DEPTH - 1
        yq, ykv, merge_w = _projection(h_ctx, h_lat, w_in, q_norm, k_norm, w_branch_a, w_branch_b, w_out, layer)
        yab_ctx, *new_caches = _ctx_attention(yq, ykv, layer_caches if last else [])
        layer_caches.append(tuple(new_caches))
        ya_lat = _lat_gqa_attention(yq, ykv, cache_attn_k, cache_attn_v, layer)
        yb_lat = _lat_na_attention(yq, ykv, cache_na_k, cache_na_v, rpb, layer)
        merge_args = (mod, norm_ffn3, merge_w, w_router_t, layer)
        x1_ctx, h2_ctx, lg_ctx = _merge([yab_ctx], h_ctx, x_ctx, *merge_args, ctx_row, MERGE_TM, MERGE_SUB)
        x1_lat, h2_lat, lg_lat, xg_ctx, g_ctx, rc_ctx = _merge(
            [ya_lat, yb_lat], h_lat, x_lat, *merge_args, _lat_mod_row(MERGE_ROUTE_TM), MERGE_ROUTE_TM,
            MERGE_ROUTE_SUB, route=(lg_ctx, h2_ctx))
        xg_lat, g_lat, rc_lat = _route(lg_lat, h2_lat, LAT_LEN, CAP_LAT, N_LAT_SETS)
        out = _experts(xg_ctx, xg_lat, g_ctx, g_lat, w_gate, w_up, w_down, layer)
        final = layer == DEPTH - 1
        norm_g = final_norm if final else norm_mix3
        res_ctx, res_lat = _combine(out, rc_ctx, rc_lat, x1_ctx, x1_lat, mod, norm_g, layer, final)
        if final:
            (y_ctx,), (y_lat,) = res_ctx, res_lat
        else:
            (x_ctx, h_ctx), (x_lat, h_lat) = res_ctx, res_lat

    heads = (N_KV_A, N_KV_A, N_HEADS_B, N_HEADS_B)
    new_caches = [a.reshape(N_CTX_SETS, DEPTH, CTX_LEN, h, HEAD_DIM) for a, h in zip(layer_caches[-1], heads)]
    return (y_ctx.reshape(N_CTX_SETS, CTX_LEN, D_MODEL), y_lat.reshape(N_LAT_SETS, LAT_LEN, D_MODEL), *new_caches)
```

```python
import functools

import numpy as np
import jax
import jax.numpy as jnp
from jax import lax
from jax.experimental import pallas as pl
from jax.experimental.pallas import tpu as pltpu

F32 = jnp.float32
BF16 = jnp.bfloat16

D_MODEL = 1024
N_CTX_SETS, CTX_LEN = 16, 256
N_LAT_SETS, LAT_LEN = 2, 1024
T_CTX = N_CTX_SETS * CTX_LEN
T_LAT = N_LAT_SETS * LAT_LEN
T_ALL = T_CTX + T_LAT
DEPTH = 2
PAST_LEN = 512
GRID_W = 64
GRID_ROWS = LAT_LEN // GRID_W
HEAD_DIM = 64
N_HEADS_A, N_KV_A, N_HEADS_B = 8, 2, 8
WIN_ROWS, WIN_COLS = 8, 16
N_EXPERTS = 16
EXPERT_FF = 2048
CAP_CTX = 2 * CTX_LEN // N_EXPERTS
CAP_LAT = 2 * LAT_LEN // N_EXPERTS
ROPE_THETA = 10000.0
EPS = 1e-6
NEG = -1e30
QKV_DIM = 2304
ATT_SCALE = HEAD_DIM ** -0.5

LANES = 128
VMEM_LIMIT = 56 * 1024 * 1024

PROJ_TN = 256
N_PROJ_TILES = QKV_DIM // PROJ_TN
PROJ_ORDER = np.array([0, 1, 3, 4, 5, 6, 7, 8, 2], np.int32)
N_Q_TILES = 4
CAST_STEPS = 8
MOD_SH1, MOD_SC1, MOD_GT1, MOD_SH2, MOD_SC2, MOD_GT2 = range(6)


def _params(n_grid_dims, vmem=VMEM_LIMIT):
    return pltpu.CompilerParams(dimension_semantics=("arbitrary",) * n_grid_dims, vmem_limit_bytes=vmem)


def _bdot(a, b):
    return jnp.dot(a, b, preferred_element_type=F32)


def _bdot_nt(a, b):
    return lax.dot_general(a, b, (((1,), (1,)), ((), ())), preferred_element_type=F32)


def _split2(x):
    hi = x.astype(BF16)
    lo = (x - hi.astype(F32)).astype(BF16)
    return hi, lo


def _split3(x):
    hi = x.astype(BF16)
    r = x - hi.astype(F32)
    mid = r.astype(BF16)
    lo = (r - mid.astype(F32)).astype(BF16)
    return hi, mid, lo


def _rms(x):
    return x * lax.rsqrt(jnp.mean(x * x, axis=-1, keepdims=True) + EPS)


def _modnorm(x, g, sh, sc):
    return (_rms(x) * g) * (1.0 + sc) + sh


def _lat_mod_row(tile_rows):
    return lambda i: 1 + (i * tile_rows) // LAT_LEN


def _mod_spec(layer, row_fn, chunk):
    return pl.BlockSpec((None, None, 1, D_MODEL), lambda *g: (layer, row_fn(*g), 0, chunk))


MOD_TN = 1536


def _mod_kernel(ct_ref, w_ref, b_ref, o_ref):
    ct = ct_ref[...]
    act = ct * jax.nn.sigmoid(ct)
    w = w_ref[...]
    for m in range(3):
        o_ref[m:m + 1, :] = jnp.sum(w * act[:, m:m + 1], axis=0, keepdims=True) + b_ref[...]
    o_ref[3:8, :] = jnp.zeros((5, MOD_TN), F32)


def _modulation(c, c_ctx, w_ada, b_ada):
    cond = jnp.concatenate([c_ctx[None, :], c, jnp.zeros((5, D_MODEL), F32)], axis=0)
    mod = pl.pallas_call(
        _mod_kernel,
        grid=(DEPTH, 6 * D_MODEL // MOD_TN),
        in_specs=[
            pl.BlockSpec((D_MODEL, 8), lambda l, j: (0, 0)),
            pl.BlockSpec((None, D_MODEL, MOD_TN), lambda l, j: (l, 0, j)),
            pl.BlockSpec((None, 1, MOD_TN), lambda l, j: (l, 0, j)),
        ],
        out_specs=pl.BlockSpec((None, 8, MOD_TN), lambda l, j: (l, 0, j)),
        out_shape=jax.ShapeDtypeStruct((DEPTH, 8, 6 * D_MODEL), F32),
        compiler_params=_params(2),
        name="modulation",
    )(cond.T, w_ada, b_ada.reshape(DEPTH, 1, 6 * D_MODEL))
    return mod.reshape(DEPTH, 8, 1, 6 * D_MODEL)


NORM_TM = 1024


def _prenorm_kernel(x_ref, g_ref, sh_ref, sc_ref, h_ref):
    h_ref[...] = _modnorm(x_ref[...], g_ref[...], sh_ref[...], sc_ref[...]).astype(BF16)


def _prenorm(x, mod, norm_g, layer, row_fn):
    n = x.shape[0]
    return pl.pallas_call(
        _prenorm_kernel,
        grid=(n // NORM_TM,),
        in_specs=[
            pl.BlockSpec((NORM_TM, D_MODEL), lambda i: (i, 0)),
            pl.BlockSpec((None, 1, D_MODEL), lambda i: (layer, 0, 0)),
            _mod_spec(layer, row_fn, MOD_SH1),
            _mod_spec(layer, row_fn, MOD_SC1),
        ],
        out_specs=pl.BlockSpec((NORM_TM, D_MODEL), lambda i: (i, 0)),
        out_shape=jax.ShapeDtypeStruct((n, D_MODEL), BF16),
        compiler_params=_params(1),
        name="prenorm",
    )(x, norm_g, mod, mod)


PROJ_CH = 1024


def _rope_tables():
    t = np.arange(LAT_LEN)
    lane = np.arange(LANES) % HEAD_DIM
    pos = np.where(lane < HEAD_DIM // 2, (t // GRID_W)[:, None], (t % GRID_W)[:, None]).astype(np.float64)
    freq = ROPE_THETA ** (-(lane % 16).astype(np.float64) / 16.0)
    ang = pos * freq[None, :]
    sign = np.where((lane & 16) == 0, -1.0, 1.0)[None, :]
    return np.cos(ang).astype(np.float32), (np.sin(ang) * sign).astype(np.float32)


def _head_norm_rope(y, gain, cos, sin):
    w = y.shape[1]
    r = (lax.broadcasted_iota(jnp.int32, (2 * w, w), 0) % w) // HEAD_DIM
    c = lax.broadcasted_iota(jnp.int32, (2 * w, w), 1) // HEAD_DIM
    seg = jnp.where(r == c, 1.0 / HEAD_DIM, 0.0).astype(BF16)
    ms = _bdot(jnp.concatenate(_split2(y * y), axis=1), seg)
    yn = y * lax.rsqrt(ms + EPS) * gain
    if cos is None:
        return yn
    lane = lax.broadcasted_iota(jnp.int32, yn.shape, 1)
    partner = jnp.where((lane & 16) == 0, pltpu.roll(yn, w - 16, 1), pltpu.roll(yn, 16, 1))
    if w > LANES:
        cos = jnp.concatenate([cos] * (w // LANES), axis=1)
        sin = jnp.concatenate([sin] * (w // LANES), axis=1)
    return yn * cos + partner * sin


def _proj_kernel(order_ref, hc_ref, hl_ref, w_ref, gain_ref, cos_ref, sin_ref, wa_ref, wbr_ref, wo_ref, wg_ref,
                 yq_ref, ykv_ref, oa_ref, ob_ref, oo_ref, og_ref, wb_ref, acc_ref):
    del order_ref
    s = pl.program_id(0)
    wb_ref[...] = w_ref[...].astype(BF16)
    oa_ref[...] = wa_ref[...].astype(BF16)
    ob_ref[...] = wbr_ref[...].astype(BF16)
    oo_ref[...] = wo_ref[...].astype(BF16)
    og_ref[...] = wg_ref[0].astype(BF16)
    chunks = [(hc_ref, k * PROJ_CH, k * PROJ_CH, False) for k in range(T_CTX // PROJ_CH)]
    chunks += [(hl_ref, k * PROJ_CH, T_CTX + k * PROJ_CH, True) for k in range(T_LAT // PROJ_CH)]

    def matmul(chunk, dst):
        h_ref, r0, o0, _ = chunk
        dst[o0:o0 + PROJ_CH, :] = _bdot(h_ref[r0:r0 + PROJ_CH, :], wb_ref[...]).astype(dst.dtype)

    def finish(chunk, width, src, dst):
        _, _, o0, is_lat = chunk
        cos, sin = (cos_ref[...], sin_ref[...]) if is_lat else (None, None)
        y = _head_norm_rope(src[o0:o0 + PROJ_CH, :width], gain_ref[:, :width], cos, sin)
        dst[o0:o0 + PROJ_CH, :width] = y.astype(dst.dtype)

    def tile(width, work, dst):
        matmul(chunks[0], work)
        for k in range(1, len(chunks)):
            matmul(chunks[k], work)
            if width:
                finish(chunks[k - 1], width, work, dst)
        if width:
            finish(chunks[-1], width, work, dst)

    pl.when(s < 2)(lambda: tile(PROJ_TN, acc_ref, yq_ref))
    pl.when((s >= 2) & (s < N_Q_TILES))(lambda: tile(0, yq_ref, yq_ref))
    pl.when((s >= N_Q_TILES) & (s < N_PROJ_TILES - 1))(lambda: tile(0, ykv_ref, ykv_ref))
    pl.when(s == N_PROJ_TILES - 1)(lambda: tile(LANES, ykv_ref, ykv_ref))


def _projection(h_ctx, h_lat, w_in, q_norm, k_norm, w_ba, w_bb, w_out, layer):
    ones = jnp.ones((2 * HEAD_DIM,), F32)
    gain = jnp.stack([jnp.tile(q_norm[layer], 4), jnp.tile(q_norm[layer], 4),
                      jnp.concatenate([jnp.tile(k_norm[layer], 2), ones])])[:, None, :]
    cos, sin = _rope_tables()
    half = N_HEADS_A * HEAD_DIM
    ra, ro = half // CAST_STEPS, D_MODEL // CAST_STEPS
    part = lambda j: jnp.minimum(j, CAST_STEPS - 1)
    gain_row = lambda j: jnp.where(j == N_PROJ_TILES - 1, 2, jnp.minimum(j, 1))
    grid_spec = pltpu.PrefetchScalarGridSpec(
        num_scalar_prefetch=1,
        grid=(N_PROJ_TILES,),
        in_specs=[
            pl.BlockSpec((T_CTX, D_MODEL), lambda j, p: (0, 0)),
            pl.BlockSpec((T_LAT, D_MODEL), lambda j, p: (0, 0)),
            pl.BlockSpec((None, D_MODEL, PROJ_TN), lambda j, p: (layer, 0, p[j])),
            pl.BlockSpec((None, 1, PROJ_TN), lambda j, p: (gain_row(j), 0, 0)),
            pl.BlockSpec((LAT_LEN, LANES), lambda j, p: (0, 0)),
            pl.BlockSpec((LAT_LEN, LANES), lambda j, p: (0, 0)),
            pl.BlockSpec((None, ra, D_MODEL), lambda j, p: (layer, part(j), 0)),
            pl.BlockSpec((None, ra, D_MODEL), lambda j, p: (layer, part(j), 0)),
            pl.BlockSpec((None, ro, D_MODEL), lambda j, p: (layer, part(j), 0)),
            pl.BlockSpec((pl.Element(1), pl.Element(ro), pl.Element(2 * D_MODEL)),
                         lambda j, p: (layer, part(j) * ro, QKV_DIM)),
        ],
        out_specs=[
            pl.BlockSpec((T_ALL, PROJ_TN), lambda j, p: (0, jnp.minimum(j, N_Q_TILES - 1))),
            pl.BlockSpec((T_ALL, PROJ_TN), lambda j, p: (0, jnp.maximum(j - N_Q_TILES, 0))),
            pl.BlockSpec((ra, D_MODEL), lambda j, p: (part(j), 0)),
            pl.BlockSpec((ra, D_MODEL), lambda j, p: (part(j), 0)),
            pl.BlockSpec((ro, D_MODEL), lambda j, p: (part(j), 0)),
            pl.BlockSpec((ro, 2 * D_MODEL), lambda j, p: (part(j), 0)),
        ],
        scratch_shapes=[pltpu.VMEM((D_MODEL, PROJ_TN), BF16), pltpu.VMEM((T_ALL, PROJ_TN), F32)],
    )
    yq, ykv, *merge_w = pl.pallas_call(
        _proj_kernel,
        grid_spec=grid_spec,
        out_shape=[jax.ShapeDtypeStruct((T_ALL, N_Q_TILES * PROJ_TN), BF16),
                   jax.ShapeDtypeStruct((T_ALL, QKV_DIM - N_Q_TILES * PROJ_TN), F32),
                   jax.ShapeDtypeStruct((half, D_MODEL), BF16), jax.ShapeDtypeStruct((half, D_MODEL), BF16),
                   jax.ShapeDtypeStruct((D_MODEL, D_MODEL), BF16), jax.ShapeDtypeStruct((D_MODEL, 2 * D_MODEL), BF16)],
        compiler_params=_params(1),
        name="projection",
    )(jnp.asarray(PROJ_ORDER), h_ctx, h_lat, w_in, gain, jnp.asarray(cos), jnp.asarray(sin), w_ba, w_bb, w_out, w_in)
    return yq, ykv, merge_w


COL_QA, COL_QB = 0, 512
COL_KB, COL_VB, COL_KA = 0, 512, 1024


def _lane_is_low(shape):
    return lax.broadcasted_iota(jnp.int32, shape, 1) < HEAD_DIM


def _pair_halves(x):
    low = _lane_is_low(x.shape)
    xb = x.astype(BF16)
    zero = jnp.zeros_like(xb)
    return jnp.where(low, xb, zero), jnp.where(low, zero, xb)


def _scaled_q(q):
    assert ATT_SCALE == 0.125
    return (q * ATT_SCALE).astype(BF16)


def _attend_pair(q, keys, values, biases, joint_pv=True):
    probs, dens, out = [], [], None
    for h in range(2):
        scores = []
        for k, b in zip(keys[h], biases[h]):
            s = _bdot_nt(q, k)
            scores.append(s if b is None else s + b)
        m = scores[0].max(axis=-1, keepdims=True)
        for s in scores[1:]:
            m = jnp.maximum(m, s.max(axis=-1, keepdims=True))
        den, num = None, None
        for s, v in zip(scores, values[h]):
            e = jnp.exp(s - m)
            d = e.sum(axis=-1, keepdims=True)
            den = d if den is None else den + d
            if joint_pv:
                probs.append(e.astype(BF16))
            else:
                o = _bdot(e.astype(BF16), v)
                num = o if num is None else num + o
        dens.append(den)
        if not joint_pv:
            out = num / den if out is None else out + num / den
    if not joint_pv:
        return out
    num = _bdot(jnp.concatenate(probs, axis=1), jnp.concatenate(values[0] + values[1], axis=0))
    return num / jnp.where(_lane_is_low(num.shape), dens[0], dens[1])


def _attend_pair_t(q, keys, values_t, biases_t):
    probs, dens = [], []
    for h in range(2):
        scores = []
        for k, b in zip(keys[h], biases_t[h]):
            s = _bdot_nt(k, q)
            scores.append(s if b is None else s + b)
        m = scores[0].max(axis=0, keepdims=True)
        for s in scores[1:]:
            m = jnp.maximum(m, s.max(axis=0, keepdims=True))
        den = None
        for s in scores:
            e = jnp.exp(s - m)
            d = e.sum(axis=0, keepdims=True)
            den = d if den is None else den + d
            probs.append(e.astype(BF16))
        dens.append(den)
    num_t = _bdot(values_t, jnp.concatenate(probs, axis=0))
    row = lax.broadcasted_iota(jnp.int32, num_t.shape, 0)
    return (num_t / jnp.where(row < HEAD_DIM, dens[0], dens[1])).T


def _gqa_variants(x):
    lo, hi = _pair_halves(x)
    sw_lo, sw_hi = _pair_halves(pltpu.roll(x, HEAD_DIM, 1))
    return [lo, sw_hi, sw_lo, hi]


def _gqa_attention(q_ref, kvar, vvar, o_ref, joint_pv):
    for pair in range(N_HEADS_A // 2):
        q = _scaled_q(q_ref[:, pair * LANES:(pair + 1) * LANES])
        kvh = (2 * pair) // (N_HEADS_A // N_KV_A)
        out = _attend_pair(q, [[kvar[2 * kvh]], [kvar[2 * kvh + 1]]], [[vvar[2 * kvh]], [vvar[2 * kvh + 1]]],
                           [[None], [None]], joint_pv)
        o_ref[:, pair * LANES:(pair + 1) * LANES] = out.astype(o_ref.dtype)


def _ctx_attn_kernel(qa_ref, qb_ref, kb_ref, vb_ref, kava_ref, *refs, n_prev):
    prev = refs[:4 * n_prev]
    y_ref, nak_ref, nav_ref, nbk_ref, nbv_ref = refs[4 * n_prev:]
    ka = kava_ref[:, :LANES]
    va = kava_ref[:, LANES:]
    new = (ka, va, kb_ref[...], vb_ref[...])
    for c, (o_ref, val) in enumerate(zip((nak_ref, nav_ref, nbk_ref, nbv_ref), new)):
        if n_prev:
            for p in range(n_prev):
                o_ref[p] = prev[4 * p + c][...]
            o_ref[n_prev] = val
        else:
            o_ref[...] = val
    _gqa_attention(qa_ref, _gqa_variants(ka), _gqa_variants(va), y_ref.at[:, :N_HEADS_A * HEAD_DIM], joint_pv=False)
    for pair in range(N_HEADS_B // 2):
        cols = slice(pair * LANES, (pair + 1) * LANES)
        q = _scaled_q(qb_ref[:, cols])
        k_lo, k_hi = _pair_halves(kb_ref[:, cols])
        v_lo, v_hi = _pair_halves(vb_ref[:, cols])
        out = _attend_pair(q, [[k_lo], [k_hi]], [[v_lo], [v_hi]], [[None], [None]], joint_pv=False)
        y_ref[:, N_HEADS_A * HEAD_DIM + pair * LANES:N_HEADS_A * HEAD_DIM + (pair + 1) * LANES] = out.astype(BF16)


def _ctx_attention(yq, ykv, prev_caches):
    wide = 4 * LANES
    widths = (LANES, LANES, wide, wide)
    n_prev = len(prev_caches)
    row = lambda b: (b, 0)
    f32 = lambda *s: jax.ShapeDtypeStruct(s, F32)
    in_specs = [
        pl.BlockSpec((CTX_LEN, wide), lambda b: (b, COL_QA // wide)),
        pl.BlockSpec((CTX_LEN, wide), lambda b: (b, COL_QB // wide)),
        pl.BlockSpec((CTX_LEN, wide), lambda b: (b, COL_KB // wide)),
        pl.BlockSpec((CTX_LEN, wide), lambda b: (b, COL_VB // wide)),
        pl.BlockSpec((CTX_LEN, 2 * LANES), lambda b: (b, COL_KA // (2 * LANES))),
    ]
    args = [yq, yq, ykv, ykv, ykv]
    for layer_caches in prev_caches:
        in_specs += [pl.BlockSpec((CTX_LEN, w), row) for w in widths]
        args += list(layer_caches)
    if n_prev:
        cache_specs = [pl.BlockSpec((None, n_prev + 1, CTX_LEN, w), lambda b: (b, 0, 0, 0)) for w in widths]
        cache_shapes = [f32(N_CTX_SETS, n_prev + 1, CTX_LEN, w) for w in widths]
    else:
        cache_specs = [pl.BlockSpec((CTX_LEN, w), row) for w in widths]
        cache_shapes = [f32(T_CTX, w) for w in widths]
    return pl.pallas_call(
        functools.partial(_ctx_attn_kernel, n_prev=n_prev),
        grid=(N_CTX_SETS,),
        in_specs=in_specs,
        out_specs=[pl.BlockSpec((CTX_LEN, 2 * wide), row)] + cache_specs,
        out_shape=[jax.ShapeDtypeStruct((T_CTX, 2 * wide), BF16)] + cache_shapes,
        compiler_params=_params(1),
        name="ctx_attention",
    )(*args)


LAT_TQ = 512
LAT_QT = LAT_LEN // LAT_TQ
GQA_TQ = 1024


def _lat_gqa_kernel(qa_ref, kava_ref, ck_ref, cv_ref, o_ref, k_s, vt_s):
    n_keys = PAST_LEN + LAT_LEN

    @pl.when(pl.program_id(1) == 0)
    def _():
        for i, (c, x) in enumerate(zip(_gqa_variants(ck_ref[...]), _gqa_variants(kava_ref[:, :LANES]))):
            k_s[i, :PAST_LEN, :] = c
            k_s[i, PAST_LEN:, :] = x
        for src, cols in ((cv_ref[...], slice(0, PAST_LEN)), (kava_ref[:, LANES:], slice(PAST_LEN, n_keys))):
            vt = src.T
            swapped = pltpu.roll(vt, HEAD_DIM, 0)
            top = lax.broadcasted_iota(jnp.int32, vt.shape, 0) < HEAD_DIM
            zero = jnp.zeros_like(vt)
            per_head = ((jnp.where(top, vt, zero), jnp.where(top, zero, swapped)),
                        (jnp.where(top, swapped, zero), jnp.where(top, zero, vt)))
            for kvh in range(N_KV_A):
                for half in range(2):
                    off = half * n_keys
                    vt_s[kvh, :, off + cols.start:off + cols.stop] = per_head[kvh][half].astype(BF16)

    for pair in range(N_HEADS_A // 2):
        q = _scaled_q(qa_ref[:, pair * LANES:(pair + 1) * LANES])
        kvh = (2 * pair) // (N_HEADS_A // N_KV_A)
        out = _attend_pair_t(q, [[k_s[2 * kvh]], [k_s[2 * kvh + 1]]], vt_s[kvh], [[None], [None]])
        o_ref[:, pair * LANES:(pair + 1) * LANES] = out.astype(o_ref.dtype)


def _lat_gqa_attention(yq, ykv, cache_k, cache_v, layer):
    wide = 4 * LANES
    tiles = LAT_LEN // GQA_TQ
    first = T_CTX // GQA_TQ
    cache = pl.BlockSpec((None, None, PAST_LEN, LANES), lambda b, t: (b, layer, 0, 0))
    return pl.pallas_call(
        _lat_gqa_kernel,
        grid=(N_LAT_SETS, tiles),
        in_specs=[
            pl.BlockSpec((GQA_TQ, wide), lambda b, t: (first + b * tiles + t, 0)),
            pl.BlockSpec((LAT_LEN, 2 * LANES), lambda b, t: (T_CTX // LAT_LEN + b, COL_KA // (2 * LANES))),
            cache, cache,
        ],
        out_specs=pl.BlockSpec((GQA_TQ, wide), lambda b, t: (b * tiles + t, 0)),
        out_shape=jax.ShapeDtypeStruct((T_LAT, wide), BF16),
        scratch_shapes=[pltpu.VMEM((4, PAST_LEN + LAT_LEN, LANES), BF16),
                        pltpu.VMEM((N_KV_A, LANES, 2 * (PAST_LEN + LAT_LEN)), BF16)],
        compiler_params=_params(2),
        name="lat_gqa_attention",
    )(yq, ykv, cache_k.reshape(N_LAT_SETS, DEPTH, PAST_LEN, LANES), cache_v.reshape(N_LAT_SETS, DEPTH, PAST_LEN, LANES))


N_DR = 2 * WIN_ROWS - 1
N_DC = 2 * WIN_COLS - 1
ROWS_PER_TQ = LAT_TQ // GRID_W
NA_LOCAL = 768
NA_KEY_STEP = 2 * LANES
NA_LAST_BASE = (LAT_LEN - NA_LOCAL) // NA_KEY_STEP
NA_TOE_ROWS = 32
NA_PAIRS = 4


def _na_key_base(qt):
    return jnp.where(qt < LAT_QT // 2, 0, NA_LAST_BASE)


def _window_mask():
    r = np.arange(GRID_ROWS)
    row_start = np.clip(r - WIN_ROWS // 2, 0, GRID_ROWS - WIN_ROWS)
    in_rows = (r[None, :] >= row_start[:, None]) & (r[None, :] < row_start[:, None] + WIN_ROWS)
    cq = np.arange(GRID_W)
    col_start = np.clip(cq - WIN_COLS // 2, 0, GRID_W - WIN_COLS)
    in_cols = (cq[None, :] >= col_start[:, None]) & (cq[None, :] < col_start[:, None] + WIN_COLS)
    valid = (in_rows[:, None, :, None] & in_cols[None, :, None, :]).reshape(LAT_LEN, LAT_LEN)
    tiles = []
    for qt in range(LAT_QT):
        base = (0 if qt < LAT_QT // 2 else NA_LAST_BASE) * NA_KEY_STEP
        tile = valid[qt * LAT_TQ:(qt + 1) * LAT_TQ]
        assert not tile[:, :base].any() and not tile[:, base + NA_LOCAL:].any()
        tiles.append(tile[:, base:base + NA_LOCAL])
    return np.where(np.stack(tiles), 0.0, NEG).astype(np.float32)


def _toeplitz_select():
    j = np.arange(LANES)
    c = np.clip(j - (GRID_W - 1), -(WIN_COLS - 1), WIN_COLS - 1) + (WIN_COLS - 1)
    return (np.arange(LANES)[:, None] == c[None, :]).astype(np.float32)


def _na_kernel(q_ref, k_ref, v_ref, ck_ref, cv_ref, mask_ref, rpb_ref, sel_ref, o_ref,
               toe_ref, bias_ref, k_s, v_s, ck_s, cv_s):
    b, qt = pl.program_id(1), pl.program_id(2)
    n_heads = 2 * NA_PAIRS

    @pl.when((qt == 0) & (b == 0))
    def _():
        low = lax.broadcasted_iota(jnp.int32, (GRID_W, LANES), 1) < GRID_W
        sel = sel_ref[...].astype(BF16)
        for hd in range(n_heads):
            gen = sum(_bdot(piece, sel) for piece in _split3(rpb_ref[hd]))
            for dd in range(N_DR + 1):
                lo = jnp.broadcast_to(gen[dd:dd + 1, :], (GRID_W, LANES))
                hi = jnp.broadcast_to(gen[dd + 1:dd + 2, :], (GRID_W, LANES))
                lo = pltpu.roll(lo, LANES - (GRID_W - 1), 1, stride=1, stride_axis=0)
                hi = pltpu.roll(hi, 1, 1, stride=1, stride_axis=0)
                toe_ref[hd, dd] = jnp.where(low, lo, hi)

    @pl.when(qt == 0)
    def _():
        for src, dst in ((k_ref, k_s), (v_ref, v_s), (ck_ref, ck_s), (cv_ref, cv_s)):
            for pair in range(NA_PAIRS):
                lo, hi = _pair_halves(src[:, pair * LANES:(pair + 1) * LANES])
                dst[2 * pair] = lo
                dst[2 * pair + 1] = hi

    base = _na_key_base(qt)
    local = pl.ds(pl.multiple_of(base * NA_KEY_STEP, NA_KEY_STEP), NA_LOCAL)
    for pair in range(NA_PAIRS):
        q = _scaled_q(q_ref[:, pair * LANES:(pair + 1) * LANES])
        heads = (2 * pair, 2 * pair + 1)
        biases = []
        for hd in heads:
            for rr in range(ROWS_PER_TQ):
                for kp in range(NA_LOCAL // LANES):
                    d = 2 * (kp + base * (NA_KEY_STEP // LANES)) - (qt * ROWS_PER_TQ + rr) + (WIN_ROWS - 1)
                    dd = jnp.clip(d, -1, N_DR - 1) + 1
                    bias_ref[hd, rr * GRID_W:(rr + 1) * GRID_W, kp * LANES:(kp + 1) * LANES] = toe_ref[hd, dd]
            biases.append([bias_ref[hd] + mask_ref[...], None])
        out = _attend_pair(q, [[k_s[hd, local, :], ck_s[hd]] for hd in heads],
                           [[v_s[hd, local, :], cv_s[hd]] for hd in heads], biases)
        o_ref[:, pair * LANES:(pair + 1) * LANES] = out.astype(o_ref.dtype)


def _lat_na_attention(yq, ykv, cache_k, cache_v, rpb, layer):
    first = T_CTX // LAT_TQ
    kv_row = T_CTX // LAT_LEN
    wide = N_HEADS_B * HEAD_DIM
    gen = jnp.pad(rpb[layer], ((0, 0), (1, NA_TOE_ROWS - N_DR - 1), (0, LANES - N_DC)))
    w = NA_PAIRS * LANES
    nh = 2 * NA_PAIRS
    cq, ck, cv = COL_QB // w, COL_KB // w, COL_VB // w
    return pl.pallas_call(
        _na_kernel,
        grid=(N_HEADS_B // nh, N_LAT_SETS, LAT_QT),
        in_specs=[
            pl.BlockSpec((LAT_TQ, w), lambda g, b, t: (first + b * LAT_QT + t, cq + g)),
            pl.BlockSpec((LAT_LEN, w), lambda g, b, t: (kv_row + b, ck + g)),
            pl.BlockSpec((LAT_LEN, w), lambda g, b, t: (kv_row + b, cv + g)),
            pl.BlockSpec((None, None, PAST_LEN, w), lambda g, b, t: (b, layer, 0, g)),
            pl.BlockSpec((None, None, PAST_LEN, w), lambda g, b, t: (b, layer, 0, g)),
            pl.BlockSpec((None, LAT_TQ, NA_LOCAL), lambda g, b, t: (t, 0, 0)),
            pl.BlockSpec((nh, NA_TOE_ROWS, LANES), lambda g, b, t: (g, 0, 0)),
            pl.BlockSpec((LANES, LANES), lambda g, b, t: (0, 0)),
        ],
        out_specs=pl.BlockSpec((LAT_TQ, w), lambda g, b, t: (b * LAT_QT + t, g)),
        out_shape=jax.ShapeDtypeStruct((T_LAT, wide), BF16),
        scratch_shapes=[pltpu.VMEM((nh, N_DR + 1, GRID_W, LANES), F32), pltpu.VMEM((nh, LAT_TQ, NA_LOCAL), F32),
                        pltpu.VMEM((nh, LAT_LEN, LANES), BF16), pltpu.VMEM((nh, LAT_LEN, LANES), BF16),
                        pltpu.VMEM((nh, PAST_LEN, LANES), BF16), pltpu.VMEM((nh, PAST_LEN, LANES), BF16)],
        compiler_params=_params(3),
        name="lat_na_attention",
    )(yq, ykv, ykv, cache_k.reshape(N_LAT_SETS, DEPTH, PAST_LEN, wide), cache_v.reshape(N_LAT_SETS, DEPTH, PAST_LEN, wide),
      jnp.asarray(_window_mask()), gen, jnp.asarray(_toeplitz_select()))


MERGE_TM = 1024
MERGE_SUB = 1024
MERGE_ROUTE_TM = 512
MERGE_ROUTE_SUB = 512


def _merge_kernel(*refs, n_y, route_sets, sub):
    y_refs, refs = refs[:n_y], refs[n_y:]
    (h_ref, x_ref, gt1_ref, sh2_ref, sc2_ref, nf_ref, wab, wbb, wob, wgb, wr_ref), refs = refs[:11], refs[11:]
    if route_sets:
        (rlg_ref, rh_ref), refs = refs[:2], refs[2:]
    (x1_ref, h2_ref, lg_ref), refs = refs[:3], refs[3:]
    if route_sets:
        xg_ref, g_ref, rc_ref, p_s, rt_s = refs

    wr_hi, wr_lo = _split2(wr_ref[...])
    wr_both = jnp.concatenate([wr_hi, wr_lo], axis=0)
    half = N_HEADS_A * HEAD_DIM
    n_sub = x_ref.shape[0] // sub
    for r in range(n_sub):
        rows = slice(r * sub, (r + 1) * sub)
        if n_y == 1:
            ya, yb = y_refs[0][rows, :half], y_refs[0][rows, half:]
        else:
            ya, yb = y_refs[0][rows, :], y_refs[1][rows, :]
        sets = list(range(r * route_sets // n_sub, (r + 1) * route_sets // n_sub))

        def route(k):
            tok = slice(k * CTX_LEN, (k + 1) * CTX_LEN)
            slots = slice(k * CAP_CTX, (k + 1) * CAP_CTX)
            _route_set(rlg_ref[:, tok], rh_ref[tok, :], xg_ref.at[:, slots, :], g_ref.at[:, slots, :],
                       rc_ref.at[tok, :], p_s.at[k], rt_s.at[k], CTX_LEN, CAP_CTX)

        gates = _bdot(h_ref[rows, :], wgb[...])
        za = _bdot(ya, wab[...])
        zb = _bdot(yb, wbb[...])
        for k in sets[:len(sets) // 2]:
            route(k)
        m = jax.nn.sigmoid(gates[:, :D_MODEL]) * za + jax.nn.sigmoid(gates[:, D_MODEL:]) * zb
        x1 = x_ref[rows, :] + gt1_ref[...] * _bdot(m.astype(BF16), wob[...])
        x1_ref[rows, :] = x1
        h2 = _modnorm(x1, nf_ref[...], sh2_ref[...], sc2_ref[...])
        h2_ref[rows, :] = h2.astype(BF16)
        hh, hl = _split2(h2)
        both = _bdot_nt(wr_both, hh)
        lg_ref[:, rows] = both[:N_EXPERTS, :] + both[N_EXPERTS:, :] + _bdot_nt(wr_hi, hl)
        for k in sets[len(sets) // 2:]:
            route(k)


def _merge(y_parts, h, x, mod, norm_ffn, merge_w, w_router_t, layer, row_fn, tm, sub, route=None):
    n = x.shape[0]
    tiles = n // tm
    once = pl.Buffered(1)
    weight = lambda w: pl.BlockSpec(w.shape, lambda i: (0, 0), pipeline_mode=once)
    tile = lambda w: pl.BlockSpec((tm, w), lambda i: (i, 0))
    in_specs = [tile(p.shape[1]) for p in y_parts] + [
        tile(D_MODEL), tile(D_MODEL),
        _mod_spec(layer, row_fn, MOD_GT1), _mod_spec(layer, row_fn, MOD_SH2), _mod_spec(layer, row_fn, MOD_SC2),
        pl.BlockSpec((None, 1, D_MODEL), lambda i: (layer, 0, 0)),
    ] + [weight(w) for w in merge_w] + [
        pl.BlockSpec((None, N_EXPERTS, D_MODEL), lambda i: (layer, 0, 0), pipeline_mode=once)]
    args = list(y_parts) + [h, x, mod, mod, mod, norm_ffn, *merge_w, w_router_t]
    out_specs = [tile(D_MODEL), tile(D_MODEL), pl.BlockSpec((N_EXPERTS, tm), lambda i: (0, i))]
    out_shape = [jax.ShapeDtypeStruct((n, D_MODEL), F32), jax.ShapeDtypeStruct((n, D_MODEL), BF16),
                 jax.ShapeDtypeStruct((N_EXPERTS, n), F32)]
    scratch = []
    route_sets = 0
    if route is not None:
        route_sets = N_CTX_SETS // tiles
        rows, slots = route_sets * CTX_LEN, route_sets * CAP_CTX
        in_specs += [pl.BlockSpec((N_EXPERTS, rows), lambda i: (0, i)), pl.BlockSpec((rows, D_MODEL), lambda i: (i, 0))]
        args += list(route)
        out_specs += [pl.BlockSpec((N_EXPERTS, slots, D_MODEL), lambda i: (0, i, 0)),
                      pl.BlockSpec((N_EXPERTS, slots, LANES), lambda i: (0, i, 0)),
                      pl.BlockSpec((rows, LANES), lambda i: (i, 0))]
        out_shape += [jax.ShapeDtypeStruct((N_EXPERTS, N_CTX_SETS * CAP_CTX, D_MODEL), BF16),
                      jax.ShapeDtypeStruct((N_EXPERTS, N_CTX_SETS * CAP_CTX, LANES), F32),
                      jax.ShapeDtypeStruct((T_CTX, LANES), F32)]
        scratch += [pltpu.VMEM((route_sets, N_EXPERTS * CAP_CTX, CTX_LEN), BF16),
                    pltpu.VMEM((route_sets, LANES, CTX_LEN), F32)]
    return pl.pallas_call(
        functools.partial(_merge_kernel, n_y=len(y_parts), route_sets=route_sets, sub=sub),
        grid=(tiles,),
        in_specs=in_specs,
        out_specs=out_specs,
        out_shape=out_shape,
        scratch_shapes=scratch,
        compiler_params=_params(1),
        name="merge_route" if route_sets else "merge",
    )(*args)


GATHER_M = 512
RANK_TILE = 128


def _rank_row(aff, a_row, e, n):
    tiles = n // RANK_TILE
    sub = lax.broadcasted_iota(jnp.int32, (RANK_TILE, RANK_TILE), 0)
    lane = lax.broadcasted_iota(jnp.int32, (RANK_TILE, RANK_TILE), 1)
    earlier = jnp.where(sub < lane, 1.0, 0.0)
    acc = [jnp.zeros((8, RANK_TILE), F32) for _ in range(tiles)]
    for c in range(tiles):
        a_col = jnp.broadcast_to(aff[c * RANK_TILE:(c + 1) * RANK_TILE, e:e + 1], (RANK_TILE, RANK_TILE))
        for j in range(tiles):
            a_rj = a_row[:, j * RANK_TILE:(j + 1) * RANK_TILE]
            if c < j:
                beats = jnp.where(a_col >= a_rj, 1.0, 0.0)
            elif c > j:
                beats = jnp.where(a_col > a_rj, 1.0, 0.0)
            else:
                beats = jnp.where(a_col > a_rj, 1.0, jnp.where(a_col == a_rj, earlier, 0.0))
            acc[j] = acc[j] + beats.reshape(RANK_TILE // 8, 8, RANK_TILE).sum(axis=0)
    return jnp.concatenate([a.sum(axis=0, keepdims=True) for a in acc], axis=1)


def _route_set(lg, h, xg_ref, g_ref, rc_ref, p_ref, rt_ref, n, cap):
    ex = jnp.exp(lg - lg.max(axis=0, keepdims=True))
    aff_t = ex / ex.sum(axis=0, keepdims=True)
    aff = jnp.concatenate([aff_t, jnp.zeros((LANES - N_EXPERTS, n), F32)], axis=0).T
    rt_ref[...] = jnp.full((LANES, n), float(cap), F32)
    slot = lax.broadcasted_iota(jnp.int32, (cap, n), 0).astype(F32)
    for e in range(N_EXPERTS):
        rank = _rank_row(aff, aff_t[e:e + 1, :], e, n)
        rt_ref[e:e + 1, :] = jnp.minimum(rank, float(cap))
        p_ref[e * cap:(e + 1) * cap, :] = jnp.where(rank == slot, 1.0, 0.0).astype(BF16)

    a1, a2, a3 = (p.astype(F32) for p in _split3(aff))
    packed = (a1 + pltpu.roll(a2, N_EXPERTS, 1) + pltpu.roll(a3, 2 * N_EXPERTS, 1)).astype(BF16)
    per = GATHER_M // cap
    glane = lax.broadcasted_iota(jnp.int32, (cap, LANES), 1)
    for grp in range(N_EXPERTS * cap // GATHER_M):
        p = p_ref[grp * GATHER_M:(grp + 1) * GATHER_M, :]
        xg = _bdot(p, h).astype(BF16)
        gg = _bdot(p, packed)
        for k in range(per):
            e = grp * per + k
            xg_ref[e] = xg[k * cap:(k + 1) * cap, :]
            mine = (glane < 3 * N_EXPERTS) & ((glane & (N_EXPERTS - 1)) == e)
            ge = jnp.where(mine, gg[k * cap:(k + 1) * cap, :], 0.0).sum(axis=-1, keepdims=True)
            g_ref[e] = jnp.broadcast_to(ge, (cap, LANES))
    rc_ref[...] = rt_ref[...].T


def _route_kernel(lg_ref, h_ref, xg_ref, g_ref, rc_ref, p_ref, rt_ref, *, n, cap):
    _route_set(lg_ref[...], h_ref[...], xg_ref, g_ref, rc_ref, p_ref, rt_ref, n, cap)


def _route(logits, h2, n, cap, n_sets):
    return pl.pallas_call(
        functools.partial(_route_kernel, n=n, cap=cap),
        grid=(n_sets,),
        in_specs=[
            pl.BlockSpec((N_EXPERTS, n), lambda s: (0, s)),
            pl.BlockSpec((n, D_MODEL), lambda s: (s, 0)),
        ],
        out_specs=[
            pl.BlockSpec((N_EXPERTS, cap, D_MODEL), lambda s: (0, s, 0)),
            pl.BlockSpec((N_EXPERTS, cap, LANES), lambda s: (0, s, 0)),
            pl.BlockSpec((n, LANES), lambda s: (s, 0)),
        ],
        out_shape=[jax.ShapeDtypeStruct((N_EXPERTS, n_sets * cap, D_MODEL), BF16),
                   jax.ShapeDtypeStruct((N_EXPERTS, n_sets * cap, LANES), F32),
                   jax.ShapeDtypeStruct((n_sets * n, LANES), F32)],
        scratch_shapes=[pltpu.VMEM((N_EXPERTS * cap, n), BF16), pltpu.VMEM((LANES, n), F32)],
        compiler_params=_params(1),
        name=f"route_n{n}",
    )(logits, h2)


EXPERT_TF = 1024
EXPERT_SUB = 256
N_FF_TILES = EXPERT_FF // EXPERT_TF


def _expert_kernel(xc_ref, xl_ref, gc_ref, gl_ref, wg_ref, wu_ref, wd_ref, o_ref, x_s, acc_s):
    f = pl.program_id(1)
    n_ctx = xc_ref.shape[0]

    @pl.when(f == 0)
    def _():
        x_s[:n_ctx, :] = xc_ref[...]
        x_s[n_ctx:, :] = xl_ref[...]
        acc_s[...] = jnp.zeros_like(acc_s)

    x = x_s[...]
    part = None
    for j in range(EXPERT_TF // EXPERT_SUB):
        cols = slice(j * EXPERT_SUB, (j + 1) * EXPERT_SUB)
        gate = _bdot(x, wg_ref[:, cols].astype(BF16))
        up = _bdot(x, wu_ref[:, cols].astype(BF16))
        hid = (gate * jax.nn.sigmoid(gate)) * up
        down = _bdot(hid.astype(BF16), wd_ref[cols, :].astype(BF16))
        part = down if part is None else part + down
    acc_s[...] += part

    @pl.when(f == N_FF_TILES - 1)
    def _():
        o_ref[:n_ctx, :] = (acc_s[:n_ctx, :] * gc_ref[:, :1]).astype(o_ref.dtype)
        o_ref[n_ctx:, :] = (acc_s[n_ctx:, :] * gl_ref[:, :1]).astype(o_ref.dtype)


def _experts(xg_ctx, xg_lat, g_ctx, g_lat, w_gate, w_up, w_down, layer):
    sc, sl = xg_ctx.shape[1], xg_lat.shape[1]
    slots = lambda s, w: pl.BlockSpec((None, s, w), lambda e, f: (e, 0, 0))
    return pl.pallas_call(
        _expert_kernel,
        grid=(N_EXPERTS, N_FF_TILES),
        in_specs=[
            slots(sc, D_MODEL), slots(sl, D_MODEL), slots(sc, LANES), slots(sl, LANES),
            pl.BlockSpec((None, None, D_MODEL, EXPERT_TF), lambda e, f: (layer, e, 0, f)),
            pl.BlockSpec((None, None, D_MODEL, EXPERT_TF), lambda e, f: (layer, e, 0, f)),
            pl.BlockSpec((None, None, EXPERT_TF, D_MODEL), lambda e, f: (layer, e, f, 0)),
        ],
        out_specs=slots(sc + sl, D_MODEL),
        out_shape=jax.ShapeDtypeStruct((N_EXPERTS, sc + sl, D_MODEL), BF16),
        scratch_shapes=[pltpu.VMEM((sc + sl, D_MODEL), BF16), pltpu.VMEM((sc + sl, D_MODEL), F32)],
        compiler_params=_params(2),
        name="experts",
    )(xg_ctx, xg_lat, g_ctx, g_lat, w_gate, w_up, w_down)


COMB_TM = 256
COMB_STEPS = T_LAT // COMB_TM
COMB_CTX_SETS = N_CTX_SETS // COMB_STEPS


def _combine_group(o_ref, rc_ref, x_ref, gt2_ref, ng_ref, norm_refs, out_refs, cap, sets, final):
    slots = N_EXPERTS * cap
    j = lax.broadcasted_iota(jnp.int32, (LANES, slots), 1)
    e = lax.broadcasted_iota(jnp.int32, (LANES, slots), 0)
    expand = jnp.where(j // cap == e, 1.0, 0.0).astype(BF16)
    slot = (lax.broadcasted_iota(jnp.int32, (1, slots), 1) % cap).astype(F32)
    for k in range(sets):
        rows = slice(k * COMB_TM, (k + 1) * COMB_TM)
        rank = _bdot(rc_ref[rows, :].astype(BF16), expand)
        pt = jnp.where(rank == slot, 1.0, 0.0).astype(BF16)
        ffn = _bdot(pt, o_ref[:, k * cap:(k + 1) * cap, :].reshape(slots, D_MODEL))
        x = x_ref[rows, :] + gt2_ref[...] * ffn
        if final:
            out_refs[0][rows, :] = _rms(x) * ng_ref[...]
        else:
            out_refs[0][rows, :] = x
            out_refs[1][rows, :] = _modnorm(x, ng_ref[...], norm_refs[0][...], norm_refs[1][...]).astype(BF16)


def _combine_kernel(*refs, final):
    n_in, n_out = (4, 1) if final else (6, 2)
    ng_ref = refs[2 * n_in]
    outs = refs[2 * n_in + 1:]
    for g, (cap, sets) in enumerate(((CAP_CTX, COMB_CTX_SETS), (CAP_LAT, 1))):
        o_ref, rc_ref, x_ref, gt2_ref, *norm_refs = refs[g * n_in:(g + 1) * n_in]
        _combine_group(o_ref, rc_ref, x_ref, gt2_ref, ng_ref, norm_refs, outs[g * n_out:(g + 1) * n_out], cap, sets,
                       final)


def _combine(out, rc_ctx, rc_lat, x1_ctx, x1_lat, mod, norm_g, layer, final):
    lat_tiles = LAT_LEN // COMB_TM
    lat_slot0 = N_CTX_SETS * CAP_CTX // CAP_LAT
    groups = (
        (rc_ctx, x1_ctx, lambda i: 0, COMB_CTX_SETS * CTX_LEN,
         pl.BlockSpec((N_EXPERTS, COMB_CTX_SETS * CAP_CTX, D_MODEL), lambda i: (0, i, 0))),
        (rc_lat, x1_lat, lambda i: 1 + i // lat_tiles, COMB_TM,
         pl.BlockSpec((N_EXPERTS, CAP_LAT, D_MODEL), lambda i: (0, lat_slot0 + i // lat_tiles, 0))),
    )
    in_specs, args, out_specs, out_shape = [], [], [], []
    for rc, x1, row_fn, tm, o_spec in groups:
        tile = lambda w, tm=tm: pl.BlockSpec((tm, w), lambda i: (i, 0))
        in_specs += [o_spec, tile(LANES), tile(D_MODEL), _mod_spec(layer, row_fn, MOD_GT2)]
        args += [out, rc, x1, mod]
        if not final:
            in_specs += [_mod_spec(layer + 1, row_fn, MOD_SH1), _mod_spec(layer + 1, row_fn, MOD_SC1)]
            args += [mod, mod]
        out_specs += [tile(D_MODEL)] * (1 if final else 2)
        out_shape += [jax.ShapeDtypeStruct(x1.shape, F32)] + ([] if final else [jax.ShapeDtypeStruct(x1.shape, BF16)])
    if final:
        in_specs.append(pl.BlockSpec((1, D_MODEL), lambda i: (0, 0)))
        args.append(norm_g.reshape(1, D_MODEL))
    else:
        in_specs.append(pl.BlockSpec((None, 1, D_MODEL), lambda i: (layer + 1, 0, 0)))
        args.append(norm_g)
    res = pl.pallas_call(
        functools.partial(_combine_kernel, final=final),
        grid=(COMB_STEPS,),
        in_specs=in_specs,
        out_specs=out_specs,
        out_shape=out_shape,
        compiler_params=_params(1),
        name="combine",
    )(*args)
    half = len(res) // 2
    return tuple(res[:half]), tuple(res[half:])


def kernel(x_prompt, x_sample, cache_attn_k, cache_attn_v, cache_na_k, cache_na_v, c, c_ctx, w_ada, b_ada,
           norm_mix, norm_ffn, w_in, q_norm, k_norm, rpb, w_branch_a, w_branch_b, w_out, w_router, w_gate,
           w_up, w_down, final_norm):
    x_ctx = x_prompt.reshape(T_CTX, D_MODEL)
    x_lat = x_sample.reshape(T_LAT, D_MODEL)
    mod = _modulation(c, c_ctx, w_ada, b_ada)
    norm_mix3 = norm_mix.reshape(DEPTH, 1, D_MODEL)
    norm_ffn3 = norm_ffn.reshape(DEPTH, 1, D_MODEL)
    w_router_t = jnp.swapaxes(w_router, 1, 2)
    ctx_row = lambda *g: 0

    h_ctx = _prenorm(x_ctx, mod, norm_mix3, 0, ctx_row)
    h_lat = _prenorm(x_lat, mod, norm_mix3, 0, _lat_mod_row(NORM_TM))
    layer_caches = []
    for layer in range(DEPTH):
        last = layer == DEPTH - 1
        yq, ykv, merge_w = _projection(h_ctx, h_lat, w_in, q_norm, k_norm, w_branch_a, w_branch_b, w_out, layer)
        yab_ctx, *new_caches = _ctx_attention(yq, ykv, layer_caches if last else [])
        layer_caches.append(tuple(new_caches))
        ya_lat = _lat_gqa_attention(yq, ykv, cache_attn_k, cache_attn_v, layer)
        yb_lat = _lat_na_attention(yq, ykv, cache_na_k, cache_na_v, rpb, layer)
        merge_args = (mod, norm_ffn3, merge_w, w_router_t, layer)
        x1_ctx, h2_ctx, lg_ctx = _merge([yab_ctx], h_ctx, x_ctx, *merge_args, ctx_row, MERGE_TM, MERGE_SUB)
        x1_lat, h2_lat, lg_lat, xg_ctx, g_ctx, rc_ctx = _merge(
            [ya_lat, yb_lat], h_lat, x_lat, *merge_args, _lat_mod_row(MERGE_ROUTE_TM), MERGE_ROUTE_TM,
            MERGE_ROUTE_SUB, route=(lg_ctx, h2_ctx))
        xg_lat, g_lat, rc_lat = _route(lg_lat, h2_lat, LAT_LEN, CAP_LAT, N_LAT_SETS)
        out = _experts(xg_ctx, xg_lat, g_ctx, g_lat, w_gate, w_up, w_down, layer)
        final = layer == DEPTH - 1
        norm_g = final_norm if final else norm_mix3
        res_ctx, res_lat = _combine(out, rc_ctx, rc_lat, x1_ctx, x1_lat, mod, norm_g, layer, final)
        if final:
            (y_ctx,), (y_lat,) = res_ctx, res_lat
        else:
            (x_ctx, h_ctx), (x_lat, h_lat) = res_ctx, res_lat

    heads = (N_KV_A, N_KV_A, N_HEADS_B, N_HEADS_B)
    new_caches = [a.reshape(N_CTX_SETS, DEPTH, CTX_LEN, h, HEAD_DIM) for a, h in zip(layer_caches[-1], heads)]
    return (y_ctx.reshape(N_CTX_SETS, CTX_LEN, D_MODEL), y_lat.reshape(N_LAT_SETS, LAT_LEN, D_MODEL), *new_caches)
```

```python
import functools

import numpy as np
import jax
import jax.numpy as jnp
from jax import lax
from jax.experimental import pallas as pl
from jax.experimental.pallas import tpu as pltpu

F32 = jnp.float32
BF16 = jnp.bfloat16

D_MODEL = 1024
N_CTX_SETS, CTX_LEN = 16, 256
N_LAT_SETS, LAT_LEN = 2, 1024
T_CTX = N_CTX_SETS * CTX_LEN
T_LAT = N_LAT_SETS * LAT_LEN
T_ALL = T_CTX + T_LAT
DEPTH = 2
PAST_LEN = 512
GRID_W = 64
GRID_ROWS = LAT_LEN // GRID_W
HEAD_DIM = 64
N_HEADS_A, N_KV_A, N_HEADS_B = 8, 2, 8
WIN_ROWS, WIN_COLS = 8, 16
N_EXPERTS = 16
EXPERT_FF = 2048
CAP_CTX = 2 * CTX_LEN // N_EXPERTS
CAP_LAT = 2 * LAT_LEN // N_EXPERTS
ROPE_THETA = 10000.0
EPS = 1e-6
NEG = -1e30
QKV_DIM = 2304
ATT_SCALE = HEAD_DIM ** -0.5

LANES = 128
VMEM_LIMIT = 56 * 1024 * 1024

PROJ_TN = 256
N_PROJ_TILES = QKV_DIM // PROJ_TN
PROJ_ORDER = np.array([0, 1, 3, 4, 5, 6, 7, 8, 2], np.int32)
N_Q_TILES = 4
CAST_STEPS = 8
MOD_SH1, MOD_SC1, MOD_GT1, MOD_SH2, MOD_SC2, MOD_GT2 = range(6)


def _params(n_grid_dims, vmem=VMEM_LIMIT):
    return pltpu.CompilerParams(dimension_semantics=("arbitrary",) * n_grid_dims, vmem_limit_bytes=vmem)


def _bdot(a, b):
    return jnp.dot(a, b, preferred_element_type=F32)


def _bdot_nt(a, b):
    return lax.dot_general(a, b, (((1,), (1,)), ((), ())), preferred_element_type=F32)


def _split2(x):
    hi = x.astype(BF16)
    lo = (x - hi.astype(F32)).astype(BF16)
    return hi, lo


def _split3(x):
    hi = x.astype(BF16)
    r = x - hi.astype(F32)
    mid = r.astype(BF16)
    lo = (r - mid.astype(F32)).astype(BF16)
    return hi, mid, lo


def _rms(x):
    return x * lax.rsqrt(jnp.mean(x * x, axis=-1, keepdims=True) + EPS)


def _modnorm(x, g, sh, sc):
    return (_rms(x) * g) * (1.0 + sc) + sh


def _lat_mod_row(tile_rows):
    return lambda i: 1 + (i * tile_rows) // LAT_LEN


def _mod_spec(layer, row_fn, chunk):
    return pl.BlockSpec((None, None, 1, D_MODEL), lambda *g: (layer, row_fn(*g), 0, chunk))


MOD_TN = 1536


def _mod_kernel(ct_ref, w_ref, b_ref, o_ref):
    ct = ct_ref[...]
    act = ct * jax.nn.sigmoid(ct)
    w = w_ref[...]
    for m in range(3):
        o_ref[m:m + 1, :] = jnp.sum(w * act[:, m:m + 1], axis=0, keepdims=True) + b_ref[...]
    o_ref[3:8, :] = jnp.zeros((5, MOD_TN), F32)


def _modulation(c, c_ctx, w_ada, b_ada):
    cond = jnp.concatenate([c_ctx[None, :], c, jnp.zeros((5, D_MODEL), F32)], axis=0)
    mod = pl.pallas_call(
        _mod_kernel,
        grid=(DEPTH, 6 * D_MODEL // MOD_TN),
        in_specs=[
            pl.BlockSpec((D_MODEL, 8), lambda l, j: (0, 0)),
            pl.BlockSpec((None, D_MODEL, MOD_TN), lambda l, j: (l, 0, j)),
            pl.BlockSpec((None, 1, MOD_TN), lambda l, j: (l, 0, j)),
        ],
        out_specs=pl.BlockSpec((None, 8, MOD_TN), lambda l, j: (l, 0, j)),
        out_shape=jax.ShapeDtypeStruct((DEPTH, 8, 6 * D_MODEL), F32),
        compiler_params=_params(2),
        name="modulation",
    )(cond.T, w_ada, b_ada.reshape(DEPTH, 1, 6 * D_MODEL))
    return mod.reshape(DEPTH, 8, 1, 6 * D_MODEL)


NORM_TM = 1024


def _prenorm_kernel(x_ref, g_ref, sh_ref, sc_ref, h_ref):
    h_ref[...] = _modnorm(x_ref[...], g_ref[...], sh_ref[...], sc_ref[...]).astype(BF16)


def _prenorm(x, mod, norm_g, layer, row_fn):
    n = x.shape[0]
    return pl.pallas_call(
        _prenorm_kernel,
        grid=(n // NORM_TM,),
        in_specs=[
            pl.BlockSpec((NORM_TM, D_MODEL), lambda i: (i, 0)),
            pl.BlockSpec((None, 1, D_MODEL), lambda i: (layer, 0, 0)),
            _mod_spec(layer, row_fn, MOD_SH1),
            _mod_spec(layer, row_fn, MOD_SC1),
        ],
        out_specs=pl.BlockSpec((NORM_TM, D_MODEL), lambda i: (i, 0)),
        out_shape=jax.ShapeDtypeStruct((n, D_MODEL), BF16),
        compiler_params=_params(1),
        name="prenorm",
    )(x, norm_g, mod, mod)


PROJ_CH = 1024


def _rope_tables():
    t = np.arange(LAT_LEN)
    lane = np.arange(LANES) % HEAD_DIM
    pos = np.where(lane < HEAD_DIM // 2, (t // GRID_W)[:, None], (t % GRID_W)[:, None]).astype(np.float64)
    freq = ROPE_THETA ** (-(lane % 16).astype(np.float64) / 16.0)
    ang = pos * freq[None, :]
    sign = np.where((lane & 16) == 0, -1.0, 1.0)[None, :]
    return np.cos(ang).astype(np.float32), (np.sin(ang) * sign).astype(np.float32)


def _head_norm_rope(y, gain, cos, sin):
    w = y.shape[1]
    r = (lax.broadcasted_iota(jnp.int32, (2 * w, w), 0) % w) // HEAD_DIM
    c = lax.broadcasted_iota(jnp.int32, (2 * w, w), 1) // HEAD_DIM
    seg = jnp.where(r == c, 1.0 / HEAD_DIM, 0.0).astype(BF16)
    ms = _bdot(jnp.concatenate(_split2(y * y), axis=1), seg)
    yn = y * lax.rsqrt(ms + EPS) * gain
    if cos is None:
        return yn
    lane = lax.broadcasted_iota(jnp.int32, yn.shape, 1)
    partner = jnp.where((lane & 16) == 0, pltpu.roll(yn, w - 16, 1), pltpu.roll(yn, 16, 1))
    if w > LANES:
        cos = jnp.concatenate([cos] * (w // LANES), axis=1)
        sin = jnp.concatenate([sin] * (w // LANES), axis=1)
    return yn * cos + partner * sin


def _proj_kernel(order_ref, hc_ref, hl_ref, w_ref, gain_ref, cos_ref, sin_ref, wa_ref, wbr_ref, wo_ref, wg_ref,
                 yq_ref, ykv_ref, oa_ref, ob_ref, oo_ref, og_ref, wb_ref, acc_ref, h_s, h_sem):
    del order_ref
    s = pl.program_id(0)
    n_ctx_chunks = T_CTX // PROJ_CH

    def h_copy(k):
        src = hc_ref.at[pl.ds(k * PROJ_CH, PROJ_CH), :] if k < n_ctx_chunks else \
            hl_ref.at[pl.ds((k - n_ctx_chunks) * PROJ_CH, PROJ_CH), :]
        return pltpu.make_async_copy(src, h_s.at[pl.ds(k * PROJ_CH, PROJ_CH), :], h_sem.at[k])

    @pl.when(s == 0)
    def _():
        for k in range(T_ALL // PROJ_CH):
            h_copy(k).start()

    wb_ref[...] = w_ref[...].astype(BF16)
    oa_ref[...] = wa_ref[...].astype(BF16)
    ob_ref[...] = wbr_ref[...].astype(BF16)
    oo_ref[...] = wo_ref[...].astype(BF16)
    og_ref[...] = wg_ref[0].astype(BF16)
    chunks = [(k, None, k * PROJ_CH, k >= n_ctx_chunks) for k in range(T_ALL // PROJ_CH)]

    def matmul(chunk, dst, first_step=False):
        k, _, o0, _ = chunk
        if first_step:
            h_copy(k).wait()
        dst[o0:o0 + PROJ_CH, :] = _bdot(h_s[o0:o0 + PROJ_CH, :], wb_ref[...]).astype(dst.dtype)

    def finish(chunk, width, src, dst):
        _, _, o0, is_lat = chunk
        cos, sin = (cos_ref[...], sin_ref[...]) if is_lat else (None, None)
        y = _head_norm_rope(src[o0:o0 + PROJ_CH, :width], gain_ref[:, :width], cos, sin)
        dst[o0:o0 + PROJ_CH, :width] = y.astype(dst.dtype)

    def tile(width, work, dst, first_step=False):
        matmul(chunks[0], work, first_step)
        for k in range(1, len(chunks)):
            matmul(chunks[k], work, first_step)
            if width:
                finish(chunks[k - 1], width, work, dst)
        if width:
            finish(chunks[-1], width, work, dst)

    pl.when(s == 0)(lambda: tile(PROJ_TN, acc_ref, yq_ref, first_step=True))
    pl.when(s == 1)(lambda: tile(PROJ_TN, acc_ref, yq_ref))
    pl.when((s >= 2) & (s < N_Q_TILES))(lambda: tile(0, yq_ref, yq_ref))
    pl.when((s >= N_Q_TILES) & (s < N_PROJ_TILES - 1))(lambda: tile(0, ykv_ref, ykv_ref))
    pl.when(s == N_PROJ_TILES - 1)(lambda: tile(LANES, ykv_ref, ykv_ref))


def _projection(h_ctx, h_lat, w_in, q_norm, k_norm, w_ba, w_bb, w_out, layer):
    ones = jnp.ones((2 * HEAD_DIM,), F32)
    gain = jnp.stack([jnp.tile(q_norm[layer], 4), jnp.tile(q_norm[layer], 4),
                      jnp.concatenate([jnp.tile(k_norm[layer], 2), ones])])[:, None, :]
    cos, sin = _rope_tables()
    half = N_HEADS_A * HEAD_DIM
    ra, ro = half // CAST_STEPS, D_MODEL // CAST_STEPS
    part = lambda j: jnp.minimum(j, CAST_STEPS - 1)
    gain_row = lambda j: jnp.where(j == N_PROJ_TILES - 1, 2, jnp.minimum(j, 1))
    grid_spec = pltpu.PrefetchScalarGridSpec(
        num_scalar_prefetch=1,
        grid=(N_PROJ_TILES,),
        in_specs=[
            pl.BlockSpec(memory_space=pl.ANY),
            pl.BlockSpec(memory_space=pl.ANY),
            pl.BlockSpec((None, D_MODEL, PROJ_TN), lambda j, p: (layer, 0, p[j])),
            pl.BlockSpec((None, 1, PROJ_TN), lambda j, p: (gain_row(j), 0, 0)),
            pl.BlockSpec((LAT_LEN, LANES), lambda j, p: (0, 0)),
            pl.BlockSpec((LAT_LEN, LANES), lambda j, p: (0, 0)),
            pl.BlockSpec((None, ra, D_MODEL), lambda j, p: (layer, part(j), 0)),
            pl.BlockSpec((None, ra, D_MODEL), lambda j, p: (layer, part(j), 0)),
            pl.BlockSpec((None, ro, D_MODEL), lambda j, p: (layer, part(j), 0)),
            pl.BlockSpec((pl.Element(1), pl.Element(ro), pl.Element(2 * D_MODEL)),
                         lambda j, p: (layer, part(j) * ro, QKV_DIM)),
        ],
        out_specs=[
            pl.BlockSpec((T_ALL, PROJ_TN), lambda j, p: (0, jnp.minimum(j, N_Q_TILES - 1))),
            pl.BlockSpec((T_ALL, PROJ_TN), lambda j, p: (0, jnp.maximum(j - N_Q_TILES, 0))),
            pl.BlockSpec((ra, D_MODEL), lambda j, p: (part(j), 0)),
            pl.BlockSpec((ra, D_MODEL), lambda j, p: (part(j), 0)),
            pl.BlockSpec((ro, D_MODEL), lambda j, p: (part(j), 0)),
            pl.BlockSpec((ro, 2 * D_MODEL), lambda j, p: (part(j), 0)),
        ],
        scratch_shapes=[pltpu.VMEM((D_MODEL, PROJ_TN), BF16), pltpu.VMEM((T_ALL, PROJ_TN), F32),
                        pltpu.VMEM((T_ALL, D_MODEL), BF16), pltpu.SemaphoreType.DMA((T_ALL // PROJ_CH,))],
    )
    yq, ykv, *merge_w = pl.pallas_call(
        _proj_kernel,
        grid_spec=grid_spec,
        out_shape=[jax.ShapeDtypeStruct((T_ALL, N_Q_TILES * PROJ_TN), BF16),
                   jax.ShapeDtypeStruct((T_ALL, QKV_DIM - N_Q_TILES * PROJ_TN), F32),
                   jax.ShapeDtypeStruct((half, D_MODEL), BF16), jax.ShapeDtypeStruct((half, D_MODEL), BF16),
                   jax.ShapeDtypeStruct((D_MODEL, D_MODEL), BF16), jax.ShapeDtypeStruct((D_MODEL, 2 * D_MODEL), BF16)],
        compiler_params=_params(1),
        name="projection",
    )(jnp.asarray(PROJ_ORDER), h_ctx, h_lat, w_in, gain, jnp.asarray(cos), jnp.asarray(sin), w_ba, w_bb, w_out, w_in)
    return yq, ykv, merge_w


COL_QA, COL_QB = 0, 512
COL_KB, COL_VB, COL_KA = 0, 512, 1024


def _lane_is_low(shape):
    return lax.broadcasted_iota(jnp.int32, shape, 1) < HEAD_DIM


def _pair_halves(x):
    low = _lane_is_low(x.shape)
    xb = x.astype(BF16)
    zero = jnp.zeros_like(xb)
    return jnp.where(low, xb, zero), jnp.where(low, zero, xb)


def _scaled_q(q):
    assert ATT_SCALE == 0.125
    return (q * ATT_SCALE).astype(BF16)


def _attend_pair(q, keys, values, biases, joint_pv=True):
    probs, dens, out = [], [], None
    for h in range(2):
        scores = []
        for k, b in zip(keys[h], biases[h]):
            s = _bdot_nt(q, k)
            scores.append(s if b is None else s + b)
        m = scores[0].max(axis=-1, keepdims=True)
        for s in scores[1:]:
            m = jnp.maximum(m, s.max(axis=-1, keepdims=True))
        den, num = None, None
        for s, v in zip(scores, values[h]):
            e = jnp.exp(s - m)
            d = e.sum(axis=-1, keepdims=True)
            den = d if den is None else den + d
            if joint_pv:
                probs.append(e.astype(BF16))
            else:
                o = _bdot(e.astype(BF16), v)
                num = o if num is None else num + o
        dens.append(den)
        if not joint_pv:
            out = num / den if out is None else out + num / den
    if not joint_pv:
        return out
    num = _bdot(jnp.concatenate(probs, axis=1), jnp.concatenate(values[0] + values[1], axis=0))
    return num / jnp.where(_lane_is_low(num.shape), dens[0], dens[1])


def _attend_pair_t(q, keys, values_t, biases_t):
    probs, dens = [], []
    for h in range(2):
        scores = []
        for k, b in zip(keys[h], biases_t[h]):
            s = _bdot_nt(k, q)
            scores.append(s if b is None else s + b)
        m = scores[0].max(axis=0, keepdims=True)
        for s in scores[1:]:
            m = jnp.maximum(m, s.max(axis=0, keepdims=True))
        den = None
        for s in scores:
            e = jnp.exp(s - m)
            d = e.sum(axis=0, keepdims=True)
            den = d if den is None else den + d
            probs.append(e.astype(BF16))
        dens.append(den)
    num_t = _bdot(values_t, jnp.concatenate(probs, axis=0))
    row = lax.broadcasted_iota(jnp.int32, num_t.shape, 0)
    return (num_t / jnp.where(row < HEAD_DIM, dens[0], dens[1])).T


def _gqa_variants(x):
    lo, hi = _pair_halves(x)
    sw_lo, sw_hi = _pair_halves(pltpu.roll(x, HEAD_DIM, 1))
    return [lo, sw_hi, sw_lo, hi]


def _gqa_attention(q_ref, kvar, vvar, o_ref, joint_pv):
    for pair in range(N_HEADS_A // 2):
        q = _scaled_q(q_ref[:, pair * LANES:(pair + 1) * LANES])
        kvh = (2 * pair) // (N_HEADS_A // N_KV_A)
        out = _attend_pair(q, [[kvar[2 * kvh]], [kvar[2 * kvh + 1]]], [[vvar[2 * kvh]], [vvar[2 * kvh + 1]]],
                           [[None], [None]], joint_pv)
        o_ref[:, pair * LANES:(pair + 1) * LANES] = out.astype(o_ref.dtype)


def _ctx_attn_kernel(qa_ref, qb_ref, kb_ref, vb_ref, kava_ref, *refs, n_prev):
    prev = refs[:4 * n_prev]
    y_ref, nak_ref, nav_ref, nbk_ref, nbv_ref = refs[4 * n_prev:]
    ka = kava_ref[:, :LANES]
    va = kava_ref[:, LANES:]
    new = (ka, va, kb_ref[...], vb_ref[...])
    for c, (o_ref, val) in enumerate(zip((nak_ref, nav_ref, nbk_ref, nbv_ref), new)):
        if n_prev:
            for p in range(n_prev):
                o_ref[p] = prev[4 * p + c][...]
            o_ref[n_prev] = val
        else:
            o_ref[...] = val
    _gqa_attention(qa_ref, _gqa_variants(ka), _gqa_variants(va), y_ref.at[:, :N_HEADS_A * HEAD_DIM], joint_pv=False)
    for pair in range(N_HEADS_B // 2):
        cols = slice(pair * LANES, (pair + 1) * LANES)
        q = _scaled_q(qb_ref[:, cols])
        k_lo, k_hi = _pair_halves(kb_ref[:, cols])
        v_lo, v_hi = _pair_halves(vb_ref[:, cols])
        out = _attend_pair(q, [[k_lo], [k_hi]], [[v_lo], [v_hi]], [[None], [None]], joint_pv=False)
        y_ref[:, N_HEADS_A * HEAD_DIM + pair * LANES:N_HEADS_A * HEAD_DIM + (pair + 1) * LANES] = out.astype(BF16)


def _ctx_attention(yq, ykv, prev_caches):
    wide = 4 * LANES
    widths = (LANES, LANES, wide, wide)
    n_prev = len(prev_caches)
    row = lambda b: (b, 0)
    f32 = lambda *s: jax.ShapeDtypeStruct(s, F32)
    in_specs = [
        pl.BlockSpec((CTX_LEN, wide), lambda b: (b, COL_QA // wide)),
        pl.BlockSpec((CTX_LEN, wide), lambda b: (b, COL_QB // wide)),
        pl.BlockSpec((CTX_LEN, wide), lambda b: (b, COL_KB // wide)),
        pl.BlockSpec((CTX_LEN, wide), lambda b: (b, COL_VB // wide)),
        pl.BlockSpec((CTX_LEN, 2 * LANES), lambda b: (b, COL_KA // (2 * LANES))),
    ]
    args = [yq, yq, ykv, ykv, ykv]
    for layer_caches in prev_caches:
        in_specs += [pl.BlockSpec((CTX_LEN, w), row) for w in widths]
        args += list(layer_caches)
    if n_prev:
        cache_specs = [pl.BlockSpec((None, n_prev + 1, CTX_LEN, w), lambda b: (b, 0, 0, 0)) for w in widths]
        cache_shapes = [f32(N_CTX_SETS, n_prev + 1, CTX_LEN, w) for w in widths]
    else:
        cache_specs = [pl.BlockSpec((CTX_LEN, w), row) for w in widths]
        cache_shapes = [f32(T_CTX, w) for w in widths]
    return pl.pallas_call(
        functools.partial(_ctx_attn_kernel, n_prev=n_prev),
        grid=(N_CTX_SETS,),
        in_specs=in_specs,
        out_specs=[pl.BlockSpec((CTX_LEN, 2 * wide), row)] + cache_specs,
        out_shape=[jax.ShapeDtypeStruct((T_CTX, 2 * wide), BF16)] + cache_shapes,
        compiler_params=_params(1),
        name="ctx_attention",
    )(*args)


LAT_TQ = 512
LAT_QT = LAT_LEN // LAT_TQ
GQA_TQ = 1024


def _lat_gqa_kernel(qa_ref, kava_ref, ck_ref, cv_ref, o_ref, k_s, vt_s):
    n_keys = PAST_LEN + LAT_LEN

    @pl.when(pl.program_id(1) == 0)
    def _():
        for i, (c, x) in enumerate(zip(_gqa_variants(ck_ref[...]), _gqa_variants(kava_ref[:, :LANES]))):
            k_s[i, :PAST_LEN, :] = c
            k_s[i, PAST_LEN:, :] = x
        for src, cols in ((cv_ref[...], slice(0, PAST_LEN)), (kava_ref[:, LANES:], slice(PAST_LEN, n_keys))):
            vt = src.T
            swapped = pltpu.roll(vt, HEAD_DIM, 0)
            top = lax.broadcasted_iota(jnp.int32, vt.shape, 0) < HEAD_DIM
            zero = jnp.zeros_like(vt)
            per_head = ((jnp.where(top, vt, zero), jnp.where(top, zero, swapped)),
                        (jnp.where(top, swapped, zero), jnp.where(top, zero, vt)))
            for kvh in range(N_KV_A):
                for half in range(2):
                    off = half * n_keys
                    vt_s[kvh, :, off + cols.start:off + cols.stop] = per_head[kvh][half].astype(BF16)

    for pair in range(N_HEADS_A // 2):
        q = _scaled_q(qa_ref[:, pair * LANES:(pair + 1) * LANES])
        kvh = (2 * pair) // (N_HEADS_A // N_KV_A)
        out = _attend_pair_t(q, [[k_s[2 * kvh]], [k_s[2 * kvh + 1]]], vt_s[kvh], [[None], [None]])
        o_ref[:, pair * LANES:(pair + 1) * LANES] = out.astype(o_ref.dtype)


def _lat_gqa_attention(yq, ykv, cache_k, cache_v, layer):
    wide = 4 * LANES
    tiles = LAT_LEN // GQA_TQ
    first = T_CTX // GQA_TQ
    cache = pl.BlockSpec((None, None, PAST_LEN, LANES), lambda b, t: (b, layer, 0, 0))
    return pl.pallas_call(
        _lat_gqa_kernel,
        grid=(N_LAT_SETS, tiles),
        in_specs=[
            pl.BlockSpec((GQA_TQ, wide), lambda b, t: (first + b * tiles + t, 0)),
            pl.BlockSpec((LAT_LEN, 2 * LANES), lambda b, t: (T_CTX // LAT_LEN + b, COL_KA // (2 * LANES))),
            cache, cache,
        ],
        out_specs=pl.BlockSpec((GQA_TQ, wide), lambda b, t: (b * tiles + t, 0)),
        out_shape=jax.ShapeDtypeStruct((T_LAT, wide), BF16),
        scratch_shapes=[pltpu.VMEM((4, PAST_LEN + LAT_LEN, LANES), BF16),
                        pltpu.VMEM((N_KV_A, LANES, 2 * (PAST_LEN + LAT_LEN)), BF16)],
        compiler_params=_params(2),
        name="lat_gqa_attention",
    )(yq, ykv, cache_k.reshape(N_LAT_SETS, DEPTH, PAST_LEN, LANES), cache_v.reshape(N_LAT_SETS, DEPTH, PAST_LEN, LANES))


N_DR = 2 * WIN_ROWS - 1
N_DC = 2 * WIN_COLS - 1
ROWS_PER_TQ = LAT_TQ // GRID_W
NA_LOCAL = 768
NA_KEY_STEP = 2 * LANES
NA_LAST_BASE = (LAT_LEN - NA_LOCAL) // NA_KEY_STEP
NA_TOE_ROWS = 32
NA_PAIRS = 4


def _na_key_base(qt):
    return jnp.where(qt < LAT_QT // 2, 0, NA_LAST_BASE)


def _window_mask():
    r = np.arange(GRID_ROWS)
    row_start = np.clip(r - WIN_ROWS // 2, 0, GRID_ROWS - WIN_ROWS)
    in_rows = (r[None, :] >= row_start[:, None]) & (r[None, :] < row_start[:, None] + WIN_ROWS)
    cq = np.arange(GRID_W)
    col_start = np.clip(cq - WIN_COLS // 2, 0, GRID_W - WIN_COLS)
    in_cols = (cq[None, :] >= col_start[:, None]) & (cq[None, :] < col_start[:, None] + WIN_COLS)
    valid = (in_rows[:, None, :, None] & in_cols[None, :, None, :]).reshape(LAT_LEN, LAT_LEN)
    tiles = []
    for qt in range(LAT_QT):
        base = (0 if qt < LAT_QT // 2 else NA_LAST_BASE) * NA_KEY_STEP
        tile = valid[qt * LAT_TQ:(qt + 1) * LAT_TQ]
        assert not tile[:, :base].any() and not tile[:, base + NA_LOCAL:].any()
        tiles.append(tile[:, base:base + NA_LOCAL])
    return np.where(np.stack(tiles), 0.0, NEG).astype(np.float32)


def _toeplitz_select():
    j = np.arange(LANES)
    c = np.clip(j - (GRID_W - 1), -(WIN_COLS - 1), WIN_COLS - 1) + (WIN_COLS - 1)
    return (np.arange(LANES)[:, None] == c[None, :]).astype(np.float32)


def _na_kernel(q_ref, k_ref, v_ref, ck_ref, cv_ref, mask_ref, rpb_ref, sel_ref, o_ref,
               toe_ref, bias_ref, k_s, v_s, ck_s, cv_s):
    b, qt = pl.program_id(1), pl.program_id(2)
    n_heads = 2 * NA_PAIRS

    @pl.when((qt == 0) & (b == 0))
    def _():
        low = lax.broadcasted_iota(jnp.int32, (GRID_W, LANES), 1) < GRID_W
        sel = sel_ref[...].astype(BF16)
        for hd in range(n_heads):
            gen = sum(_bdot(piece, sel) for piece in _split3(rpb_ref[hd]))
            for dd in range(N_DR + 1):
                lo = jnp.broadcast_to(gen[dd:dd + 1, :], (GRID_W, LANES))
                hi = jnp.broadcast_to(gen[dd + 1:dd + 2, :], (GRID_W, LANES))
                lo = pltpu.roll(lo, LANES - (GRID_W - 1), 1, stride=1, stride_axis=0)
                hi = pltpu.roll(hi, 1, 1, stride=1, stride_axis=0)
                toe_ref[hd, dd] = jnp.where(low, lo, hi)

    @pl.when(qt == 0)
    def _():
        for src, dst in ((k_ref, k_s), (v_ref, v_s), (ck_ref, ck_s), (cv_ref, cv_s)):
            for pair in range(NA_PAIRS):
                lo, hi = _pair_halves(src[:, pair * LANES:(pair + 1) * LANES])
                dst[2 * pair] = lo
                dst[2 * pair + 1] = hi

    base = _na_key_base(qt)
    local = pl.ds(pl.multiple_of(base * NA_KEY_STEP, NA_KEY_STEP), NA_LOCAL)
    for pair in range(NA_PAIRS):
        q = _scaled_q(q_ref[:, pair * LANES:(pair + 1) * LANES])
        heads = (2 * pair, 2 * pair + 1)
        biases = []
        for hd in heads:
            for rr in range(ROWS_PER_TQ):
                for kp in range(NA_LOCAL // LANES):
                    d = 2 * (kp + base * (NA_KEY_STEP // LANES)) - (qt * ROWS_PER_TQ + rr) + (WIN_ROWS - 1)
                    dd = jnp.clip(d, -1, N_DR - 1) + 1
                    bias_ref[hd, rr * GRID_W:(rr + 1) * GRID_W, kp * LANES:(kp + 1) * LANES] = toe_ref[hd, dd]
            biases.append([bias_ref[hd] + mask_ref[...], None])
        out = _attend_pair(q, [[k_s[hd, local, :], ck_s[hd]] for hd in heads],
                           [[v_s[hd, local, :], cv_s[hd]] for hd in heads], biases)
        o_ref[:, pair * LANES:(pair + 1) * LANES] = out.astype(o_ref.dtype)


def _lat_na_attention(yq, ykv, cache_k, cache_v, rpb, layer):
    first = T_CTX // LAT_TQ
    kv_row = T_CTX // LAT_LEN
    wide = N_HEADS_B * HEAD_DIM
    gen = jnp.pad(rpb[layer], ((0, 0), (1, NA_TOE_ROWS - N_DR - 1), (0, LANES - N_DC)))
    w = NA_PAIRS * LANES
    nh = 2 * NA_PAIRS
    cq, ck, cv = COL_QB // w, COL_KB // w, COL_VB // w
    return pl.pallas_call(
        _na_kernel,
        grid=(N_HEADS_B // nh, N_LAT_SETS, LAT_QT),
        in_specs=[
            pl.BlockSpec((LAT_TQ, w), lambda g, b, t: (first + b * LAT_QT + t, cq + g)),
            pl.BlockSpec((LAT_LEN, w), lambda g, b, t: (kv_row + b, ck + g)),
            pl.BlockSpec((LAT_LEN, w), lambda g, b, t: (kv_row + b, cv + g)),
            pl.BlockSpec((None, None, PAST_LEN, w), lambda g, b, t: (b, layer, 0, g)),
            pl.BlockSpec((None, None, PAST_LEN, w), lambda g, b, t: (b, layer, 0, g)),
            pl.BlockSpec((None, LAT_TQ, NA_LOCAL), lambda g, b, t: (t, 0, 0)),
            pl.BlockSpec((nh, NA_TOE_ROWS, LANES), lambda g, b, t: (g, 0, 0)),
            pl.BlockSpec((LANES, LANES), lambda g, b, t: (0, 0)),
        ],
        out_specs=pl.BlockSpec((LAT_TQ, w), lambda g, b, t: (b * LAT_QT + t, g)),
        out_shape=jax.ShapeDtypeStruct((T_LAT, wide), BF16),
        scratch_shapes=[pltpu.VMEM((nh, N_DR + 1, GRID_W, LANES), F32), pltpu.VMEM((nh, LAT_TQ, NA_LOCAL), F32),
                        pltpu.VMEM((nh, LAT_LEN, LANES), BF16), pltpu.VMEM((nh, LAT_LEN, LANES), BF16),
                        pltpu.VMEM((nh, PAST_LEN, LANES), BF16), pltpu.VMEM((nh, PAST_LEN, LANES), BF16)],
        compiler_params=_params(3),
        name="lat_na_attention",
    )(yq, ykv, ykv, cache_k.reshape(N_LAT_SETS, DEPTH, PAST_LEN, wide), cache_v.reshape(N_LAT_SETS, DEPTH, PAST_LEN, wide),
      jnp.asarray(_window_mask()), gen, jnp.asarray(_toeplitz_select()))


MERGE_TM = 1024
MERGE_SUB = 1024
MERGE_ROUTE_TM = 512
MERGE_ROUTE_SUB = 512


def _merge_kernel(*refs, n_y, route_sets, sub):
    y_refs, refs = refs[:n_y], refs[n_y:]
    (h_ref, x_ref, gt1_ref, sh2_ref, sc2_ref, nf_ref, wab, wbb, wob, wgb, wr_ref), refs = refs[:11], refs[11:]
    if route_sets:
        (rlg_ref, rh_ref), refs = refs[:2], refs[2:]
    (x1_ref, h2_ref, lg_ref), refs = refs[:3], refs[3:]
    if route_sets:
        xg_ref, g_ref, rc_ref, p_s, rt_s = refs

    wr_hi, wr_lo = _split2(wr_ref[...])
    wr_both = jnp.concatenate([wr_hi, wr_lo], axis=0)
    half = N_HEADS_A * HEAD_DIM
    n_sub = x_ref.shape[0] // sub
    for r in range(n_sub):
        rows = slice(r * sub, (r + 1) * sub)
        if n_y == 1:
            ya, yb = y_refs[0][rows, :half], y_refs[0][rows, half:]
        else:
            ya, yb = y_refs[0][rows, :], y_refs[1][rows, :]
        sets = list(range(r * route_sets // n_sub, (r + 1) * route_sets // n_sub))

        def route(k):
            tok = slice(k * CTX_LEN, (k + 1) * CTX_LEN)
            slots = slice(k * CAP_CTX, (k + 1) * CAP_CTX)
            _route_set(rlg_ref[:, tok], rh_ref[tok, :], xg_ref.at[:, slots, :], g_ref.at[:, slots, :],
                       rc_ref.at[tok, :], p_s.at[k], rt_s.at[k], CTX_LEN, CAP_CTX)

        gates = _bdot(h_ref[rows, :], wgb[...])
        za = _bdot(ya, wab[...])
        zb = _bdot(yb, wbb[...])
        for k in sets[:len(sets) // 2]:
            route(k)
        m = jax.nn.sigmoid(gates[:, :D_MODEL]) * za + jax.nn.sigmoid(gates[:, D_MODEL:]) * zb
        x1 = x_ref[rows, :] + gt1_ref[...] * _bdot(m.astype(BF16), wob[...])
        x1_ref[rows, :] = x1
        h2 = _modnorm(x1, nf_ref[...], sh2_ref[...], sc2_ref[...])
        h2_ref[rows, :] = h2.astype(BF16)
        hh, hl = _split2(h2)
        both = _bdot_nt(wr_both, hh)
        lg_ref[:, rows] = both[:N_EXPERTS, :] + both[N_EXPERTS:, :] + _bdot_nt(wr_hi, hl)
        for k in sets[len(sets) // 2:]:
            route(k)


def _merge(y_parts, h, x, mod, norm_ffn, merge_w, w_router_t, layer, row_fn, tm, sub, route=None):
    n = x.shape[0]
    tiles = n // tm
    once = pl.Buffered(1)
    weight = lambda w: pl.BlockSpec(w.shape, lambda i: (0, 0), pipeline_mode=once)
    tile = lambda w: pl.BlockSpec((tm, w), lambda i: (i, 0))
    in_specs = [tile(p.shape[1]) for p in y_parts] + [
        tile(D_MODEL), tile(D_MODEL),
        _mod_spec(layer, row_fn, MOD_GT1), _mod_spec(layer, row_fn, MOD_SH2), _mod_spec(layer, row_fn, MOD_SC2),
        pl.BlockSpec((None, 1, D_MODEL), lambda i: (layer, 0, 0)),
    ] + [weight(w) for w in merge_w] + [
        pl.BlockSpec((None, N_EXPERTS, D_MODEL), lambda i: (layer, 0, 0), pipeline_mode=once)]
    args = list(y_parts) + [h, x, mod, mod, mod, norm_ffn, *merge_w, w_router_t]
    out_specs = [tile(D_MODEL), tile(D_MODEL), pl.BlockSpec((N_EXPERTS, tm), lambda i: (0, i))]
    out_shape = [jax.ShapeDtypeStruct((n, D_MODEL), F32), jax.ShapeDtypeStruct((n, D_MODEL), BF16),
                 jax.ShapeDtypeStruct((N_EXPERTS, n), F32)]
    scratch = []
    route_sets = 0
    if route is not None:
        route_sets = N_CTX_SETS // tiles
        rows, slots = route_sets * CTX_LEN, route_sets * CAP_CTX
        in_specs += [pl.BlockSpec((N_EXPERTS, rows), lambda i: (0, i)), pl.BlockSpec((rows, D_MODEL), lambda i: (i, 0))]
        args += list(route)
        out_specs += [pl.BlockSpec((N_EXPERTS, slots, D_MODEL), lambda i: (0, i, 0)),
                      pl.BlockSpec((N_EXPERTS, slots, LANES), lambda i: (0, i, 0)),
                      pl.BlockSpec((rows, LANES), lambda i: (i, 0))]
        out_shape += [jax.ShapeDtypeStruct((N_EXPERTS, N_CTX_SETS * CAP_CTX, D_MODEL), BF16),
                      jax.ShapeDtypeStruct((N_EXPERTS, N_CTX_SETS * CAP_CTX, LANES), F32),
                      jax.ShapeDtypeStruct((T_CTX, LANES), F32)]
        scratch += [pltpu.VMEM((route_sets, N_EXPERTS * CAP_CTX, CTX_LEN), BF16),
                    pltpu.VMEM((route_sets, LANES, CTX_LEN), F32)]
    return pl.pallas_call(
        functools.partial(_merge_kernel, n_y=len(y_parts), route_sets=route_sets, sub=sub),
        grid=(tiles,),
        in_specs=in_specs,
        out_specs=out_specs,
        out_shape=out_shape,
        scratch_shapes=scratch,
        compiler_params=_params(1),
        name="merge_route" if route_sets else "merge",
    )(*args)


GATHER_M = 512
RANK_TILE = 128


def _rank_row(aff, a_row, e, n):
    tiles = n // RANK_TILE
    sub = lax.broadcasted_iota(jnp.int32, (RANK_TILE, RANK_TILE), 0)
    lane = lax.broadcasted_iota(jnp.int32, (RANK_TILE, RANK_TILE), 1)
    earlier = jnp.where(sub < lane, 1.0, 0.0)
    acc = [jnp.zeros((8, RANK_TILE), F32) for _ in range(tiles)]
    for c in range(tiles):
        a_col = jnp.broadcast_to(aff[c * RANK_TILE:(c + 1) * RANK_TILE, e:e + 1], (RANK_TILE, RANK_TILE))
        for j in range(tiles):
            a_rj = a_row[:, j * RANK_TILE:(j + 1) * RANK_TILE]
            if c < j:
                beats = jnp.where(a_col >= a_rj, 1.0, 0.0)
            elif c > j:
                beats = jnp.where(a_col > a_rj, 1.0, 0.0)
            else:
                beats = jnp.where(a_col > a_rj, 1.0, jnp.where(a_col == a_rj, earlier, 0.0))
            acc[j] = acc[j] + beats.reshape(RANK_TILE // 8, 8, RANK_TILE).sum(axis=0)
    return jnp.concatenate([a.sum(axis=0, keepdims=True) for a in acc], axis=1)


def _route_set(lg, h, xg_ref, g_ref, rc_ref, p_ref, rt_ref, n, cap):
    ex = jnp.exp(lg - lg.max(axis=0, keepdims=True))
    aff_t = ex / ex.sum(axis=0, keepdims=True)
    aff = jnp.concatenate([aff_t, jnp.zeros((LANES - N_EXPERTS, n), F32)], axis=0).T
    rt_ref[...] = jnp.full((LANES, n), float(cap), F32)
    slot = lax.broadcasted_iota(jnp.int32, (cap, n), 0).astype(F32)
    for e in range(N_EXPERTS):
        rank = _rank_row(aff, aff_t[e:e + 1, :], e, n)
        rt_ref[e:e + 1, :] = jnp.minimum(rank, float(cap))
        p_ref[e * cap:(e + 1) * cap, :] = jnp.where(rank == slot, 1.0, 0.0).astype(BF16)

    a1, a2, a3 = (p.astype(F32) for p in _split3(aff))
    packed = (a1 + pltpu.roll(a2, N_EXPERTS, 1) + pltpu.roll(a3, 2 * N_EXPERTS, 1)).astype(BF16)
    per = GATHER_M // cap
    glane = lax.broadcasted_iota(jnp.int32, (cap, LANES), 1)
    for grp in range(N_EXPERTS * cap // GATHER_M):
        p = p_ref[grp * GATHER_M:(grp + 1) * GATHER_M, :]
        xg = _bdot(p, h).astype(BF16)
        gg = _bdot(p, packed)
        for k in range(per):
            e = grp * per + k
            xg_ref[e] = xg[k * cap:(k + 1) * cap, :]
            mine = (glane < 3 * N_EXPERTS) & ((glane & (N_EXPERTS - 1)) == e)
            ge = jnp.where(mine, gg[k * cap:(k + 1) * cap, :], 0.0).sum(axis=-1, keepdims=True)
            g_ref[e] = jnp.broadcast_to(ge, (cap, LANES))
    rc_ref[...] = rt_ref[...].T


def _route_kernel(lg_ref, h_ref, xg_ref, g_ref, rc_ref, p_ref, rt_ref, *, n, cap):
    _route_set(lg_ref[...], h_ref[...], xg_ref, g_ref, rc_ref, p_ref, rt_ref, n, cap)


def _route(logits, h2, n, cap, n_sets):
    return pl.pallas_call(
        functools.partial(_route_kernel, n=n, cap=cap),
        grid=(n_sets,),
        in_specs=[
            pl.BlockSpec((N_EXPERTS, n), lambda s: (0, s)),
            pl.BlockSpec((n, D_MODEL), lambda s: (s, 0)),
        ],
        out_specs=[
            pl.BlockSpec((N_EXPERTS, cap, D_MODEL), lambda s: (0, s, 0)),
            pl.BlockSpec((N_EXPERTS, cap, LANES), lambda s: (0, s, 0)),
            pl.BlockSpec((n, LANES), lambda s: (s, 0)),
        ],
        out_shape=[jax.ShapeDtypeStruct((N_EXPERTS, n_sets * cap, D_MODEL), BF16),
                   jax.ShapeDtypeStruct((N_EXPERTS, n_sets * cap, LANES), F32),
                   jax.ShapeDtypeStruct((n_sets * n, LANES), F32)],
        scratch_shapes=[pltpu.VMEM((N_EXPERTS * cap, n), BF16), pltpu.VMEM((LANES, n), F32)],
        compiler_params=_params(1),
        name=f"route_n{n}",
    )(logits, h2)


EXPERT_TF = 1024
EXPERT_SUB = 256
N_FF_TILES = EXPERT_FF // EXPERT_TF


def _expert_kernel(xc_ref, xl_ref, gc_ref, gl_ref, wg_ref, wu_ref, wd_ref, o_ref, x_s, acc_s):
    f = pl.program_id(1)
    n_ctx = xc_ref.shape[0]

    @pl.when(f == 0)
    def _():
        x_s[:n_ctx, :] = xc_ref[...]
        x_s[n_ctx:, :] = xl_ref[...]
        acc_s[...] = jnp.zeros_like(acc_s)

    x = x_s[...]
    part = None
    for j in range(EXPERT_TF // EXPERT_SUB):
        cols = slice(j * EXPERT_SUB, (j + 1) * EXPERT_SUB)
        gate = _bdot(x, wg_ref[:, cols].astype(BF16))
        up = _bdot(x, wu_ref[:, cols].astype(BF16))
        hid = (gate * jax.nn.sigmoid(gate)) * up
        down = _bdot(hid.astype(BF16), wd_ref[cols, :].astype(BF16))
        part = down if part is None else part + down
    acc_s[...] += part

    @pl.when(f == N_FF_TILES - 1)
    def _():
        o_ref[:n_ctx, :] = (acc_s[:n_ctx, :] * gc_ref[:, :1]).astype(o_ref.dtype)
        o_ref[n_ctx:, :] = (acc_s[n_ctx:, :] * gl_ref[:, :1]).astype(o_ref.dtype)


def _experts(xg_ctx, xg_lat, g_ctx, g_lat, w_gate, w_up, w_down, layer):
    sc, sl = xg_ctx.shape[1], xg_lat.shape[1]
    slots = lambda s, w: pl.BlockSpec((None, s, w), lambda e, f: (e, 0, 0))
    return pl.pallas_call(
        _expert_kernel,
        grid=(N_EXPERTS, N_FF_TILES),
        in_specs=[
            slots(sc, D_MODEL), slots(sl, D_MODEL), slots(sc, LANES), slots(sl, LANES),
            pl.BlockSpec((None, None, D_MODEL, EXPERT_TF), lambda e, f: (layer, e, 0, f)),
            pl.BlockSpec((None, None, D_MODEL, EXPERT_TF), lambda e, f: (layer, e, 0, f)),
            pl.BlockSpec((None, None, EXPERT_TF, D_MODEL), lambda e, f: (layer, e, f, 0)),
        ],
        out_specs=slots(sc + sl, D_MODEL),
        out_shape=jax.ShapeDtypeStruct((N_EXPERTS, sc + sl, D_MODEL), BF16),
        scratch_shapes=[pltpu.VMEM((sc + sl, D_MODEL), BF16), pltpu.VMEM((sc + sl, D_MODEL), F32)],
        compiler_params=_params(2),
        name="experts",
    )(xg_ctx, xg_lat, g_ctx, g_lat, w_gate, w_up, w_down)


COMB_TM = 256
COMB_STEPS = T_LAT // COMB_TM
COMB_CTX_SETS = N_CTX_SETS // COMB_STEPS


def _combine_group(o_ref, rc_ref, x_ref, gt2_ref, ng_ref, norm_refs, out_refs, cap, sets, final):
    slots = N_EXPERTS * cap
    j = lax.broadcasted_iota(jnp.int32, (LANES, slots), 1)
    e = lax.broadcasted_iota(jnp.int32, (LANES, slots), 0)
    expand = jnp.where(j // cap == e, 1.0, 0.0).astype(BF16)
    slot = (lax.broadcasted_iota(jnp.int32, (1, slots), 1) % cap).astype(F32)
    for k in range(sets):
        rows = slice(k * COMB_TM, (k + 1) * COMB_TM)
        rank = _bdot(rc_ref[rows, :].astype(BF16), expand)
        pt = jnp.where(rank == slot, 1.0, 0.0).astype(BF16)
        ffn = _bdot(pt, o_ref[:, k * cap:(k + 1) * cap, :].reshape(slots, D_MODEL))
        x = x_ref[rows, :] + gt2_ref[...] * ffn
        if final:
            out_refs[0][rows, :] = _rms(x) * ng_ref[...]
        else:
            out_refs[0][rows, :] = x
            out_refs[1][rows, :] = _modnorm(x, ng_ref[...], norm_refs[0][...], norm_refs[1][...]).astype(BF16)


def _combine_kernel(*refs, final):
    n_in, n_out = (4, 1) if final else (6, 2)
    ng_ref = refs[2 * n_in]
    outs = refs[2 * n_in + 1:]
    for g, (cap, sets) in enumerate(((CAP_CTX, COMB_CTX_SETS), (CAP_LAT, 1))):
        o_ref, rc_ref, x_ref, gt2_ref, *norm_refs = refs[g * n_in:(g + 1) * n_in]
        _combine_group(o_ref, rc_ref, x_ref, gt2_ref, ng_ref, norm_refs, outs[g * n_out:(g + 1) * n_out], cap, sets,
                       final)


def _combine(out, rc_ctx, rc_lat, x1_ctx, x1_lat, mod, norm_g, layer, final):
    lat_tiles = LAT_LEN // COMB_TM
    lat_slot0 = N_CTX_SETS * CAP_CTX // CAP_LAT
    groups = (
        (rc_ctx, x1_ctx, lambda i: 0, COMB_CTX_SETS * CTX_LEN,
         pl.BlockSpec((N_EXPERTS, COMB_CTX_SETS * CAP_CTX, D_MODEL), lambda i: (0, i, 0))),
        (rc_lat, x1_lat, lambda i: 1 + i // lat_tiles, COMB_TM,
         pl.BlockSpec((N_EXPERTS, CAP_LAT, D_MODEL), lambda i: (0, lat_slot0 + i // lat_tiles, 0))),
    )
    in_specs, args, out_specs, out_shape = [], [], [], []
    for rc, x1, row_fn, tm, o_spec in groups:
        tile = lambda w, tm=tm: pl.BlockSpec((tm, w), lambda i: (i, 0))
        in_specs += [o_spec, tile(LANES), tile(D_MODEL), _mod_spec(layer, row_fn, MOD_GT2)]
        args += [out, rc, x1, mod]
        if not final:
            in_specs += [_mod_spec(layer + 1, row_fn, MOD_SH1), _mod_spec(layer + 1, row_fn, MOD_SC1)]
            args += [mod, mod]
        out_specs += [tile(D_MODEL)] * (1 if final else 2)
        out_shape += [jax.ShapeDtypeStruct(x1.shape, F32)] + ([] if final else [jax.ShapeDtypeStruct(x1.shape, BF16)])
    if final:
        in_specs.append(pl.BlockSpec((1, D_MODEL), lambda i: (0, 0)))
        args.append(norm_g.reshape(1, D_MODEL))
    else:
        in_specs.append(pl.BlockSpec((None, 1, D_MODEL), lambda i: (layer + 1, 0, 0)))
        args.append(norm_g)
    res = pl.pallas_call(
        functools.partial(_combine_kernel, final=final),
        grid=(COMB_STEPS,),
        in_specs=in_specs,
        out_specs=out_specs,
        out_shape=out_shape,
        compiler_params=_params(1),
        name="combine",
    )(*args)
    half = len(res) // 2
    return tuple(res[:half]), tuple(res[half:])


def kernel(x_prompt, x_sample, cache_attn_k, cache_attn_v, cache_na_k, cache_na_v, c, c_ctx, w_ada, b_ada,
           norm_mix, norm_ffn, w_in, q_norm, k_norm, rpb, w_branch_a, w_branch_b, w_out, w_router, w_gate,
           w_up, w_down, final_norm):
    x_ctx = x_prompt.reshape(T_CTX, D_MODEL)
    x_lat = x_sample.reshape(T_LAT, D_MODEL)
    mod = _modulation(c, c_ctx, w_ada, b_ada)
    norm_mix3 = norm_mix.reshape(DEPTH, 1, D_MODEL)
    norm_ffn3 = norm_ffn.reshape(DEPTH, 1, D_MODEL)
    w_router_t = jnp.swapaxes(w_router, 1, 2)
    ctx_row = lambda *g: 0

    h_ctx = _prenorm(x_ctx, mod, norm_mix3, 0, ctx_row)
    h_lat = _prenorm(x_lat, mod, norm_mix3, 0, _lat_mod_row(NORM_TM))
    layer_caches = []
    for layer in range(DEPTH):
        last = layer == DEPTH - 1
        yq, ykv, merge_w = _projection(h_ctx, h_lat, w_in, q_norm, k_norm, w_branch_a, w_branch_b, w_out, layer)
        yab_ctx, *new_caches = _ctx_attention(yq, ykv, layer_caches if last else [])
        layer_caches.append(tuple(new_caches))
        ya_lat = _lat_gqa_attention(yq, ykv, cache_attn_k, cache_attn_v, layer)
        yb_lat = _lat_na_attention(yq, ykv, cache_na_k, cache_na_v, rpb, layer)
        merge_args = (mod, norm_ffn3, merge_w, w_router_t, layer)
        x1_ctx, h2_ctx, lg_ctx = _merge([yab_ctx], h_ctx, x_ctx, *merge_args, ctx_row, MERGE_TM, MERGE_SUB)
        x1_lat, h2_lat, lg_lat, xg_ctx, g_ctx, rc_ctx = _merge(
            [ya_lat, yb_lat], h_lat, x_lat, *merge_args, _lat_mod_row(MERGE_ROUTE_TM), MERGE_ROUTE_TM,
            MERGE_ROUTE_SUB, route=(lg_ctx, h2_ctx))
        xg_lat, g_lat, rc_lat = _route(lg_lat, h2_lat, LAT_LEN, CAP_LAT, N_LAT_SETS)
        out = _experts(xg_ctx, xg_lat, g_ctx, g_lat, w_gate, w_up, w_down, layer)
        final = layer == DEPTH - 1
        norm_g = final_norm if final else norm_mix3
        res_ctx, res_lat = _combine(out, rc_ctx, rc_lat, x1_ctx, x1_lat, mod, norm_g, layer, final)
        if final:
            (y_ctx,), (y_lat,) = res_ctx, res_lat
        else:
            (x_ctx, h_ctx), (x_lat, h_lat) = res_ctx, res_lat

    heads = (N_KV_A, N_KV_A, N_HEADS_B, N_HEADS_B)
    new_caches = [a.reshape(N_CTX_SETS, DEPTH, CTX_LEN, h, HEAD_DIM) for a, h in zip(layer_caches[-1], heads)]
    return (y_ctx.reshape(N_CTX_SETS, CTX_LEN, D_MODEL), y_lat.reshape(N_LAT_SETS, LAT_LEN, D_MODEL), *new_caches)
```

```python
import functools

import numpy as np
import jax
import jax.numpy as jnp
from jax import lax
from jax.experimental import pallas as pl
from jax.experimental.pallas import tpu as pltpu

F32 = jnp.float32
BF16 = jnp.bfloat16

D_MODEL = 1024
N_CTX_SETS, CTX_LEN = 16, 256
N_LAT_SETS, LAT_LEN = 2, 1024
T_CTX = N_CTX_SETS * CTX_LEN
T_LAT = N_LAT_SETS * LAT_LEN
T_ALL = T_CTX + T_LAT
DEPTH = 2
PAST_LEN = 512
GRID_W = 64
GRID_ROWS = LAT_LEN // GRID_W
HEAD_DIM = 64
N_HEADS_A, N_KV_A, N_HEADS_B = 8, 2, 8
WIN_ROWS, WIN_COLS = 8, 16
N_EXPERTS = 16
EXPERT_FF = 2048
CAP_CTX = 2 * CTX_LEN // N_EXPERTS
CAP_LAT = 2 * LAT_LEN // N_EXPERTS
ROPE_THETA = 10000.0
EPS = 1e-6
NEG = -1e30
QKV_DIM = 2304
ATT_SCALE = HEAD_DIM ** -0.5

LANES = 128
VMEM_LIMIT = 56 * 1024 * 1024

PROJ_TN = 256
N_PROJ_TILES = QKV_DIM // PROJ_TN
PROJ_ORDER = np.array([0, 1, 3, 4, 5, 6, 7, 8, 2], np.int32)
N_Q_TILES = 4
CAST_STEPS = 8
MOD_SH1, MOD_SC1, MOD_GT1, MOD_SH2, MOD_SC2, MOD_GT2 = range(6)


def _params(n_grid_dims, vmem=VMEM_LIMIT):
    return pltpu.CompilerParams(dimension_semantics=("arbitrary",) * n_grid_dims, vmem_limit_bytes=vmem)


def _bdot(a, b):
    return jnp.dot(a, b, preferred_element_type=F32)


def _bdot_nt(a, b):
    return lax.dot_general(a, b, (((1,), (1,)), ((), ())), preferred_element_type=F32)


def _split2(x):
    hi = x.astype(BF16)
    lo = (x - hi.astype(F32)).astype(BF16)
    return hi, lo


def _split3(x):
    hi = x.astype(BF16)
    r = x - hi.astype(F32)
    mid = r.astype(BF16)
    lo = (r - mid.astype(F32)).astype(BF16)
    return hi, mid, lo


def _rms(x):
    return x * lax.rsqrt(jnp.mean(x * x, axis=-1, keepdims=True) + EPS)


def _modnorm(x, g, sh, sc):
    return (_rms(x) * g) * (1.0 + sc) + sh


def _lat_mod_row(tile_rows):
    return lambda i: 1 + (i * tile_rows) // LAT_LEN


def _mod_spec(layer, row_fn, chunk):
    return pl.BlockSpec((None, None, 1, D_MODEL), lambda *g: (layer, row_fn(*g), 0, chunk))


MOD_TN = 1536


def _mod_kernel(ct_ref, w_ref, b_ref, o_ref):
    ct = ct_ref[...]
    act = ct * jax.nn.sigmoid(ct)
    w = w_ref[...]
    for m in range(3):
        o_ref[m:m + 1, :] = jnp.sum(w * act[:, m:m + 1], axis=0, keepdims=True) + b_ref[...]
    o_ref[3:8, :] = jnp.zeros((5, MOD_TN), F32)


def _modulation(c, c_ctx, w_ada, b_ada):
    cond = jnp.concatenate([c_ctx[None, :], c, jnp.zeros((5, D_MODEL), F32)], axis=0)
    mod = pl.pallas_call(
        _mod_kernel,
        grid=(DEPTH, 6 * D_MODEL // MOD_TN),
        in_specs=[
            pl.BlockSpec((D_MODEL, 8), lambda l, j: (0, 0)),
            pl.BlockSpec((None, D_MODEL, MOD_TN), lambda l, j: (l, 0, j)),
            pl.BlockSpec((None, 1, MOD_TN), lambda l, j: (l, 0, j)),
        ],
        out_specs=pl.BlockSpec((None, 8, MOD_TN), lambda l, j: (l, 0, j)),
        out_shape=jax.ShapeDtypeStruct((DEPTH, 8, 6 * D_MODEL), F32),
        compiler_params=_params(2),
        name="modulation",
    )(cond.T, w_ada, b_ada.reshape(DEPTH, 1, 6 * D_MODEL))
    return mod.reshape(DEPTH, 8, 1, 6 * D_MODEL)


NORM_TM = 1024


def _prenorm_kernel(xc_ref, xl_ref, g_ref, sh_ref, sc_ref, hc_ref, hl_ref):
    i = pl.program_id(0)

    @pl.when(i < T_CTX // NORM_TM)
    def _():
        hc_ref[...] = _modnorm(xc_ref[...], g_ref[...], sh_ref[...], sc_ref[...]).astype(BF16)

    @pl.when(i >= T_CTX // NORM_TM)
    def _():
        hl_ref[...] = _modnorm(xl_ref[...], g_ref[...], sh_ref[...], sc_ref[...]).astype(BF16)


def _prenorm(x_ctx, x_lat, mod, norm_g, layer):
    n_ctx = T_CTX // NORM_TM
    ctx_tile = pl.BlockSpec((NORM_TM, D_MODEL), lambda i: (jnp.minimum(i, n_ctx - 1), 0))
    lat_tile = pl.BlockSpec((NORM_TM, D_MODEL), lambda i: (jnp.maximum(i - n_ctx, 0), 0))
    row_fn = lambda i: jnp.where(i < n_ctx, 0, 1 + ((i - n_ctx) * NORM_TM) // LAT_LEN)
    return pl.pallas_call(
        _prenorm_kernel,
        grid=(T_ALL // NORM_TM,),
        in_specs=[
            ctx_tile, lat_tile,
            pl.BlockSpec((None, 1, D_MODEL), lambda i: (layer, 0, 0)),
            _mod_spec(layer, row_fn, MOD_SH1),
            _mod_spec(layer, row_fn, MOD_SC1),
        ],
        out_specs=[ctx_tile, lat_tile],
        out_shape=[jax.ShapeDtypeStruct((T_CTX, D_MODEL), BF16), jax.ShapeDtypeStruct((T_LAT, D_MODEL), BF16)],
        compiler_params=_params(1),
        name="prenorm",
    )(x_ctx, x_lat, norm_g, mod, mod)


PROJ_CH = 1024


def _rope_tables():
    t = np.arange(LAT_LEN)
    lane = np.arange(LANES) % HEAD_DIM
    pos = np.where(lane < HEAD_DIM // 2, (t // GRID_W)[:, None], (t % GRID_W)[:, None]).astype(np.float64)
    freq = ROPE_THETA ** (-(lane % 16).astype(np.float64) / 16.0)
    ang = pos * freq[None, :]
    sign = np.where((lane & 16) == 0, -1.0, 1.0)[None, :]
    return np.cos(ang).astype(np.float32), (np.sin(ang) * sign).astype(np.float32)


def _head_norm_rope(y, gain, cos, sin):
    w = y.shape[1]
    r = (lax.broadcasted_iota(jnp.int32, (2 * w, w), 0) % w) // HEAD_DIM
    c = lax.broadcasted_iota(jnp.int32, (2 * w, w), 1) // HEAD_DIM
    seg = jnp.where(r == c, 1.0 / HEAD_DIM, 0.0).astype(BF16)
    ms = _bdot(jnp.concatenate(_split2(y * y), axis=1), seg)
    yn = y * lax.rsqrt(ms + EPS) * gain
    if cos is None:
        return yn
    lane = lax.broadcasted_iota(jnp.int32, yn.shape, 1)
    partner = jnp.where((lane & 16) == 0, pltpu.roll(yn, w - 16, 1), pltpu.roll(yn, 16, 1))
    if w > LANES:
        cos = jnp.concatenate([cos] * (w // LANES), axis=1)
        sin = jnp.concatenate([sin] * (w // LANES), axis=1)
    return yn * cos + partner * sin


def _proj_kernel(order_ref, hc_ref, hl_ref, w_ref, gain_ref, cos_ref, sin_ref, wa_ref, wbr_ref, wo_ref, wg_ref,
                 yq_ref, ykv_ref, oa_ref, ob_ref, oo_ref, og_ref, wb_ref, acc_ref):
    del order_ref
    s = pl.program_id(0)
    wb_ref[...] = w_ref[...].astype(BF16)
    oa_ref[...] = wa_ref[...].astype(BF16)
    ob_ref[...] = wbr_ref[...].astype(BF16)
    oo_ref[...] = wo_ref[...].astype(BF16)
    og_ref[...] = wg_ref[0].astype(BF16)
    chunks = [(hc_ref, k * PROJ_CH, k * PROJ_CH, False) for k in range(T_CTX // PROJ_CH)]
    chunks += [(hl_ref, k * PROJ_CH, T_CTX + k * PROJ_CH, True) for k in range(T_LAT // PROJ_CH)]

    def matmul(chunk, dst):
        h_ref, r0, o0, _ = chunk
        dst[o0:o0 + PROJ_CH, :] = _bdot(h_ref[r0:r0 + PROJ_CH, :], wb_ref[...]).astype(dst.dtype)

    def finish(chunk, width, src, dst):
        _, _, o0, is_lat = chunk
        cos, sin = (cos_ref[...], sin_ref[...]) if is_lat else (None, None)
        y = _head_norm_rope(src[o0:o0 + PROJ_CH, :width], gain_ref[:, :width], cos, sin)
        dst[o0:o0 + PROJ_CH, :width] = y.astype(dst.dtype)

    def tile(width, work, dst):
        matmul(chunks[0], work)
        for k in range(1, len(chunks)):
            matmul(chunks[k], work)
            if width:
                finish(chunks[k - 1], width, work, dst)
        if width:
            finish(chunks[-1], width, work, dst)

    pl.when(s < 2)(lambda: tile(PROJ_TN, acc_ref, yq_ref))
    pl.when((s >= 2) & (s < N_Q_TILES))(lambda: tile(0, yq_ref, yq_ref))
    pl.when((s >= N_Q_TILES) & (s < N_PROJ_TILES - 1))(lambda: tile(0, ykv_ref, ykv_ref))
    pl.when(s == N_PROJ_TILES - 1)(lambda: tile(LANES, ykv_ref, ykv_ref))


def _projection(h_ctx, h_lat, w_in, q_norm, k_norm, w_ba, w_bb, w_out, layer):
    ones = jnp.ones((2 * HEAD_DIM,), F32)
    gain = jnp.stack([jnp.tile(q_norm[layer], 4), jnp.tile(q_norm[layer], 4),
                      jnp.concatenate([jnp.tile(k_norm[layer], 2), ones])])[:, None, :]
    cos, sin = _rope_tables()
    half = N_HEADS_A * HEAD_DIM
    ra, ro = half // CAST_STEPS, D_MODEL // CAST_STEPS
    part = lambda j: jnp.minimum(j, CAST_STEPS - 1)
    gain_row = lambda j: jnp.where(j == N_PROJ_TILES - 1, 2, jnp.minimum(j, 1))
    grid_spec = pltpu.PrefetchScalarGridSpec(
        num_scalar_prefetch=1,
        grid=(N_PROJ_TILES,),
        in_specs=[
            pl.BlockSpec((T_CTX, D_MODEL), lambda j, p: (0, 0)),
            pl.BlockSpec((T_LAT, D_MODEL), lambda j, p: (0, 0)),
            pl.BlockSpec((None, D_MODEL, PROJ_TN), lambda j, p: (layer, 0, p[j])),
            pl.BlockSpec((None, 1, PROJ_TN), lambda j, p: (gain_row(j), 0, 0)),
            pl.BlockSpec((LAT_LEN, LANES), lambda j, p: (0, 0)),
            pl.BlockSpec((LAT_LEN, LANES), lambda j, p: (0, 0)),
            pl.BlockSpec((None, ra, D_MODEL), lambda j, p: (layer, part(j), 0)),
            pl.BlockSpec((None, ra, D_MODEL), lambda j, p: (layer, part(j), 0)),
            pl.BlockSpec((None, ro, D_MODEL), lambda j, p: (layer, part(j), 0)),
            pl.BlockSpec((pl.Element(1), pl.Element(ro), pl.Element(2 * D_MODEL)),
                         lambda j, p: (layer, part(j) * ro, QKV_DIM)),
        ],
        out_specs=[
            pl.BlockSpec((T_ALL, PROJ_TN), lambda j, p: (0, jnp.minimum(j, N_Q_TILES - 1))),
            pl.BlockSpec((T_ALL, PROJ_TN), lambda j, p: (0, jnp.maximum(j - N_Q_TILES, 0))),
            pl.BlockSpec((ra, D_MODEL), lambda j, p: (part(j), 0)),
            pl.BlockSpec((ra, D_MODEL), lambda j, p: (part(j), 0)),
            pl.BlockSpec((ro, D_MODEL), lambda j, p: (part(j), 0)),
            pl.BlockSpec((ro, 2 * D_MODEL), lambda j, p: (part(j), 0)),
        ],
        scratch_shapes=[pltpu.VMEM((D_MODEL, PROJ_TN), BF16), pltpu.VMEM((T_ALL, PROJ_TN), F32)],
    )
    yq, ykv, *merge_w = pl.pallas_call(
        _proj_kernel,
        grid_spec=grid_spec,
        out_shape=[jax.ShapeDtypeStruct((T_ALL, N_Q_TILES * PROJ_TN), BF16),
                   jax.ShapeDtypeStruct((T_ALL, QKV_DIM - N_Q_TILES * PROJ_TN), F32),
                   jax.ShapeDtypeStruct((half, D_MODEL), BF16), jax.ShapeDtypeStruct((half, D_MODEL), BF16),
                   jax.ShapeDtypeStruct((D_MODEL, D_MODEL), BF16), jax.ShapeDtypeStruct((D_MODEL, 2 * D_MODEL), BF16)],
        compiler_params=_params(1),
        name="projection",
    )(jnp.asarray(PROJ_ORDER), h_ctx, h_lat, w_in, gain, jnp.asarray(cos), jnp.asarray(sin), w_ba, w_bb, w_out, w_in)
    return yq, ykv, merge_w


COL_QA, COL_QB = 0, 512
COL_KB, COL_VB, COL_KA = 0, 512, 1024


def _lane_is_low(shape):
    return lax.broadcasted_iota(jnp.int32, shape, 1) < HEAD_DIM


def _pair_halves(x):
    low = _lane_is_low(x.shape)
    xb = x.astype(BF16)
    zero = jnp.zeros_like(xb)
    return jnp.where(low, xb, zero), jnp.where(low, zero, xb)


def _scaled_q(q):
    assert ATT_SCALE == 0.125
    return (q * ATT_SCALE).astype(BF16)


def _attend_pair(q, keys, values, biases, joint_pv=True):
    probs, dens, out = [], [], None
    for h in range(2):
        scores = []
        for k, b in zip(keys[h], biases[h]):
            s = _bdot_nt(q, k)
            scores.append(s if b is None else s + b)
        m = scores[0].max(axis=-1, keepdims=True)
        for s in scores[1:]:
            m = jnp.maximum(m, s.max(axis=-1, keepdims=True))
        den, num = None, None
        for s, v in zip(scores, values[h]):
            e = jnp.exp(s - m)
            d = e.sum(axis=-1, keepdims=True)
            den = d if den is None else den + d
            if joint_pv:
                probs.append(e.astype(BF16))
            else:
                o = _bdot(e.astype(BF16), v)
                num = o if num is None else num + o
        dens.append(den)
        if not joint_pv:
            out = num / den if out is None else out + num / den
    if not joint_pv:
        return out
    num = _bdot(jnp.concatenate(probs, axis=1), jnp.concatenate(values[0] + values[1], axis=0))
    return num / jnp.where(_lane_is_low(num.shape), dens[0], dens[1])


def _attend_pair_t(q, keys, values_t, biases_t):
    probs, dens = [], []
    for h in range(2):
        scores = []
        for k, b in zip(keys[h], biases_t[h]):
            s = _bdot_nt(k, q)
            scores.append(s if b is None else s + b)
        m = scores[0].max(axis=0, keepdims=True)
        for s in scores[1:]:
            m = jnp.maximum(m, s.max(axis=0, keepdims=True))
        den = None
        for s in scores:
            e = jnp.exp(s - m)
            d = e.sum(axis=0, keepdims=True)
            den = d if den is None else den + d
            probs.append(e.astype(BF16))
        dens.append(den)
    num_t = _bdot(values_t, jnp.concatenate(probs, axis=0))
    row = lax.broadcasted_iota(jnp.int32, num_t.shape, 0)
    return (num_t / jnp.where(row < HEAD_DIM, dens[0], dens[1])).T


def _gqa_variants(x):
    lo, hi = _pair_halves(x)
    sw_lo, sw_hi = _pair_halves(pltpu.roll(x, HEAD_DIM, 1))
    return [lo, sw_hi, sw_lo, hi]


def _gqa_attention(q_ref, kvar, vvar, o_ref, joint_pv):
    for pair in range(N_HEADS_A // 2):
        q = _scaled_q(q_ref[:, pair * LANES:(pair + 1) * LANES])
        kvh = (2 * pair) // (N_HEADS_A // N_KV_A)
        out = _attend_pair(q, [[kvar[2 * kvh]], [kvar[2 * kvh + 1]]], [[vvar[2 * kvh]], [vvar[2 * kvh + 1]]],
                           [[None], [None]], joint_pv)
        o_ref[:, pair * LANES:(pair + 1) * LANES] = out.astype(o_ref.dtype)


def _ctx_attn_kernel(qa_ref, qb_ref, kb_ref, vb_ref, kava_ref, *refs, n_prev):
    prev = refs[:4 * n_prev]
    y_ref, nak_ref, nav_ref, nbk_ref, nbv_ref = refs[4 * n_prev:]
    ka = kava_ref[:, :LANES]
    va = kava_ref[:, LANES:]
    new = (ka, va, kb_ref[...], vb_ref[...])
    for c, (o_ref, val) in enumerate(zip((nak_ref, nav_ref, nbk_ref, nbv_ref), new)):
        if n_prev:
            for p in range(n_prev):
                o_ref[p] = prev[4 * p + c][...]
            o_ref[n_prev] = val
        else:
            o_ref[...] = val
    _gqa_attention(qa_ref, _gqa_variants(ka), _gqa_variants(va), y_ref.at[:, :N_HEADS_A * HEAD_DIM], joint_pv=False)
    for pair in range(N_HEADS_B // 2):
        cols = slice(pair * LANES, (pair + 1) * LANES)
        q = _scaled_q(qb_ref[:, cols])
        k_lo, k_hi = _pair_halves(kb_ref[:, cols])
        v_lo, v_hi = _pair_halves(vb_ref[:, cols])
        out = _attend_pair(q, [[k_lo], [k_hi]], [[v_lo], [v_hi]], [[None], [None]], joint_pv=False)
        y_ref[:, N_HEADS_A * HEAD_DIM + pair * LANES:N_HEADS_A * HEAD_DIM + (pair + 1) * LANES] = out.astype(BF16)


def _ctx_attention(yq, ykv, prev_caches):
    wide = 4 * LANES
    widths = (LANES, LANES, wide, wide)
    n_prev = len(prev_caches)
    row = lambda b: (b, 0)
    f32 = lambda *s: jax.ShapeDtypeStruct(s, F32)
    in_specs = [
        pl.BlockSpec((CTX_LEN, wide), lambda b: (b, COL_QA // wide)),
        pl.BlockSpec((CTX_LEN, wide), lambda b: (b, COL_QB // wide)),
        pl.BlockSpec((CTX_LEN, wide), lambda b: (b, COL_KB // wide)),
        pl.BlockSpec((CTX_LEN, wide), lambda b: (b, COL_VB // wide)),
        pl.BlockSpec((CTX_LEN, 2 * LANES), lambda b: (b, COL_KA // (2 * LANES))),
    ]
    args = [yq, yq, ykv, ykv, ykv]
    for layer_caches in prev_caches:
        in_specs += [pl.BlockSpec((CTX_LEN, w), row) for w in widths]
        args += list(layer_caches)
    if n_prev:
        cache_specs = [pl.BlockSpec((None, n_prev + 1, CTX_LEN, w), lambda b: (b, 0, 0, 0)) for w in widths]
        cache_shapes = [f32(N_CTX_SETS, n_prev + 1, CTX_LEN, w) for w in widths]
    else:
        cache_specs = [pl.BlockSpec((CTX_LEN, w), row) for w in widths]
        cache_shapes = [f32(T_CTX, w) for w in widths]
    return pl.pallas_call(
        functools.partial(_ctx_attn_kernel, n_prev=n_prev),
        grid=(N_CTX_SETS,),
        in_specs=in_specs,
        out_specs=[pl.BlockSpec((CTX_LEN, 2 * wide), row)] + cache_specs,
        out_shape=[jax.ShapeDtypeStruct((T_CTX, 2 * wide), BF16)] + cache_shapes,
        compiler_params=_params(1),
        name="ctx_attention",
    )(*args)


LAT_TQ = 512
LAT_QT = LAT_LEN // LAT_TQ
GQA_TQ = 1024


def _lat_gqa_kernel(qa_ref, kava_ref, ck_ref, cv_ref, o_ref, k_s, vt_s):
    n_keys = PAST_LEN + LAT_LEN

    @pl.when(pl.program_id(1) == 0)
    def _():
        for i, (c, x) in enumerate(zip(_gqa_variants(ck_ref[...]), _gqa_variants(kava_ref[:, :LANES]))):
            k_s[i, :PAST_LEN, :] = c
            k_s[i, PAST_LEN:, :] = x
        for src, cols in ((cv_ref[...], slice(0, PAST_LEN)), (kava_ref[:, LANES:], slice(PAST_LEN, n_keys))):
            vt = src.T
            swapped = pltpu.roll(vt, HEAD_DIM, 0)
            top = lax.broadcasted_iota(jnp.int32, vt.shape, 0) < HEAD_DIM
            zero = jnp.zeros_like(vt)
            per_head = ((jnp.where(top, vt, zero), jnp.where(top, zero, swapped)),
                        (jnp.where(top, swapped, zero), jnp.where(top, zero, vt)))
            for kvh in range(N_KV_A):
                for half in range(2):
                    off = half * n_keys
                    vt_s[kvh, :, off + cols.start:off + cols.stop] = per_head[kvh][half].astype(BF16)

    for pair in range(N_HEADS_A // 2):
        q = _scaled_q(qa_ref[:, pair * LANES:(pair + 1) * LANES])
        kvh = (2 * pair) // (N_HEADS_A // N_KV_A)
        out = _attend_pair_t(q, [[k_s[2 * kvh]], [k_s[2 * kvh + 1]]], vt_s[kvh], [[None], [None]])
        o_ref[:, pair * LANES:(pair + 1) * LANES] = out.astype(o_ref.dtype)


def _lat_gqa_attention(yq, ykv, cache_k, cache_v, layer):
    wide = 4 * LANES
    tiles = LAT_LEN // GQA_TQ
    first = T_CTX // GQA_TQ
    cache = pl.BlockSpec((None, None, PAST_LEN, LANES), lambda b, t: (b, layer, 0, 0))
    return pl.pallas_call(
        _lat_gqa_kernel,
        grid=(N_LAT_SETS, tiles),
        in_specs=[
            pl.BlockSpec((GQA_TQ, wide), lambda b, t: (first + b * tiles + t, 0)),
            pl.BlockSpec((LAT_LEN, 2 * LANES), lambda b, t: (T_CTX // LAT_LEN + b, COL_KA // (2 * LANES))),
            cache, cache,
        ],
        out_specs=pl.BlockSpec((GQA_TQ, wide), lambda b, t: (b * tiles + t, 0)),
        out_shape=jax.ShapeDtypeStruct((T_LAT, wide), BF16),
        scratch_shapes=[pltpu.VMEM((4, PAST_LEN + LAT_LEN, LANES), BF16),
                        pltpu.VMEM((N_KV_A, LANES, 2 * (PAST_LEN + LAT_LEN)), BF16)],
        compiler_params=_params(2),
        name="lat_gqa_attention",
    )(yq, ykv, cache_k.reshape(N_LAT_SETS, DEPTH, PAST_LEN, LANES), cache_v.reshape(N_LAT_SETS, DEPTH, PAST_LEN, LANES))


N_DR = 2 * WIN_ROWS - 1
N_DC = 2 * WIN_COLS - 1
ROWS_PER_TQ = LAT_TQ // GRID_W
NA_LOCAL = 768
NA_KEY_STEP = 2 * LANES
NA_LAST_BASE = (LAT_LEN - NA_LOCAL) // NA_KEY_STEP
NA_TOE_ROWS = 32
NA_PAIRS = 4


def _na_key_base(qt):
    return jnp.where(qt < LAT_QT // 2, 0, NA_LAST_BASE)


def _window_mask():
    r = np.arange(GRID_ROWS)
    row_start = np.clip(r - WIN_ROWS // 2, 0, GRID_ROWS - WIN_ROWS)
    in_rows = (r[None, :] >= row_start[:, None]) & (r[None, :] < row_start[:, None] + WIN_ROWS)
    cq = np.arange(GRID_W)
    col_start = np.clip(cq - WIN_COLS // 2, 0, GRID_W - WIN_COLS)
    in_cols = (cq[None, :] >= col_start[:, None]) & (cq[None, :] < col_start[:, None] + WIN_COLS)
    valid = (in_rows[:, None, :, None] & in_cols[None, :, None, :]).reshape(LAT_LEN, LAT_LEN)
    tiles = []
    for qt in range(LAT_QT):
        base = (0 if qt < LAT_QT // 2 else NA_LAST_BASE) * NA_KEY_STEP
        tile = valid[qt * LAT_TQ:(qt + 1) * LAT_TQ]
        assert not tile[:, :base].any() and not tile[:, base + NA_LOCAL:].any()
        tiles.append(tile[:, base:base + NA_LOCAL])
    return np.where(np.stack(tiles), 0.0, NEG).astype(np.float32)


def _toeplitz_select():
    j = np.arange(LANES)
    c = np.clip(j - (GRID_W - 1), -(WIN_COLS - 1), WIN_COLS - 1) + (WIN_COLS - 1)
    return (np.arange(LANES)[:, None] == c[None, :]).astype(np.float32)


def _na_kernel(q_ref, k_ref, v_ref, ck_ref, cv_ref, mask_ref, rpb_ref, sel_ref, o_ref,
               toe_ref, bias_ref, k_s, v_s, ck_s, cv_s):
    b, qt = pl.program_id(1), pl.program_id(2)
    n_heads = 2 * NA_PAIRS

    @pl.when((qt == 0) & (b == 0))
    def _():
        low = lax.broadcasted_iota(jnp.int32, (GRID_W, LANES), 1) < GRID_W
        sel = sel_ref[...].astype(BF16)
        for hd in range(n_heads):
            gen = sum(_bdot(piece, sel) for piece in _split3(rpb_ref[hd]))
            for dd in range(N_DR + 1):
                lo = jnp.broadcast_to(gen[dd:dd + 1, :], (GRID_W, LANES))
                hi = jnp.broadcast_to(gen[dd + 1:dd + 2, :], (GRID_W, LANES))
                lo = pltpu.roll(lo, LANES - (GRID_W - 1), 1, stride=1, stride_axis=0)
                hi = pltpu.roll(hi, 1, 1, stride=1, stride_axis=0)
                toe_ref[hd, dd] = jnp.where(low, lo, hi)

    @pl.when(qt == 0)
    def _():
        for src, dst in ((k_ref, k_s), (v_ref, v_s), (ck_ref, ck_s), (cv_ref, cv_s)):
            for pair in range(NA_PAIRS):
                lo, hi = _pair_halves(src[:, pair * LANES:(pair + 1) * LANES])
                dst[2 * pair] = lo
                dst[2 * pair + 1] = hi

    base = _na_key_base(qt)
    local = pl.ds(pl.multiple_of(base * NA_KEY_STEP, NA_KEY_STEP), NA_LOCAL)
    for pair in range(NA_PAIRS):
        q = _scaled_q(q_ref[:, pair * LANES:(pair + 1) * LANES])
        heads = (2 * pair, 2 * pair + 1)
        biases = []
        for hd in heads:
            for rr in range(ROWS_PER_TQ):
                for kp in range(NA_LOCAL // LANES):
                    d = 2 * (kp + base * (NA_KEY_STEP // LANES)) - (qt * ROWS_PER_TQ + rr) + (WIN_ROWS - 1)
                    dd = jnp.clip(d, -1, N_DR - 1) + 1
                    bias_ref[hd, rr * GRID_W:(rr + 1) * GRID_W, kp * LANES:(kp + 1) * LANES] = toe_ref[hd, dd]
            biases.append([bias_ref[hd] + mask_ref[...], None])
        out = _attend_pair(q, [[k_s[hd, local, :], ck_s[hd]] for hd in heads],
                           [[v_s[hd, local, :], cv_s[hd]] for hd in heads], biases)
        o_ref[:, pair * LANES:(pair + 1) * LANES] = out.astype(o_ref.dtype)


def _lat_na_attention(yq, ykv, cache_k, cache_v, rpb, layer):
    first = T_CTX // LAT_TQ
    kv_row = T_CTX // LAT_LEN
    wide = N_HEADS_B * HEAD_DIM
    gen = jnp.pad(rpb[layer], ((0, 0), (1, NA_TOE_ROWS - N_DR - 1), (0, LANES - N_DC)))
    w = NA_PAIRS * LANES
    nh = 2 * NA_PAIRS
    cq, ck, cv = COL_QB // w, COL_KB // w, COL_VB // w
    return pl.pallas_call(
        _na_kernel,
        grid=(N_HEADS_B // nh, N_LAT_SETS, LAT_QT),
        in_specs=[
            pl.BlockSpec((LAT_TQ, w), lambda g, b, t: (first + b * LAT_QT + t, cq + g)),
            pl.BlockSpec((LAT_LEN, w), lambda g, b, t: (kv_row + b, ck + g)),
            pl.BlockSpec((LAT_LEN, w), lambda g, b, t: (kv_row + b, cv + g)),
            pl.BlockSpec((None, None, PAST_LEN, w), lambda g, b, t: (b, layer, 0, g)),
            pl.BlockSpec((None, None, PAST_LEN, w), lambda g, b, t: (b, layer, 0, g)),
            pl.BlockSpec((None, LAT_TQ, NA_LOCAL), lambda g, b, t: (t, 0, 0)),
            pl.BlockSpec((nh, NA_TOE_ROWS, LANES), lambda g, b, t: (g, 0, 0)),
            pl.BlockSpec((LANES, LANES), lambda g, b, t: (0, 0)),
        ],
        out_specs=pl.BlockSpec((LAT_TQ, w), lambda g, b, t: (b * LAT_QT + t, g)),
        out_shape=jax.ShapeDtypeStruct((T_LAT, wide), BF16),
        scratch_shapes=[pltpu.VMEM((nh, N_DR + 1, GRID_W, LANES), F32), pltpu.VMEM((nh, LAT_TQ, NA_LOCAL), F32),
                        pltpu.VMEM((nh, LAT_LEN, LANES), BF16), pltpu.VMEM((nh, LAT_LEN, LANES), BF16),
                        pltpu.VMEM((nh, PAST_LEN, LANES), BF16), pltpu.VMEM((nh, PAST_LEN, LANES), BF16)],
        compiler_params=_params(3),
        name="lat_na_attention",
    )(yq, ykv, ykv, cache_k.reshape(N_LAT_SETS, DEPTH, PAST_LEN, wide), cache_v.reshape(N_LAT_SETS, DEPTH, PAST_LEN, wide),
      jnp.asarray(_window_mask()), gen, jnp.asarray(_toeplitz_select()))


MERGE_TM = 1024
MERGE_SUB = 1024
MERGE_ROUTE_TM = 512
MERGE_ROUTE_SUB = 512


def _merge_kernel(*refs, n_y, route_sets, sub):
    y_refs, refs = refs[:n_y], refs[n_y:]
    (h_ref, x_ref, gt1_ref, sh2_ref, sc2_ref, nf_ref, wab, wbb, wob, wgb, wr_ref), refs = refs[:11], refs[11:]
    if route_sets:
        (rlg_ref, rh_ref), refs = refs[:2], refs[2:]
    (x1_ref, h2_ref, lg_ref), refs = refs[:3], refs[3:]
    if route_sets:
        xg_ref, g_ref, rc_ref, p_s, rt_s = refs

    wr_hi, wr_lo = _split2(wr_ref[...])
    wr_both = jnp.concatenate([wr_hi, wr_lo], axis=0)
    half = N_HEADS_A * HEAD_DIM
    n_sub = x_ref.shape[0] // sub
    for r in range(n_sub):
        rows = slice(r * sub, (r + 1) * sub)
        if n_y == 1:
            ya, yb = y_refs[0][rows, :half], y_refs[0][rows, half:]
        else:
            ya, yb = y_refs[0][rows, :], y_refs[1][rows, :]
        sets = list(range(r * route_sets // n_sub, (r + 1) * route_sets // n_sub))

        def route(k):
            tok = slice(k * CTX_LEN, (k + 1) * CTX_LEN)
            slots = slice(k * CAP_CTX, (k + 1) * CAP_CTX)
            _route_set(rlg_ref[:, tok], rh_ref[tok, :], xg_ref.at[:, slots, :], g_ref.at[:, slots, :],
                       rc_ref.at[tok, :], p_s.at[k], rt_s.at[k], CTX_LEN, CAP_CTX)

        gates = _bdot(h_ref[rows, :], wgb[...])
        za = _bdot(ya, wab[...])
        zb = _bdot(yb, wbb[...])
        for k in sets[:len(sets) // 2]:
            route(k)
        m = jax.nn.sigmoid(gates[:, :D_MODEL]) * za + jax.nn.sigmoid(gates[:, D_MODEL:]) * zb
        x1 = x_ref[rows, :] + gt1_ref[...] * _bdot(m.astype(BF16), wob[...])
        x1_ref[rows, :] = x1
        h2 = _modnorm(x1, nf_ref[...], sh2_ref[...], sc2_ref[...])
        h2_ref[rows, :] = h2.astype(BF16)
        hh, hl = _split2(h2)
        both = _bdot_nt(wr_both, hh)
        lg_ref[:, rows] = both[:N_EXPERTS, :] + both[N_EXPERTS:, :] + _bdot_nt(wr_hi, hl)
        for k in sets[len(sets) // 2:]:
            route(k)


def _merge(y_parts, h, x, mod, norm_ffn, merge_w, w_router_t, layer, row_fn, tm, sub, route=None):
    n = x.shape[0]
    tiles = n // tm
    once = pl.Buffered(1)
    weight = lambda w: pl.BlockSpec(w.shape, lambda i: (0, 0), pipeline_mode=once)
    tile = lambda w: pl.BlockSpec((tm, w), lambda i: (i, 0))
    in_specs = [tile(p.shape[1]) for p in y_parts] + [
        tile(D_MODEL), tile(D_MODEL),
        _mod_spec(layer, row_fn, MOD_GT1), _mod_spec(layer, row_fn, MOD_SH2), _mod_spec(layer, row_fn, MOD_SC2),
        pl.BlockSpec((None, 1, D_MODEL), lambda i: (layer, 0, 0)),
    ] + [weight(w) for w in merge_w] + [
        pl.BlockSpec((None, N_EXPERTS, D_MODEL), lambda i: (layer, 0, 0), pipeline_mode=once)]
    args = list(y_parts) + [h, x, mod, mod, mod, norm_ffn, *merge_w, w_router_t]
    out_specs = [tile(D_MODEL), tile(D_MODEL), pl.BlockSpec((N_EXPERTS, tm), lambda i: (0, i))]
    out_shape = [jax.ShapeDtypeStruct((n, D_MODEL), F32), jax.ShapeDtypeStruct((n, D_MODEL), BF16),
                 jax.ShapeDtypeStruct((N_EXPERTS, n), F32)]
    scratch = []
    route_sets = 0
    if route is not None:
        route_sets = N_CTX_SETS // tiles
        rows, slots = route_sets * CTX_LEN, route_sets * CAP_CTX
        in_specs += [pl.BlockSpec((N_EXPERTS, rows), lambda i: (0, i)), pl.BlockSpec((rows, D_MODEL), lambda i: (i, 0))]
        args += list(route)
        out_specs += [pl.BlockSpec((N_EXPERTS, slots, D_MODEL), lambda i: (0, i, 0)),
                      pl.BlockSpec((N_EXPERTS, slots, LANES), lambda i: (0, i, 0)),
                      pl.BlockSpec((rows, LANES), lambda i: (i, 0))]
        out_shape += [jax.ShapeDtypeStruct((N_EXPERTS, N_CTX_SETS * CAP_CTX, D_MODEL), BF16),
                      jax.ShapeDtypeStruct((N_EXPERTS, N_CTX_SETS * CAP_CTX, LANES), F32),
                      jax.ShapeDtypeStruct((T_CTX, LANES), F32)]
        scratch += [pltpu.VMEM((route_sets, N_EXPERTS * CAP_CTX, CTX_LEN), BF16),
                    pltpu.VMEM((route_sets, LANES, CTX_LEN), F32)]
    return pl.pallas_call(
        functools.partial(_merge_kernel, n_y=len(y_parts), route_sets=route_sets, sub=sub),
        grid=(tiles,),
        in_specs=in_specs,
        out_specs=out_specs,
        out_shape=out_shape,
        scratch_shapes=scratch,
        compiler_params=_params(1),
        name="merge_route" if route_sets else "merge",
    )(*args)


GATHER_M = 512
RANK_TILE = 128


def _rank_row(aff, a_row, e, n):
    tiles = n // RANK_TILE
    sub = lax.broadcasted_iota(jnp.int32, (RANK_TILE, RANK_TILE), 0)
    lane = lax.broadcasted_iota(jnp.int32, (RANK_TILE, RANK_TILE), 1)
    earlier = jnp.where(sub < lane, 1.0, 0.0)
    acc = [jnp.zeros((8, RANK_TILE), F32) for _ in range(tiles)]
    for c in range(tiles):
        a_col = jnp.broadcast_to(aff[c * RANK_TILE:(c + 1) * RANK_TILE, e:e + 1], (RANK_TILE, RANK_TILE))
        for j in range(tiles):
            a_rj = a_row[:, j * RANK_TILE:(j + 1) * RANK_TILE]
            if c < j:
                beats = jnp.where(a_col >= a_rj, 1.0, 0.0)
            elif c > j:
                beats = jnp.where(a_col > a_rj, 1.0, 0.0)
            else:
                beats = jnp.where(a_col > a_rj, 1.0, jnp.where(a_col == a_rj, earlier, 0.0))
            acc[j] = acc[j] + beats.reshape(RANK_TILE // 8, 8, RANK_TILE).sum(axis=0)
    return jnp.concatenate([a.sum(axis=0, keepdims=True) for a in acc], axis=1)


def _route_set(lg, h, xg_ref, g_ref, rc_ref, p_ref, rt_ref, n, cap):
    ex = jnp.exp(lg - lg.max(axis=0, keepdims=True))
    aff_t = ex / ex.sum(axis=0, keepdims=True)
    aff = jnp.concatenate([aff_t, jnp.zeros((LANES - N_EXPERTS, n), F32)], axis=0).T
    rt_ref[...] = jnp.full((LANES, n), float(cap), F32)
    slot = lax.broadcasted_iota(jnp.int32, (cap, n), 0).astype(F32)
    for e in range(N_EXPERTS):
        rank = _rank_row(aff, aff_t[e:e + 1, :], e, n)
        rt_ref[e:e + 1, :] = jnp.minimum(rank, float(cap))
        p_ref[e * cap:(e + 1) * cap, :] = jnp.where(rank == slot, 1.0, 0.0).astype(BF16)

    a1, a2, a3 = (p.astype(F32) for p in _split3(aff))
    packed = (a1 + pltpu.roll(a2, N_EXPERTS, 1) + pltpu.roll(a3, 2 * N_EXPERTS, 1)).astype(BF16)
    per = GATHER_M // cap
    glane = lax.broadcasted_iota(jnp.int32, (cap, LANES), 1)
    for grp in range(N_EXPERTS * cap // GATHER_M):
        p = p_ref[grp * GATHER_M:(grp + 1) * GATHER_M, :]
        xg = _bdot(p, h).astype(BF16)
        gg = _bdot(p, packed)
        for k in range(per):
            e = grp * per + k
            xg_ref[e] = xg[k * cap:(k + 1) * cap, :]
            mine = (glane < 3 * N_EXPERTS) & ((glane & (N_EXPERTS - 1)) == e)
            ge = jnp.where(mine, gg[k * cap:(k + 1) * cap, :], 0.0).sum(axis=-1, keepdims=True)
            g_ref[e] = jnp.broadcast_to(ge, (cap, LANES))
    rc_ref[...] = rt_ref[...].T


def _route_kernel(lg_ref, h_ref, xg_ref, g_ref, rc_ref, p_ref, rt_ref, *, n, cap):
    _route_set(lg_ref[...], h_ref[...], xg_ref, g_ref, rc_ref, p_ref, rt_ref, n, cap)


def _route(logits, h2, n, cap, n_sets):
    return pl.pallas_call(
        functools.partial(_route_kernel, n=n, cap=cap),
        grid=(n_sets,),
        in_specs=[
            pl.BlockSpec((N_EXPERTS, n), lambda s: (0, s)),
            pl.BlockSpec((n, D_MODEL), lambda s: (s, 0)),
        ],
        out_specs=[
            pl.BlockSpec((N_EXPERTS, cap, D_MODEL), lambda s: (0, s, 0)),
            pl.BlockSpec((N_EXPERTS, cap, LANES), lambda s: (0, s, 0)),
            pl.BlockSpec((n, LANES), lambda s: (s, 0)),
        ],
        out_shape=[jax.ShapeDtypeStruct((N_EXPERTS, n_sets * cap, D_MODEL), BF16),
                   jax.ShapeDtypeStruct((N_EXPERTS, n_sets * cap, LANES), F32),
                   jax.ShapeDtypeStruct((n_sets * n, LANES), F32)],
        scratch_shapes=[pltpu.VMEM((N_EXPERTS * cap, n), BF16), pltpu.VMEM((LANES, n), F32)],
        compiler_params=_params(1),
        name=f"route_n{n}",
    )(logits, h2)


EXPERT_TF = 1024
EXPERT_SUB = 256
N_FF_TILES = EXPERT_FF // EXPERT_TF


def _expert_kernel(xc_ref, xl_ref, gc_ref, gl_ref, wg_ref, wu_ref, wd_ref, o_ref, x_s, acc_s):
    f = pl.program_id(1)
    n_ctx = xc_ref.shape[0]

    @pl.when(f == 0)
    def _():
        x_s[:n_ctx, :] = xc_ref[...]
        x_s[n_ctx:, :] = xl_ref[...]
        acc_s[...] = jnp.zeros_like(acc_s)

    x = x_s[...]
    part = None
    for j in range(EXPERT_TF // EXPERT_SUB):
        cols = slice(j * EXPERT_SUB, (j + 1) * EXPERT_SUB)
        gate = _bdot(x, wg_ref[:, cols].astype(BF16))
        up = _bdot(x, wu_ref[:, cols].astype(BF16))
        hid = (gate * jax.nn.sigmoid(gate)) * up
        down = _bdot(hid.astype(BF16), wd_ref[cols, :].astype(BF16))
        part = down if part is None else part + down
    acc_s[...] += part

    @pl.when(f == N_FF_TILES - 1)
    def _():
        o_ref[:n_ctx, :] = (acc_s[:n_ctx, :] * gc_ref[:, :1]).astype(o_ref.dtype)
        o_ref[n_ctx:, :] = (acc_s[n_ctx:, :] * gl_ref[:, :1]).astype(o_ref.dtype)


def _experts(xg_ctx, xg_lat, g_ctx, g_lat, w_gate, w_up, w_down, layer):
    sc, sl = xg_ctx.shape[1], xg_lat.shape[1]
    slots = lambda s, w: pl.BlockSpec((None, s, w), lambda e, f: (e, 0, 0))
    return pl.pallas_call(
        _expert_kernel,
        grid=(N_EXPERTS, N_FF_TILES),
        in_specs=[
            slots(sc, D_MODEL), slots(sl, D_MODEL), slots(sc, LANES), slots(sl, LANES),
            pl.BlockSpec((None, None, D_MODEL, EXPERT_TF), lambda e, f: (layer, e, 0, f)),
            pl.BlockSpec((None, None, D_MODEL, EXPERT_TF), lambda e, f: (layer, e, 0, f)),
            pl.BlockSpec((None, None, EXPERT_TF, D_MODEL), lambda e, f: (layer, e, f, 0)),
        ],
        out_specs=slots(sc + sl, D_MODEL),
        out_shape=jax.ShapeDtypeStruct((N_EXPERTS, sc + sl, D_MODEL), BF16),
        scratch_shapes=[pltpu.VMEM((sc + sl, D_MODEL), BF16), pltpu.VMEM((sc + sl, D_MODEL), F32)],
        compiler_params=_params(2),
        name="experts",
    )(xg_ctx, xg_lat, g_ctx, g_lat, w_gate, w_up, w_down)


COMB_TM = 256
COMB_STEPS = T_LAT // COMB_TM
COMB_CTX_SETS = N_CTX_SETS // COMB_STEPS


def _combine_group(o_ref, rc_ref, x_ref, gt2_ref, ng_ref, norm_refs, out_refs, cap, sets, final):
    slots = N_EXPERTS * cap
    j = lax.broadcasted_iota(jnp.int32, (LANES, slots), 1)
    e = lax.broadcasted_iota(jnp.int32, (LANES, slots), 0)
    expand = jnp.where(j // cap == e, 1.0, 0.0).astype(BF16)
    slot = (lax.broadcasted_iota(jnp.int32, (1, slots), 1) % cap).astype(F32)
    for k in range(sets):
        rows = slice(k * COMB_TM, (k + 1) * COMB_TM)
        rank = _bdot(rc_ref[rows, :].astype(BF16), expand)
        pt = jnp.where(rank == slot, 1.0, 0.0).astype(BF16)
        ffn = _bdot(pt, o_ref[:, k * cap:(k + 1) * cap, :].reshape(slots, D_MODEL))
        x = x_ref[rows, :] + gt2_ref[...] * ffn
        if final:
            out_refs[0][rows, :] = _rms(x) * ng_ref[...]
        else:
            out_refs[0][rows, :] = x
            out_refs[1][rows, :] = _modnorm(x, ng_ref[...], norm_refs[0][...], norm_refs[1][...]).astype(BF16)


def _combine_kernel(*refs, final):
    n_in, n_out = (4, 1) if final else (6, 2)
    ng_ref = refs[2 * n_in]
    outs = refs[2 * n_in + 1:]
    for g, (cap, sets) in enumerate(((CAP_CTX, COMB_CTX_SETS), (CAP_LAT, 1))):
        o_ref, rc_ref, x_ref, gt2_ref, *norm_refs = refs[g * n_in:(g + 1) * n_in]
        _combine_group(o_ref, rc_ref, x_ref, gt2_ref, ng_ref, norm_refs, outs[g * n_out:(g + 1) * n_out], cap, sets,
                       final)


def _combine(out, rc_ctx, rc_lat, x1_ctx, x1_lat, mod, norm_g, layer, final):
    lat_tiles = LAT_LEN // COMB_TM
    lat_slot0 = N_CTX_SETS * CAP_CTX // CAP_LAT
    groups = (
        (rc_ctx, x1_ctx, lambda i: 0, COMB_CTX_SETS * CTX_LEN,
         pl.BlockSpec((N_EXPERTS, COMB_CTX_SETS * CAP_CTX, D_MODEL), lambda i: (0, i, 0))),
        (rc_lat, x1_lat, lambda i: 1 + i // lat_tiles, COMB_TM,
         pl.BlockSpec((N_EXPERTS, CAP_LAT, D_MODEL), lambda i: (0, lat_slot0 + i // lat_tiles, 0))),
    )
    in_specs, args, out_specs, out_shape = [], [], [], []
    for rc, x1, row_fn, tm, o_spec in groups:
        tile = lambda w, tm=tm: pl.BlockSpec((tm, w), lambda i: (i, 0))
        in_specs += [o_spec, tile(LANES), tile(D_MODEL), _mod_spec(layer, row_fn, MOD_GT2)]
        args += [out, rc, x1, mod]
        if not final:
            in_specs += [_mod_spec(layer + 1, row_fn, MOD_SH1), _mod_spec(layer + 1, row_fn, MOD_SC1)]
            args += [mod, mod]
        out_specs += [tile(D_MODEL)] * (1 if final else 2)
        out_shape += [jax.ShapeDtypeStruct(x1.shape, F32)] + ([] if final else [jax.ShapeDtypeStruct(x1.shape, BF16)])
    if final:
        in_specs.append(pl.BlockSpec((1, D_MODEL), lambda i: (0, 0)))
        args.append(norm_g.reshape(1, D_MODEL))
    else:
        in_specs.append(pl.BlockSpec((None, 1, D_MODEL), lambda i: (layer + 1, 0, 0)))
        args.append(norm_g)
    res = pl.pallas_call(
        functools.partial(_combine_kernel, final=final),
        grid=(COMB_STEPS,),
        in_specs=in_specs,
        out_specs=out_specs,
        out_shape=out_shape,
        compiler_params=_params(1),
        name="combine",
    )(*args)
    half = len(res) // 2
    return tuple(res[:half]), tuple(res[half:])


def kernel(x_prompt, x_sample, cache_attn_k, cache_attn_v, cache_na_k, cache_na_v, c, c_ctx, w_ada, b_ada,
           norm_mix, norm_ffn, w_in, q_norm, k_norm, rpb, w_branch_a, w_branch_b, w_out, w_router, w_gate,
           w_up, w_down, final_norm):
    x_ctx = x_prompt.reshape(T_CTX, D_MODEL)
    x_lat = x_sample.reshape(T_LAT, D_MODEL)
    mod = _modulation(c, c_ctx, w_ada, b_ada)
    norm_mix3 = norm_mix.reshape(DEPTH, 1, D_MODEL)
    norm_ffn3 = norm_ffn.reshape(DEPTH, 1, D_MODEL)
    w_router_t = jnp.swapaxes(w_router, 1, 2)
    ctx_row = lambda *g: 0

    h_ctx, h_lat = _prenorm(x_ctx, x_lat, mod, norm_mix3, 0)
    layer_caches = []
    for layer in range(DEPTH):
        last = layer == DEPTH - 1
        yq, ykv, merge_w = _projection(h_ctx, h_lat, w_in, q_norm, k_norm, w_branch_a, w_branch_b, w_out, layer)
        yab_ctx, *new_caches = _ctx_attention(yq, ykv, layer_caches if last else [])
        layer_caches.append(tuple(new_caches))
        ya_lat = _lat_gqa_attention(yq, ykv, cache_attn_k, cache_attn_v, layer)
        yb_lat = _lat_na_attention(yq, ykv, cache_na_k, cache_na_v, rpb, layer)
        merge_args = (mod, norm_ffn3, merge_w, w_router_t, layer)
        x1_ctx, h2_ctx, lg_ctx = _merge([yab_ctx], h_ctx, x_ctx, *merge_args, ctx_row, MERGE_TM, MERGE_SUB)
        x1_lat, h2_lat, lg_lat, xg_ctx, g_ctx, rc_ctx = _merge(
            [ya_lat, yb_lat], h_lat, x_lat, *merge_args, _lat_mod_row(MERGE_ROUTE_TM), MERGE_ROUTE_TM,
            MERGE_ROUTE_SUB, route=(lg_ctx, h2_ctx))
        xg_lat, g_lat, rc_lat = _route(lg_lat, h2_lat, LAT_LEN, CAP_LAT, N_LAT_SETS)
        out = _experts(xg_ctx, xg_lat, g_ctx, g_lat, w_gate, w_up, w_down, layer)
        final = layer == DEPTH - 1
        norm_g = final_norm if final else norm_mix3
        res_ctx, res_lat = _combine(out, rc_ctx, rc_lat, x1_ctx, x1_lat, mod, norm_g, layer, final)
        if final:
            (y_ctx,), (y_lat,) = res_ctx, res_lat
        else:
            (x_ctx, h_ctx), (x_lat, h_lat) = res_ctx, res_lat

    heads = (N_KV_A, N_KV_A, N_HEADS_B, N_HEADS_B)
    new_caches = [a.reshape(N_CTX_SETS, DEPTH, CTX_LEN, h, HEAD_DIM) for a, h in zip(layer_caches[-1], heads)]
    return (y_ctx.reshape(N_CTX_SETS, CTX_LEN, D_MODEL), y_lat.reshape(N_LAT_SETS, LAT_LEN, D_MODEL), *new_caches)
```
